```python
import math
import jax
import jax.numpy as jnp
from jax import lax
import numpy as np

D_MODEL = 1024
BATCH = 4
SEQ = 8192
DEPTH = 2

GRID_W = 64
CTX_LEN = 256
EPS = 1e-6
SUBLN_EPS = 1e-5

FN_GROUPS = 4
FN_GROUP_DIM = 64
FN_WIDTH = FN_GROUPS * FN_GROUP_DIM

HY_WIDTH = 256
HY_ORDER = 2
HY_SHORT = 3
HY_EMB_BANDS = 16
HY_EMB_DIM = 1 + 2 * HY_EMB_BANDS
HY_FILTER_ORDER = 64
HY_DECAY_TARGET = 1e-2
HY_FAST_DECAY = 0.3
HY_SLOW_DECAY = 1.5

DA_HEADS = 4
DA_QK_DIM = 64
DA_V_DIM = 2 * DA_QK_DIM
DA_WIDTH = DA_HEADS * DA_V_DIM
ROPE_BASE = 10000.0
Q_BLOCK = 128

N_BRANCHES = 3
COL_F = FN_WIDTH
COL_HY = (HY_ORDER + 1) * HY_WIDTH
COL_QK = DA_HEADS * 2 * DA_QK_DIM
COL_V = DA_WIDTH
COL_G = N_BRANCHES * D_MODEL
OFF_F = 0
OFF_HY = OFF_F + COL_F
OFF_Q = OFF_HY + COL_HY
OFF_K = OFF_Q + COL_QK
OFF_V = OFF_K + COL_QK
OFF_G = OFF_V + COL_V
IN_COLS = OFF_G + COL_G

N_EXPERTS = 32
TOP_K = 4
D_EXPERT = 1024
SWIGLU_ALPHA = 1.702
SWIGLU_LIMIT = 7.0
MOE_BLOCK = 256

kernel_name = 'hybrid_fourier_hyena_diffattn_moe_dit'


def rms_norm(x, g, eps=EPS):
    x32 = x.astype(jnp.float32)
    y = x32 * lax.rsqrt(jnp.mean(x32 * x32, axis=-1, keepdims=True) + eps)
    return (y * g.astype(jnp.float32)).astype(x.dtype)


def modulate(h, shift, scale):
    return h * (1 + scale) + shift


def fourier_mix(u):
    b, l, _ = u.shape
    ug = u.astype(jnp.float32).reshape(b, l, FN_GROUPS, FN_GROUP_DIM)
    y = jnp.fft.fftn(ug, axes=(1, 3), norm='ortho').real
    return y.reshape(b, l, FN_WIDTH).astype(u.dtype)


def short_conv(u, w, b):
    l = u.shape[1]
    r = HY_SHORT // 2
    up = jnp.pad(u, ((0, 0), (r, r), (0, 0)))
    return sum(up[:, j:j + l] * w[j] for j in range(HY_SHORT)) + b


def hyena_filter_spectra(l, p):
    t = jnp.linspace(0.0, 1.0, l, dtype=jnp.float32)[:, None]
    ang = (2.0 * math.pi / l) * jnp.arange(l, dtype=jnp.float32)[:, None]
    bands = jnp.linspace(1e-4, HY_EMB_BANDS - 1, HY_EMB_BANDS, dtype=jnp.float32)[None, :]
    emb = jnp.concatenate([t, jnp.cos(bands * ang), -jnp.sin(bands * ang)], axis=-1)
    z = jnp.sin(p['hy_freq1'] * (emb @ p['hy_w1'] + p['hy_b1']))
    z = jnp.sin(p['hy_freq2'] * (z @ p['hy_w2'] + p['hy_b2']))
    h = (z @ p['hy_w3'] + p['hy_b3']).astype(jnp.float32).reshape(l, HY_ORDER, 2, HY_WIDTH)
    deltas = jnp.abs(jnp.linspace(math.log(HY_DECAY_TARGET) / HY_SLOW_DECAY,
                                  math.log(HY_DECAY_TARGET) / HY_FAST_DECAY, HY_WIDTH, dtype=jnp.float32))
    h = h * jnp.exp(-t * deltas)[:, None, None, :]
    h_fwd, h_bwd = h[:, :, 0], h[:, :, 1]
    filt = jnp.concatenate([h_fwd, jnp.zeros((1, HY_ORDER, HY_WIDTH), jnp.float32), h_bwd[1:][::-1]], axis=0)
    filt = filt / jnp.sum(jnp.abs(filt), axis=0, keepdims=True)
    return jnp.fft.rfft(filt, axis=0)


def long_conv(u, spec, bias):
    l = u.shape[1]
    u32 = u.astype(jnp.float32)
    uf = jnp.fft.rfft(u32, n=2 * l, axis=1)
    y = jnp.fft.irfft(uf * spec[None], n=2 * l, axis=1)[:, :l]
    return (y + u32 * bias).astype(u.dtype)


def hyena_mix(z, p, spec):
    z = short_conv(z, p['hy_conv_w'], p['hy_conv_b'])
    v, x1, x2 = jnp.split(z, HY_ORDER + 1, axis=-1)
    y = x1 * long_conv(v, spec[:, 0], p['hy_bias'][0])
    return x2 * long_conv(y, spec[:, 1], p['hy_bias'][1])


def axial_rope_tables(n_lat):
    rows = n_lat // GRID_W
    row = jnp.repeat(jnp.arange(rows), GRID_W)
    col = jnp.tile(jnp.arange(GRID_W), rows)
    pos = jnp.stack([row, col], axis=-1).astype(jnp.float32)
    n_freq = DA_QK_DIM // 4
    inv = ROPE_BASE ** (-jnp.arange(n_freq, dtype=jnp.float32) / n_freq)
    ang = pos[:, :, None] * inv
    return jnp.cos(ang), jnp.sin(ang)


def apply_axial_rope(t, cos, sin):
    tr = t.reshape(t.shape[:-1] + (2, 2, DA_QK_DIM // 4))
    a, b = tr[..., 0, :], tr[..., 1, :]
    c = cos[None, :, None, None]
    s = sin[None, :, None, None]
    out = jnp.stack([a * c - b * s, b * c + a * s], axis=-2)
    return out.reshape(t.shape).astype(t.dtype)


def diff_attend(q, k, v, lam):
    s = jnp.einsum('bqhmd,bkhmd->bhmqk', q, k).astype(jnp.float32) * (DA_QK_DIM ** -0.5)
    pr = jax.nn.softmax(s, axis=-1)
    a = pr[:, :, 0] - lam * pr[:, :, 1]
    return jnp.einsum('bhqk,bkhe->bqhe', a.astype(v.dtype), v)


def latent_diff_attention(q, k_all, v_all, lam):
    b, l = q.shape[:2]
    nb = l // Q_BLOCK
    qb = q.reshape((b, nb, Q_BLOCK) + q.shape[2:]).swapaxes(0, 1)
    o = lax.map(lambda blk: diff_attend(blk, k_all, v_all, lam), qb)
    return o.swapaxes(0, 1).reshape(b, l, DA_HEADS, DA_V_DIM)


def mixer_merge(proj, attn_heads, spec, p, lam_init):
    y_f = fourier_mix(proj[..., OFF_F:OFF_HY]) @ p['w_f']
    y_h = hyena_mix(proj[..., OFF_HY:OFF_Q], p, spec) @ p['w_h']
    o = rms_norm(attn_heads, p['subln_g'], SUBLN_EPS) * (1.0 - lam_init)
    y_a = o.reshape(o.shape[:2] + (DA_WIDTH,)) @ p['w_a']
    g_f, g_h, g_a = jnp.split(jax.nn.sigmoid(proj[..., OFF_G:]), N_BRANCHES, axis=-1)
    return (g_f * y_f + g_h * y_h + g_a * y_a) @ p['w_o']


def swiglu_clamped(u):
    x_glu, x_lin = u[..., ::2], u[..., 1::2]
    x_glu = jnp.minimum(x_glu, SWIGLU_LIMIT)
    x_lin = jnp.clip(x_lin, -SWIGLU_LIMIT, SWIGLU_LIMIT)
    return x_glu * jax.nn.sigmoid(SWIGLU_ALPHA * x_glu) * (x_lin + 1)


def moe_ffn(h, p):
    t_tok, d = h.shape
    logits = (h @ p['w_router'] + p['b_router']).astype(jnp.float32)
    top_v, top_i = lax.top_k(logits, TOP_K)
    gate = jax.nn.softmax(top_v, axis=-1)
    n_assign = t_tok * TOP_K
    flat_e = top_i.reshape(-1)
    order = jnp.argsort(flat_e)
    sorted_e = flat_e[order]
    sorted_tok = order // TOP_K
    sorted_gate = gate.reshape(-1)[order]
    counts = jnp.bincount(flat_e, length=N_EXPERTS)
    padded = (counts + MOE_BLOCK - 1) // MOE_BLOCK * MOE_BLOCK
    start = jnp.cumsum(counts) - counts
    pad_end = jnp.cumsum(padded)
    pad_start = pad_end - padded
    dest = pad_start[sorted_e] + jnp.arange(n_assign) - start[sorted_e]
    n_blocks = -(-n_assign // MOE_BLOCK) + N_EXPERTS
    n_rows = n_blocks * MOE_BLOCK
    row_tok = jnp.full((n_rows,), t_tok, jnp.int32).at[dest].set(sorted_tok.astype(jnp.int32))
    row_gate = jnp.zeros((n_rows,), jnp.float32).at[dest].set(sorted_gate)
    blk_exp = jnp.minimum(jnp.searchsorted(pad_end, jnp.arange(n_blocks) * MOE_BLOCK, side='right'), N_EXPERTS - 1)
    h_pad = jnp.concatenate([h, jnp.zeros((1, d), h.dtype)], axis=0)
    w1, b1, w2, b2 = p['w_e1'], p['b_e1'], p['w_e2'], p['b_e2']

    def expert_block(args):
        tok, e = args
        u = h_pad[tok] @ w1[e] + b1[e]
        return swiglu_clamped(u) @ w2[e] + b2[e]

    y = lax.map(expert_block, (row_tok.reshape(n_blocks, MOE_BLOCK), blk_exp))
    y = y.reshape(n_rows, d) * row_gate[:, None].astype(y.dtype)
    return jax.ops.segment_sum(y, row_tok, num_segments=t_tok + 1)[:t_tok]


def hybrid_layer(l, x, xc, c, c_ctx, p, ctx_out):
    b, n_lat, d = x.shape
    n_ctx = xc.shape[1]
    lam_init = 0.8 - 0.6 * math.exp(-0.3 * l)
    lam = (jnp.exp(jnp.sum(p['lam_q'][0] * p['lam_k'][0]).astype(jnp.float32))
           - jnp.exp(jnp.sum(p['lam_q'][1] * p['lam_k'][1]).astype(jnp.float32)) + lam_init)
    mod = jnp.split((jax.nn.silu(c) @ p['w_mod'] + p['b_mod'])[:, None, :], 6, axis=-1)
    mod_c = jnp.split(jax.nn.silu(c_ctx) @ p['w_mod'] + p['b_mod'], 6, axis=-1)

    h = modulate(rms_norm(x, p['norm1_g']), mod[0], mod[1])
    hc = modulate(rms_norm(xc, p['norm1_g']), mod_c[0], mod_c[1])
    proj = h @ p['w_in']
    if ctx_out:
        proj_c = hc @ p['w_in']
        kv_c = proj_c[..., OFF_K:OFF_G]
    else:
        kv_c = hc @ p['w_in'][:, OFF_K:OFF_G]
    k_c = rms_norm(kv_c[..., :COL_QK].reshape(b, n_ctx, DA_HEADS, 2, DA_QK_DIM), p['k_norm_g'])
    v_c = kv_c[..., COL_QK:].reshape(b, n_ctx, DA_HEADS, DA_V_DIM)

    cos, sin = axial_rope_tables(n_lat)
    q = apply_axial_rope(rms_norm(proj[..., OFF_Q:OFF_K].reshape(b, n_lat, DA_HEADS, 2, DA_QK_DIM), p['q_norm_g']), cos, sin)
    k = apply_axial_rope(rms_norm(proj[..., OFF_K:OFF_V].reshape(b, n_lat, DA_HEADS, 2, DA_QK_DIM), p['k_norm_g']), cos, sin)
    v = proj[..., OFF_V:OFF_G].reshape(b, n_lat, DA_HEADS, DA_V_DIM)
    o = latent_diff_attention(q, jnp.concatenate([k, k_c], axis=1), jnp.concatenate([v, v_c], axis=1), lam)
    x_mixed = x + mod[2] * mixer_merge(proj, o, hyena_filter_spectra(n_lat, p), p, lam_init)
    if ctx_out:
        q_c = rms_norm(proj_c[..., OFF_Q:OFF_K].reshape(b, n_ctx, DA_HEADS, 2, DA_QK_DIM), p['q_norm_g'])
        o_c = diff_attend(q_c, k_c, v_c, lam)
        xc = xc + mod_c[2] * mixer_merge(proj_c, o_c, hyena_filter_spectra(n_ctx, p), p, lam_init)
    x = x_mixed

    h2 = modulate(rms_norm(x, p['norm2_g']), mod[3], mod[4]).reshape(b * n_lat, d)
    if ctx_out:
        h2c = modulate(rms_norm(xc, p['norm2_g']), mod_c[3], mod_c[4]).reshape(b * n_ctx, d)
        y = moe_ffn(jnp.concatenate([h2c, h2], axis=0), p)
        xc = xc + mod_c[5] * y[:b * n_ctx].reshape(b, n_ctx, d)
        y_lat = y[b * n_ctx:]
    else:
        y_lat = moe_ffn(h2, p)
    x = x + mod[5] * y_lat.reshape(b, n_lat, d)
    return x, xc


def setup_inputs(seed: int = 0) -> dict:
    key = jax.random.key(seed)
    ks = jax.random.split(key, 35)
    D = D_MODEL
    hy_cols = (HY_ORDER + 1) * HY_WIDTH
    hy_out = HY_ORDER * 2 * HY_WIDTH

    def nrm(k, shape, scale):
        return jax.random.normal(k, shape, jnp.float32) * scale

    return {
        'x': nrm(ks[0], (BATCH, SEQ, D), 1.0),
        'c': nrm(ks[1], (BATCH, D), 1.0),
        'ctx': nrm(ks[2], (BATCH, CTX_LEN, D), 1.0),
        'c_ctx': nrm(ks[3], (D,), 1.0),
        'w_mod': nrm(ks[4], (DEPTH, D, 6 * D), 0.5 * D ** -0.5),
        'b_mod': nrm(ks[5], (DEPTH, 6 * D), 0.02),
        'norm1_g': 1.0 + nrm(ks[6], (DEPTH, D), 0.02),
        'norm2_g': 1.0 + nrm(ks[7], (DEPTH, D), 0.02),
        'w_in': nrm(ks[8], (DEPTH, D, IN_COLS), D ** -0.5),
        'hy_conv_w': nrm(ks[9], (DEPTH, HY_SHORT, hy_cols), HY_SHORT ** -0.5),
        'hy_conv_b': nrm(ks[10], (DEPTH, hy_cols), 0.02),
        'hy_w1': nrm(ks[11], (DEPTH, HY_EMB_DIM, HY_FILTER_ORDER), HY_EMB_DIM ** -0.5),
        'hy_b1': nrm(ks[12], (DEPTH, HY_FILTER_ORDER), 0.02),
        'hy_freq1': 1.0 + nrm(ks[13], (DEPTH, HY_FILTER_ORDER), 0.02),
        'hy_w2': nrm(ks[14], (DEPTH, HY_FILTER_ORDER, HY_FILTER_ORDER), HY_FILTER_ORDER ** -0.5),
        'hy_b2': nrm(ks[15], (DEPTH, HY_FILTER_ORDER), 0.02),
        'hy_freq2': 1.0 + nrm(ks[16], (DEPTH, HY_FILTER_ORDER), 0.02),
        'hy_w3': nrm(ks[17], (DEPTH, HY_FILTER_ORDER, hy_out), HY_FILTER_ORDER ** -0.5),
        'hy_b3': nrm(ks[18], (DEPTH, hy_out), 0.02),
        'hy_bias': nrm(ks[19], (DEPTH, HY_ORDER, HY_WIDTH), 1.0),
        'q_norm_g': 1.0 + nrm(ks[20], (DEPTH, DA_QK_DIM), 0.02),
        'k_norm_g': 1.0 + nrm(ks[21], (DEPTH, DA_QK_DIM), 0.02),
        'lam_q': nrm(ks[22], (DEPTH, 2, DA_QK_DIM), 0.1),
        'lam_k': nrm(ks[23], (DEPTH, 2, DA_QK_DIM), 0.1),
        'subln_g': 1.0 + nrm(ks[24], (DEPTH, DA_V_DIM), 0.02),
        'w_f': nrm(ks[25], (DEPTH, FN_WIDTH, D), FN_WIDTH ** -0.5),
        'w_h': nrm(ks[26], (DEPTH, HY_WIDTH, D), HY_WIDTH ** -0.5),
        'w_a': nrm(ks[27], (DEPTH, DA_WIDTH, D), DA_WIDTH ** -0.5),
        'w_o': nrm(ks[28], (DEPTH, D, D), D ** -0.5),
        'w_router': nrm(ks[29], (DEPTH, D, N_EXPERTS), D ** -0.5),
        'b_router': nrm(ks[30], (DEPTH, N_EXPERTS), 0.01),
        'w_e1': nrm(ks[31], (DEPTH, N_EXPERTS, D, 2 * D_EXPERT), D ** -0.5),
        'b_e1': nrm(ks[32], (DEPTH, N_EXPERTS, 2 * D_EXPERT), 0.02),
        'w_e2': nrm(ks[33], (DEPTH, N_EXPERTS, D_EXPERT, D), D_EXPERT ** -0.5),
        'b_e2': nrm(ks[34], (DEPTH, N_EXPERTS, D), 0.02),
    }


def reference(x, c, ctx, c_ctx, w_mod, b_mod, norm1_g, norm2_g, w_in, hy_conv_w, hy_conv_b,
              hy_w1, hy_b1, hy_freq1, hy_w2, hy_b2, hy_freq2, hy_w3, hy_b3, hy_bias,
              q_norm_g, k_norm_g, lam_q, lam_k, subln_g, w_f, w_h, w_a, w_o,
              w_router, b_router, w_e1, b_e1, w_e2, b_e2):
    xc = ctx
    for l in range(DEPTH):
        p = {
            'w_mod': w_mod[l], 'b_mod': b_mod[l], 'norm1_g': norm1_g[l], 'norm2_g': norm2_g[l],
            'w_in': w_in[l], 'hy_conv_w': hy_conv_w[l], 'hy_conv_b': hy_conv_b[l],
            'hy_w1': hy_w1[l], 'hy_b1': hy_b1[l], 'hy_freq1': hy_freq1[l],
            'hy_w2': hy_w2[l], 'hy_b2': hy_b2[l], 'hy_freq2': hy_freq2[l],
            'hy_w3': hy_w3[l], 'hy_b3': hy_b3[l], 'hy_bias': hy_bias[l],
            'q_norm_g': q_norm_g[l], 'k_norm_g': k_norm_g[l], 'lam_q': lam_q[l], 'lam_k': lam_k[l],
            'subln_g': subln_g[l], 'w_f': w_f[l], 'w_h': w_h[l], 'w_a': w_a[l], 'w_o': w_o[l],
            'w_router': w_router[l], 'b_router': b_router[l],
            'w_e1': w_e1[l], 'b_e1': b_e1[l], 'w_e2': w_e2[l], 'b_e2': b_e2[l],
        }
        x, xc = hybrid_layer(l, x, xc, c, c_ctx, p, l < DEPTH - 1)
    return x
```

```python
import functools
import math

import numpy as np
import jax
import jax.numpy as jnp
from jax import lax
from jax.experimental import pallas as pl
from jax.experimental.pallas import tpu as pltpu

F32 = jnp.float32
BF16 = jnp.bfloat16

LANES = 128
VMEM_LIMIT = 56 * 1024 * 1024

GRID_W = 64
EPS = 1e-6
SUBLN_EPS = 1e-5
FN_GROUPS = 4
FN_GROUP_DIM = 64
FN_WIDTH = FN_GROUPS * FN_GROUP_DIM
HY_WIDTH = 256
HY_ORDER = 2
HY_SHORT = 3
HY_EMB_BANDS = 16
HY_DECAY_TARGET = 1e-2
HY_FAST_DECAY = 0.3
HY_SLOW_DECAY = 1.5
DA_HEADS = 4
DA_QK_DIM = 64
DA_V_DIM = 2 * DA_QK_DIM
DA_WIDTH = DA_HEADS * DA_V_DIM
ROPE_BASE = 10000.0
N_BRANCHES = 3
COL_QK = DA_HEADS * 2 * DA_QK_DIM
N_EXPERTS = 32
TOP_K = 4
SWIGLU_ALPHA = 1.702
SWIGLU_LIMIT = 7.0
MOE_BLOCK = 256
DFT_MIN_N1 = 16


def _dft_n2(seq):
    return min(LANES, seq // DFT_MIN_N1)


def _cparams(sem):
    return pltpu.CompilerParams(dimension_semantics=sem, vmem_limit_bytes=VMEM_LIMIT)


def _tile(n, pref):
    if n <= pref:
        return n
    for t in range(pref, 7, -1):
        if n % t == 0 and t % 8 == 0:
            return t
    return n


def _const_spec(shape):
    nd = len(shape)
    return pl.BlockSpec(shape, lambda *_: (0,) * nd, pipeline_mode=pl.Buffered(1))


def _normmod_kernel(x_ref, g_ref, sh_ref, sc_ref, o_ref):
    x = x_ref[0]
    y = x * lax.rsqrt(jnp.mean(x * x, axis=-1, keepdims=True) + EPS)
    y = y * g_ref[...]
    o_ref[0] = (y * (1.0 + sc_ref[0]) + sh_ref[0]).astype(o_ref.dtype)


def normmod(x, g, shift, scale):
    b, l, d = x.shape
    tl = _tile(l, 1024)
    return pl.pallas_call(
        _normmod_kernel,
        grid=(b, l // tl),
        in_specs=[
            pl.BlockSpec((1, tl, d), lambda i, j: (i, j, 0)),
            pl.BlockSpec((1, d), lambda i, j: (0, 0)),
            pl.BlockSpec((1, 1, d), lambda i, j: (i, 0, 0)),
            pl.BlockSpec((1, 1, d), lambda i, j: (i, 0, 0)),
        ],
        out_specs=pl.BlockSpec((1, tl, d), lambda i, j: (i, j, 0)),
        out_shape=jax.ShapeDtypeStruct((b, l, d), BF16),
        compiler_params=_cparams(("parallel", "parallel")),
        name="normmod",
    )(x, g.reshape(1, d), shift.reshape(b, 1, d), scale.reshape(b, 1, d))


def _mm_kernel(a_ref, w_ref, *rest, epi):
    acc = jnp.dot(a_ref[...], w_ref[...], preferred_element_type=F32)
    if epi == "plain":
        (o_ref,) = rest
    elif epi == "bias":
        b_ref, o_ref = rest
        acc = acc + b_ref[...]
    elif epi == "sigmoid":
        (o_ref,) = rest
        acc = jax.nn.sigmoid(acc)
    elif epi == "qk":
        gm_ref, gain_ref, cos_ref, sin_ref, o_ref = rest
        ms = jnp.dot((acc * acc).astype(BF16), gm_ref[...], preferred_element_type=F32)
        y = acc * lax.rsqrt(ms + EPS) * gain_ref[...]
        n = y.shape[1]
        reps = n // LANES
        lane = lax.broadcasted_iota(jnp.int32, y.shape, 1)
        is_a = (lane % (DA_QK_DIM // 2)) < (DA_QK_DIM // 4)
        half = DA_QK_DIM // 4
        swapped = jnp.where(is_a, pltpu.roll(y, n - half, axis=1), pltpu.roll(y, half, axis=1))
        acc = y * jnp.tile(cos_ref[...], (1, reps)) + swapped * jnp.tile(sin_ref[...], (1, reps))
    else:
        raise ValueError(epi)
    o_ref[...] = acc.astype(o_ref.dtype)


def matmul(a, w, *, out_dtype=F32, epi="plain", extra=(), extra_specs=(), tm=512, tn=1024, name="mm"):
    m, k = a.shape
    k2, n = w.shape
    assert k == k2
    tm = _tile(m, tm)
    tn = _tile(n, tn)
    return pl.pallas_call(
        functools.partial(_mm_kernel, epi=epi),
        grid=(m // tm, n // tn),
        in_specs=[
            pl.BlockSpec((tm, k), lambda i, j: (i, 0)),
            pl.BlockSpec((k, tn), lambda i, j: (0, j)),
            *extra_specs,
        ],
        out_specs=pl.BlockSpec((tm, tn), lambda i, j: (i, j)),
        out_shape=jax.ShapeDtypeStruct((m, n), out_dtype),
        compiler_params=_cparams(("parallel", "parallel")),
        name=name,
    )(a, w, *extra)


def _group_mean_matrix(n, group):
    idx = np.arange(n)
    return jnp.asarray((idx[:, None] // group == idx[None, :] // group).astype(np.float32) / group, BF16)


def rope_tables(n_lat):
    rows = n_lat // GRID_W
    row = np.repeat(np.arange(rows), GRID_W).astype(np.float64)
    col = np.tile(np.arange(GRID_W), rows).astype(np.float64)
    n_freq = DA_QK_DIM // 4
    inv = ROPE_BASE ** (-np.arange(n_freq, dtype=np.float64) / n_freq)
    ang_r = row[:, None] * inv
    ang_c = col[:, None] * inv
    cos = np.concatenate([np.cos(ang_r), np.cos(ang_r), np.cos(ang_c), np.cos(ang_c)], axis=1)
    sin = np.concatenate([-np.sin(ang_r), np.sin(ang_r), -np.sin(ang_c), np.sin(ang_c)], axis=1)
    cos = np.tile(cos, (1, LANES // DA_QK_DIM))
    sin = np.tile(sin, (1, LANES // DA_QK_DIM))
    return jnp.asarray(cos, F32), jnp.asarray(sin, F32)


def qk_project(h, w, gain, cos, sin, seq):
    m, _ = h.shape
    n = w.shape[1]
    tm = _tile(seq, 512)
    nblk = seq // tm
    gm = _group_mean_matrix(n, DA_QK_DIM)
    gain_t = jnp.tile(gain.astype(F32), n // DA_QK_DIM).reshape(1, n)
    extra_specs = [
        pl.BlockSpec((n, n), lambda i, j: (0, 0)),
        pl.BlockSpec((1, n), lambda i, j: (0, 0)),
        pl.BlockSpec((tm, LANES), lambda i, j: (i % nblk, 0)),
        pl.BlockSpec((tm, LANES), lambda i, j: (i % nblk, 0)),
    ]
    return matmul(h, w, out_dtype=BF16, epi="qk", extra=(gm, gain_t, cos, sin), extra_specs=extra_specs,
                  tm=tm, tn=n, name="mm_qk")


def _attn_kernel(lam_ref, q_ref, k_ref, v_ref, g_ref, o_ref, qs_ref, m_ref, l_ref, acc_ref, *, tq, tk, out_scale):
    ki = pl.program_id(3)

    @pl.when(ki == 0)
    def _():
        q = q_ref[...]
        lane = lax.broadcasted_iota(jnp.int32, q.shape, 1)
        zero = jnp.zeros_like(q)
        qs_ref[0:tq, :] = jnp.where(lane < DA_QK_DIM, q, zero)
        qs_ref[tq:2 * tq, :] = jnp.where(lane >= DA_QK_DIM, q, zero)
        m_ref[...] = jnp.full(m_ref.shape, -jnp.inf, F32)
        l_ref[...] = jnp.zeros(l_ref.shape, F32)
        acc_ref[...] = jnp.zeros(acc_ref.shape, F32)

    s = lax.dot_general(qs_ref[...], k_ref[...], (((1,), (1,)), ((), ())), preferred_element_type=F32)
    m_prev = m_ref[...]
    m_next = jnp.maximum(m_prev, jnp.max(s, axis=1, keepdims=True))
    p = jnp.exp(s - jnp.tile(m_next, (1, tk // LANES)))
    alpha = jnp.exp(m_prev - m_next)
    l_ref[...] = alpha * l_ref[...] + jnp.sum(p, axis=1, keepdims=True)
    m_ref[...] = m_next
    acc_ref[...] = acc_ref[...] * alpha + jnp.dot(p.astype(BF16), v_ref[...], preferred_element_type=F32)

    @pl.when(ki == pl.num_programs(3) - 1)
    def _():
        o1 = acc_ref[0:tq, :] / l_ref[0:tq, :]
        o2 = acc_ref[tq:2 * tq, :] / l_ref[tq:2 * tq, :]
        o = o1 - lam_ref[0, 0] * o2
        o = o * lax.rsqrt(jnp.mean(o * o, axis=-1, keepdims=True) + SUBLN_EPS)
        o_ref[...] = (o * g_ref[...] * out_scale).astype(o_ref.dtype)


def diff_attention(q, k, v, lam, subln_g, out_scale, nq, nk):
    b = q.shape[0] // nq
    tq = _tile(nq, 256)
    tk = next(t for t in (768, 512, 256, 128) if nk % t == 0)
    nqb, nkb = nq // tq, nk // tk
    kern = functools.partial(_attn_kernel, tq=tq, tk=tk, out_scale=out_scale)
    return pl.pallas_call(
        kern,
        grid=(b, DA_HEADS, nqb, nkb),
        in_specs=[
            pl.BlockSpec(memory_space=pltpu.SMEM),
            pl.BlockSpec((tq, DA_V_DIM), lambda bi, h, qi, ki: (bi * nqb + qi, h)),
            pl.BlockSpec((tk, DA_V_DIM), lambda bi, h, qi, ki: (bi * nkb + ki, h)),
            pl.BlockSpec((tk, DA_V_DIM), lambda bi, h, qi, ki: (bi * nkb + ki, h)),
            pl.BlockSpec((1, DA_V_DIM), lambda bi, h, qi, ki: (0, 0)),
        ],
        out_specs=pl.BlockSpec((tq, DA_V_DIM), lambda bi, h, qi, ki: (bi * nqb + qi, h)),
        out_shape=jax.ShapeDtypeStruct((b * nq, DA_WIDTH), BF16),
        scratch_shapes=[
            pltpu.VMEM((2 * tq, DA_V_DIM), BF16),
            pltpu.VMEM((2 * tq, LANES), F32),
            pltpu.VMEM((2 * tq, LANES), F32),
            pltpu.VMEM((2 * tq, DA_V_DIM), F32),
        ],
        compiler_params=_cparams(("parallel", "parallel", "parallel", "arbitrary")),
        name="diff_attn",
    )(lam.reshape(1, 1).astype(F32), q, k, v, subln_g.reshape(1, DA_V_DIM).astype(F32))


def _dft_tables_real(seq):
    n2 = _dft_n2(seq)
    n1h = seq // n2
    n1 = 2 * n1h
    n = n1 * n2
    k1 = np.arange(n1, dtype=np.float64)[None, :, None]
    nn = (n2 * np.arange(n1h, dtype=np.float64)[None, None, :] + np.arange(n2, dtype=np.float64)[:, None, None])
    ang = 2.0 * np.pi * k1 * nn / n
    e_fwd = np.concatenate([np.cos(ang), -np.sin(ang)], axis=1)
    e_inv = np.transpose(e_fwd, (0, 2, 1))
    a2 = 2.0 * np.pi * np.outer(np.arange(n2), np.arange(n2)) / n2
    c, s = np.cos(a2), np.sin(a2)
    f_fwd = np.block([[c, s], [-s, c]])
    f_inv = np.block([[c, -s], [s, c]])
    return tuple(jnp.asarray(t, BF16) for t in (e_fwd, f_fwd, f_inv, e_inv))


def _spectrum_kernel(u_ref, ef_ref, ff_ref, o_ref, scr, *, n1, n1h, n2, kc):
    kk = pl.program_id(2)

    @pl.when(kk == 0)
    def _():
        def stage1(j, c):
            x = u_ref[pl.ds(j, n1h, stride=n2), :].astype(BF16)
            scr[pl.ds(j, 2 * n1, stride=n2), :] = jnp.dot(ef_ref[j], x, preferred_element_type=F32)
            return c

        lax.fori_loop(0, n2, stage1, 0)

    def stage2(t, c):
        k1 = kk * kc + t
        re = scr[pl.ds(pl.multiple_of(k1 * n2, n2), n2), :]
        im = scr[pl.ds(pl.multiple_of((n1 + k1) * n2, n2), n2), :]
        a = jnp.concatenate([re, im], axis=0).astype(BF16)
        o_ref[t] = jnp.dot(ff_ref[...], a, preferred_element_type=F32)
        return c

    lax.fori_loop(0, kc, stage2, 0)


def dft_spectrum(h, tables):
    s, seq, ch = h.shape
    n2 = _dft_n2(seq)
    n1h = seq // n2
    n1 = 2 * n1h
    e_fwd, f_fwd, _, _ = tables
    kc = min(n1, 16)
    kern = functools.partial(_spectrum_kernel, n1=n1, n1h=n1h, n2=n2, kc=kc)
    return pl.pallas_call(
        kern,
        grid=(s, ch // LANES, n1 // kc),
        in_specs=[
            pl.BlockSpec((None, seq, LANES), lambda i, c, k: (i, 0, c)),
            _const_spec(e_fwd.shape),
            _const_spec(f_fwd.shape),
        ],
        out_specs=pl.BlockSpec((None, kc, 2 * n2, LANES), lambda i, c, k: (i, k, 0, c)),
        out_shape=jax.ShapeDtypeStruct((s, n1, 2 * n2, ch), F32),
        scratch_shapes=[pltpu.VMEM((2 * n1 * n2, LANES), F32)],
        compiler_params=_cparams(("parallel", "parallel", "arbitrary")),
        name="dft_spectrum",
    )(h, e_fwd, f_fwd)


def _longconv_kernel(u_ref, g_ref, h_ref, bias_ref, ef_ref, ff_ref, fi_ref, ei_ref, o_ref, scr, *, n1, n1h, n2):
    def stage1(j, c):
        x = u_ref[pl.ds(j, n1h, stride=n2), :].astype(BF16)
        scr[pl.ds(j, 2 * n1, stride=n2), :] = jnp.dot(ef_ref[j], x, preferred_element_type=F32)
        return c

    lax.fori_loop(0, n2, stage1, 0)

    def stage2(k1, c):
        r0 = pl.multiple_of(k1 * n2, n2)
        i0 = pl.multiple_of((n1 + k1) * n2, n2)
        a = jnp.concatenate([scr[pl.ds(r0, n2), :], scr[pl.ds(i0, n2), :]], axis=0).astype(BF16)
        y = jnp.dot(ff_ref[...], a, preferred_element_type=F32)
        hk = h_ref[k1].astype(F32)
        yr, yi = y[:n2], y[n2:]
        hr, hi = hk[:n2], hk[n2:]
        z = jnp.concatenate([yr * hr - yi * hi, yr * hi + yi * hr], axis=0).astype(BF16)
        bk = jnp.dot(fi_ref[...], z, preferred_element_type=F32)
        scr[pl.ds(r0, n2), :] = bk[:n2]
        scr[pl.ds(i0, n2), :] = bk[n2:]
        return c

    lax.fori_loop(0, n1, stage2, 0)

    def stage3(j, c):
        bmat = scr[pl.ds(j, 2 * n1, stride=n2), :].astype(BF16)
        y = jnp.dot(ei_ref[j], bmat, preferred_element_type=F32)
        u = u_ref[pl.ds(j, n1h, stride=n2), :]
        g = g_ref[pl.ds(j, n1h, stride=n2), :]
        o_ref[pl.ds(j, n1h, stride=n2), :] = g * (y + u * bias_ref[...])
        return c

    lax.fori_loop(0, n2, stage3, 0)


def long_conv_gated(u, u_blk, g, g_blk, spec, bias, tables):
    b, seq, _ = u.shape
    ch = bias.shape[0]
    n2 = _dft_n2(seq)
    n1h = seq // n2
    n1 = 2 * n1h
    e_fwd, f_fwd, f_inv, e_inv = tables
    kern = functools.partial(_longconv_kernel, n1=n1, n1h=n1h, n2=n2)
    one = pl.Buffered(1)
    return pl.pallas_call(
        kern,
        grid=(ch // LANES, b),
        in_specs=[
            pl.BlockSpec((None, seq, LANES), lambda c, i: (i, 0, u_blk + c), pipeline_mode=one),
            pl.BlockSpec((None, seq, LANES), lambda c, i: (i, 0, g_blk + c), pipeline_mode=one),
            pl.BlockSpec((n1, 2 * n2, LANES), lambda c, i: (0, 0, c), pipeline_mode=one),
            pl.BlockSpec((1, LANES), lambda c, i: (0, c)),
            _const_spec(e_fwd.shape),
            _const_spec(f_fwd.shape),
            _const_spec(f_inv.shape),
            _const_spec(e_inv.shape),
        ],
        out_specs=pl.BlockSpec((None, seq, LANES), lambda c, i: (i, 0, c)),
        out_shape=jax.ShapeDtypeStruct((b, seq, ch), F32),
        scratch_shapes=[pltpu.VMEM((2 * n1 * n2, LANES), F32)],
        compiler_params=_cparams(("parallel", "parallel")),
        name="long_conv",
    )(u, g, spec, bias.reshape(1, ch).astype(F32), e_fwd, f_fwd, f_inv, e_inv)


def _dft_tables_complex(seq):
    n2 = _dft_n2(seq)
    n1 = seq // n2
    k1 = np.arange(n1, dtype=np.float64)[None, :, None]
    nn = (n2 * np.arange(n1, dtype=np.float64)[None, None, :] + np.arange(n2, dtype=np.float64)[:, None, None])
    ang = 2.0 * np.pi * k1 * nn / seq
    c, s = np.cos(ang), np.sin(ang)
    e_fwd = np.concatenate([np.concatenate([c, s], axis=2), np.concatenate([-s, c], axis=2)], axis=1)
    a2 = 2.0 * np.pi * np.outer(np.arange(n2), np.arange(n2)) / n2
    f_re = np.concatenate([np.cos(a2), np.sin(a2)], axis=1)
    return jnp.asarray(e_fwd, BF16), jnp.asarray(f_re, BF16)


def _seqdft_kernel(vr_ref, vi_ref, ef_ref, fr_ref, o_ref, scr, *, n1, n2, scale):
    def stage1(j, c):
        x = jnp.concatenate([vr_ref[pl.ds(j, n1, stride=n2), :], vi_ref[pl.ds(j, n1, stride=n2), :]], axis=0)
        scr[pl.ds(j, 2 * n1, stride=n2), :] = jnp.dot(ef_ref[j], x.astype(BF16), preferred_element_type=F32)
        return c

    lax.fori_loop(0, n2, stage1, 0)

    def stage2(k1, c):
        r0 = pl.multiple_of(k1 * n2, n2)
        i0 = pl.multiple_of((n1 + k1) * n2, n2)
        a = jnp.concatenate([scr[pl.ds(r0, n2), :], scr[pl.ds(i0, n2), :]], axis=0).astype(BF16)
        o_ref[pl.ds(k1, n2, stride=n1), :] = jnp.dot(fr_ref[...], a, preferred_element_type=F32) * scale
        return c

    lax.fori_loop(0, n1, stage2, 0)


def seq_dft_real(v, ch, scale, tables):
    b, seq, _ = v.shape
    n2 = _dft_n2(seq)
    n1 = seq // n2
    e_fwd, f_re = tables
    nblk = ch // LANES
    kern = functools.partial(_seqdft_kernel, n1=n1, n2=n2, scale=scale)
    return pl.pallas_call(
        kern,
        grid=(nblk, b),
        in_specs=[
            pl.BlockSpec((None, seq, LANES), lambda c, i: (i, 0, c)),
            pl.BlockSpec((None, seq, LANES), lambda c, i: (i, 0, nblk + c)),
            _const_spec(e_fwd.shape),
            _const_spec(f_re.shape),
        ],
        out_specs=pl.BlockSpec((None, seq, LANES), lambda c, i: (i, 0, c)),
        out_shape=jax.ShapeDtypeStruct((b, seq, ch), F32),
        scratch_shapes=[pltpu.VMEM((2 * n1 * n2, LANES), F32)],
        compiler_params=_cparams(("parallel", "parallel")),
        name="seq_dft",
    )(v, v, e_fwd, f_re)


def fourier_channel_matrix():
    a = 2.0 * np.pi * np.outer(np.arange(FN_GROUP_DIM), np.arange(FN_GROUP_DIM)) / FN_GROUP_DIM
    eye = np.eye(FN_GROUPS)
    return jnp.asarray(np.concatenate([np.kron(eye, np.cos(a)), -np.kron(eye, np.sin(a))], axis=1), BF16)


def _shortconv_kernel(u_ref, w_ref, b_ref, o_ref):
    u = u_ref[...]
    n = u.shape[0]
    row = lax.broadcasted_iota(jnp.int32, u.shape, 0)
    prev = jnp.where(row == 0, 0.0, pltpu.roll(u, 1, axis=0))
    nxt = jnp.where(row == n - 1, 0.0, pltpu.roll(u, n - 1, axis=0))
    o_ref[...] = prev * w_ref[0:1, :] + u * w_ref[1:2, :] + nxt * w_ref[2:3, :] + b_ref[...]


def short_conv(u, w, bias):
    b, seq, ch = u.shape
    return pl.pallas_call(
        _shortconv_kernel,
        grid=(b, ch // LANES),
        in_specs=[
            pl.BlockSpec((None, seq, LANES), lambda i, c: (i, 0, c)),
            pl.BlockSpec((HY_SHORT, LANES), lambda i, c: (0, c)),
            pl.BlockSpec((1, LANES), lambda i, c: (0, c)),
        ],
        out_specs=pl.BlockSpec((None, seq, LANES), lambda i, c: (i, 0, c)),
        out_shape=jax.ShapeDtypeStruct((b, seq, ch), F32),
        compiler_params=_cparams(("parallel", "parallel")),
        name="short_conv",
    )(u, w.astype(F32), bias.reshape(1, ch).astype(F32))


def _filter_kernel(emb_ref, w1_ref, b1_ref, f1_ref, w2_ref, b2_ref, f2_ref, w3_ref, b3_ref, dec_ref, o_ref):
    z = jnp.dot(emb_ref[...].astype(BF16), w1_ref[...], preferred_element_type=F32) + b1_ref[...]
    z = jnp.sin(f1_ref[...] * z)
    z = jnp.dot(z.astype(BF16), w2_ref[...], preferred_element_type=F32) + b2_ref[...]
    z = jnp.sin(f2_ref[...] * z)
    h = jnp.dot(z.astype(BF16), w3_ref[...], preferred_element_type=F32) + b3_ref[...]
    o_ref[...] = h * dec_ref[...]


def hyena_filters(seq, hy_w1, hy_b1, hy_freq1, hy_w2, hy_b2, hy_freq2, hy_w3, hy_b3):
    t = jnp.linspace(0.0, 1.0, seq, dtype=F32)[:, None]
    ang = (2.0 * math.pi / seq) * jnp.arange(seq, dtype=F32)[:, None]
    bands = jnp.linspace(1e-4, HY_EMB_BANDS - 1, HY_EMB_BANDS, dtype=F32)[None, :]
    emb = jnp.concatenate([t, jnp.cos(bands * ang), -jnp.sin(bands * ang)], axis=-1)
    kdim = emb.shape[1]
    kpad = LANES - kdim
    emb = jnp.pad(emb, ((0, 0), (0, kpad)))
    w1 = jnp.pad(hy_w1, ((0, kpad), (0, 0))).astype(BF16)
    deltas = jnp.abs(jnp.linspace(math.log(HY_DECAY_TARGET) / HY_SLOW_DECAY,
                                  math.log(HY_DECAY_TARGET) / HY_FAST_DECAY, HY_WIDTH, dtype=F32))
    decay = jnp.tile(jnp.exp(-t * deltas), (1, 2 * HY_ORDER))
    fo = hy_w1.shape[1]
    nout = hy_w3.shape[1]
    tl = _tile(seq, 1024)
    row = lambda a: a.reshape(1, -1).astype(F32)
    full = lambda shape: pl.BlockSpec(shape, lambda i: (0, 0))
    return pl.pallas_call(
        _filter_kernel,
        grid=(seq // tl,),
        in_specs=[
            pl.BlockSpec((tl, LANES), lambda i: (i, 0)),
            full((LANES, fo)), full((1, fo)), full((1, fo)),
            full((fo, fo)), full((1, fo)), full((1, fo)),
            full((fo, nout)), full((1, nout)),
            pl.BlockSpec((tl, nout), lambda i: (i, 0)),
        ],
        out_specs=pl.BlockSpec((tl, nout), lambda i: (i, 0)),
        out_shape=jax.ShapeDtypeStruct((seq, nout), F32),
        compiler_params=_cparams(("parallel",)),
        name="hyena_filter",
    )(emb, w1, row(hy_b1), row(hy_freq1), hy_w2.astype(BF16), row(hy_b2), row(hy_freq2),
      hy_w3.astype(BF16), row(hy_b3), decay)


def hyena_spectra(seq, filt, tables):
    n2 = _dft_n2(seq)
    h = filt.reshape(seq, HY_ORDER, 2, HY_WIDTH)
    h_fwd, h_bwd = h[:, :, 0], h[:, :, 1]
    h_bwd = h_bwd.at[0].set(0.0)
    norm = jnp.sum(jnp.abs(h_fwd), axis=0) + jnp.sum(jnp.abs(h_bwd), axis=0)
    stacked = jnp.concatenate([h_fwd, h_bwd], axis=1).transpose(1, 0, 2)
    sp = dft_spectrum(stacked, tables)
    sf, sb = sp[:HY_ORDER], sp[HY_ORDER:]
    re = sf[:, :, :n2] + sb[:, :, :n2]
    im = sf[:, :, n2:] - sb[:, :, n2:]
    scale = (1.0 / (2 * seq)) / norm
    return (jnp.concatenate([re, im], axis=2) * scale[:, None, None, :]).astype(BF16)


def _merge_kernel(x_ref, yf_ref, yh_ref, ya_ref, g_ref, wf_ref, wh_ref, wa_ref, wo_ref,
                  gate_ref, ng_ref, sh_ref, sc_ref, xo_ref, ho_ref):
    d = x_ref.shape[-1]
    g = g_ref[0].astype(F32)
    yf = jnp.dot(yf_ref[0].astype(BF16), wf_ref[...], preferred_element_type=F32)
    yh = jnp.dot(yh_ref[0].astype(BF16), wh_ref[...], preferred_element_type=F32)
    ya = jnp.dot(ya_ref[0], wa_ref[...], preferred_element_type=F32)
    mix = g[:, 0:d] * yf + g[:, d:2 * d] * yh + g[:, 2 * d:3 * d] * ya
    x = x_ref[0] + gate_ref[0] * jnp.dot(mix.astype(BF16), wo_ref[...], preferred_element_type=F32)
    xo_ref[0] = x
    y = x * lax.rsqrt(jnp.mean(x * x, axis=-1, keepdims=True) + EPS) * ng_ref[...]
    ho_ref[0] = y * (1.0 + sc_ref[0]) + sh_ref[0]


def merge_branches(x, yf, yh, ya, g, w_f, w_h, w_a, w_o, gate, norm_g, shift, scale):
    b, l, d = x.shape
    tl = _tile(l, 512)
    rows = lambda w: pl.BlockSpec((1, tl, w), lambda i, j: (i, j, 0))
    full = lambda a: pl.BlockSpec(a.shape, lambda i, j: (0, 0))
    per_b = pl.BlockSpec((1, 1, d), lambda i, j: (i, 0, 0))
    wf, wh, wa, wo = (w.astype(BF16) for w in (w_f, w_h, w_a, w_o))
    return pl.pallas_call(
        _merge_kernel,
        grid=(b, l // tl),
        in_specs=[rows(d), rows(yf.shape[-1]), rows(yh.shape[-1]), rows(ya.shape[-1]), rows(3 * d),
                  full(wf), full(wh), full(wa), full(wo),
                  per_b, pl.BlockSpec((1, d), lambda i, j: (0, 0)), per_b, per_b],
        out_specs=[rows(d), rows(d)],
        out_shape=[jax.ShapeDtypeStruct((b, l, d), F32), jax.ShapeDtypeStruct((b, l, d), F32)],
        compiler_params=_cparams(("parallel", "parallel")),
        name="merge",
    )(x, yf, yh, ya, g, wf, wh, wa, wo, gate.reshape(b, 1, d), norm_g.reshape(1, d).astype(F32),
      shift.reshape(b, 1, d), scale.reshape(b, 1, d))


def _moe_kernel(be_ref, act_ref, rows_ref, w1g_ref, w1l_ref, b1g_ref, b1l_ref, w2_ref, b2_ref, gate_ref, o_ref):
    i = pl.program_id(0)

    @pl.when(act_ref[i] > 0)
    def _():
        r = rows_ref[...].astype(BF16)
        xg = jnp.dot(r, w1g_ref[0], preferred_element_type=F32) + b1g_ref[0]
        xl = jnp.dot(r, w1l_ref[0], preferred_element_type=F32) + b1l_ref[0]
        xg = jnp.minimum(xg, SWIGLU_LIMIT)
        xl = jnp.clip(xl, -SWIGLU_LIMIT, SWIGLU_LIMIT)
        a = xg * jax.nn.sigmoid(SWIGLU_ALPHA * xg) * (xl + 1.0)
        y = jnp.dot(a.astype(BF16), w2_ref[0], preferred_element_type=F32) + b2_ref[0]
        o_ref[...] = y * gate_ref[...]

    @pl.when(act_ref[i] == 0)
    def _():
        o_ref[...] = jnp.zeros(o_ref.shape, o_ref.dtype)


def moe_experts(rows, row_gate, blk_exp, blk_act, w1g, w1l, b1g, b1l, w2, b2):
    r, d = rows.shape
    de = w2.shape[1]
    nblk = r // MOE_BLOCK
    grid_spec = pltpu.PrefetchScalarGridSpec(
        num_scalar_prefetch=2,
        grid=(nblk,),
        in_specs=[
            pl.BlockSpec((MOE_BLOCK, d), lambda i, be, act: (i, 0)),
            pl.BlockSpec((1, d, de), lambda i, be, act: (be[i], 0, 0)),
            pl.BlockSpec((1, d, de), lambda i, be, act: (be[i], 0, 0)),
            pl.BlockSpec((1, 1, de), lambda i, be, act: (be[i], 0, 0)),
            pl.BlockSpec((1, 1, de), lambda i, be, act: (be[i], 0, 0)),
            pl.BlockSpec((1, de, d), lambda i, be, act: (be[i], 0, 0)),
            pl.BlockSpec((1, 1, d), lambda i, be, act: (be[i], 0, 0)),
            pl.BlockSpec((MOE_BLOCK, 1), lambda i, be, act: (i, 0)),
        ],
        out_specs=pl.BlockSpec((MOE_BLOCK, d), lambda i, be, act: (i, 0)),
    )
    return pl.pallas_call(
        _moe_kernel,
        grid_spec=grid_spec,
        out_shape=jax.ShapeDtypeStruct((r, d), F32),
        compiler_params=_cparams(("arbitrary",)),
        name="moe_experts",
    )(blk_exp, blk_act, rows, w1g, w1l, b1g, b1l, w2, b2, row_gate)


def moe_ffn(h, p):
    t_tok, d = h.shape
    wr = jnp.pad(p["w_router"], ((0, 0), (0, LANES - N_EXPERTS))).astype(BF16)
    br = jnp.pad(p["b_router"], (0, LANES - N_EXPERTS)).reshape(1, LANES).astype(F32)
    logits = matmul(h.astype(BF16), wr, epi="bias", extra=(br,),
                    extra_specs=[pl.BlockSpec((1, LANES), lambda i, j: (0, 0))], name="mm_router")[:, :N_EXPERTS]
    top_v, top_i = lax.top_k(logits, TOP_K)
    gate = jax.nn.softmax(top_v, axis=-1)
    n_assign = t_tok * TOP_K
    flat_e = top_i.reshape(-1)
    onehot = (flat_e[:, None] == jnp.arange(N_EXPERTS, dtype=flat_e.dtype)[None, :]).astype(jnp.int32)
    csum = jnp.cumsum(onehot, axis=0)
    counts = csum[-1]
    rank = jnp.take_along_axis(csum, flat_e[:, None], axis=1)[:, 0] - 1
    padded = (counts + MOE_BLOCK - 1) // MOE_BLOCK * MOE_BLOCK
    pad_end = jnp.cumsum(padded)
    pad_start = pad_end - padded
    dest = pad_start[flat_e] + rank
    n_blocks = -(-n_assign // MOE_BLOCK) + N_EXPERTS
    n_rows = n_blocks * MOE_BLOCK
    row_tok = jnp.full((n_rows,), t_tok, jnp.int32).at[dest].set(jnp.arange(n_assign, dtype=jnp.int32) // TOP_K)
    row_gate = jnp.zeros((n_rows,), F32).at[dest].set(gate.reshape(-1))
    blk_start = jnp.arange(n_blocks, dtype=jnp.int32) * MOE_BLOCK
    blk_exp = jnp.minimum(jnp.searchsorted(pad_end, blk_start, side="right"), N_EXPERTS - 1).astype(jnp.int32)
    blk_act = (blk_start < pad_end[-1]).astype(jnp.int32)
    h_pad = jnp.concatenate([h, jnp.zeros((1, d), h.dtype)], axis=0)
    rows = h_pad[row_tok]
    y_rows = moe_experts(rows, row_gate[:, None], blk_exp, blk_act,
                         p["w1g"], p["w1l"], p["b1g"], p["b1l"], p["w2"], p["b2"])
    return jnp.sum(y_rows[dest.reshape(t_tok, TOP_K)], axis=1)


OFF_F = 0
OFF_HY = OFF_F + FN_WIDTH
OFF_Q = OFF_HY + (HY_ORDER + 1) * HY_WIDTH
OFF_K = OFF_Q + COL_QK
OFF_V = OFF_K + COL_QK
OFF_G = OFF_V + DA_WIDTH


def _token_mixer(x, h, mod_gate, norm2_g, mod_shift2, mod_scale2, p, lam, lam_init, rope, kv_extra, q_only_self):
    b, s, d = x.shape
    h2d = h.reshape(b * s, d)
    w = p["w_in"]
    cos, sin = rope

    w_fv = matmul(w[:, OFF_F:OFF_HY], fourier_channel_matrix(), out_dtype=BF16, name="mm_wfold")
    v_f = matmul(h2d, w_fv, name="mm_fproj").reshape(b, s, 2 * FN_WIDTH)
    y_f = seq_dft_real(v_f, FN_WIDTH, 1.0 / math.sqrt(s * FN_GROUP_DIM), _dft_tables_complex(s))

    z = matmul(h2d, w[:, OFF_HY:OFF_Q], name="mm_hproj").reshape(b, s, (HY_ORDER + 1) * HY_WIDTH)
    z = short_conv(z, p["hy_conv_w"], p["hy_conv_b"])
    tables = _dft_tables_real(s)
    filt = hyena_filters(s, p["hy_w1"], p["hy_b1"], p["hy_freq1"], p["hy_w2"], p["hy_b2"], p["hy_freq2"],
                         p["hy_w3"], p["hy_b3"])
    spec = hyena_spectra(s, filt, tables)
    cb = HY_WIDTH // LANES
    y_h = long_conv_gated(z, 0, z, cb, spec[0], p["hy_bias"][0], tables)
    y_h = long_conv_gated(y_h, 0, z, 2 * cb, spec[1], p["hy_bias"][1], tables)

    k_gain = p["k_norm_g"]
    k = qk_project(h2d, w[:, OFF_K:OFF_V], k_gain, cos, sin, s)
    v = matmul(h2d, w[:, OFF_V:OFF_G], out_dtype=BF16, name="mm_vproj")
    k3, v3 = k.reshape(b, s, COL_QK), v.reshape(b, s, DA_WIDTH)
    if q_only_self is None:
        y_a = None
    else:
        q = qk_project(h2d, w[:, OFF_Q:OFF_K], p["q_norm_g"] * (DA_QK_DIM ** -0.5), cos, sin, s)
        if kv_extra is not None:
            k_all = jnp.concatenate([k3, kv_extra[0]], axis=1)
            v_all = jnp.concatenate([v3, kv_extra[1]], axis=1)
        else:
            k_all, v_all = k3, v3
        nk = k_all.shape[1]
        y_a = diff_attention(q, k_all.reshape(b * nk, COL_QK), v_all.reshape(b * nk, DA_WIDTH), lam,
                             p["subln_g"], 1.0 - lam_init, s, nk).reshape(b, s, DA_WIDTH)

    g = matmul(h2d, w[:, OFF_G:], out_dtype=BF16, epi="sigmoid", name="mm_gates").reshape(b, s, N_BRANCHES * d)
    x_new, h2 = merge_branches(x, y_f, y_h, y_a, g, p["w_f"], p["w_h"], p["w_a"], p["w_o"],
                               mod_gate, norm2_g, mod_shift2, mod_scale2)
    return x_new, h2, (k3, v3)


def _context_kv(h, p, rope):
    b, s, d = h.shape
    h2d = h.reshape(b * s, d)
    w = p["w_in"]
    k = qk_project(h2d, w[:, OFF_K:OFF_V], p["k_norm_g"], rope[0], rope[1], s)
    v = matmul(h2d, w[:, OFF_V:OFF_G], out_dtype=BF16, name="mm_vproj_ctx")
    return k.reshape(b, s, COL_QK), v.reshape(b, s, DA_WIDTH)


def _layer(l, x, xc, c, c_ctx, p, ctx_out):
    b, n_lat, d = x.shape
    n_ctx = xc.shape[1]
    lam_init = 0.8 - 0.6 * math.exp(-0.3 * l)
    lam = (jnp.exp(jnp.sum(p["lam_q"][0] * p["lam_k"][0]).astype(F32))
           - jnp.exp(jnp.sum(p["lam_q"][1] * p["lam_k"][1]).astype(F32)) + lam_init)

    cond = jnp.concatenate([c, c_ctx[None, :], jnp.zeros((16 - b - 1, d), F32)], axis=0)
    mod_all = matmul(jax.nn.silu(cond).astype(BF16), p["w_mod"].astype(BF16), epi="bias",
                     extra=(p["b_mod"].reshape(1, 6 * d).astype(F32),),
                     extra_specs=[pl.BlockSpec((1, 1024), lambda i, j: (0, j))], name="mm_mod")
    mod = [mod_all[:b, i * d:(i + 1) * d] for i in range(6)]
    mod_c = [jnp.broadcast_to(mod_all[b, i * d:(i + 1) * d], (b, d)) for i in range(6)]

    pw = dict(p)
    pw["w_in"] = p["w_in"].astype(BF16)

    no_rope = (jnp.ones((n_ctx, LANES), F32), jnp.zeros((n_ctx, LANES), F32))
    hc = normmod(xc, p["norm1_g"], mod_c[0], mod_c[1])
    h = normmod(x, p["norm1_g"], mod[0], mod[1])
    if ctx_out:
        xc_new, h2c, kv_c = _token_mixer(xc, hc, mod_c[2], p["norm2_g"], mod_c[3], mod_c[4], pw, lam, lam_init,
                                         no_rope, None, True)
    else:
        kv_c = _context_kv(hc, pw, no_rope)
    x_new, h2, _ = _token_mixer(x, h, mod[2], p["norm2_g"], mod[3], mod[4], pw, lam, lam_init,
                                rope_tables(n_lat), kv_c, True)

    pe = {
        "w_router": p["w_router"], "b_router": p["b_router"],
        "w1g": p["w_e1"][:, :, 0::2].astype(BF16), "w1l": p["w_e1"][:, :, 1::2].astype(BF16),
        "b1g": p["b_e1"][:, None, 0::2], "b1l": p["b_e1"][:, None, 1::2],
        "w2": p["w_e2"].astype(BF16), "b2": p["b_e2"][:, None, :],
    }
    if ctx_out:
        y = moe_ffn(jnp.concatenate([h2c.reshape(b * n_ctx, d), h2.reshape(b * n_lat, d)], axis=0), pe)
        xc = xc_new + mod_c[5][:, None, :] * y[:b * n_ctx].reshape(b, n_ctx, d)
        y_lat = y[b * n_ctx:]
    else:
        y_lat = moe_ffn(h2.reshape(b * n_lat, d), pe)
    x = x_new + mod[5][:, None, :] * y_lat.reshape(b, n_lat, d)
    return x, xc


_PARAM_NAMES = ("w_mod", "b_mod", "norm1_g", "norm2_g", "w_in", "hy_conv_w", "hy_conv_b", "hy_w1", "hy_b1",
                "hy_freq1", "hy_w2", "hy_b2", "hy_freq2", "hy_w3", "hy_b3", "hy_bias", "q_norm_g", "k_norm_g",
                "lam_q", "lam_k", "subln_g", "w_f", "w_h", "w_a", "w_o", "w_router", "b_router",
                "w_e1", "b_e1", "w_e2", "b_e2")


def kernel(x, c, ctx, c_ctx, w_mod, b_mod, norm1_g, norm2_g, w_in, hy_conv_w, hy_conv_b, hy_w1, hy_b1, hy_freq1,
           hy_w2, hy_b2, hy_freq2, hy_w3, hy_b3, hy_bias, q_norm_g, k_norm_g, lam_q, lam_k, subln_g, w_f, w_h,
           w_a, w_o, w_router, b_router, w_e1, b_e1, w_e2, b_e2):
    stacked = (w_mod, b_mod, norm1_g, norm2_g, w_in, hy_conv_w, hy_conv_b, hy_w1, hy_b1, hy_freq1, hy_w2, hy_b2,
               hy_freq2, hy_w3, hy_b3, hy_bias, q_norm_g, k_norm_g, lam_q, lam_k, subln_g, w_f, w_h, w_a, w_o,
               w_router, b_router, w_e1, b_e1, w_e2, b_e2)
    depth = w_mod.shape[0]
    xc = ctx
    for l in range(depth):
        p = {name: arr[l] for name, arr in zip(_PARAM_NAMES, stacked)}
        x, xc = _layer(l, x, xc, c, c_ctx, p, l < depth - 1)
    return x
```

```python
import functools
import math

import numpy as np
import jax
import jax.numpy as jnp
from jax import lax
from jax.experimental import pallas as pl
from jax.experimental.pallas import tpu as pltpu

F32 = jnp.float32
BF16 = jnp.bfloat16

LANES = 128
VMEM_LIMIT = 56 * 1024 * 1024

GRID_W = 64
EPS = 1e-6
SUBLN_EPS = 1e-5
FN_GROUPS = 4
FN_GROUP_DIM = 64
FN_WIDTH = FN_GROUPS * FN_GROUP_DIM
HY_WIDTH = 256
HY_ORDER = 2
HY_SHORT = 3
HY_EMB_BANDS = 16
HY_DECAY_TARGET = 1e-2
HY_FAST_DECAY = 0.3
HY_SLOW_DECAY = 1.5
DA_HEADS = 4
DA_QK_DIM = 64
DA_V_DIM = 2 * DA_QK_DIM
DA_WIDTH = DA_HEADS * DA_V_DIM
ROPE_BASE = 10000.0
N_BRANCHES = 3
COL_QK = DA_HEADS * 2 * DA_QK_DIM
N_EXPERTS = 32
TOP_K = 4
SWIGLU_ALPHA = 1.702
SWIGLU_LIMIT = 7.0
MOE_BLOCK = 256
DFT_MIN_N1 = 16
DFT_UNROLL = 4


def _dft_n2(seq):
    return min(LANES, seq // DFT_MIN_N1)


def _cparams(sem):
    return pltpu.CompilerParams(dimension_semantics=sem, vmem_limit_bytes=VMEM_LIMIT)


def _tile(n, pref):
    if n <= pref:
        return n
    for t in range(pref, 7, -1):
        if n % t == 0 and t % 8 == 0:
            return t
    return n


def _const_spec(shape):
    nd = len(shape)
    return pl.BlockSpec(shape, lambda *_: (0,) * nd, pipeline_mode=pl.Buffered(1))


def _normmod_kernel(x_ref, g_ref, sh_ref, sc_ref, o_ref):
    x = x_ref[0]
    y = x * lax.rsqrt(jnp.mean(x * x, axis=-1, keepdims=True) + EPS)
    y = y * g_ref[...]
    o_ref[0] = (y * (1.0 + sc_ref[0]) + sh_ref[0]).astype(o_ref.dtype)


def normmod(x, g, shift, scale):
    b, l, d = x.shape
    tl = _tile(l, 1024)
    return pl.pallas_call(
        _normmod_kernel,
        grid=(b, l // tl),
        in_specs=[
            pl.BlockSpec((1, tl, d), lambda i, j: (i, j, 0)),
            pl.BlockSpec((1, d), lambda i, j: (0, 0)),
            pl.BlockSpec((1, 1, d), lambda i, j: (i, 0, 0)),
            pl.BlockSpec((1, 1, d), lambda i, j: (i, 0, 0)),
        ],
        out_specs=pl.BlockSpec((1, tl, d), lambda i, j: (i, j, 0)),
        out_shape=jax.ShapeDtypeStruct((b, l, d), BF16),
        compiler_params=_cparams(("parallel", "parallel")),
        name="normmod",
    )(x, g.reshape(1, d), shift.reshape(b, 1, d), scale.reshape(b, 1, d))


def _mm_kernel(a_ref, w_ref, *rest, epi):
    acc = jnp.dot(a_ref[...], w_ref[...], preferred_element_type=F32)
    if epi == "plain":
        (o_ref,) = rest
    elif epi == "bias":
        b_ref, o_ref = rest
        acc = acc + b_ref[...]
    elif epi == "sigmoid":
        (o_ref,) = rest
        acc = jax.nn.sigmoid(acc)
    elif epi == "qk":
        gm_ref, gain_ref, cos_ref, sin_ref, o_ref = rest
        ms = jnp.dot((acc * acc).astype(BF16), gm_ref[...], preferred_element_type=F32)
        y = acc * lax.rsqrt(ms + EPS) * gain_ref[...]
        n = y.shape[1]
        reps = n // LANES
        lane = lax.broadcasted_iota(jnp.int32, y.shape, 1)
        is_a = (lane % (DA_QK_DIM // 2)) < (DA_QK_DIM // 4)
        half = DA_QK_DIM // 4
        swapped = jnp.where(is_a, pltpu.roll(y, n - half, axis=1), pltpu.roll(y, half, axis=1))
        acc = y * jnp.tile(cos_ref[...], (1, reps)) + swapped * jnp.tile(sin_ref[...], (1, reps))
    else:
        raise ValueError(epi)
    o_ref[...] = acc.astype(o_ref.dtype)


def matmul(a, w, *, out_dtype=F32, epi="plain", extra=(), extra_specs=(), tm=512, tn=1024, name="mm"):
    m, k = a.shape
    k2, n = w.shape
    assert k == k2
    tm = _tile(m, tm)
    tn = _tile(n, tn)
    return pl.pallas_call(
        functools.partial(_mm_kernel, epi=epi),
        grid=(m // tm, n // tn),
        in_specs=[
            pl.BlockSpec((tm, k), lambda i, j: (i, 0)),
            pl.BlockSpec((k, tn), lambda i, j: (0, j)),
            *extra_specs,
        ],
        out_specs=pl.BlockSpec((tm, tn), lambda i, j: (i, j)),
        out_shape=jax.ShapeDtypeStruct((m, n), out_dtype),
        compiler_params=_cparams(("parallel", "parallel")),
        name=name,
    )(a, w, *extra)


def _group_mean_matrix(n, group):
    idx = np.arange(n)
    return jnp.asarray((idx[:, None] // group == idx[None, :] // group).astype(np.float32) / group, BF16)


def rope_tables(n_lat):
    rows = n_lat // GRID_W
    row = np.repeat(np.arange(rows), GRID_W).astype(np.float64)
    col = np.tile(np.arange(GRID_W), rows).astype(np.float64)
    n_freq = DA_QK_DIM // 4
    inv = ROPE_BASE ** (-np.arange(n_freq, dtype=np.float64) / n_freq)
    ang_r = row[:, None] * inv
    ang_c = col[:, None] * inv
    cos = np.concatenate([np.cos(ang_r), np.cos(ang_r), np.cos(ang_c), np.cos(ang_c)], axis=1)
    sin = np.concatenate([-np.sin(ang_r), np.sin(ang_r), -np.sin(ang_c), np.sin(ang_c)], axis=1)
    cos = np.tile(cos, (1, LANES // DA_QK_DIM))
    sin = np.tile(sin, (1, LANES // DA_QK_DIM))
    return jnp.asarray(cos, F32), jnp.asarray(sin, F32)


def qk_project(h, w, gain, cos, sin, seq):
    m, _ = h.shape
    n = w.shape[1]
    tm = _tile(seq, 512)
    nblk = seq // tm
    gm = _group_mean_matrix(n, DA_QK_DIM)
    gain_t = jnp.tile(gain.astype(F32), n // DA_QK_DIM).reshape(1, n)
    extra_specs = [
        pl.BlockSpec((n, n), lambda i, j: (0, 0)),
        pl.BlockSpec((1, n), lambda i, j: (0, 0)),
        pl.BlockSpec((tm, LANES), lambda i, j: (i % nblk, 0)),
        pl.BlockSpec((tm, LANES), lambda i, j: (i % nblk, 0)),
    ]
    return matmul(h, w, out_dtype=BF16, epi="qk", extra=(gm, gain_t, cos, sin), extra_specs=extra_specs,
                  tm=tm, tn=n, name="mm_qk")


def _attn_kernel(lam_ref, q_ref, k_ref, v_ref, g_ref, o_ref, qs_ref, m_ref, l_ref, acc_ref, *, tq, tk, out_scale):
    ki = pl.program_id(3)

    @pl.when(ki == 0)
    def _():
        q = q_ref[...]
        lane = lax.broadcasted_iota(jnp.int32, q.shape, 1)
        zero = jnp.zeros_like(q)
        qs_ref[0:tq, :] = jnp.where(lane < DA_QK_DIM, q, zero)
        qs_ref[tq:2 * tq, :] = jnp.where(lane >= DA_QK_DIM, q, zero)
        m_ref[...] = jnp.full(m_ref.shape, -jnp.inf, F32)
        l_ref[...] = jnp.zeros(l_ref.shape, F32)
        acc_ref[...] = jnp.zeros(acc_ref.shape, F32)

    s = lax.dot_general(qs_ref[...], k_ref[...], (((1,), (1,)), ((), ())), preferred_element_type=F32)
    m_prev = m_ref[...]
    m_next = jnp.maximum(m_prev, jnp.max(s, axis=1, keepdims=True))
    p = jnp.exp2(s - jnp.tile(m_next, (1, tk // LANES)))
    alpha = jnp.exp2(m_prev - m_next)
    l_ref[...] = alpha * l_ref[...] + jnp.sum(p, axis=1, keepdims=True)
    m_ref[...] = m_next
    acc_ref[...] = acc_ref[...] * alpha + jnp.dot(p.astype(BF16), v_ref[...], preferred_element_type=F32)

    @pl.when(ki == pl.num_programs(3) - 1)
    def _():
        o1 = acc_ref[0:tq, :] / l_ref[0:tq, :]
        o2 = acc_ref[tq:2 * tq, :] / l_ref[tq:2 * tq, :]
        o = o1 - lam_ref[0, 0] * o2
        o = o * lax.rsqrt(jnp.mean(o * o, axis=-1, keepdims=True) + SUBLN_EPS)
        o_ref[...] = (o * g_ref[...] * out_scale).astype(o_ref.dtype)


def diff_attention(q, k, v, lam, subln_g, out_scale, nq, nk):
    b = q.shape[0] // nq
    tq = _tile(nq, 256)
    tk = next(t for t in (768, 512, 256, 128) if nk % t == 0)
    nqb, nkb = nq // tq, nk // tk
    kern = functools.partial(_attn_kernel, tq=tq, tk=tk, out_scale=out_scale)
    return pl.pallas_call(
        kern,
        grid=(b, DA_HEADS, nqb, nkb),
        in_specs=[
            pl.BlockSpec(memory_space=pltpu.SMEM),
            pl.BlockSpec((tq, DA_V_DIM), lambda bi, h, qi, ki: (bi * nqb + qi, h)),
            pl.BlockSpec((tk, DA_V_DIM), lambda bi, h, qi, ki: (bi * nkb + ki, h)),
            pl.BlockSpec((tk, DA_V_DIM), lambda bi, h, qi, ki: (bi * nkb + ki, h)),
            pl.BlockSpec((1, DA_V_DIM), lambda bi, h, qi, ki: (0, 0)),
        ],
        out_specs=pl.BlockSpec((tq, DA_V_DIM), lambda bi, h, qi, ki: (bi * nqb + qi, h)),
        out_shape=jax.ShapeDtypeStruct((b * nq, DA_WIDTH), BF16),
        scratch_shapes=[
            pltpu.VMEM((2 * tq, DA_V_DIM), BF16),
            pltpu.VMEM((2 * tq, LANES), F32),
            pltpu.VMEM((2 * tq, LANES), F32),
            pltpu.VMEM((2 * tq, DA_V_DIM), F32),
        ],
        compiler_params=_cparams(("parallel", "parallel", "parallel", "arbitrary")),
        name="diff_attn",
    )(lam.reshape(1, 1).astype(F32), q, k, v, subln_g.reshape(1, DA_V_DIM).astype(F32))


def _dft_tables_real(seq):
    n2 = _dft_n2(seq)
    n1h = seq // n2
    n1 = 2 * n1h
    n = n1 * n2
    k1 = np.arange(n1, dtype=np.float64)[None, :, None]
    nn = (n2 * np.arange(n1h, dtype=np.float64)[None, None, :] + np.arange(n2, dtype=np.float64)[:, None, None])
    ang = 2.0 * np.pi * k1 * nn / n
    e_fwd = np.concatenate([np.cos(ang), -np.sin(ang)], axis=1)
    e_inv = np.transpose(e_fwd, (0, 2, 1))
    a2 = 2.0 * np.pi * np.outer(np.arange(n2), np.arange(n2)) / n2
    c, s = np.cos(a2), np.sin(a2)
    f_fwd = np.block([[c, s], [-s, c]])
    f_inv = np.block([[c, -s], [s, c]])
    return tuple(jnp.asarray(t, BF16) for t in (e_fwd, f_fwd, f_inv, e_inv))


def _spectrum_kernel(u_ref, ef_ref, ff_ref, o_ref, scr, *, n1, n1h, n2, kc):
    kk = pl.program_id(2)

    @pl.when(kk == 0)
    def _():
        def stage1(j, c):
            x = u_ref[pl.ds(j, n1h, stride=n2), :].astype(BF16)
            scr[pl.ds(j, 2 * n1, stride=n2), :] = jnp.dot(ef_ref[j], x, preferred_element_type=F32)
            return c

        lax.fori_loop(0, n2, stage1, 0, unroll=DFT_UNROLL)

    def stage2(t, c):
        k1 = kk * kc + t
        re = scr[pl.ds(pl.multiple_of(k1 * n2, n2), n2), :]
        im = scr[pl.ds(pl.multiple_of((n1 + k1) * n2, n2), n2), :]
        a = jnp.concatenate([re, im], axis=0).astype(BF16)
        o_ref[t] = jnp.dot(ff_ref[...], a, preferred_element_type=F32)
        return c

    lax.fori_loop(0, kc, stage2, 0, unroll=DFT_UNROLL)


def dft_spectrum(h, tables):
    s, seq, ch = h.shape
    n2 = _dft_n2(seq)
    n1h = seq // n2
    n1 = 2 * n1h
    e_fwd, f_fwd, _, _ = tables
    kc = min(n1, 16)
    kern = functools.partial(_spectrum_kernel, n1=n1, n1h=n1h, n2=n2, kc=kc)
    return pl.pallas_call(
        kern,
        grid=(s, ch // LANES, n1 // kc),
        in_specs=[
            pl.BlockSpec((None, seq, LANES), lambda i, c, k: (i, 0, c)),
            _const_spec(e_fwd.shape),
            _const_spec(f_fwd.shape),
        ],
        out_specs=pl.BlockSpec((None, kc, 2 * n2, LANES), lambda i, c, k: (i, k, 0, c)),
        out_shape=jax.ShapeDtypeStruct((s, n1, 2 * n2, ch), F32),
        scratch_shapes=[pltpu.VMEM((2 * n1 * n2, LANES), F32)],
        compiler_params=_cparams(("parallel", "parallel", "arbitrary")),
        name="dft_spectrum",
    )(h, e_fwd, f_fwd)


def _longconv_kernel(u_ref, g_ref, h_ref, bias_ref, ef_ref, ff_ref, fi_ref, ei_ref, o_ref, scr, *, n1, n1h, n2):
    def stage1(j, c):
        x = u_ref[pl.ds(j, n1h, stride=n2), :].astype(BF16)
        scr[pl.ds(j, 2 * n1, stride=n2), :] = jnp.dot(ef_ref[j], x, preferred_element_type=F32)
        return c

    lax.fori_loop(0, n2, stage1, 0, unroll=DFT_UNROLL)

    def stage2(k1, c):
        r0 = pl.multiple_of(k1 * n2, n2)
        i0 = pl.multiple_of((n1 + k1) * n2, n2)
        a = jnp.concatenate([scr[pl.ds(r0, n2), :], scr[pl.ds(i0, n2), :]], axis=0).astype(BF16)
        y = jnp.dot(ff_ref[...], a, preferred_element_type=F32)
        hk = h_ref[k1].astype(F32)
        yr, yi = y[:n2], y[n2:]
        hr, hi = hk[:n2], hk[n2:]
        z = jnp.concatenate([yr * hr - yi * hi, yr * hi + yi * hr], axis=0).astype(BF16)
        bk = jnp.dot(fi_ref[...], z, preferred_element_type=F32)
        scr[pl.ds(r0, n2), :] = bk[:n2]
        scr[pl.ds(i0, n2), :] = bk[n2:]
        return c

    lax.fori_loop(0, n1, stage2, 0, unroll=DFT_UNROLL)

    def stage3(j, c):
        bmat = scr[pl.ds(j, 2 * n1, stride=n2), :].astype(BF16)
        y = jnp.dot(ei_ref[j], bmat, preferred_element_type=F32)
        u = u_ref[pl.ds(j, n1h, stride=n2), :]
        g = g_ref[pl.ds(j, n1h, stride=n2), :]
        o_ref[pl.ds(j, n1h, stride=n2), :] = g * (y + u * bias_ref[...])
        return c

    lax.fori_loop(0, n2, stage3, 0, unroll=DFT_UNROLL)


def long_conv_gated(u, u_blk, g, g_blk, spec, bias, tables):
    b, seq, _ = u.shape
    ch = bias.shape[0]
    n2 = _dft_n2(seq)
    n1h = seq // n2
    n1 = 2 * n1h
    e_fwd, f_fwd, f_inv, e_inv = tables
    kern = functools.partial(_longconv_kernel, n1=n1, n1h=n1h, n2=n2)
    one = pl.Buffered(1)
    return pl.pallas_call(
        kern,
        grid=(ch // LANES, b),
        in_specs=[
            pl.BlockSpec((None, seq, LANES), lambda c, i: (i, 0, u_blk + c), pipeline_mode=one),
            pl.BlockSpec((None, seq, LANES), lambda c, i: (i, 0, g_blk + c), pipeline_mode=one),
            pl.BlockSpec((n1, 2 * n2, LANES), lambda c, i: (0, 0, c), pipeline_mode=one),
            pl.BlockSpec((1, LANES), lambda c, i: (0, c)),
            _const_spec(e_fwd.shape),
            _const_spec(f_fwd.shape),
            _const_spec(f_inv.shape),
            _const_spec(e_inv.shape),
        ],
        out_specs=pl.BlockSpec((None, seq, LANES), lambda c, i: (i, 0, c)),
        out_shape=jax.ShapeDtypeStruct((b, seq, ch), F32),
        scratch_shapes=[pltpu.VMEM((2 * n1 * n2, LANES), F32)],
        compiler_params=_cparams(("parallel", "parallel")),
        name="long_conv",
    )(u, g, spec, bias.reshape(1, ch).astype(F32), e_fwd, f_fwd, f_inv, e_inv)


def _dft_tables_complex(seq):
    n2 = _dft_n2(seq)
    n1 = seq // n2
    k1 = np.arange(n1, dtype=np.float64)[None, :, None]
    nn = (n2 * np.arange(n1, dtype=np.float64)[None, None, :] + np.arange(n2, dtype=np.float64)[:, None, None])
    ang = 2.0 * np.pi * k1 * nn / seq
    c, s = np.cos(ang), np.sin(ang)
    e_fwd = np.concatenate([np.concatenate([c, s], axis=2), np.concatenate([-s, c], axis=2)], axis=1)
    a2 = 2.0 * np.pi * np.outer(np.arange(n2), np.arange(n2)) / n2
    f_re = np.concatenate([np.cos(a2), np.sin(a2)], axis=1)
    return jnp.asarray(e_fwd, BF16), jnp.asarray(f_re, BF16)


def _seqdft_kernel(vr_ref, vi_ref, ef_ref, fr_ref, o_ref, scr, *, n1, n2, scale):
    def stage1(j, c):
        x = jnp.concatenate([vr_ref[pl.ds(j, n1, stride=n2), :], vi_ref[pl.ds(j, n1, stride=n2), :]], axis=0)
        scr[pl.ds(j, 2 * n1, stride=n2), :] = jnp.dot(ef_ref[j], x.astype(BF16), preferred_element_type=F32)
        return c

    lax.fori_loop(0, n2, stage1, 0, unroll=DFT_UNROLL)

    def stage2(k1, c):
        r0 = pl.multiple_of(k1 * n2, n2)
        i0 = pl.multiple_of((n1 + k1) * n2, n2)
        a = jnp.concatenate([scr[pl.ds(r0, n2), :], scr[pl.ds(i0, n2), :]], axis=0).astype(BF16)
        o_ref[pl.ds(k1, n2, stride=n1), :] = jnp.dot(fr_ref[...], a, preferred_element_type=F32) * scale
        return c

    lax.fori_loop(0, n1, stage2, 0, unroll=DFT_UNROLL)


def seq_dft_real(v, ch, scale, tables):
    b, seq, _ = v.shape
    n2 = _dft_n2(seq)
    n1 = seq // n2
    e_fwd, f_re = tables
    nblk = ch // LANES
    kern = functools.partial(_seqdft_kernel, n1=n1, n2=n2, scale=scale)
    return pl.pallas_call(
        kern,
        grid=(nblk, b),
        in_specs=[
            pl.BlockSpec((None, seq, LANES), lambda c, i: (i, 0, c)),
            pl.BlockSpec((None, seq, LANES), lambda c, i: (i, 0, nblk + c)),
            _const_spec(e_fwd.shape),
            _const_spec(f_re.shape),
        ],
        out_specs=pl.BlockSpec((None, seq, LANES), lambda c, i: (i, 0, c)),
        out_shape=jax.ShapeDtypeStruct((b, seq, ch), F32),
        scratch_shapes=[pltpu.VMEM((2 * n1 * n2, LANES), F32)],
        compiler_params=_cparams(("parallel", "parallel")),
        name="seq_dft",
    )(v, v, e_fwd, f_re)


def fourier_channel_matrix():
    a = 2.0 * np.pi * np.outer(np.arange(FN_GROUP_DIM), np.arange(FN_GROUP_DIM)) / FN_GROUP_DIM
    eye = np.eye(FN_GROUPS)
    return jnp.asarray(np.concatenate([np.kron(eye, np.cos(a)), -np.kron(eye, np.sin(a))], axis=1), BF16)


def _shortconv_kernel(u_ref, w_ref, b_ref, o_ref):
    u = u_ref[...]
    n = u.shape[0]
    row = lax.broadcasted_iota(jnp.int32, u.shape, 0)
    prev = jnp.where(row == 0, 0.0, pltpu.roll(u, 1, axis=0))
    nxt = jnp.where(row == n - 1, 0.0, pltpu.roll(u, n - 1, axis=0))
    o_ref[...] = prev * w_ref[0:1, :] + u * w_ref[1:2, :] + nxt * w_ref[2:3, :] + b_ref[...]


def short_conv(u, w, bias):
    b, seq, ch = u.shape
    return pl.pallas_call(
        _shortconv_kernel,
        grid=(b, ch // LANES),
        in_specs=[
            pl.BlockSpec((None, seq, LANES), lambda i, c: (i, 0, c)),
            pl.BlockSpec((HY_SHORT, LANES), lambda i, c: (0, c)),
            pl.BlockSpec((1, LANES), lambda i, c: (0, c)),
        ],
        out_specs=pl.BlockSpec((None, seq, LANES), lambda i, c: (i, 0, c)),
        out_shape=jax.ShapeDtypeStruct((b, seq, ch), F32),
        compiler_params=_cparams(("parallel", "parallel")),
        name="short_conv",
    )(u, w.astype(F32), bias.reshape(1, ch).astype(F32))


def _filter_kernel(emb_ref, w1_ref, b1_ref, f1_ref, w2_ref, b2_ref, f2_ref, w3_ref, b3_ref, dec_ref, o_ref):
    z = jnp.dot(emb_ref[...].astype(BF16), w1_ref[...], preferred_element_type=F32) + b1_ref[...]
    z = jnp.sin(f1_ref[...] * z)
    z = jnp.dot(z.astype(BF16), w2_ref[...], preferred_element_type=F32) + b2_ref[...]
    z = jnp.sin(f2_ref[...] * z)
    h = jnp.dot(z.astype(BF16), w3_ref[...], preferred_element_type=F32) + b3_ref[...]
    o_ref[...] = h * dec_ref[...]


def hyena_filters(seq, hy_w1, hy_b1, hy_freq1, hy_w2, hy_b2, hy_freq2, hy_w3, hy_b3):
    t = jnp.linspace(0.0, 1.0, seq, dtype=F32)[:, None]
    ang = (2.0 * math.pi / seq) * jnp.arange(seq, dtype=F32)[:, None]
    bands = jnp.linspace(1e-4, HY_EMB_BANDS - 1, HY_EMB_BANDS, dtype=F32)[None, :]
    emb = jnp.concatenate([t, jnp.cos(bands * ang), -jnp.sin(bands * ang)], axis=-1)
    kdim = emb.shape[1]
    kpad = LANES - kdim
    emb = jnp.pad(emb, ((0, 0), (0, kpad)))
    w1 = jnp.pad(hy_w1, ((0, kpad), (0, 0))).astype(BF16)
    deltas = jnp.abs(jnp.linspace(math.log(HY_DECAY_TARGET) / HY_SLOW_DECAY,
                                  math.log(HY_DECAY_TARGET) / HY_FAST_DECAY, HY_WIDTH, dtype=F32))
    decay = jnp.tile(jnp.exp(-t * deltas), (1, 2 * HY_ORDER))
    fo = hy_w1.shape[1]
    nout = hy_w3.shape[1]
    tl = _tile(seq, 1024)
    row = lambda a: a.reshape(1, -1).astype(F32)
    full = lambda shape: pl.BlockSpec(shape, lambda i: (0, 0))
    return pl.pallas_call(
        _filter_kernel,
        grid=(seq // tl,),
        in_specs=[
            pl.BlockSpec((tl, LANES), lambda i: (i, 0)),
            full((LANES, fo)), full((1, fo)), full((1, fo)),
            full((fo, fo)), full((1, fo)), full((1, fo)),
            full((fo, nout)), full((1, nout)),
            pl.BlockSpec((tl, nout), lambda i: (i, 0)),
        ],
        out_specs=pl.BlockSpec((tl, nout), lambda i: (i, 0)),
        out_shape=jax.ShapeDtypeStruct((seq, nout), F32),
        compiler_params=_cparams(("parallel",)),
        name="hyena_filter",
    )(emb, w1, row(hy_b1), row(hy_freq1), hy_w2.astype(BF16), row(hy_b2), row(hy_freq2),
      hy_w3.astype(BF16), row(hy_b3), decay)


def hyena_spectra(seq, filt, tables):
    n2 = _dft_n2(seq)
    h = filt.reshape(seq, HY_ORDER, 2, HY_WIDTH)
    h_fwd, h_bwd = h[:, :, 0], h[:, :, 1]
    h_bwd = h_bwd.at[0].set(0.0)
    norm = jnp.sum(jnp.abs(h_fwd), axis=0) + jnp.sum(jnp.abs(h_bwd), axis=0)
    stacked = jnp.concatenate([h_fwd, h_bwd], axis=1).transpose(1, 0, 2)
    sp = dft_spectrum(stacked, tables)
    sf, sb = sp[:HY_ORDER], sp[HY_ORDER:]
    re = sf[:, :, :n2] + sb[:, :, :n2]
    im = sf[:, :, n2:] - sb[:, :, n2:]
    scale = (1.0 / (2 * seq)) / norm
    return (jnp.concatenate([re, im], axis=2) * scale[:, None, None, :]).astype(BF16)


def _merge_kernel(x_ref, yf_ref, yh_ref, ya_ref, g_ref, wf_ref, wh_ref, wa_ref, wo_ref,
                  gate_ref, ng_ref, sh_ref, sc_ref, xo_ref, ho_ref):
    d = x_ref.shape[-1]
    g = g_ref[0].astype(F32)
    yf = jnp.dot(yf_ref[0].astype(BF16), wf_ref[...], preferred_element_type=F32)
    yh = jnp.dot(yh_ref[0].astype(BF16), wh_ref[...], preferred_element_type=F32)
    ya = jnp.dot(ya_ref[0], wa_ref[...], preferred_element_type=F32)
    mix = g[:, 0:d] * yf + g[:, d:2 * d] * yh + g[:, 2 * d:3 * d] * ya
    x = x_ref[0] + gate_ref[0] * jnp.dot(mix.astype(BF16), wo_ref[...], preferred_element_type=F32)
    xo_ref[0] = x
    y = x * lax.rsqrt(jnp.mean(x * x, axis=-1, keepdims=True) + EPS) * ng_ref[...]
    ho_ref[0] = (y * (1.0 + sc_ref[0]) + sh_ref[0]).astype(ho_ref.dtype)


def merge_branches(x, yf, yh, ya, g, w_f, w_h, w_a, w_o, gate, norm_g, shift, scale):
    b, l, d = x.shape
    tl = _tile(l, 512)
    rows = lambda w: pl.BlockSpec((1, tl, w), lambda i, j: (i, j, 0))
    full = lambda a: pl.BlockSpec(a.shape, lambda i, j: (0, 0))
    per_b = pl.BlockSpec((1, 1, d), lambda i, j: (i, 0, 0))
    wf, wh, wa, wo = (w.astype(BF16) for w in (w_f, w_h, w_a, w_o))
    return pl.pallas_call(
        _merge_kernel,
        grid=(b, l // tl),
        in_specs=[rows(d), rows(yf.shape[-1]), rows(yh.shape[-1]), rows(ya.shape[-1]), rows(3 * d),
                  full(wf), full(wh), full(wa), full(wo),
                  per_b, pl.BlockSpec((1, d), lambda i, j: (0, 0)), per_b, per_b],
        out_specs=[rows(d), rows(d)],
        out_shape=[jax.ShapeDtypeStruct((b, l, d), F32), jax.ShapeDtypeStruct((b, l, d), BF16)],
        compiler_params=_cparams(("parallel", "parallel")),
        name="merge",
    )(x, yf, yh, ya, g, wf, wh, wa, wo, gate.reshape(b, 1, d), norm_g.reshape(1, d).astype(F32),
      shift.reshape(b, 1, d), scale.reshape(b, 1, d))


GLU_GROUP = 2 * LANES


def _glu_group_permutation():
    p = np.zeros((GLU_GROUP, GLU_GROUP), np.float32)
    j = np.arange(LANES)
    p[2 * j, j] = 1.0
    p[2 * j + 1, LANES + j] = 1.0
    return jnp.asarray(p, BF16)


def _regroup_kernel(w_ref, p_ref, o_ref):
    for q in range(w_ref.shape[1] // GLU_GROUP):
        cols = slice(q * GLU_GROUP, (q + 1) * GLU_GROUP)
        o_ref[:, cols] = jnp.dot(w_ref[:, cols].astype(BF16), p_ref[...],
                                 preferred_element_type=F32).astype(o_ref.dtype)


def regroup_glu_columns(w):
    r, n = w.shape
    tr = _tile(r, 512)
    return pl.pallas_call(
        _regroup_kernel,
        grid=(r // tr,),
        in_specs=[pl.BlockSpec((tr, n), lambda i: (i, 0)),
                  pl.BlockSpec((GLU_GROUP, GLU_GROUP), lambda i: (0, 0))],
        out_specs=pl.BlockSpec((tr, n), lambda i: (i, 0)),
        out_shape=jax.ShapeDtypeStruct((r, n), BF16),
        compiler_params=_cparams(("parallel",)),
        name="regroup_glu",
    )(w, _glu_group_permutation())


def _moe_kernel(be_ref, act_ref, rows_ref, w1_ref, b1_ref, w2_ref, b2_ref, o_ref):
    i = pl.program_id(0)

    @pl.when(act_ref[i] > 0)
    def _():
        u = jnp.dot(rows_ref[...], w1_ref[0], preferred_element_type=F32) + b1_ref[0]
        parts = []
        for q in range(u.shape[1] // GLU_GROUP):
            xg = jnp.minimum(u[:, q * GLU_GROUP:q * GLU_GROUP + LANES], SWIGLU_LIMIT)
            xl = jnp.clip(u[:, q * GLU_GROUP + LANES:(q + 1) * GLU_GROUP], -SWIGLU_LIMIT, SWIGLU_LIMIT)
            parts.append((xg * jax.nn.sigmoid(SWIGLU_ALPHA * xg) * (xl + 1.0)).astype(BF16))
        a = jnp.concatenate(parts, axis=1)
        y = jnp.dot(a, w2_ref[0], preferred_element_type=F32) + b2_ref[0]
        o_ref[...] = y.astype(o_ref.dtype)

    @pl.when(act_ref[i] == 0)
    def _():
        o_ref[...] = jnp.zeros(o_ref.shape, o_ref.dtype)


def moe_experts(rows, blk_exp, blk_act, w1, b1, w2, b2):
    r, d = rows.shape
    de = w2.shape[1]
    nblk = r // MOE_BLOCK
    grid_spec = pltpu.PrefetchScalarGridSpec(
        num_scalar_prefetch=2,
        grid=(nblk,),
        in_specs=[
            pl.BlockSpec((MOE_BLOCK, d), lambda i, be, act: (i, 0)),
            pl.BlockSpec((1, d, 2 * de), lambda i, be, act: (be[i], 0, 0)),
            pl.BlockSpec((1, 1, 2 * de), lambda i, be, act: (be[i], 0, 0)),
            pl.BlockSpec((1, de, d), lambda i, be, act: (be[i], 0, 0)),
            pl.BlockSpec((1, 1, d), lambda i, be, act: (be[i], 0, 0)),
        ],
        out_specs=pl.BlockSpec((MOE_BLOCK, d), lambda i, be, act: (i, 0)),
    )
    return pl.pallas_call(
        _moe_kernel,
        grid_spec=grid_spec,
        out_shape=jax.ShapeDtypeStruct((r, d), BF16),
        compiler_params=_cparams(("arbitrary",)),
        name="moe_experts",
    )(blk_exp, blk_act, rows, w1, b1, w2, b2)


def _combine_kernel(y_ref, g_ref, x_ref, m_ref, o_ref):
    g = g_ref[...]
    acc = g[:, 0:1] * y_ref[0].astype(F32)
    for j in range(1, TOP_K):
        acc = acc + g[:, j:j + 1] * y_ref[j].astype(F32)
    o_ref[...] = x_ref[...] + m_ref[0] * acc


def moe_combine(y_sel, gate, resid, mod_blocks, tm):
    k, t, d = y_sel.shape
    return pl.pallas_call(
        _combine_kernel,
        grid=(t // tm,),
        in_specs=[pl.BlockSpec((k, tm, d), lambda i: (0, i, 0)),
                  pl.BlockSpec((tm, k), lambda i: (i, 0)),
                  pl.BlockSpec((tm, d), lambda i: (i, 0)),
                  pl.BlockSpec((1, 1, d), lambda i: (i, 0, 0))],
        out_specs=pl.BlockSpec((tm, d), lambda i: (i, 0)),
        out_shape=jax.ShapeDtypeStruct((t, d), F32),
        compiler_params=_cparams(("parallel",)),
        name="moe_combine",
    )(y_sel, gate, resid, mod_blocks)


def moe_ffn(h, resid, mod_blocks, tm, p):
    t_tok, d = h.shape
    wr = jnp.pad(p["w_router"], ((0, 0), (0, LANES - N_EXPERTS))).astype(BF16)
    br = jnp.pad(p["b_router"], (0, LANES - N_EXPERTS)).reshape(1, LANES).astype(F32)
    logits = matmul(h, wr, epi="bias", extra=(br,),
                    extra_specs=[pl.BlockSpec((1, LANES), lambda i, j: (0, 0))], name="mm_router")[:, :N_EXPERTS]
    top_v, top_i = lax.top_k(logits, TOP_K)
    gate = jax.nn.softmax(top_v, axis=-1)
    n_assign = t_tok * TOP_K
    flat_e = top_i.reshape(-1)
    experts = jnp.arange(N_EXPERTS, dtype=flat_e.dtype)[None, :]
    onehot = (flat_e[:, None] == experts).astype(jnp.int32)
    csum = jnp.cumsum(onehot, axis=0)
    counts = csum[-1]
    padded = (counts + MOE_BLOCK - 1) // MOE_BLOCK * MOE_BLOCK
    pad_end = jnp.cumsum(padded)
    pad_start = pad_end - padded
    dest = jnp.sum(onehot * (csum - 1 + pad_start[None, :]), axis=1)
    n_blocks = -(-n_assign // MOE_BLOCK) + N_EXPERTS
    n_rows = n_blocks * MOE_BLOCK
    row_tok = jnp.zeros((n_rows,), jnp.int32).at[dest].set(jnp.arange(n_assign, dtype=jnp.int32) // TOP_K)
    blk_start = jnp.arange(n_blocks, dtype=jnp.int32) * MOE_BLOCK
    blk_exp = jnp.minimum(jnp.sum((blk_start[:, None] >= pad_end[None, :]).astype(jnp.int32), axis=1),
                          N_EXPERTS - 1)
    blk_act = (blk_start < pad_end[-1]).astype(jnp.int32)
    y_rows = moe_experts(h[row_tok], blk_exp, blk_act, p["w1"], p["b1"], p["w2"], p["b2"])
    y_sel = y_rows[dest.reshape(t_tok, TOP_K).T]
    return moe_combine(y_sel, gate, resid, mod_blocks, tm)


OFF_F = 0
OFF_HY = OFF_F + FN_WIDTH
OFF_Q = OFF_HY + (HY_ORDER + 1) * HY_WIDTH
OFF_K = OFF_Q + COL_QK
OFF_V = OFF_K + COL_QK
OFF_G = OFF_V + DA_WIDTH


def _token_mixer(x, h, mod_gate, norm2_g, mod_shift2, mod_scale2, p, lam, lam_init, rope, kv_extra, q_only_self):
    b, s, d = x.shape
    h2d = h.reshape(b * s, d)
    w = p["w_in"]
    cos, sin = rope

    w_fv = matmul(w[:, OFF_F:OFF_HY], fourier_channel_matrix(), out_dtype=BF16, name="mm_wfold")
    v_f = matmul(h2d, w_fv, name="mm_fproj").reshape(b, s, 2 * FN_WIDTH)
    y_f = seq_dft_real(v_f, FN_WIDTH, 1.0 / math.sqrt(s * FN_GROUP_DIM), _dft_tables_complex(s))

    z = matmul(h2d, w[:, OFF_HY:OFF_Q], name="mm_hproj").reshape(b, s, (HY_ORDER + 1) * HY_WIDTH)
    z = short_conv(z, p["hy_conv_w"], p["hy_conv_b"])
    tables = _dft_tables_real(s)
    filt = hyena_filters(s, p["hy_w1"], p["hy_b1"], p["hy_freq1"], p["hy_w2"], p["hy_b2"], p["hy_freq2"],
                         p["hy_w3"], p["hy_b3"])
    spec = hyena_spectra(s, filt, tables)
    cb = HY_WIDTH // LANES
    y_h = long_conv_gated(z, 0, z, cb, spec[0], p["hy_bias"][0], tables)
    y_h = long_conv_gated(y_h, 0, z, 2 * cb, spec[1], p["hy_bias"][1], tables)

    k_gain = p["k_norm_g"]
    k = qk_project(h2d, w[:, OFF_K:OFF_V], k_gain, cos, sin, s)
    v = matmul(h2d, w[:, OFF_V:OFF_G], out_dtype=BF16, name="mm_vproj")
    k3, v3 = k.reshape(b, s, COL_QK), v.reshape(b, s, DA_WIDTH)
    if q_only_self is None:
        y_a = None
    else:
        q = qk_project(h2d, w[:, OFF_Q:OFF_K], p["q_norm_g"] * (DA_QK_DIM ** -0.5 * math.log2(math.e)), cos, sin, s)
        if kv_extra is not None:
            k_all = jnp.concatenate([k3, kv_extra[0]], axis=1)
            v_all = jnp.concatenate([v3, kv_extra[1]], axis=1)
        else:
            k_all, v_all = k3, v3
        nk = k_all.shape[1]
        y_a = diff_attention(q, k_all.reshape(b * nk, COL_QK), v_all.reshape(b * nk, DA_WIDTH), lam,
                             p["subln_g"], 1.0 - lam_init, s, nk).reshape(b, s, DA_WIDTH)

    g = matmul(h2d, w[:, OFF_G:], out_dtype=BF16, epi="sigmoid", name="mm_gates").reshape(b, s, N_BRANCHES * d)
    x_new, h2 = merge_branches(x, y_f, y_h, y_a, g, p["w_f"], p["w_h"], p["w_a"], p["w_o"],
                               mod_gate, norm2_g, mod_shift2, mod_scale2)
    return x_new, h2, (k3, v3)


def _context_kv(h, p, rope):
    b, s, d = h.shape
    h2d = h.reshape(b * s, d)
    w = p["w_in"]
    k = qk_project(h2d, w[:, OFF_K:OFF_V], p["k_norm_g"], rope[0], rope[1], s)
    v = matmul(h2d, w[:, OFF_V:OFF_G], out_dtype=BF16, name="mm_vproj_ctx")
    return k.reshape(b, s, COL_QK), v.reshape(b, s, DA_WIDTH)


def _layer(l, x, xc, c, c_ctx, p, ctx_out):
    b, n_lat, d = x.shape
    n_ctx = xc.shape[1]
    lam_init = 0.8 - 0.6 * math.exp(-0.3 * l)
    lam = (jnp.exp(jnp.sum(p["lam_q"][0] * p["lam_k"][0]).astype(F32))
           - jnp.exp(jnp.sum(p["lam_q"][1] * p["lam_k"][1]).astype(F32)) + lam_init)

    cond = jnp.concatenate([c, c_ctx[None, :], jnp.zeros((16 - b - 1, d), F32)], axis=0)
    mod_all = matmul(jax.nn.silu(cond).astype(BF16), p["w_mod"].astype(BF16), epi="bias",
                     extra=(p["b_mod"].reshape(1, 6 * d).astype(F32),),
                     extra_specs=[pl.BlockSpec((1, 1024), lambda i, j: (0, j))], name="mm_mod")
    mod = [mod_all[:b, i * d:(i + 1) * d] for i in range(6)]
    mod_c = [jnp.broadcast_to(mod_all[b, i * d:(i + 1) * d], (b, d)) for i in range(6)]

    pw = dict(p)
    pw["w_in"] = p["w_in"].astype(BF16)

    no_rope = (jnp.ones((n_ctx, LANES), F32), jnp.zeros((n_ctx, LANES), F32))
    hc = normmod(xc, p["norm1_g"], mod_c[0], mod_c[1])
    h = normmod(x, p["norm1_g"], mod[0], mod[1])
    if ctx_out:
        xc_new, h2c, kv_c = _token_mixer(xc, hc, mod_c[2], p["norm2_g"], mod_c[3], mod_c[4], pw, lam, lam_init,
                                         no_rope, None, True)
    else:
        kv_c = _context_kv(hc, pw, no_rope)
    x_new, h2, _ = _token_mixer(x, h, mod[2], p["norm2_g"], mod[3], mod[4], pw, lam, lam_init,
                                rope_tables(n_lat), kv_c, True)

    n_exp, _, two_f = p["w_e1"].shape
    b1 = p["b_e1"].reshape(n_exp, two_f // GLU_GROUP, LANES, 2).transpose(0, 1, 3, 2).reshape(n_exp, 1, two_f)
    pe = {
        "w_router": p["w_router"], "b_router": p["b_router"],
        "w1": regroup_glu_columns(p["w_e1"].reshape(n_exp * d, two_f)).reshape(n_exp, d, two_f), "b1": b1,
        "w2": p["w_e2"].astype(BF16), "b2": p["b_e2"][:, None, :],
    }
    tm = 512
    lat_mod = jnp.repeat(mod[5], n_lat // tm, axis=0)
    if ctx_out:
        h_all = jnp.concatenate([h2c.reshape(b * n_ctx, d), h2.reshape(b * n_lat, d)], axis=0)
        resid = jnp.concatenate([xc_new.reshape(b * n_ctx, d), x_new.reshape(b * n_lat, d)], axis=0)
        mod_blocks = jnp.concatenate([jnp.tile(mod_c[5][:1], (b * n_ctx // tm, 1)), lat_mod], axis=0)
        out = moe_ffn(h_all, resid, mod_blocks[:, None, :], tm, pe)
        xc = out[:b * n_ctx].reshape(b, n_ctx, d)
        x = out[b * n_ctx:].reshape(b, n_lat, d)
    else:
        x = moe_ffn(h2.reshape(b * n_lat, d), x_new.reshape(b * n_lat, d), lat_mod[:, None, :], tm, pe)
        x = x.reshape(b, n_lat, d)
    return x, xc


_PARAM_NAMES = ("w_mod", "b_mod", "norm1_g", "norm2_g", "w_in", "hy_conv_w", "hy_conv_b", "hy_w1", "hy_b1",
                "hy_freq1", "hy_w2", "hy_b2", "hy_freq2", "hy_w3", "hy_b3", "hy_bias", "q_norm_g", "k_norm_g",
                "lam_q", "lam_k", "subln_g", "w_f", "w_h", "w_a", "w_o", "w_router", "b_router",
                "w_e1", "b_e1", "w_e2", "b_e2")


def kernel(x, c, ctx, c_ctx, w_mod, b_mod, norm1_g, norm2_g, w_in, hy_conv_w, hy_conv_b, hy_w1, hy_b1, hy_freq1,
           hy_w2, hy_b2, hy_freq2, hy_w3, hy_b3, hy_bias, q_norm_g, k_norm_g, lam_q, lam_k, subln_g, w_f, w_h,
           w_a, w_o, w_router, b_router, w_e1, b_e1, w_e2, b_e2):
    stacked = (w_mod, b_mod, norm1_g, norm2_g, w_in, hy_conv_w, hy_conv_b, hy_w1, hy_b1, hy_freq1, hy_w2, hy_b2,
               hy_freq2, hy_w3, hy_b3, hy_bias, q_norm_g, k_norm_g, lam_q, lam_k, subln_g, w_f, w_h, w_a, w_o,
               w_router, b_router, w_e1, b_e1, w_e2, b_e2)
    depth = w_mod.shape[0]
    xc = ctx
    for l in range(depth):
        p = {name: arr[l] for name, arr in zip(_PARAM_NAMES, stacked)}
        x, xc = _layer(l, x, xc, c, c_ctx, p, l < depth - 1)
    return x
```

```python
import functools
import math

import numpy as np
import jax
import jax.numpy as jnp
from jax import lax
from jax.experimental import pallas as pl
from jax.experimental.pallas import tpu as pltpu

F32 = jnp.float32
BF16 = jnp.bfloat16

LANES = 128
VMEM_LIMIT = 56 * 1024 * 1024

GRID_W = 64
EPS = 1e-6
SUBLN_EPS = 1e-5
FN_GROUPS = 4
FN_GROUP_DIM = 64
FN_WIDTH = FN_GROUPS * FN_GROUP_DIM
HY_WIDTH = 256
HY_ORDER = 2
HY_SHORT = 3
HY_EMB_BANDS = 16
HY_DECAY_TARGET = 1e-2
HY_FAST_DECAY = 0.3
HY_SLOW_DECAY = 1.5
DA_HEADS = 4
DA_QK_DIM = 64
DA_V_DIM = 2 * DA_QK_DIM
DA_WIDTH = DA_HEADS * DA_V_DIM
ROPE_BASE = 10000.0
N_BRANCHES = 3
COL_QK = DA_HEADS * 2 * DA_QK_DIM
N_EXPERTS = 32
TOP_K = 4
SWIGLU_ALPHA = 1.702
SWIGLU_LIMIT = 7.0
MOE_BLOCK = 256
DFT_MIN_N1 = 16
DFT_UNROLL = 4


def _dft_n2(seq):
    return min(LANES, seq // DFT_MIN_N1)


def _cparams(sem):
    return pltpu.CompilerParams(dimension_semantics=sem, vmem_limit_bytes=VMEM_LIMIT)


def _tile(n, pref):
    if n <= pref:
        return n
    for t in range(pref, 7, -1):
        if n % t == 0 and t % 8 == 0:
            return t
    return n


def _const_spec(shape):
    nd = len(shape)
    return pl.BlockSpec(shape, lambda *_: (0,) * nd, pipeline_mode=pl.Buffered(1))


def _normmod_kernel(x_ref, g_ref, sh_ref, sc_ref, o_ref):
    x = x_ref[0]
    y = x * lax.rsqrt(jnp.mean(x * x, axis=-1, keepdims=True) + EPS)
    y = y * g_ref[...]
    o_ref[0] = (y * (1.0 + sc_ref[0]) + sh_ref[0]).astype(o_ref.dtype)


def normmod(x, g, shift, scale):
    b, l, d = x.shape
    tl = _tile(l, 1024)
    return pl.pallas_call(
        _normmod_kernel,
        grid=(b, l // tl),
        in_specs=[
            pl.BlockSpec((1, tl, d), lambda i, j: (i, j, 0)),
            pl.BlockSpec((1, d), lambda i, j: (0, 0)),
            pl.BlockSpec((1, 1, d), lambda i, j: (i, 0, 0)),
            pl.BlockSpec((1, 1, d), lambda i, j: (i, 0, 0)),
        ],
        out_specs=pl.BlockSpec((1, tl, d), lambda i, j: (i, j, 0)),
        out_shape=jax.ShapeDtypeStruct((b, l, d), BF16),
        compiler_params=_cparams(("parallel", "parallel")),
        name="normmod",
    )(x, g.reshape(1, d), shift.reshape(b, 1, d), scale.reshape(b, 1, d))


def _mm_kernel(a_ref, w_ref, *rest, epi):
    acc = jnp.dot(a_ref[...], w_ref[...], preferred_element_type=F32)
    if epi == "plain":
        (o_ref,) = rest
    elif epi == "bias":
        b_ref, o_ref = rest
        acc = acc + b_ref[...]
    elif epi == "sigmoid":
        (o_ref,) = rest
        acc = jax.nn.sigmoid(acc)
    elif epi == "qk":
        gm_ref, gain_ref, cos_ref, sin_ref, o_ref = rest
        ms = jnp.dot((acc * acc).astype(BF16), gm_ref[...], preferred_element_type=F32)
        y = acc * lax.rsqrt(ms + EPS) * gain_ref[...]
        n = y.shape[1]
        reps = n // LANES
        lane = lax.broadcasted_iota(jnp.int32, y.shape, 1)
        is_a = (lane % (DA_QK_DIM // 2)) < (DA_QK_DIM // 4)
        half = DA_QK_DIM // 4
        swapped = jnp.where(is_a, pltpu.roll(y, n - half, axis=1), pltpu.roll(y, half, axis=1))
        acc = y * jnp.tile(cos_ref[...], (1, reps)) + swapped * jnp.tile(sin_ref[...], (1, reps))
    else:
        raise ValueError(epi)
    o_ref[...] = acc.astype(o_ref.dtype)


def matmul(a, w, *, out_dtype=F32, epi="plain", extra=(), extra_specs=(), tm=512, tn=1024, name="mm"):
    m, k = a.shape
    k2, n = w.shape
    assert k == k2
    tm = _tile(m, tm)
    tn = _tile(n, tn)
    return pl.pallas_call(
        functools.partial(_mm_kernel, epi=epi),
        grid=(m // tm, n // tn),
        in_specs=[
            pl.BlockSpec((tm, k), lambda i, j: (i, 0)),
            pl.BlockSpec((k, tn), lambda i, j: (0, j)),
            *extra_specs,
        ],
        out_specs=pl.BlockSpec((tm, tn), lambda i, j: (i, j)),
        out_shape=jax.ShapeDtypeStruct((m, n), out_dtype),
        compiler_params=_cparams(("parallel", "parallel")),
        name=name,
    )(a, w, *extra)


def _group_mean_matrix(n, group):
    idx = np.arange(n)
    return jnp.asarray((idx[:, None] // group == idx[None, :] // group).astype(np.float32) / group, BF16)


def rope_tables(n_lat):
    rows = n_lat // GRID_W
    row = np.repeat(np.arange(rows), GRID_W).astype(np.float64)
    col = np.tile(np.arange(GRID_W), rows).astype(np.float64)
    n_freq = DA_QK_DIM // 4
    inv = ROPE_BASE ** (-np.arange(n_freq, dtype=np.float64) / n_freq)
    ang_r = row[:, None] * inv
    ang_c = col[:, None] * inv
    cos = np.concatenate([np.cos(ang_r), np.cos(ang_r), np.cos(ang_c), np.cos(ang_c)], axis=1)
    sin = np.concatenate([-np.sin(ang_r), np.sin(ang_r), -np.sin(ang_c), np.sin(ang_c)], axis=1)
    cos = np.tile(cos, (1, LANES // DA_QK_DIM))
    sin = np.tile(sin, (1, LANES // DA_QK_DIM))
    return jnp.asarray(cos, F32), jnp.asarray(sin, F32)


def qk_project(h, w, gain, cos, sin, seq):
    m, _ = h.shape
    n = w.shape[1]
    tm = _tile(seq, 512)
    nblk = seq // tm
    gm = _group_mean_matrix(n, DA_QK_DIM)
    gain_t = jnp.tile(gain.astype(F32), n // DA_QK_DIM).reshape(1, n)
    extra_specs = [
        pl.BlockSpec((n, n), lambda i, j: (0, 0)),
        pl.BlockSpec((1, n), lambda i, j: (0, 0)),
        pl.BlockSpec((tm, LANES), lambda i, j: (i % nblk, 0)),
        pl.BlockSpec((tm, LANES), lambda i, j: (i % nblk, 0)),
    ]
    return matmul(h, w, out_dtype=BF16, epi="qk", extra=(gm, gain_t, cos, sin), extra_specs=extra_specs,
                  tm=tm, tn=n, name="mm_qk")


ATTN_TQ = 512
ATTN_TK = 768


def _attn_kernel(lam_ref, q_ref, k_ref, v_ref, g_ref, o_ref, qs_ref, s_ref, m_ref, acc_ref, *, tq, tk, nkc, out_scale):
    q = q_ref[...]
    lane = lax.broadcasted_iota(jnp.int32, q.shape, 1)
    zero = jnp.zeros_like(q)
    qs_ref[0:tq, :] = jnp.where(lane < DA_QK_DIM, q, zero)
    qs_ref[tq:2 * tq, :] = jnp.where(lane >= DA_QK_DIM, q, zero)
    m_ref[...] = jnp.full(m_ref.shape, -jnp.inf, F32)
    acc_ref[...] = jnp.zeros(acc_ref.shape, F32)

    def scores(j, slot):
        kj = k_ref[pl.ds(pl.multiple_of(j * tk, tk), tk), :]
        s_ref[slot] = lax.dot_general(qs_ref[...], kj, (((1,), (1,)), ((), ())), preferred_element_type=F32)

    def update(j, slot):
        s = s_ref[slot]
        vj = v_ref[pl.ds(pl.multiple_of(j * tk, tk), tk), :]
        m_prev = m_ref[...]
        m_next = jnp.maximum(m_prev, jnp.max(s, axis=1, keepdims=True))
        p = jnp.exp2(s - jnp.tile(m_next, (1, tk // LANES)))
        alpha = jnp.exp2(m_prev - m_next)
        m_ref[...] = m_next
        acc_ref[...] = acc_ref[...] * jnp.tile(alpha, (1, 2)) + jnp.dot(p.astype(BF16), vj,
                                                                          preferred_element_type=F32)

    scores(0, 0)
    pairs = (nkc - 1) // 2

    def body(i, c):
        j = 2 * i
        scores(j + 1, 1)
        update(j, 0)
        scores(j + 2, 0)
        update(j + 1, 1)
        return c

    lax.fori_loop(0, pairs, body, 0)
    if nkc % 2 == 0:
        scores(nkc - 1, 1)
        update(nkc - 2, 0)
        update(nkc - 1, 1)
    else:
        update(nkc - 1, 0)

    o1 = acc_ref[0:tq, 0:DA_V_DIM] / acc_ref[0:tq, DA_V_DIM:]
    o2 = acc_ref[tq:2 * tq, 0:DA_V_DIM] / acc_ref[tq:2 * tq, DA_V_DIM:]
    o = o1 - lam_ref[0, 0] * o2
    o = o * lax.rsqrt(jnp.mean(o * o, axis=-1, keepdims=True) + SUBLN_EPS)
    o_ref[...] = (o * g_ref[...] * out_scale).astype(o_ref.dtype)


def diff_attention(q, k, v, lam, subln_g, out_scale, nq, nk):
    b = q.shape[0] // nq
    tq = _tile(nq, ATTN_TQ)
    tk = next(t for t in (ATTN_TK, 256, 128) if nk % t == 0)
    nqb, nkc = nq // tq, nk // tk
    v_ext = jnp.concatenate([v.reshape(b * nk, DA_HEADS, DA_V_DIM),
                             jnp.ones((b * nk, DA_HEADS, DA_V_DIM), v.dtype)], axis=2)
    v_ext = v_ext.reshape(b * nk, 2 * DA_WIDTH)
    kern = functools.partial(_attn_kernel, tq=tq, tk=tk, nkc=nkc, out_scale=out_scale)
    return pl.pallas_call(
        kern,
        grid=(b, DA_HEADS, nqb),
        in_specs=[
            pl.BlockSpec(memory_space=pltpu.SMEM),
            pl.BlockSpec((tq, DA_V_DIM), lambda bi, h, qi: (bi * nqb + qi, h)),
            pl.BlockSpec((nk, DA_V_DIM), lambda bi, h, qi: (bi, h)),
            pl.BlockSpec((nk, 2 * DA_V_DIM), lambda bi, h, qi: (bi, h)),
            pl.BlockSpec((1, DA_V_DIM), lambda bi, h, qi: (0, 0)),
        ],
        out_specs=pl.BlockSpec((tq, DA_V_DIM), lambda bi, h, qi: (bi * nqb + qi, h)),
        out_shape=jax.ShapeDtypeStruct((b * nq, DA_WIDTH), BF16),
        scratch_shapes=[
            pltpu.VMEM((2 * tq, DA_V_DIM), BF16),
            pltpu.VMEM((2, 2 * tq, tk), F32),
            pltpu.VMEM((2 * tq, LANES), F32),
            pltpu.VMEM((2 * tq, 2 * DA_V_DIM), F32),
        ],
        compiler_params=_cparams(("parallel", "parallel", "parallel")),
        name="diff_attn",
    )(lam.reshape(1, 1).astype(F32), q, k, v_ext, subln_g.reshape(1, DA_V_DIM).astype(F32))


def _dft_tables_real(seq):
    n2 = _dft_n2(seq)
    n1h = seq // n2
    n1 = 2 * n1h
    n = n1 * n2
    k1 = np.arange(n1, dtype=np.float64)[None, :, None]
    nn = (n2 * np.arange(n1h, dtype=np.float64)[None, None, :] + np.arange(n2, dtype=np.float64)[:, None, None])
    ang = 2.0 * np.pi * k1 * nn / n
    e_fwd = np.concatenate([np.cos(ang), -np.sin(ang)], axis=1)
    e_inv = np.transpose(e_fwd, (0, 2, 1))
    a2 = 2.0 * np.pi * np.outer(np.arange(n2), np.arange(n2)) / n2
    c, s = np.cos(a2), np.sin(a2)
    f_fwd = np.block([[c, s], [-s, c]])
    f_inv = np.block([[c, -s], [s, c]])
    return tuple(jnp.asarray(t, BF16) for t in (e_fwd, f_fwd, f_inv, e_inv))


def _spectrum_kernel(u_ref, ef_ref, ff_ref, o_ref, scr, *, n1, n1h, n2, kc):
    kk = pl.program_id(2)

    @pl.when(kk == 0)
    def _():
        def stage1(j, c):
            x = u_ref[pl.ds(j, n1h, stride=n2), :].astype(BF16)
            scr[pl.ds(j, 2 * n1, stride=n2), :] = jnp.dot(ef_ref[j], x, preferred_element_type=F32)
            return c

        lax.fori_loop(0, n2, stage1, 0, unroll=DFT_UNROLL)

    def stage2(t, c):
        k1 = kk * kc + t
        re = scr[pl.ds(pl.multiple_of(k1 * n2, n2), n2), :]
        im = scr[pl.ds(pl.multiple_of((n1 + k1) * n2, n2), n2), :]
        a = jnp.concatenate([re, im], axis=0).astype(BF16)
        o_ref[t] = jnp.dot(ff_ref[...], a, preferred_element_type=F32)
        return c

    lax.fori_loop(0, kc, stage2, 0, unroll=DFT_UNROLL)


def dft_spectrum(h, tables):
    s, seq, ch = h.shape
    n2 = _dft_n2(seq)
    n1h = seq // n2
    n1 = 2 * n1h
    e_fwd, f_fwd, _, _ = tables
    kc = min(n1, 16)
    kern = functools.partial(_spectrum_kernel, n1=n1, n1h=n1h, n2=n2, kc=kc)
    return pl.pallas_call(
        kern,
        grid=(s, ch // LANES, n1 // kc),
        in_specs=[
            pl.BlockSpec((None, seq, LANES), lambda i, c, k: (i, 0, c)),
            _const_spec(e_fwd.shape),
            _const_spec(f_fwd.shape),
        ],
        out_specs=pl.BlockSpec((None, kc, 2 * n2, LANES), lambda i, c, k: (i, k, 0, c)),
        out_shape=jax.ShapeDtypeStruct((s, n1, 2 * n2, ch), F32),
        scratch_shapes=[pltpu.VMEM((2 * n1 * n2, LANES), F32)],
        compiler_params=_cparams(("parallel", "parallel", "arbitrary")),
        name="dft_spectrum",
    )(h, e_fwd, f_fwd)


def _longconv_kernel(u_ref, g_ref, h_ref, bias_ref, ef_ref, ff_ref, fi_ref, ei_ref, o_ref, scr, *, n1, n1h, n2):
    def stage1(j, c):
        x = u_ref[pl.ds(j, n1h, stride=n2), :].astype(BF16)
        scr[pl.ds(j, 2 * n1, stride=n2), :] = jnp.dot(ef_ref[j], x, preferred_element_type=F32)
        return c

    lax.fori_loop(0, n2, stage1, 0, unroll=DFT_UNROLL)

    def stage2(k1, c):
        r0 = pl.multiple_of(k1 * n2, n2)
        i0 = pl.multiple_of((n1 + k1) * n2, n2)
        a = jnp.concatenate([scr[pl.ds(r0, n2), :], scr[pl.ds(i0, n2), :]], axis=0).astype(BF16)
        y = jnp.dot(ff_ref[...], a, preferred_element_type=F32)
        hk = h_ref[k1].astype(F32)
        yr, yi = y[:n2], y[n2:]
        hr, hi = hk[:n2], hk[n2:]
        z = jnp.concatenate([yr * hr - yi * hi, yr * hi + yi * hr], axis=0).astype(BF16)
        bk = jnp.dot(fi_ref[...], z, preferred_element_type=F32)
        scr[pl.ds(r0, n2), :] = bk[:n2]
        scr[pl.ds(i0, n2), :] = bk[n2:]
        return c

    lax.fori_loop(0, n1, stage2, 0, unroll=DFT_UNROLL)

    def stage3(j, c):
        bmat = scr[pl.ds(j, 2 * n1, stride=n2), :].astype(BF16)
        y = jnp.dot(ei_ref[j], bmat, preferred_element_type=F32)
        u = u_ref[pl.ds(j, n1h, stride=n2), :]
        g = g_ref[pl.ds(j, n1h, stride=n2), :]
        o_ref[pl.ds(j, n1h, stride=n2), :] = g * (y + u * bias_ref[...])
        return c

    lax.fori_loop(0, n2, stage3, 0, unroll=DFT_UNROLL)


def long_conv_gated(u, u_blk, g, g_blk, spec, bias, tables):
    b, seq, _ = u.shape
    ch = bias.shape[0]
    n2 = _dft_n2(seq)
    n1h = seq // n2
    n1 = 2 * n1h
    e_fwd, f_fwd, f_inv, e_inv = tables
    kern = functools.partial(_longconv_kernel, n1=n1, n1h=n1h, n2=n2)
    one = pl.Buffered(1)
    return pl.pallas_call(
        kern,
        grid=(ch // LANES, b),
        in_specs=[
            pl.BlockSpec((None, seq, LANES), lambda c, i: (i, 0, u_blk + c), pipeline_mode=one),
            pl.BlockSpec((None, seq, LANES), lambda c, i: (i, 0, g_blk + c), pipeline_mode=one),
            pl.BlockSpec((n1, 2 * n2, LANES), lambda c, i: (0, 0, c), pipeline_mode=one),
            pl.BlockSpec((1, LANES), lambda c, i: (0, c)),
            _const_spec(e_fwd.shape),
            _const_spec(f_fwd.shape),
            _const_spec(f_inv.shape),
            _const_spec(e_inv.shape),
        ],
        out_specs=pl.BlockSpec((None, seq, LANES), lambda c, i: (i, 0, c)),
        out_shape=jax.ShapeDtypeStruct((b, seq, ch), F32),
        scratch_shapes=[pltpu.VMEM((2 * n1 * n2, LANES), F32)],
        compiler_params=_cparams(("parallel", "parallel")),
        name="long_conv",
    )(u, g, spec, bias.reshape(1, ch).astype(F32), e_fwd, f_fwd, f_inv, e_inv)


def _dft_tables_complex(seq):
    n2 = _dft_n2(seq)
    n1 = seq // n2
    k1 = np.arange(n1, dtype=np.float64)[None, :, None]
    nn = (n2 * np.arange(n1, dtype=np.float64)[None, None, :] + np.arange(n2, dtype=np.float64)[:, None, None])
    ang = 2.0 * np.pi * k1 * nn / seq
    c, s = np.cos(ang), np.sin(ang)
    e_fwd = np.concatenate([np.concatenate([c, s], axis=2), np.concatenate([-s, c], axis=2)], axis=1)
    a2 = 2.0 * np.pi * np.outer(np.arange(n2), np.arange(n2)) / n2
    f_re = np.concatenate([np.cos(a2), np.sin(a2)], axis=1)
    return jnp.asarray(e_fwd, BF16), jnp.asarray(f_re, BF16)


def _seqdft_kernel(vr_ref, vi_ref, ef_ref, fr_ref, o_ref, scr, *, n1, n2, scale):
    def stage1(j, c):
        x = jnp.concatenate([vr_ref[pl.ds(j, n1, stride=n2), :], vi_ref[pl.ds(j, n1, stride=n2), :]], axis=0)
        scr[pl.ds(j, 2 * n1, stride=n2), :] = jnp.dot(ef_ref[j], x.astype(BF16), preferred_element_type=F32)
        return c

    lax.fori_loop(0, n2, stage1, 0, unroll=DFT_UNROLL)

    def stage2(k1, c):
        r0 = pl.multiple_of(k1 * n2, n2)
        i0 = pl.multiple_of((n1 + k1) * n2, n2)
        a = jnp.concatenate([scr[pl.ds(r0, n2), :], scr[pl.ds(i0, n2), :]], axis=0).astype(BF16)
        o_ref[pl.ds(k1, n2, stride=n1), :] = jnp.dot(fr_ref[...], a, preferred_element_type=F32) * scale
        return c

    lax.fori_loop(0, n1, stage2, 0, unroll=DFT_UNROLL)


def seq_dft_real(v, ch, scale, tables):
    b, seq, _ = v.shape
    n2 = _dft_n2(seq)
    n1 = seq // n2
    e_fwd, f_re = tables
    nblk = ch // LANES
    kern = functools.partial(_seqdft_kernel, n1=n1, n2=n2, scale=scale)
    return pl.pallas_call(
        kern,
        grid=(nblk, b),
        in_specs=[
            pl.BlockSpec((None, seq, LANES), lambda c, i: (i, 0, c)),
            pl.BlockSpec((None, seq, LANES), lambda c, i: (i, 0, nblk + c)),
            _const_spec(e_fwd.shape),
            _const_spec(f_re.shape),
        ],
        out_specs=pl.BlockSpec((None, seq, LANES), lambda c, i: (i, 0, c)),
        out_shape=jax.ShapeDtypeStruct((b, seq, ch), F32),
        scratch_shapes=[pltpu.VMEM((2 * n1 * n2, LANES), F32)],
        compiler_params=_cparams(("parallel", "parallel")),
        name="seq_dft",
    )(v, v, e_fwd, f_re)


def fourier_channel_matrix():
    a = 2.0 * np.pi * np.outer(np.arange(FN_GROUP_DIM), np.arange(FN_GROUP_DIM)) / FN_GROUP_DIM
    eye = np.eye(FN_GROUPS)
    return jnp.asarray(np.concatenate([np.kron(eye, np.cos(a)), -np.kron(eye, np.sin(a))], axis=1), BF16)


def _shortconv_kernel(u_ref, w_ref, b_ref, o_ref):
    u = u_ref[...]
    n = u.shape[0]
    row = lax.broadcasted_iota(jnp.int32, u.shape, 0)
    prev = jnp.where(row == 0, 0.0, pltpu.roll(u, 1, axis=0))
    nxt = jnp.where(row == n - 1, 0.0, pltpu.roll(u, n - 1, axis=0))
    o_ref[...] = prev * w_ref[0:1, :] + u * w_ref[1:2, :] + nxt * w_ref[2:3, :] + b_ref[...]


def short_conv(u, w, bias):
    b, seq, ch = u.shape
    return pl.pallas_call(
        _shortconv_kernel,
        grid=(b, ch // LANES),
        in_specs=[
            pl.BlockSpec((None, seq, LANES), lambda i, c: (i, 0, c)),
            pl.BlockSpec((HY_SHORT, LANES), lambda i, c: (0, c)),
            pl.BlockSpec((1, LANES), lambda i, c: (0, c)),
        ],
        out_specs=pl.BlockSpec((None, seq, LANES), lambda i, c: (i, 0, c)),
        out_shape=jax.ShapeDtypeStruct((b, seq, ch), F32),
        compiler_params=_cparams(("parallel", "parallel")),
        name="short_conv",
    )(u, w.astype(F32), bias.reshape(1, ch).astype(F32))


def _filter_kernel(emb_ref, w1_ref, b1_ref, f1_ref, w2_ref, b2_ref, f2_ref, w3_ref, b3_ref, dec_ref, o_ref):
    z = jnp.dot(emb_ref[...].astype(BF16), w1_ref[...], preferred_element_type=F32) + b1_ref[...]
    z = jnp.sin(f1_ref[...] * z)
    z = jnp.dot(z.astype(BF16), w2_ref[...], preferred_element_type=F32) + b2_ref[...]
    z = jnp.sin(f2_ref[...] * z)
    h = jnp.dot(z.astype(BF16), w3_ref[...], preferred_element_type=F32) + b3_ref[...]
    o_ref[...] = h * dec_ref[...]


def hyena_filters(seq, hy_w1, hy_b1, hy_freq1, hy_w2, hy_b2, hy_freq2, hy_w3, hy_b3):
    t = jnp.linspace(0.0, 1.0, seq, dtype=F32)[:, None]
    ang = (2.0 * math.pi / seq) * jnp.arange(seq, dtype=F32)[:, None]
    bands = jnp.linspace(1e-4, HY_EMB_BANDS - 1, HY_EMB_BANDS, dtype=F32)[None, :]
    emb = jnp.concatenate([t, jnp.cos(bands * ang), -jnp.sin(bands * ang)], axis=-1)
    kdim = emb.shape[1]
    kpad = LANES - kdim
    emb = jnp.pad(emb, ((0, 0), (0, kpad)))
    w1 = jnp.pad(hy_w1, ((0, kpad), (0, 0))).astype(BF16)
    deltas = jnp.abs(jnp.linspace(math.log(HY_DECAY_TARGET) / HY_SLOW_DECAY,
                                  math.log(HY_DECAY_TARGET) / HY_FAST_DECAY, HY_WIDTH, dtype=F32))
    decay = jnp.tile(jnp.exp(-t * deltas), (1, 2 * HY_ORDER))
    fo = hy_w1.shape[1]
    nout = hy_w3.shape[1]
    tl = _tile(seq, 1024)
    row = lambda a: a.reshape(1, -1).astype(F32)
    full = lambda shape: pl.BlockSpec(shape, lambda i: (0, 0))
    return pl.pallas_call(
        _filter_kernel,
        grid=(seq // tl,),
        in_specs=[
            pl.BlockSpec((tl, LANES), lambda i: (i, 0)),
            full((LANES, fo)), full((1, fo)), full((1, fo)),
            full((fo, fo)), full((1, fo)), full((1, fo)),
            full((fo, nout)), full((1, nout)),
            pl.BlockSpec((tl, nout), lambda i: (i, 0)),
        ],
        out_specs=pl.BlockSpec((tl, nout), lambda i: (i, 0)),
        out_shape=jax.ShapeDtypeStruct((seq, nout), F32),
        compiler_params=_cparams(("parallel",)),
        name="hyena_filter",
    )(emb, w1, row(hy_b1), row(hy_freq1), hy_w2.astype(BF16), row(hy_b2), row(hy_freq2),
      hy_w3.astype(BF16), row(hy_b3), decay)


def hyena_spectra(seq, filt, tables):
    n2 = _dft_n2(seq)
    h = filt.reshape(seq, HY_ORDER, 2, HY_WIDTH)
    h_fwd, h_bwd = h[:, :, 0], h[:, :, 1]
    h_bwd = h_bwd.at[0].set(0.0)
    norm = jnp.sum(jnp.abs(h_fwd), axis=0) + jnp.sum(jnp.abs(h_bwd), axis=0)
    stacked = jnp.concatenate([h_fwd, h_bwd], axis=1).transpose(1, 0, 2)
    sp = dft_spectrum(stacked, tables)
    sf, sb = sp[:HY_ORDER], sp[HY_ORDER:]
    re = sf[:, :, :n2] + sb[:, :, :n2]
    im = sf[:, :, n2:] - sb[:, :, n2:]
    scale = (1.0 / (2 * seq)) / norm
    return (jnp.concatenate([re, im], axis=2) * scale[:, None, None, :]).astype(BF16)


def _merge_kernel(x_ref, yf_ref, yh_ref, ya_ref, g_ref, wf_ref, wh_ref, wa_ref, wo_ref,
                  gate_ref, ng_ref, sh_ref, sc_ref, xo_ref, ho_ref):
    d = x_ref.shape[-1]
    g = g_ref[0].astype(F32)
    yf = jnp.dot(yf_ref[0].astype(BF16), wf_ref[...], preferred_element_type=F32)
    yh = jnp.dot(yh_ref[0].astype(BF16), wh_ref[...], preferred_element_type=F32)
    ya = jnp.dot(ya_ref[0], wa_ref[...], preferred_element_type=F32)
    mix = g[:, 0:d] * yf + g[:, d:2 * d] * yh + g[:, 2 * d:3 * d] * ya
    x = x_ref[0] + gate_ref[0] * jnp.dot(mix.astype(BF16), wo_ref[...], preferred_element_type=F32)
    xo_ref[0] = x
    y = x * lax.rsqrt(jnp.mean(x * x, axis=-1, keepdims=True) + EPS) * ng_ref[...]
    ho_ref[0] = (y * (1.0 + sc_ref[0]) + sh_ref[0]).astype(ho_ref.dtype)


def merge_branches(x, yf, yh, ya, g, w_f, w_h, w_a, w_o, gate, norm_g, shift, scale):
    b, l, d = x.shape
    tl = _tile(l, 512)
    rows = lambda w: pl.BlockSpec((1, tl, w), lambda i, j: (i, j, 0))
    full = lambda a: pl.BlockSpec(a.shape, lambda i, j: (0, 0))
    per_b = pl.BlockSpec((1, 1, d), lambda i, j: (i, 0, 0))
    wf, wh, wa, wo = (w.astype(BF16) for w in (w_f, w_h, w_a, w_o))
    return pl.pallas_call(
        _merge_kernel,
        grid=(b, l // tl),
        in_specs=[rows(d), rows(yf.shape[-1]), rows(yh.shape[-1]), rows(ya.shape[-1]), rows(3 * d),
                  full(wf), full(wh), full(wa), full(wo),
                  per_b, pl.BlockSpec((1, d), lambda i, j: (0, 0)), per_b, per_b],
        out_specs=[rows(d), rows(d)],
        out_shape=[jax.ShapeDtypeStruct((b, l, d), F32), jax.ShapeDtypeStruct((b, l, d), BF16)],
        compiler_params=_cparams(("parallel", "parallel")),
        name="merge",
    )(x, yf, yh, ya, g, wf, wh, wa, wo, gate.reshape(b, 1, d), norm_g.reshape(1, d).astype(F32),
      shift.reshape(b, 1, d), scale.reshape(b, 1, d))


GLU_GROUP = 2 * LANES


def _glu_group_permutation():
    p = np.zeros((GLU_GROUP, GLU_GROUP), np.float32)
    j = np.arange(LANES)
    p[2 * j, j] = 1.0
    p[2 * j + 1, LANES + j] = 1.0
    return jnp.asarray(p, BF16)


def _regroup_kernel(w_ref, p_ref, o_ref):
    for q in range(w_ref.shape[1] // GLU_GROUP):
        cols = slice(q * GLU_GROUP, (q + 1) * GLU_GROUP)
        o_ref[:, cols] = jnp.dot(w_ref[:, cols].astype(BF16), p_ref[...],
                                 preferred_element_type=F32).astype(o_ref.dtype)


def regroup_glu_columns(w):
    r, n = w.shape
    tr = _tile(r, 512)
    return pl.pallas_call(
        _regroup_kernel,
        grid=(r // tr,),
        in_specs=[pl.BlockSpec((tr, n), lambda i: (i, 0)),
                  pl.BlockSpec((GLU_GROUP, GLU_GROUP), lambda i: (0, 0))],
        out_specs=pl.BlockSpec((tr, n), lambda i: (i, 0)),
        out_shape=jax.ShapeDtypeStruct((r, n), BF16),
        compiler_params=_cparams(("parallel",)),
        name="regroup_glu",
    )(w, _glu_group_permutation())


def _moe_kernel(be_ref, act_ref, rows_ref, w1_ref, b1_ref, w2_ref, b2_ref, o_ref):
    i = pl.program_id(0)

    @pl.when(act_ref[i] > 0)
    def _():
        u = jnp.dot(rows_ref[...], w1_ref[0], preferred_element_type=F32) + b1_ref[0]
        parts = []
        for q in range(u.shape[1] // GLU_GROUP):
            xg = jnp.minimum(u[:, q * GLU_GROUP:q * GLU_GROUP + LANES], SWIGLU_LIMIT)
            xl = jnp.clip(u[:, q * GLU_GROUP + LANES:(q + 1) * GLU_GROUP], -SWIGLU_LIMIT, SWIGLU_LIMIT)
            parts.append((xg * jax.nn.sigmoid(SWIGLU_ALPHA * xg) * (xl + 1.0)).astype(BF16))
        a = jnp.concatenate(parts, axis=1)
        y = jnp.dot(a, w2_ref[0], preferred_element_type=F32) + b2_ref[0]
        o_ref[...] = y.astype(o_ref.dtype)

    @pl.when(act_ref[i] == 0)
    def _():
        o_ref[...] = jnp.zeros(o_ref.shape, o_ref.dtype)


def moe_experts(rows, blk_exp, blk_act, w1, b1, w2, b2):
    r, d = rows.shape
    de = w2.shape[1]
    nblk = r // MOE_BLOCK
    grid_spec = pltpu.PrefetchScalarGridSpec(
        num_scalar_prefetch=2,
        grid=(nblk,),
        in_specs=[
            pl.BlockSpec((MOE_BLOCK, d), lambda i, be, act: (i, 0)),
            pl.BlockSpec((1, d, 2 * de), lambda i, be, act: (be[i], 0, 0)),
            pl.BlockSpec((1, 1, 2 * de), lambda i, be, act: (be[i], 0, 0)),
            pl.BlockSpec((1, de, d), lambda i, be, act: (be[i], 0, 0)),
            pl.BlockSpec((1, 1, d), lambda i, be, act: (be[i], 0, 0)),
        ],
        out_specs=pl.BlockSpec((MOE_BLOCK, d), lambda i, be, act: (i, 0)),
    )
    return pl.pallas_call(
        _moe_kernel,
        grid_spec=grid_spec,
        out_shape=jax.ShapeDtypeStruct((r, d), BF16),
        compiler_params=_cparams(("arbitrary",)),
        name="moe_experts",
    )(blk_exp, blk_act, rows, w1, b1, w2, b2)


def _combine_kernel(y_ref, g_ref, x_ref, m_ref, o_ref):
    g = g_ref[...]
    acc = g[:, 0:1] * y_ref[0].astype(F32)
    for j in range(1, TOP_K):
        acc = acc + g[:, j:j + 1] * y_ref[j].astype(F32)
    o_ref[...] = x_ref[...] + m_ref[0] * acc


def moe_combine(y_sel, gate, resid, mod_blocks, tm):
    k, t, d = y_sel.shape
    return pl.pallas_call(
        _combine_kernel,
        grid=(t // tm,),
        in_specs=[pl.BlockSpec((k, tm, d), lambda i: (0, i, 0)),
                  pl.BlockSpec((tm, k), lambda i: (i, 0)),
                  pl.BlockSpec((tm, d), lambda i: (i, 0)),
                  pl.BlockSpec((1, 1, d), lambda i: (i, 0, 0))],
        out_specs=pl.BlockSpec((tm, d), lambda i: (i, 0)),
        out_shape=jax.ShapeDtypeStruct((t, d), F32),
        compiler_params=_cparams(("parallel",)),
        name="moe_combine",
    )(y_sel, gate, resid, mod_blocks)


def moe_ffn(h, resid, mod_blocks, tm, p):
    t_tok, d = h.shape
    wr = jnp.pad(p["w_router"], ((0, 0), (0, LANES - N_EXPERTS))).astype(BF16)
    br = jnp.pad(p["b_router"], (0, LANES - N_EXPERTS)).reshape(1, LANES).astype(F32)
    logits = matmul(h, wr, epi="bias", extra=(br,),
                    extra_specs=[pl.BlockSpec((1, LANES), lambda i, j: (0, 0))], name="mm_router")[:, :N_EXPERTS]
    top_v, top_i = lax.top_k(logits, TOP_K)
    gate = jax.nn.softmax(top_v, axis=-1)
    n_assign = t_tok * TOP_K
    flat_e = top_i.reshape(-1)
    experts = jnp.arange(N_EXPERTS, dtype=flat_e.dtype)[None, :]
    onehot = (flat_e[:, None] == experts).astype(jnp.int32)
    csum = jnp.cumsum(onehot, axis=0)
    counts = csum[-1]
    padded = (counts + MOE_BLOCK - 1) // MOE_BLOCK * MOE_BLOCK
    pad_end = jnp.cumsum(padded)
    pad_start = pad_end - padded
    dest = jnp.sum(onehot * (csum - 1 + pad_start[None, :]), axis=1)
    n_blocks = -(-n_assign // MOE_BLOCK) + N_EXPERTS
    n_rows = n_blocks * MOE_BLOCK
    row_tok = jnp.zeros((n_rows,), jnp.int32).at[dest].set(jnp.arange(n_assign, dtype=jnp.int32) // TOP_K)
    blk_start = jnp.arange(n_blocks, dtype=jnp.int32) * MOE_BLOCK
    blk_exp = jnp.minimum(jnp.sum((blk_start[:, None] >= pad_end[None, :]).astype(jnp.int32), axis=1),
                          N_EXPERTS - 1)
    blk_act = (blk_start < pad_end[-1]).astype(jnp.int32)
    y_rows = moe_experts(h[row_tok], blk_exp, blk_act, p["w1"], p["b1"], p["w2"], p["b2"])
    y_sel = y_rows[dest.reshape(t_tok, TOP_K).T]
    return moe_combine(y_sel, gate, resid, mod_blocks, tm)


OFF_F = 0
OFF_HY = OFF_F + FN_WIDTH
OFF_Q = OFF_HY + (HY_ORDER + 1) * HY_WIDTH
OFF_K = OFF_Q + COL_QK
OFF_V = OFF_K + COL_QK
OFF_G = OFF_V + DA_WIDTH


def _token_mixer(x, h, mod_gate, norm2_g, mod_shift2, mod_scale2, p, lam, lam_init, rope, kv_extra, q_only_self):
    b, s, d = x.shape
    h2d = h.reshape(b * s, d)
    w = p["w_in"]
    cos, sin = rope

    w_fv = matmul(w[:, OFF_F:OFF_HY], fourier_channel_matrix(), out_dtype=BF16, name="mm_wfold")
    v_f = matmul(h2d, w_fv, name="mm_fproj").reshape(b, s, 2 * FN_WIDTH)
    y_f = seq_dft_real(v_f, FN_WIDTH, 1.0 / math.sqrt(s * FN_GROUP_DIM), _dft_tables_complex(s))

    z = matmul(h2d, w[:, OFF_HY:OFF_Q], name="mm_hproj").reshape(b, s, (HY_ORDER + 1) * HY_WIDTH)
    z = short_conv(z, p["hy_conv_w"], p["hy_conv_b"])
    tables = _dft_tables_real(s)
    filt = hyena_filters(s, p["hy_w1"], p["hy_b1"], p["hy_freq1"], p["hy_w2"], p["hy_b2"], p["hy_freq2"],
                         p["hy_w3"], p["hy_b3"])
    spec = hyena_spectra(s, filt, tables)
    cb = HY_WIDTH // LANES
    y_h = long_conv_gated(z, 0, z, cb, spec[0], p["hy_bias"][0], tables)
    y_h = long_conv_gated(y_h, 0, z, 2 * cb, spec[1], p["hy_bias"][1], tables)

    k_gain = p["k_norm_g"]
    k = qk_project(h2d, w[:, OFF_K:OFF_V], k_gain, cos, sin, s)
    v = matmul(h2d, w[:, OFF_V:OFF_G], out_dtype=BF16, name="mm_vproj")
    k3, v3 = k.reshape(b, s, COL_QK), v.reshape(b, s, DA_WIDTH)
    if q_only_self is None:
        y_a = None
    else:
        q = qk_project(h2d, w[:, OFF_Q:OFF_K], p["q_norm_g"] * (DA_QK_DIM ** -0.5 * math.log2(math.e)), cos, sin, s)
        if kv_extra is not None:
            k_all = jnp.concatenate([k3, kv_extra[0]], axis=1)
            v_all = jnp.concatenate([v3, kv_extra[1]], axis=1)
        else:
            k_all, v_all = k3, v3
        nk = k_all.shape[1]
        y_a = diff_attention(q, k_all.reshape(b * nk, COL_QK), v_all.reshape(b * nk, DA_WIDTH), lam,
                             p["subln_g"], 1.0 - lam_init, s, nk).reshape(b, s, DA_WIDTH)

    g = matmul(h2d, w[:, OFF_G:], out_dtype=BF16, epi="sigmoid", name="mm_gates").reshape(b, s, N_BRANCHES * d)
    x_new, h2 = merge_branches(x, y_f, y_h, y_a, g, p["w_f"], p["w_h"], p["w_a"], p["w_o"],
                               mod_gate, norm2_g, mod_shift2, mod_scale2)
    return x_new, h2, (k3, v3)


def _context_kv(h, p, rope):
    b, s, d = h.shape
    h2d = h.reshape(b * s, d)
    w = p["w_in"]
    k = qk_project(h2d, w[:, OFF_K:OFF_V], p["k_norm_g"], rope[0], rope[1], s)
    v = matmul(h2d, w[:, OFF_V:OFF_G], out_dtype=BF16, name="mm_vproj_ctx")
    return k.reshape(b, s, COL_QK), v.reshape(b, s, DA_WIDTH)


def _layer(l, x, xc, c, c_ctx, p, ctx_out):
    b, n_lat, d = x.shape
    n_ctx = xc.shape[1]
    lam_init = 0.8 - 0.6 * math.exp(-0.3 * l)
    lam = (jnp.exp(jnp.sum(p["lam_q"][0] * p["lam_k"][0]).astype(F32))
           - jnp.exp(jnp.sum(p["lam_q"][1] * p["lam_k"][1]).astype(F32)) + lam_init)

    cond = jnp.concatenate([c, c_ctx[None, :], jnp.zeros((16 - b - 1, d), F32)], axis=0)
    mod_all = matmul(jax.nn.silu(cond).astype(BF16), p["w_mod"].astype(BF16), epi="bias",
                     extra=(p["b_mod"].reshape(1, 6 * d).astype(F32),),
                     extra_specs=[pl.BlockSpec((1, 1024), lambda i, j: (0, j))], name="mm_mod")
    mod = [mod_all[:b, i * d:(i + 1) * d] for i in range(6)]
    mod_c = [jnp.broadcast_to(mod_all[b, i * d:(i + 1) * d], (b, d)) for i in range(6)]

    pw = dict(p)
    pw["w_in"] = p["w_in"].astype(BF16)

    no_rope = (jnp.ones((n_ctx, LANES), F32), jnp.zeros((n_ctx, LANES), F32))
    hc = normmod(xc, p["norm1_g"], mod_c[0], mod_c[1])
    h = normmod(x, p["norm1_g"], mod[0], mod[1])
    if ctx_out:
        xc_new, h2c, kv_c = _token_mixer(xc, hc, mod_c[2], p["norm2_g"], mod_c[3], mod_c[4], pw, lam, lam_init,
                                         no_rope, None, True)
    else:
        kv_c = _context_kv(hc, pw, no_rope)
    x_new, h2, _ = _token_mixer(x, h, mod[2], p["norm2_g"], mod[3], mod[4], pw, lam, lam_init,
                                rope_tables(n_lat), kv_c, True)

    n_exp, _, two_f = p["w_e1"].shape
    b1 = p["b_e1"].reshape(n_exp, two_f // GLU_GROUP, LANES, 2).transpose(0, 1, 3, 2).reshape(n_exp, 1, two_f)
    pe = {
        "w_router": p["w_router"], "b_router": p["b_router"],
        "w1": regroup_glu_columns(p["w_e1"].reshape(n_exp * d, two_f)).reshape(n_exp, d, two_f), "b1": b1,
        "w2": p["w_e2"].astype(BF16), "b2": p["b_e2"][:, None, :],
    }
    tm = 512
    lat_mod = jnp.repeat(mod[5], n_lat // tm, axis=0)
    if ctx_out:
        h_all = jnp.concatenate([h2c.reshape(b * n_ctx, d), h2.reshape(b * n_lat, d)], axis=0)
        resid = jnp.concatenate([xc_new.reshape(b * n_ctx, d), x_new.reshape(b * n_lat, d)], axis=0)
        mod_blocks = jnp.concatenate([jnp.tile(mod_c[5][:1], (b * n_ctx // tm, 1)), lat_mod], axis=0)
        out = moe_ffn(h_all, resid, mod_blocks[:, None, :], tm, pe)
        xc = out[:b * n_ctx].reshape(b, n_ctx, d)
        x = out[b * n_ctx:].reshape(b, n_lat, d)
    else:
        x = moe_ffn(h2.reshape(b * n_lat, d), x_new.reshape(b * n_lat, d), lat_mod[:, None, :], tm, pe)
        x = x.reshape(b, n_lat, d)
    return x, xc


_PARAM_NAMES = ("w_mod", "b_mod", "norm1_g", "norm2_g", "w_in", "hy_conv_w", "hy_conv_b", "hy_w1", "hy_b1",
                "hy_freq1", "hy_w2", "hy_b2", "hy_freq2", "hy_w3", "hy_b3", "hy_bias", "q_norm_g", "k_norm_g",
                "lam_q", "lam_k", "subln_g", "w_f", "w_h", "w_a", "w_o", "w_router", "b_router",
                "w_e1", "b_e1", "w_e2", "b_e2")


def kernel(x, c, ctx, c_ctx, w_mod, b_mod, norm1_g, norm2_g, w_in, hy_conv_w, hy_conv_b, hy_w1, hy_b1, hy_freq1,
           hy_w2, hy_b2, hy_freq2, hy_w3, hy_b3, hy_bias, q_norm_g, k_norm_g, lam_q, lam_k, subln_g, w_f, w_h,
           w_a, w_o, w_router, b_router, w_e1, b_e1, w_e2, b_e2):
    stacked = (w_mod, b_mod, norm1_g, norm2_g, w_in, hy_conv_w, hy_conv_b, hy_w1, hy_b1, hy_freq1, hy_w2, hy_b2,
               hy_freq2, hy_w3, hy_b3, hy_bias, q_norm_g, k_norm_g, lam_q, lam_k, subln_g, w_f, w_h, w_a, w_o,
               w_router, b_router, w_e1, b_e1, w_e2, b_e2)
    depth = w_mod.shape[0]
    xc = ctx
    for l in range(depth):
        p = {name: arr[l] for name, arr in zip(_PARAM_NAMES, stacked)}
        x, xc = _layer(l, x, xc, c, c_ctx, p, l < depth - 1)
    return x
```

```python
import functools
import math

import numpy as np
import jax
import jax.numpy as jnp
from jax import lax
from jax.experimental import pallas as pl
from jax.experimental.pallas import tpu as pltpu

F32 = jnp.float32
BF16 = jnp.bfloat16

LANES = 128
VMEM_LIMIT = 56 * 1024 * 1024

GRID_W = 64
EPS = 1e-6
SUBLN_EPS = 1e-5
FN_GROUPS = 4
FN_GROUP_DIM = 64
FN_WIDTH = FN_GROUPS * FN_GROUP_DIM
HY_WIDTH = 256
HY_ORDER = 2
HY_SHORT = 3
HY_EMB_BANDS = 16
HY_DECAY_TARGET = 1e-2
HY_FAST_DECAY = 0.3
HY_SLOW_DECAY = 1.5
DA_HEADS = 4
DA_QK_DIM = 64
DA_V_DIM = 2 * DA_QK_DIM
DA_WIDTH = DA_HEADS * DA_V_DIM
ROPE_BASE = 10000.0
N_BRANCHES = 3
COL_QK = DA_HEADS * 2 * DA_QK_DIM
N_EXPERTS = 32
TOP_K = 4
SWIGLU_ALPHA = 1.702
SWIGLU_LIMIT = 7.0
MOE_BLOCK = 256
DFT_MIN_N1 = 16
DFT_UNROLL = 4


def _dft_n2(seq):
    return min(LANES, seq // DFT_MIN_N1)


def _cparams(sem):
    return pltpu.CompilerParams(dimension_semantics=sem, vmem_limit_bytes=VMEM_LIMIT)


def _tile(n, pref):
    if n <= pref:
        return n
    for t in range(pref, 7, -1):
        if n % t == 0 and t % 8 == 0:
            return t
    return n


def _const_spec(shape):
    nd = len(shape)
    return pl.BlockSpec(shape, lambda *_: (0,) * nd, pipeline_mode=pl.Buffered(1))


def _normmod_kernel(x_ref, g_ref, sh_ref, sc_ref, o_ref):
    x = x_ref[0]
    y = x * lax.rsqrt(jnp.mean(x * x, axis=-1, keepdims=True) + EPS)
    y = y * g_ref[...]
    o_ref[0] = (y * (1.0 + sc_ref[0]) + sh_ref[0]).astype(o_ref.dtype)


def normmod(x, g, shift, scale):
    b, l, d = x.shape
    tl = _tile(l, 1024)
    return pl.pallas_call(
        _normmod_kernel,
        grid=(b, l // tl),
        in_specs=[
            pl.BlockSpec((1, tl, d), lambda i, j: (i, j, 0)),
            pl.BlockSpec((1, d), lambda i, j: (0, 0)),
            pl.BlockSpec((1, 1, d), lambda i, j: (i, 0, 0)),
            pl.BlockSpec((1, 1, d), lambda i, j: (i, 0, 0)),
        ],
        out_specs=pl.BlockSpec((1, tl, d), lambda i, j: (i, j, 0)),
        out_shape=jax.ShapeDtypeStruct((b, l, d), BF16),
        compiler_params=_cparams(("parallel", "parallel")),
        name="normmod",
    )(x, g.reshape(1, d), shift.reshape(b, 1, d), scale.reshape(b, 1, d))


def _mm_kernel(a_ref, w_ref, *rest, epi):
    acc = jnp.dot(a_ref[...], w_ref[...], preferred_element_type=F32)
    if epi == "plain":
        (o_ref,) = rest
    elif epi == "bias":
        b_ref, o_ref = rest
        acc = acc + b_ref[...]
    elif epi == "sigmoid":
        (o_ref,) = rest
        acc = jax.nn.sigmoid(acc)
    elif epi == "qk":
        gm_ref, gain_ref, cos_ref, sin_ref, o_ref = rest
        ms = jnp.dot((acc * acc).astype(BF16), gm_ref[...], preferred_element_type=F32)
        y = acc * lax.rsqrt(ms + EPS) * gain_ref[...]
        n = y.shape[1]
        reps = n // LANES
        lane = lax.broadcasted_iota(jnp.int32, y.shape, 1)
        is_a = (lane % (DA_QK_DIM // 2)) < (DA_QK_DIM // 4)
        half = DA_QK_DIM // 4
        swapped = jnp.where(is_a, pltpu.roll(y, n - half, axis=1), pltpu.roll(y, half, axis=1))
        acc = y * jnp.tile(cos_ref[...], (1, reps)) + swapped * jnp.tile(sin_ref[...], (1, reps))
    else:
        raise ValueError(epi)
    o_ref[...] = acc.astype(o_ref.dtype)


def matmul(a, w, *, out_dtype=F32, epi="plain", extra=(), extra_specs=(), tm=512, tn=1024, name="mm"):
    m, k = a.shape
    k2, n = w.shape
    assert k == k2
    tm = _tile(m, tm)
    tn = _tile(n, tn)
    return pl.pallas_call(
        functools.partial(_mm_kernel, epi=epi),
        grid=(m // tm, n // tn),
        in_specs=[
            pl.BlockSpec((tm, k), lambda i, j: (i, 0)),
            pl.BlockSpec((k, tn), lambda i, j: (0, j)),
            *extra_specs,
        ],
        out_specs=pl.BlockSpec((tm, tn), lambda i, j: (i, j)),
        out_shape=jax.ShapeDtypeStruct((m, n), out_dtype),
        compiler_params=_cparams(("parallel", "parallel")),
        name=name,
    )(a, w, *extra)


def _group_mean_matrix(n, group):
    idx = np.arange(n)
    return jnp.asarray((idx[:, None] // group == idx[None, :] // group).astype(np.float32) / group, BF16)


def rope_tables(n_lat):
    rows = n_lat // GRID_W
    row = np.repeat(np.arange(rows), GRID_W).astype(np.float64)
    col = np.tile(np.arange(GRID_W), rows).astype(np.float64)
    n_freq = DA_QK_DIM // 4
    inv = ROPE_BASE ** (-np.arange(n_freq, dtype=np.float64) / n_freq)
    ang_r = row[:, None] * inv
    ang_c = col[:, None] * inv
    cos = np.concatenate([np.cos(ang_r), np.cos(ang_r), np.cos(ang_c), np.cos(ang_c)], axis=1)
    sin = np.concatenate([-np.sin(ang_r), np.sin(ang_r), -np.sin(ang_c), np.sin(ang_c)], axis=1)
    cos = np.tile(cos, (1, LANES // DA_QK_DIM))
    sin = np.tile(sin, (1, LANES // DA_QK_DIM))
    return jnp.asarray(cos, F32), jnp.asarray(sin, F32)


def qk_project(h, w, gain, cos, sin, seq):
    m, _ = h.shape
    n = w.shape[1]
    tm = _tile(seq, 512)
    nblk = seq // tm
    gm = _group_mean_matrix(n, DA_QK_DIM)
    gain_t = jnp.tile(gain.astype(F32), n // DA_QK_DIM).reshape(1, n)
    extra_specs = [
        pl.BlockSpec((n, n), lambda i, j: (0, 0)),
        pl.BlockSpec((1, n), lambda i, j: (0, 0)),
        pl.BlockSpec((tm, LANES), lambda i, j: (i % nblk, 0)),
        pl.BlockSpec((tm, LANES), lambda i, j: (i % nblk, 0)),
    ]
    return matmul(h, w, out_dtype=BF16, epi="qk", extra=(gm, gain_t, cos, sin), extra_specs=extra_specs,
                  tm=tm, tn=n, name="mm_qk")


ATTN_TQ = 512
ATTN_TK = 768


def _attn_kernel(lam_ref, q_ref, k_ref, v_ref, g_ref, o_ref, qs_ref, s_ref, m_ref, acc_ref, *, tq, tk, nkc, out_scale):
    q = q_ref[...]
    lane = lax.broadcasted_iota(jnp.int32, q.shape, 1)
    zero = jnp.zeros_like(q)
    qs_ref[0:tq, :] = jnp.where(lane < DA_QK_DIM, q, zero)
    qs_ref[tq:2 * tq, :] = jnp.where(lane >= DA_QK_DIM, q, zero)
    m_ref[...] = jnp.full(m_ref.shape, -jnp.inf, F32)
    acc_ref[...] = jnp.zeros(acc_ref.shape, F32)

    def scores(j, slot):
        kj = k_ref[pl.ds(pl.multiple_of(j * tk, tk), tk), :]
        s_ref[slot] = lax.dot_general(qs_ref[...], kj, (((1,), (1,)), ((), ())), preferred_element_type=F32)

    def update(j, slot):
        s = s_ref[slot]
        vj = v_ref[pl.ds(pl.multiple_of(j * tk, tk), tk), :]
        m_prev = m_ref[...]
        m_next = jnp.maximum(m_prev, jnp.max(s, axis=1, keepdims=True))
        p = jnp.exp2(s - jnp.tile(m_next, (1, tk // LANES)))
        alpha = jnp.exp2(m_prev - m_next)
        m_ref[...] = m_next
        acc_ref[...] = acc_ref[...] * jnp.tile(alpha, (1, 2)) + jnp.dot(p.astype(BF16), vj,
                                                                          preferred_element_type=F32)

    scores(0, 0)
    pairs = (nkc - 1) // 2

    def body(i, c):
        j = 2 * i
        scores(j + 1, 1)
        update(j, 0)
        scores(j + 2, 0)
        update(j + 1, 1)
        return c

    lax.fori_loop(0, pairs, body, 0)
    if nkc % 2 == 0:
        scores(nkc - 1, 1)
        update(nkc - 2, 0)
        update(nkc - 1, 1)
    else:
        update(nkc - 1, 0)

    o1 = acc_ref[0:tq, 0:DA_V_DIM] / acc_ref[0:tq, DA_V_DIM:]
    o2 = acc_ref[tq:2 * tq, 0:DA_V_DIM] / acc_ref[tq:2 * tq, DA_V_DIM:]
    o = o1 - lam_ref[0, 0] * o2
    o = o * lax.rsqrt(jnp.mean(o * o, axis=-1, keepdims=True) + SUBLN_EPS)
    o_ref[...] = (o * g_ref[...] * out_scale).astype(o_ref.dtype)


def diff_attention(q, k, v, lam, subln_g, out_scale, nq, nk):
    b = q.shape[0] // nq
    tq = _tile(nq, ATTN_TQ)
    tk = next(t for t in (ATTN_TK, 256, 128) if nk % t == 0)
    nqb, nkc = nq // tq, nk // tk
    v_ext = jnp.concatenate([v.reshape(b * nk, DA_HEADS, DA_V_DIM),
                             jnp.ones((b * nk, DA_HEADS, DA_V_DIM), v.dtype)], axis=2)
    v_ext = v_ext.reshape(b * nk, 2 * DA_WIDTH)
    kern = functools.partial(_attn_kernel, tq=tq, tk=tk, nkc=nkc, out_scale=out_scale)
    return pl.pallas_call(
        kern,
        grid=(b, DA_HEADS, nqb),
        in_specs=[
            pl.BlockSpec(memory_space=pltpu.SMEM),
            pl.BlockSpec((tq, DA_V_DIM), lambda bi, h, qi: (bi * nqb + qi, h)),
            pl.BlockSpec((nk, DA_V_DIM), lambda bi, h, qi: (bi, h)),
            pl.BlockSpec((nk, 2 * DA_V_DIM), lambda bi, h, qi: (bi, h)),
            pl.BlockSpec((1, DA_V_DIM), lambda bi, h, qi: (0, 0)),
        ],
        out_specs=pl.BlockSpec((tq, DA_V_DIM), lambda bi, h, qi: (bi * nqb + qi, h)),
        out_shape=jax.ShapeDtypeStruct((b * nq, DA_WIDTH), BF16),
        scratch_shapes=[
            pltpu.VMEM((2 * tq, DA_V_DIM), BF16),
            pltpu.VMEM((2, 2 * tq, tk), F32),
            pltpu.VMEM((2 * tq, LANES), F32),
            pltpu.VMEM((2 * tq, 2 * DA_V_DIM), F32),
        ],
        compiler_params=_cparams(("parallel", "parallel", "parallel")),
        name="diff_attn",
    )(lam.reshape(1, 1).astype(F32), q, k, v_ext, subln_g.reshape(1, DA_V_DIM).astype(F32))


def _dft_tables_real(seq):
    n2 = _dft_n2(seq)
    n1h = seq // n2
    n1 = 2 * n1h
    n = n1 * n2
    k1 = np.arange(n1, dtype=np.float64)[None, :, None]
    nn = (n2 * np.arange(n1h, dtype=np.float64)[None, None, :] + np.arange(n2, dtype=np.float64)[:, None, None])
    ang = 2.0 * np.pi * k1 * nn / n
    e_fwd = np.concatenate([np.cos(ang), -np.sin(ang)], axis=1)
    e_inv = np.transpose(e_fwd, (0, 2, 1))
    a2 = 2.0 * np.pi * np.outer(np.arange(n2), np.arange(n2)) / n2
    c, s = np.cos(a2), np.sin(a2)
    f_fwd = np.block([[c, s], [-s, c]])
    f_inv = np.block([[c, -s], [s, c]])
    return tuple(jnp.asarray(t, BF16) for t in (e_fwd, f_fwd, f_inv, e_inv))


def _spectrum_kernel(u_ref, ef_ref, ff_ref, o_ref, scr, *, n1, n1h, n2, kc):
    kk = pl.program_id(2)

    @pl.when(kk == 0)
    def _():
        def stage1(j, c):
            x = u_ref[pl.ds(j, n1h, stride=n2), :].astype(BF16)
            scr[pl.ds(j, 2 * n1, stride=n2), :] = jnp.dot(ef_ref[j], x, preferred_element_type=F32)
            return c

        lax.fori_loop(0, n2, stage1, 0, unroll=DFT_UNROLL)

    def stage2(t, c):
        k1 = kk * kc + t
        re = scr[pl.ds(pl.multiple_of(k1 * n2, n2), n2), :]
        im = scr[pl.ds(pl.multiple_of((n1 + k1) * n2, n2), n2), :]
        a = jnp.concatenate([re, im], axis=0).astype(BF16)
        o_ref[t] = jnp.dot(ff_ref[...], a, preferred_element_type=F32)
        return c

    lax.fori_loop(0, kc, stage2, 0, unroll=DFT_UNROLL)


def dft_spectrum(h, tables):
    s, seq, ch = h.shape
    n2 = _dft_n2(seq)
    n1h = seq // n2
    n1 = 2 * n1h
    e_fwd, f_fwd, _, _ = tables
    kc = min(n1, 16)
    kern = functools.partial(_spectrum_kernel, n1=n1, n1h=n1h, n2=n2, kc=kc)
    return pl.pallas_call(
        kern,
        grid=(s, ch // LANES, n1 // kc),
        in_specs=[
            pl.BlockSpec((None, seq, LANES), lambda i, c, k: (i, 0, c)),
            _const_spec(e_fwd.shape),
            _const_spec(f_fwd.shape),
        ],
        out_specs=pl.BlockSpec((None, kc, 2 * n2, LANES), lambda i, c, k: (i, k, 0, c)),
        out_shape=jax.ShapeDtypeStruct((s, n1, 2 * n2, ch), F32),
        scratch_shapes=[pltpu.VMEM((2 * n1 * n2, LANES), F32)],
        compiler_params=_cparams(("parallel", "parallel", "arbitrary")),
        name="dft_spectrum",
    )(h, e_fwd, f_fwd)


def _longconv_kernel(u_ref, g_ref, h_ref, bias_ref, ef_ref, ff_ref, fi_ref, ei_ref, o_ref, scr, *, n1, n1h, n2):
    def stage1(j, c):
        x = u_ref[pl.ds(j, n1h, stride=n2), :].astype(BF16)
        scr[pl.ds(j, 2 * n1, stride=n2), :] = jnp.dot(ef_ref[j], x, preferred_element_type=F32)
        return c

    lax.fori_loop(0, n2, stage1, 0, unroll=DFT_UNROLL)

    def stage2(k1, c):
        r0 = pl.multiple_of(k1 * n2, n2)
        i0 = pl.multiple_of((n1 + k1) * n2, n2)
        a = jnp.concatenate([scr[pl.ds(r0, n2), :], scr[pl.ds(i0, n2), :]], axis=0).astype(BF16)
        y = jnp.dot(ff_ref[...], a, preferred_element_type=F32)
        hk = h_ref[k1].astype(F32)
        yr, yi = y[:n2], y[n2:]
        hr, hi = hk[:n2], hk[n2:]
        z = jnp.concatenate([yr * hr - yi * hi, yr * hi + yi * hr], axis=0).astype(BF16)
        bk = jnp.dot(fi_ref[...], z, preferred_element_type=F32)
        scr[pl.ds(r0, n2), :] = bk[:n2]
        scr[pl.ds(i0, n2), :] = bk[n2:]
        return c

    lax.fori_loop(0, n1, stage2, 0, unroll=DFT_UNROLL)

    def stage3(j, c):
        bmat = scr[pl.ds(j, 2 * n1, stride=n2), :].astype(BF16)
        y = jnp.dot(ei_ref[j], bmat, preferred_element_type=F32)
        u = u_ref[pl.ds(j, n1h, stride=n2), :]
        g = g_ref[pl.ds(j, n1h, stride=n2), :]
        o_ref[pl.ds(j, n1h, stride=n2), :] = g * (y + u * bias_ref[...])
        return c

    lax.fori_loop(0, n2, stage3, 0, unroll=DFT_UNROLL)


def long_conv_gated(u, u_blk, g, g_blk, spec, bias, tables):
    b, seq, _ = u.shape
    ch = bias.shape[0]
    n2 = _dft_n2(seq)
    n1h = seq // n2
    n1 = 2 * n1h
    e_fwd, f_fwd, f_inv, e_inv = tables
    kern = functools.partial(_longconv_kernel, n1=n1, n1h=n1h, n2=n2)
    one = pl.Buffered(1)
    return pl.pallas_call(
        kern,
        grid=(ch // LANES, b),
        in_specs=[
            pl.BlockSpec((None, seq, LANES), lambda c, i: (i, 0, u_blk + c), pipeline_mode=one),
            pl.BlockSpec((None, seq, LANES), lambda c, i: (i, 0, g_blk + c), pipeline_mode=one),
            pl.BlockSpec((n1, 2 * n2, LANES), lambda c, i: (0, 0, c), pipeline_mode=one),
            pl.BlockSpec((1, LANES), lambda c, i: (0, c)),
            _const_spec(e_fwd.shape),
            _const_spec(f_fwd.shape),
            _const_spec(f_inv.shape),
            _const_spec(e_inv.shape),
        ],
        out_specs=pl.BlockSpec((None, seq, LANES), lambda c, i: (i, 0, c)),
        out_shape=jax.ShapeDtypeStruct((b, seq, ch), F32),
        scratch_shapes=[pltpu.VMEM((2 * n1 * n2, LANES), F32)],
        compiler_params=_cparams(("parallel", "parallel")),
        name="long_conv",
    )(u, g, spec, bias.reshape(1, ch).astype(F32), e_fwd, f_fwd, f_inv, e_inv)


def _dft_tables_complex(seq):
    n2 = _dft_n2(seq)
    n1 = seq // n2
    k1 = np.arange(n1, dtype=np.float64)[None, :, None]
    nn = (n2 * np.arange(n1, dtype=np.float64)[None, None, :] + np.arange(n2, dtype=np.float64)[:, None, None])
    ang = 2.0 * np.pi * k1 * nn / seq
    c, s = np.cos(ang), np.sin(ang)
    e_fwd = np.concatenate([np.concatenate([c, s], axis=2), np.concatenate([-s, c], axis=2)], axis=1)
    a2 = 2.0 * np.pi * np.outer(np.arange(n2), np.arange(n2)) / n2
    f_re = np.concatenate([np.cos(a2), np.sin(a2)], axis=1)
    return jnp.asarray(e_fwd, BF16), jnp.asarray(f_re, BF16)


def _seqdft_kernel(vr_ref, vi_ref, ef_ref, fr_ref, o_ref, scr, *, n1, n2, scale):
    def stage1(j, c):
        x = jnp.concatenate([vr_ref[pl.ds(j, n1, stride=n2), :], vi_ref[pl.ds(j, n1, stride=n2), :]], axis=0)
        scr[pl.ds(j, 2 * n1, stride=n2), :] = jnp.dot(ef_ref[j], x.astype(BF16), preferred_element_type=F32)
        return c

    lax.fori_loop(0, n2, stage1, 0, unroll=DFT_UNROLL)

    def stage2(k1, c):
        r0 = pl.multiple_of(k1 * n2, n2)
        i0 = pl.multiple_of((n1 + k1) * n2, n2)
        a = jnp.concatenate([scr[pl.ds(r0, n2), :], scr[pl.ds(i0, n2), :]], axis=0).astype(BF16)
        o_ref[pl.ds(k1, n2, stride=n1), :] = jnp.dot(fr_ref[...], a, preferred_element_type=F32) * scale
        return c

    lax.fori_loop(0, n1, stage2, 0, unroll=DFT_UNROLL)


def seq_dft_real(v, ch, scale, tables):
    b, seq, _ = v.shape
    n2 = _dft_n2(seq)
    n1 = seq // n2
    e_fwd, f_re = tables
    nblk = ch // LANES
    kern = functools.partial(_seqdft_kernel, n1=n1, n2=n2, scale=scale)
    return pl.pallas_call(
        kern,
        grid=(nblk, b),
        in_specs=[
            pl.BlockSpec((None, seq, LANES), lambda c, i: (i, 0, c)),
            pl.BlockSpec((None, seq, LANES), lambda c, i: (i, 0, nblk + c)),
            _const_spec(e_fwd.shape),
            _const_spec(f_re.shape),
        ],
        out_specs=pl.BlockSpec((None, seq, LANES), lambda c, i: (i, 0, c)),
        out_shape=jax.ShapeDtypeStruct((b, seq, ch), F32),
        scratch_shapes=[pltpu.VMEM((2 * n1 * n2, LANES), F32)],
        compiler_params=_cparams(("parallel", "parallel")),
        name="seq_dft",
    )(v, v, e_fwd, f_re)


def fourier_channel_matrix():
    a = 2.0 * np.pi * np.outer(np.arange(FN_GROUP_DIM), np.arange(FN_GROUP_DIM)) / FN_GROUP_DIM
    eye = np.eye(FN_GROUPS)
    return jnp.asarray(np.concatenate([np.kron(eye, np.cos(a)), -np.kron(eye, np.sin(a))], axis=1), BF16)


def _shortconv_kernel(u_ref, w_ref, b_ref, o_ref):
    u = u_ref[...]
    n = u.shape[0]
    row = lax.broadcasted_iota(jnp.int32, u.shape, 0)
    prev = jnp.where(row == 0, 0.0, pltpu.roll(u, 1, axis=0))
    nxt = jnp.where(row == n - 1, 0.0, pltpu.roll(u, n - 1, axis=0))
    o_ref[...] = prev * w_ref[0:1, :] + u * w_ref[1:2, :] + nxt * w_ref[2:3, :] + b_ref[...]


def short_conv(u, w, bias):
    b, seq, ch = u.shape
    return pl.pallas_call(
        _shortconv_kernel,
        grid=(b, ch // LANES),
        in_specs=[
            pl.BlockSpec((None, seq, LANES), lambda i, c: (i, 0, c)),
            pl.BlockSpec((HY_SHORT, LANES), lambda i, c: (0, c)),
            pl.BlockSpec((1, LANES), lambda i, c: (0, c)),
        ],
        out_specs=pl.BlockSpec((None, seq, LANES), lambda i, c: (i, 0, c)),
        out_shape=jax.ShapeDtypeStruct((b, seq, ch), F32),
        compiler_params=_cparams(("parallel", "parallel")),
        name="short_conv",
    )(u, w.astype(F32), bias.reshape(1, ch).astype(F32))


def _filter_kernel(emb_ref, w1_ref, b1_ref, f1_ref, w2_ref, b2_ref, f2_ref, w3_ref, b3_ref, dec_ref, o_ref):
    z = jnp.dot(emb_ref[...].astype(BF16), w1_ref[...], preferred_element_type=F32) + b1_ref[...]
    z = jnp.sin(f1_ref[...] * z)
    z = jnp.dot(z.astype(BF16), w2_ref[...], preferred_element_type=F32) + b2_ref[...]
    z = jnp.sin(f2_ref[...] * z)
    h = jnp.dot(z.astype(BF16), w3_ref[...], preferred_element_type=F32) + b3_ref[...]
    o_ref[...] = h * dec_ref[...]


def hyena_filters(seq, hy_w1, hy_b1, hy_freq1, hy_w2, hy_b2, hy_freq2, hy_w3, hy_b3):
    t = jnp.linspace(0.0, 1.0, seq, dtype=F32)[:, None]
    ang = (2.0 * math.pi / seq) * jnp.arange(seq, dtype=F32)[:, None]
    bands = jnp.linspace(1e-4, HY_EMB_BANDS - 1, HY_EMB_BANDS, dtype=F32)[None, :]
    emb = jnp.concatenate([t, jnp.cos(bands * ang), -jnp.sin(bands * ang)], axis=-1)
    kdim = emb.shape[1]
    kpad = LANES - kdim
    emb = jnp.pad(emb, ((0, 0), (0, kpad)))
    w1 = jnp.pad(hy_w1, ((0, kpad), (0, 0))).astype(BF16)
    deltas = jnp.abs(jnp.linspace(math.log(HY_DECAY_TARGET) / HY_SLOW_DECAY,
                                  math.log(HY_DECAY_TARGET) / HY_FAST_DECAY, HY_WIDTH, dtype=F32))
    decay = jnp.tile(jnp.exp(-t * deltas), (1, 2 * HY_ORDER))
    fo = hy_w1.shape[1]
    nout = hy_w3.shape[1]
    tl = _tile(seq, 1024)
    row = lambda a: a.reshape(1, -1).astype(F32)
    full = lambda shape: pl.BlockSpec(shape, lambda i: (0, 0))
    return pl.pallas_call(
        _filter_kernel,
        grid=(seq // tl,),
        in_specs=[
            pl.BlockSpec((tl, LANES), lambda i: (i, 0)),
            full((LANES, fo)), full((1, fo)), full((1, fo)),
            full((fo, fo)), full((1, fo)), full((1, fo)),
            full((fo, nout)), full((1, nout)),
            pl.BlockSpec((tl, nout), lambda i: (i, 0)),
        ],
        out_specs=pl.BlockSpec((tl, nout), lambda i: (i, 0)),
        out_shape=jax.ShapeDtypeStruct((seq, nout), F32),
        compiler_params=_cparams(("parallel",)),
        name="hyena_filter",
    )(emb, w1, row(hy_b1), row(hy_freq1), hy_w2.astype(BF16), row(hy_b2), row(hy_freq2),
      hy_w3.astype(BF16), row(hy_b3), decay)


def hyena_spectra(seq, filt, tables):
    n2 = _dft_n2(seq)
    h = filt.reshape(seq, HY_ORDER, 2, HY_WIDTH)
    h_fwd, h_bwd = h[:, :, 0], h[:, :, 1]
    h_bwd = h_bwd.at[0].set(0.0)
    norm = jnp.sum(jnp.abs(h_fwd), axis=0) + jnp.sum(jnp.abs(h_bwd), axis=0)
    stacked = jnp.concatenate([h_fwd, h_bwd], axis=1).transpose(1, 0, 2)
    sp = dft_spectrum(stacked, tables)
    sf, sb = sp[:HY_ORDER], sp[HY_ORDER:]
    re = sf[:, :, :n2] + sb[:, :, :n2]
    im = sf[:, :, n2:] - sb[:, :, n2:]
    scale = (1.0 / (2 * seq)) / norm
    return (jnp.concatenate([re, im], axis=2) * scale[:, None, None, :]).astype(BF16)


def _merge_kernel(x_ref, yf_ref, yh_ref, ya_ref, g_ref, wf_ref, wh_ref, wa_ref, wo_ref,
                  gate_ref, ng_ref, sh_ref, sc_ref, xo_ref, ho_ref):
    d = x_ref.shape[-1]
    g = g_ref[0].astype(F32)
    yf = jnp.dot(yf_ref[0].astype(BF16), wf_ref[...], preferred_element_type=F32)
    yh = jnp.dot(yh_ref[0].astype(BF16), wh_ref[...], preferred_element_type=F32)
    ya = jnp.dot(ya_ref[0], wa_ref[...], preferred_element_type=F32)
    mix = g[:, 0:d] * yf + g[:, d:2 * d] * yh + g[:, 2 * d:3 * d] * ya
    x = x_ref[0] + gate_ref[0] * jnp.dot(mix.astype(BF16), wo_ref[...], preferred_element_type=F32)
    xo_ref[0] = x
    y = x * lax.rsqrt(jnp.mean(x * x, axis=-1, keepdims=True) + EPS) * ng_ref[...]
    ho_ref[0] = (y * (1.0 + sc_ref[0]) + sh_ref[0]).astype(ho_ref.dtype)


def merge_branches(x, yf, yh, ya, g, w_f, w_h, w_a, w_o, gate, norm_g, shift, scale):
    b, l, d = x.shape
    tl = _tile(l, 512)
    rows = lambda w: pl.BlockSpec((1, tl, w), lambda i, j: (i, j, 0))
    full = lambda a: pl.BlockSpec(a.shape, lambda i, j: (0, 0))
    per_b = pl.BlockSpec((1, 1, d), lambda i, j: (i, 0, 0))
    wf, wh, wa, wo = (w.astype(BF16) for w in (w_f, w_h, w_a, w_o))
    return pl.pallas_call(
        _merge_kernel,
        grid=(b, l // tl),
        in_specs=[rows(d), rows(yf.shape[-1]), rows(yh.shape[-1]), rows(ya.shape[-1]), rows(3 * d),
                  full(wf), full(wh), full(wa), full(wo),
                  per_b, pl.BlockSpec((1, d), lambda i, j: (0, 0)), per_b, per_b],
        out_specs=[rows(d), rows(d)],
        out_shape=[jax.ShapeDtypeStruct((b, l, d), F32), jax.ShapeDtypeStruct((b, l, d), BF16)],
        compiler_params=_cparams(("parallel", "parallel")),
        name="merge",
    )(x, yf, yh, ya, g, wf, wh, wa, wo, gate.reshape(b, 1, d), norm_g.reshape(1, d).astype(F32),
      shift.reshape(b, 1, d), scale.reshape(b, 1, d))


GLU_GROUP = 2 * LANES


def _glu_group_permutation():
    p = np.zeros((GLU_GROUP, GLU_GROUP), np.float32)
    j = np.arange(LANES)
    p[2 * j, j] = 1.0
    p[2 * j + 1, LANES + j] = 1.0
    return jnp.asarray(p, BF16)


def _moe_kernel(be_ref, act_ref, new_ref, rows_ref, w1_ref, b1_ref, w2_ref, b2_ref, p_ref, o_ref, w1s, w2s):
    i = pl.program_id(0)

    @pl.when(new_ref[i] > 0)
    def _():
        for q in range(w1s.shape[1] // GLU_GROUP):
            cols = slice(q * GLU_GROUP, (q + 1) * GLU_GROUP)
            w1s[:, cols] = jnp.dot(w1_ref[:, cols].astype(BF16), p_ref[...],
                                   preferred_element_type=F32).astype(BF16)
        w2s[...] = w2_ref[...].astype(BF16)

    @pl.when(act_ref[i] > 0)
    def _():
        u = jnp.dot(rows_ref[...], w1s[...], preferred_element_type=F32) + b1_ref[0]
        parts = []
        for q in range(u.shape[1] // GLU_GROUP):
            xg = jnp.minimum(u[:, q * GLU_GROUP:q * GLU_GROUP + LANES], SWIGLU_LIMIT)
            xl = jnp.clip(u[:, q * GLU_GROUP + LANES:(q + 1) * GLU_GROUP], -SWIGLU_LIMIT, SWIGLU_LIMIT)
            parts.append((xg * jax.nn.sigmoid(SWIGLU_ALPHA * xg) * (xl + 1.0)).astype(BF16))
        a = jnp.concatenate(parts, axis=1)
        y = jnp.dot(a, w2s[...], preferred_element_type=F32) + b2_ref[0]
        o_ref[...] = y.astype(o_ref.dtype)

    @pl.when(act_ref[i] == 0)
    def _():
        o_ref[...] = jnp.zeros(o_ref.shape, o_ref.dtype)


def moe_experts(rows, blk_exp, blk_act, blk_new, layer, w1_all, b1, w2_all, b2):
    r, d = rows.shape
    de = w2_all.shape[2]
    nblk = r // MOE_BLOCK
    grid_spec = pltpu.PrefetchScalarGridSpec(
        num_scalar_prefetch=3,
        grid=(nblk,),
        in_specs=[
            pl.BlockSpec((MOE_BLOCK, d), lambda i, be, act, new: (i, 0)),
            pl.BlockSpec((None, None, d, 2 * de), lambda i, be, act, new: (layer, be[i], 0, 0)),
            pl.BlockSpec((1, 1, 2 * de), lambda i, be, act, new: (be[i], 0, 0)),
            pl.BlockSpec((None, None, de, d), lambda i, be, act, new: (layer, be[i], 0, 0)),
            pl.BlockSpec((1, 1, d), lambda i, be, act, new: (be[i], 0, 0)),
            pl.BlockSpec((GLU_GROUP, GLU_GROUP), lambda i, be, act, new: (0, 0)),
        ],
        out_specs=pl.BlockSpec((MOE_BLOCK, d), lambda i, be, act, new: (i, 0)),
        scratch_shapes=[pltpu.VMEM((d, 2 * de), BF16), pltpu.VMEM((de, d), BF16)],
    )
    return pl.pallas_call(
        _moe_kernel,
        grid_spec=grid_spec,
        out_shape=jax.ShapeDtypeStruct((r, d), BF16),
        compiler_params=_cparams(("arbitrary",)),
        name="moe_experts",
    )(blk_exp, blk_act, blk_new, rows, w1_all, b1, w2_all, b2, _glu_group_permutation())


def _combine_kernel(y_ref, g_ref, x_ref, m_ref, o_ref):
    g = g_ref[...]
    acc = g[:, 0:1] * y_ref[0].astype(F32)
    for j in range(1, TOP_K):
        acc = acc + g[:, j:j + 1] * y_ref[j].astype(F32)
    o_ref[...] = x_ref[...] + m_ref[0] * acc


def moe_combine(y_sel, gate, resid, mod_blocks, tm):
    k, t, d = y_sel.shape
    return pl.pallas_call(
        _combine_kernel,
        grid=(t // tm,),
        in_specs=[pl.BlockSpec((k, tm, d), lambda i: (0, i, 0)),
                  pl.BlockSpec((tm, k), lambda i: (i, 0)),
                  pl.BlockSpec((tm, d), lambda i: (i, 0)),
                  pl.BlockSpec((1, 1, d), lambda i: (i, 0, 0))],
        out_specs=pl.BlockSpec((tm, d), lambda i: (i, 0)),
        out_shape=jax.ShapeDtypeStruct((t, d), F32),
        compiler_params=_cparams(("parallel",)),
        name="moe_combine",
    )(y_sel, gate, resid, mod_blocks)


def moe_ffn(h, resid, mod_blocks, tm, p):
    t_tok, d = h.shape
    wr = jnp.pad(p["w_router"], ((0, 0), (0, LANES - N_EXPERTS))).astype(BF16)
    br = jnp.pad(p["b_router"], (0, LANES - N_EXPERTS)).reshape(1, LANES).astype(F32)
    logits = matmul(h, wr, epi="bias", extra=(br,),
                    extra_specs=[pl.BlockSpec((1, LANES), lambda i, j: (0, 0))], name="mm_router")[:, :N_EXPERTS]
    top_v, top_i = lax.top_k(logits, TOP_K)
    gate = jax.nn.softmax(top_v, axis=-1)
    n_assign = t_tok * TOP_K
    flat_e = top_i.reshape(-1)
    experts = jnp.arange(N_EXPERTS, dtype=flat_e.dtype)[None, :]
    onehot = (flat_e[:, None] == experts).astype(jnp.int32)
    csum = jnp.cumsum(onehot, axis=0)
    counts = csum[-1]
    padded = (counts + MOE_BLOCK - 1) // MOE_BLOCK * MOE_BLOCK
    pad_end = jnp.cumsum(padded)
    pad_start = pad_end - padded
    dest = jnp.sum(onehot * (csum - 1 + pad_start[None, :]), axis=1)
    n_blocks = -(-n_assign // MOE_BLOCK) + N_EXPERTS
    n_rows = n_blocks * MOE_BLOCK
    row_tok = jnp.zeros((n_rows,), jnp.int32).at[dest].set(jnp.arange(n_assign, dtype=jnp.int32) // TOP_K)
    blk_start = jnp.arange(n_blocks, dtype=jnp.int32) * MOE_BLOCK
    blk_exp = jnp.minimum(jnp.sum((blk_start[:, None] >= pad_end[None, :]).astype(jnp.int32), axis=1),
                          N_EXPERTS - 1)
    blk_act = (blk_start < pad_end[-1]).astype(jnp.int32)
    blk_new = jnp.concatenate([jnp.ones((1,), jnp.int32), (blk_exp[1:] != blk_exp[:-1]).astype(jnp.int32)])
    y_rows = moe_experts(h[row_tok], blk_exp, blk_act, blk_new, p["layer"], p["w1_all"], p["b1"],
                         p["w2_all"], p["b2"])
    y_sel = y_rows[dest.reshape(t_tok, TOP_K).T]
    return moe_combine(y_sel, gate, resid, mod_blocks, tm)


OFF_F = 0
OFF_HY = OFF_F + FN_WIDTH
OFF_Q = OFF_HY + (HY_ORDER + 1) * HY_WIDTH
OFF_K = OFF_Q + COL_QK
OFF_V = OFF_K + COL_QK
OFF_G = OFF_V + DA_WIDTH


def _token_mixer(x, h, mod_gate, norm2_g, mod_shift2, mod_scale2, p, lam, lam_init, rope, kv_extra, q_only_self):
    b, s, d = x.shape
    h2d = h.reshape(b * s, d)
    w = p["w_in"]
    cos, sin = rope

    w_fv = matmul(w[:, OFF_F:OFF_HY], fourier_channel_matrix(), out_dtype=BF16, name="mm_wfold")
    v_f = matmul(h2d, w_fv, name="mm_fproj").reshape(b, s, 2 * FN_WIDTH)
    y_f = seq_dft_real(v_f, FN_WIDTH, 1.0 / math.sqrt(s * FN_GROUP_DIM), _dft_tables_complex(s))

    z = matmul(h2d, w[:, OFF_HY:OFF_Q], name="mm_hproj").reshape(b, s, (HY_ORDER + 1) * HY_WIDTH)
    z = short_conv(z, p["hy_conv_w"], p["hy_conv_b"])
    tables = _dft_tables_real(s)
    filt = hyena_filters(s, p["hy_w1"], p["hy_b1"], p["hy_freq1"], p["hy_w2"], p["hy_b2"], p["hy_freq2"],
                         p["hy_w3"], p["hy_b3"])
    spec = hyena_spectra(s, filt, tables)
    cb = HY_WIDTH // LANES
    y_h = long_conv_gated(z, 0, z, cb, spec[0], p["hy_bias"][0], tables)
    y_h = long_conv_gated(y_h, 0, z, 2 * cb, spec[1], p["hy_bias"][1], tables)

    k_gain = p["k_norm_g"]
    k = qk_project(h2d, w[:, OFF_K:OFF_V], k_gain, cos, sin, s)
    v = matmul(h2d, w[:, OFF_V:OFF_G], out_dtype=BF16, name="mm_vproj")
    k3, v3 = k.reshape(b, s, COL_QK), v.reshape(b, s, DA_WIDTH)
    if q_only_self is None:
        y_a = None
    else:
        q = qk_project(h2d, w[:, OFF_Q:OFF_K], p["q_norm_g"] * (DA_QK_DIM ** -0.5 * math.log2(math.e)), cos, sin, s)
        if kv_extra is not None:
            k_all = jnp.concatenate([k3, kv_extra[0]], axis=1)
            v_all = jnp.concatenate([v3, kv_extra[1]], axis=1)
        else:
            k_all, v_all = k3, v3
        nk = k_all.shape[1]
        y_a = diff_attention(q, k_all.reshape(b * nk, COL_QK), v_all.reshape(b * nk, DA_WIDTH), lam,
                             p["subln_g"], 1.0 - lam_init, s, nk).reshape(b, s, DA_WIDTH)

    g = matmul(h2d, w[:, OFF_G:], out_dtype=BF16, epi="sigmoid", name="mm_gates").reshape(b, s, N_BRANCHES * d)
    x_new, h2 = merge_branches(x, y_f, y_h, y_a, g, p["w_f"], p["w_h"], p["w_a"], p["w_o"],
                               mod_gate, norm2_g, mod_shift2, mod_scale2)
    return x_new, h2, (k3, v3)


def _context_kv(h, p, rope):
    b, s, d = h.shape
    h2d = h.reshape(b * s, d)
    w = p["w_in"]
    k = qk_project(h2d, w[:, OFF_K:OFF_V], p["k_norm_g"], rope[0], rope[1], s)
    v = matmul(h2d, w[:, OFF_V:OFF_G], out_dtype=BF16, name="mm_vproj_ctx")
    return k.reshape(b, s, COL_QK), v.reshape(b, s, DA_WIDTH)


def _layer(l, x, xc, c, c_ctx, p, ctx_out):
    b, n_lat, d = x.shape
    n_ctx = xc.shape[1]
    lam_init = 0.8 - 0.6 * math.exp(-0.3 * l)
    lam = (jnp.exp(jnp.sum(p["lam_q"][0] * p["lam_k"][0]).astype(F32))
           - jnp.exp(jnp.sum(p["lam_q"][1] * p["lam_k"][1]).astype(F32)) + lam_init)

    cond = jnp.concatenate([c, c_ctx[None, :], jnp.zeros((16 - b - 1, d), F32)], axis=0)
    mod_all = matmul(jax.nn.silu(cond).astype(BF16), p["w_mod"].astype(BF16), epi="bias",
                     extra=(p["b_mod"].reshape(1, 6 * d).astype(F32),),
                     extra_specs=[pl.BlockSpec((1, 1024), lambda i, j: (0, j))], name="mm_mod")
    mod = [mod_all[:b, i * d:(i + 1) * d] for i in range(6)]
    mod_c = [jnp.broadcast_to(mod_all[b, i * d:(i + 1) * d], (b, d)) for i in range(6)]

    pw = dict(p)
    pw["w_in"] = p["w_in"].astype(BF16)

    no_rope = (jnp.ones((n_ctx, LANES), F32), jnp.zeros((n_ctx, LANES), F32))
    hc = normmod(xc, p["norm1_g"], mod_c[0], mod_c[1])
    h = normmod(x, p["norm1_g"], mod[0], mod[1])
    if ctx_out:
        xc_new, h2c, kv_c = _token_mixer(xc, hc, mod_c[2], p["norm2_g"], mod_c[3], mod_c[4], pw, lam, lam_init,
                                         no_rope, None, True)
    else:
        kv_c = _context_kv(hc, pw, no_rope)
    x_new, h2, _ = _token_mixer(x, h, mod[2], p["norm2_g"], mod[3], mod[4], pw, lam, lam_init,
                                rope_tables(n_lat), kv_c, True)

    n_exp, two_f = p["b_e1"].shape
    b1 = p["b_e1"].reshape(n_exp, two_f // GLU_GROUP, LANES, 2).transpose(0, 1, 3, 2).reshape(n_exp, 1, two_f)
    pe = {
        "w_router": p["w_router"], "b_router": p["b_router"], "layer": l,
        "w1_all": p["w_e1_all"], "b1": b1, "w2_all": p["w_e2_all"], "b2": p["b_e2"][:, None, :],
    }
    tm = 512
    lat_mod = jnp.repeat(mod[5], n_lat // tm, axis=0)
    if ctx_out:
        h_all = jnp.concatenate([h2c.reshape(b * n_ctx, d), h2.reshape(b * n_lat, d)], axis=0)
        resid = jnp.concatenate([xc_new.reshape(b * n_ctx, d), x_new.reshape(b * n_lat, d)], axis=0)
        mod_blocks = jnp.concatenate([jnp.tile(mod_c[5][:1], (b * n_ctx // tm, 1)), lat_mod], axis=0)
        out = moe_ffn(h_all, resid, mod_blocks[:, None, :], tm, pe)
        xc = out[:b * n_ctx].reshape(b, n_ctx, d)
        x = out[b * n_ctx:].reshape(b, n_lat, d)
    else:
        x = moe_ffn(h2.reshape(b * n_lat, d), x_new.reshape(b * n_lat, d), lat_mod[:, None, :], tm, pe)
        x = x.reshape(b, n_lat, d)
    return x, xc


_PARAM_NAMES = ("w_mod", "b_mod", "norm1_g", "norm2_g", "w_in", "hy_conv_w", "hy_conv_b", "hy_w1", "hy_b1",
                "hy_freq1", "hy_w2", "hy_b2", "hy_freq2", "hy_w3", "hy_b3", "hy_bias", "q_norm_g", "k_norm_g",
                "lam_q", "lam_k", "subln_g", "w_f", "w_h", "w_a", "w_o", "w_router", "b_router",
                "w_e1", "b_e1", "w_e2", "b_e2")


def kernel(x, c, ctx, c_ctx, w_mod, b_mod, norm1_g, norm2_g, w_in, hy_conv_w, hy_conv_b, hy_w1, hy_b1, hy_freq1,
           hy_w2, hy_b2, hy_freq2, hy_w3, hy_b3, hy_bias, q_norm_g, k_norm_g, lam_q, lam_k, subln_g, w_f, w_h,
           w_a, w_o, w_router, b_router, w_e1, b_e1, w_e2, b_e2):
    stacked = (w_mod, b_mod, norm1_g, norm2_g, w_in, hy_conv_w, hy_conv_b, hy_w1, hy_b1, hy_freq1, hy_w2, hy_b2,
               hy_freq2, hy_w3, hy_b3, hy_bias, q_norm_g, k_norm_g, lam_q, lam_k, subln_g, w_f, w_h, w_a, w_o,
               w_router, b_router, w_e1, b_e1, w_e2, b_e2)
    depth = w_mod.shape[0]
    xc = ctx
    for l in range(depth):
        p = {name: arr[l] for name, arr in zip(_PARAM_NAMES, stacked) if name not in ("w_e1", "w_e2")}
        p["w_e1_all"], p["w_e2_all"] = w_e1, w_e2
        x, xc = _layer(l, x, xc, c, c_ctx, p, l < depth - 1)
    return x
```

```python
import functools
import math

import numpy as np
import jax
import jax.numpy as jnp
from jax import lax
from jax.experimental import pallas as pl
from jax.experimental.pallas import tpu as pltpu

F32 = jnp.float32
BF16 = jnp.bfloat16

LANES = 128
VMEM_LIMIT = 56 * 1024 * 1024

GRID_W = 64
EPS = 1e-6
SUBLN_EPS = 1e-5
FN_GROUPS = 4
FN_GROUP_DIM = 64
FN_WIDTH = FN_GROUPS * FN_GROUP_DIM
HY_WIDTH = 256
HY_ORDER = 2
HY_SHORT = 3
HY_EMB_BANDS = 16
HY_DECAY_TARGET = 1e-2
HY_FAST_DECAY = 0.3
HY_SLOW_DECAY = 1.5
DA_HEADS = 4
DA_QK_DIM = 64
DA_V_DIM = 2 * DA_QK_DIM
DA_WIDTH = DA_HEADS * DA_V_DIM
ROPE_BASE = 10000.0
N_BRANCHES = 3
COL_QK = DA_HEADS * 2 * DA_QK_DIM
N_EXPERTS = 32
TOP_K = 4
SWIGLU_ALPHA = 1.702
SWIGLU_LIMIT = 7.0
MOE_BLOCK = 256
DFT_MIN_N1 = 16
DFT_UNROLL = 8


def _dft_n2(seq):
    return min(LANES, seq // DFT_MIN_N1)


def _cparams(sem):
    return pltpu.CompilerParams(dimension_semantics=sem, vmem_limit_bytes=VMEM_LIMIT)


def _tile(n, pref):
    if n <= pref:
        return n
    for t in range(pref, 7, -1):
        if n % t == 0 and t % 8 == 0:
            return t
    return n


def _const_spec(shape):
    nd = len(shape)
    return pl.BlockSpec(shape, lambda *_: (0,) * nd, pipeline_mode=pl.Buffered(1))


def _normmod_kernel(x_ref, g_ref, sh_ref, sc_ref, o_ref):
    x = x_ref[0]
    y = x * lax.rsqrt(jnp.mean(x * x, axis=-1, keepdims=True) + EPS)
    y = y * g_ref[...]
    o_ref[0] = (y * (1.0 + sc_ref[0]) + sh_ref[0]).astype(o_ref.dtype)


def normmod(x, g, shift, scale):
    b, l, d = x.shape
    tl = _tile(l, 1024)
    return pl.pallas_call(
        _normmod_kernel,
        grid=(b, l // tl),
        in_specs=[
            pl.BlockSpec((1, tl, d), lambda i, j: (i, j, 0)),
            pl.BlockSpec((1, d), lambda i, j: (0, 0)),
            pl.BlockSpec((1, 1, d), lambda i, j: (i, 0, 0)),
            pl.BlockSpec((1, 1, d), lambda i, j: (i, 0, 0)),
        ],
        out_specs=pl.BlockSpec((1, tl, d), lambda i, j: (i, j, 0)),
        out_shape=jax.ShapeDtypeStruct((b, l, d), BF16),
        compiler_params=_cparams(("parallel", "parallel")),
        name="normmod",
    )(x, g.reshape(1, d), shift.reshape(b, 1, d), scale.reshape(b, 1, d))


def _mm_kernel(a_ref, w_ref, *rest, epi):
    acc = jnp.dot(a_ref[...], w_ref[...], preferred_element_type=F32)
    if epi == "plain":
        (o_ref,) = rest
    elif epi == "bias":
        b_ref, o_ref = rest
        acc = acc + b_ref[...]
    elif epi == "sigmoid":
        (o_ref,) = rest
        acc = jax.nn.sigmoid(acc)
    elif epi == "qk":
        gm_ref, gain_ref, cos_ref, sin_ref, o_ref = rest
        ms = jnp.dot((acc * acc).astype(BF16), gm_ref[...], preferred_element_type=F32)
        y = acc * lax.rsqrt(ms + EPS) * gain_ref[...]
        n = y.shape[1]
        reps = n // LANES
        lane = lax.broadcasted_iota(jnp.int32, y.shape, 1)
        is_a = (lane % (DA_QK_DIM // 2)) < (DA_QK_DIM // 4)
        half = DA_QK_DIM // 4
        swapped = jnp.where(is_a, pltpu.roll(y, n - half, axis=1), pltpu.roll(y, half, axis=1))
        acc = y * jnp.tile(cos_ref[...], (1, reps)) + swapped * jnp.tile(sin_ref[...], (1, reps))
    else:
        raise ValueError(epi)
    o_ref[...] = acc.astype(o_ref.dtype)


def matmul(a, w, *, out_dtype=F32, epi="plain", extra=(), extra_specs=(), tm=512, tn=1024, name="mm"):
    m, k = a.shape
    k2, n = w.shape
    assert k == k2
    tm = _tile(m, tm)
    tn = _tile(n, tn)
    return pl.pallas_call(
        functools.partial(_mm_kernel, epi=epi),
        grid=(m // tm, n // tn),
        in_specs=[
            pl.BlockSpec((tm, k), lambda i, j: (i, 0)),
            pl.BlockSpec((k, tn), lambda i, j: (0, j)),
            *extra_specs,
        ],
        out_specs=pl.BlockSpec((tm, tn), lambda i, j: (i, j)),
        out_shape=jax.ShapeDtypeStruct((m, n), out_dtype),
        compiler_params=_cparams(("parallel", "parallel")),
        name=name,
    )(a, w, *extra)


def _group_mean_matrix(n, group):
    idx = np.arange(n)
    return jnp.asarray((idx[:, None] // group == idx[None, :] // group).astype(np.float32) / group, BF16)


def rope_tables(n_lat):
    rows = n_lat // GRID_W
    row = np.repeat(np.arange(rows), GRID_W).astype(np.float64)
    col = np.tile(np.arange(GRID_W), rows).astype(np.float64)
    n_freq = DA_QK_DIM // 4
    inv = ROPE_BASE ** (-np.arange(n_freq, dtype=np.float64) / n_freq)
    ang_r = row[:, None] * inv
    ang_c = col[:, None] * inv
    cos = np.concatenate([np.cos(ang_r), np.cos(ang_r), np.cos(ang_c), np.cos(ang_c)], axis=1)
    sin = np.concatenate([-np.sin(ang_r), np.sin(ang_r), -np.sin(ang_c), np.sin(ang_c)], axis=1)
    cos = np.tile(cos, (1, LANES // DA_QK_DIM))
    sin = np.tile(sin, (1, LANES // DA_QK_DIM))
    return jnp.asarray(cos, F32), jnp.asarray(sin, F32)


def qk_project(h, w, gain, cos, sin, seq):
    m, _ = h.shape
    n = w.shape[1]
    tm = _tile(seq, 512)
    nblk = seq // tm
    gm = _group_mean_matrix(n, DA_QK_DIM)
    gain_t = jnp.tile(gain.astype(F32), n // DA_QK_DIM).reshape(1, n)
    extra_specs = [
        pl.BlockSpec((n, n), lambda i, j: (0, 0)),
        pl.BlockSpec((1, n), lambda i, j: (0, 0)),
        pl.BlockSpec((tm, LANES), lambda i, j: (i % nblk, 0)),
        pl.BlockSpec((tm, LANES), lambda i, j: (i % nblk, 0)),
    ]
    return matmul(h, w, out_dtype=BF16, epi="qk", extra=(gm, gain_t, cos, sin), extra_specs=extra_specs,
                  tm=tm, tn=n, name="mm_qk")


ATTN_TQ = 512
ATTN_TK = 768


def _attn_kernel(lam_ref, q_ref, k_ref, v_ref, g_ref, o_ref, qs_ref, s_ref, m_ref, acc_ref, *, tq, tk, nkc, out_scale):
    q = q_ref[...]
    lane = lax.broadcasted_iota(jnp.int32, q.shape, 1)
    zero = jnp.zeros_like(q)
    qs_ref[0:tq, :] = jnp.where(lane < DA_QK_DIM, q, zero)
    qs_ref[tq:2 * tq, :] = jnp.where(lane >= DA_QK_DIM, q, zero)
    m_ref[...] = jnp.full(m_ref.shape, -jnp.inf, F32)
    acc_ref[...] = jnp.zeros(acc_ref.shape, F32)

    def scores(j, slot):
        kj = k_ref[pl.ds(pl.multiple_of(j * tk, tk), tk), :]
        s_ref[slot] = lax.dot_general(qs_ref[...], kj, (((1,), (1,)), ((), ())), preferred_element_type=F32)

    def update(j, slot):
        s = s_ref[slot]
        vj = v_ref[pl.ds(pl.multiple_of(j * tk, tk), tk), :]
        m_prev = m_ref[...]
        m_next = jnp.maximum(m_prev, jnp.max(s, axis=1, keepdims=True))
        p = jnp.exp2(s - jnp.tile(m_next, (1, tk // LANES)))
        alpha = jnp.exp2(m_prev - m_next)
        m_ref[...] = m_next
        acc_ref[...] = acc_ref[...] * jnp.tile(alpha, (1, 2)) + jnp.dot(p.astype(BF16), vj,
                                                                          preferred_element_type=F32)

    scores(0, 0)
    pairs = (nkc - 1) // 2

    def body(i, c):
        j = 2 * i
        scores(j + 1, 1)
        update(j, 0)
        scores(j + 2, 0)
        update(j + 1, 1)
        return c

    lax.fori_loop(0, pairs, body, 0)
    if nkc % 2 == 0:
        scores(nkc - 1, 1)
        update(nkc - 2, 0)
        update(nkc - 1, 1)
    else:
        update(nkc - 1, 0)

    o1 = acc_ref[0:tq, 0:DA_V_DIM] / acc_ref[0:tq, DA_V_DIM:]
    o2 = acc_ref[tq:2 * tq, 0:DA_V_DIM] / acc_ref[tq:2 * tq, DA_V_DIM:]
    o = o1 - lam_ref[0, 0] * o2
    o = o * lax.rsqrt(jnp.mean(o * o, axis=-1, keepdims=True) + SUBLN_EPS)
    o_ref[...] = (o * g_ref[...] * out_scale).astype(o_ref.dtype)


def diff_attention(q, k, v, lam, subln_g, out_scale, nq, nk):
    b = q.shape[0] // nq
    tq = _tile(nq, ATTN_TQ)
    tk = next(t for t in (ATTN_TK, 256, 128) if nk % t == 0)
    nqb, nkc = nq // tq, nk // tk
    v_ext = jnp.concatenate([v.reshape(b * nk, DA_HEADS, DA_V_DIM),
                             jnp.ones((b * nk, DA_HEADS, DA_V_DIM), v.dtype)], axis=2)
    v_ext = v_ext.reshape(b * nk, 2 * DA_WIDTH)
    kern = functools.partial(_attn_kernel, tq=tq, tk=tk, nkc=nkc, out_scale=out_scale)
    return pl.pallas_call(
        kern,
        grid=(b, DA_HEADS, nqb),
        in_specs=[
            pl.BlockSpec(memory_space=pltpu.SMEM),
            pl.BlockSpec((tq, DA_V_DIM), lambda bi, h, qi: (bi * nqb + qi, h)),
            pl.BlockSpec((nk, DA_V_DIM), lambda bi, h, qi: (bi, h)),
            pl.BlockSpec((nk, 2 * DA_V_DIM), lambda bi, h, qi: (bi, h)),
            pl.BlockSpec((1, DA_V_DIM), lambda bi, h, qi: (0, 0)),
        ],
        out_specs=pl.BlockSpec((tq, DA_V_DIM), lambda bi, h, qi: (bi * nqb + qi, h)),
        out_shape=jax.ShapeDtypeStruct((b * nq, DA_WIDTH), BF16),
        scratch_shapes=[
            pltpu.VMEM((2 * tq, DA_V_DIM), BF16),
            pltpu.VMEM((2, 2 * tq, tk), F32),
            pltpu.VMEM((2 * tq, LANES), F32),
            pltpu.VMEM((2 * tq, 2 * DA_V_DIM), F32),
        ],
        compiler_params=_cparams(("parallel", "parallel", "parallel")),
        name="diff_attn",
    )(lam.reshape(1, 1).astype(F32), q, k, v_ext, subln_g.reshape(1, DA_V_DIM).astype(F32))


def _dft_tables_real(seq):
    n2 = _dft_n2(seq)
    n1h = seq // n2
    n1 = 2 * n1h
    n = n1 * n2
    k1 = np.arange(n1, dtype=np.float64)[None, :, None]
    nn = (n2 * np.arange(n1h, dtype=np.float64)[None, None, :] + np.arange(n2, dtype=np.float64)[:, None, None])
    ang = 2.0 * np.pi * k1 * nn / n
    e_fwd = np.concatenate([np.cos(ang), -np.sin(ang)], axis=1)
    e_inv = np.transpose(e_fwd, (0, 2, 1))
    a2 = 2.0 * np.pi * np.outer(np.arange(n2), np.arange(n2)) / n2
    c, s = np.cos(a2), np.sin(a2)
    f_fwd = np.block([[c, s], [-s, c]])
    f_inv = np.block([[c, -s], [s, c]])
    return tuple(jnp.asarray(t, BF16) for t in (e_fwd, f_fwd, f_inv, e_inv))


def _pack_complex(re, im):
    r = lax.bitcast_convert_type(re.astype(BF16).astype(F32), jnp.uint32)
    i = lax.bitcast_convert_type(im.astype(BF16).astype(F32), jnp.uint32)
    return r | (i >> 16)


def _unpack_complex(w):
    re = lax.bitcast_convert_type(w & jnp.uint32(0xFFFF0000), F32)
    im = lax.bitcast_convert_type(w << 16, F32)
    return jnp.concatenate([re, im], axis=0).astype(BF16)


def _spectrum_kernel(u_ref, ef_ref, ff_ref, o_ref, scr, *, n1, n1h, n2, kc):
    kk = pl.program_id(2)

    @pl.when(kk == 0)
    def _():
        def stage1(j, c):
            x = u_ref[pl.ds(j, n1h, stride=n2), :].astype(BF16)
            a = jnp.dot(ef_ref[j], x, preferred_element_type=F32)
            scr[pl.ds(pl.multiple_of(j * n1, n1), n1), :] = _pack_complex(a[:n1], a[n1:])
            return c

        lax.fori_loop(0, n2, stage1, 0, unroll=DFT_UNROLL)

    def stage2(t, c):
        a = _unpack_complex(scr[pl.ds(kk * kc + t, n2, stride=n1), :])
        o_ref[t] = jnp.dot(ff_ref[...], a, preferred_element_type=F32)
        return c

    lax.fori_loop(0, kc, stage2, 0, unroll=DFT_UNROLL)


def dft_spectrum(h, tables):
    s, seq, ch = h.shape
    n2 = _dft_n2(seq)
    n1h = seq // n2
    n1 = 2 * n1h
    e_fwd, f_fwd, _, _ = tables
    kc = min(n1, 16)
    kern = functools.partial(_spectrum_kernel, n1=n1, n1h=n1h, n2=n2, kc=kc)
    return pl.pallas_call(
        kern,
        grid=(s, ch // LANES, n1 // kc),
        in_specs=[
            pl.BlockSpec((None, seq, LANES), lambda i, c, k: (i, 0, c)),
            _const_spec(e_fwd.shape),
            _const_spec(f_fwd.shape),
        ],
        out_specs=pl.BlockSpec((None, kc, 2 * n2, LANES), lambda i, c, k: (i, k, 0, c)),
        out_shape=jax.ShapeDtypeStruct((s, n1, 2 * n2, ch), F32),
        scratch_shapes=[pltpu.VMEM((n1 * n2, LANES), jnp.uint32)],
        compiler_params=_cparams(("parallel", "parallel", "arbitrary")),
        name="dft_spectrum",
    )(h, e_fwd, f_fwd)


def _longconv_kernel(u_ref, g_ref, h_ref, bias_ref, ef_ref, ff_ref, fi_ref, ei_ref, o_ref, scr_a, scr_b,
                     *, n1, n1h, n2):
    def stage1(j, c):
        x = u_ref[pl.ds(j, n1h, stride=n2), :].astype(BF16)
        a = jnp.dot(ef_ref[j], x, preferred_element_type=F32)
        scr_a[pl.ds(pl.multiple_of(j * n1, n1), n1), :] = _pack_complex(a[:n1], a[n1:])
        return c

    lax.fori_loop(0, n2, stage1, 0, unroll=DFT_UNROLL)

    def stage2(k1, c):
        a = _unpack_complex(scr_a[pl.ds(k1, n2, stride=n1), :])
        y = jnp.dot(ff_ref[...], a, preferred_element_type=F32)
        hk = h_ref[k1].astype(F32)
        yr, yi = y[:n2], y[n2:]
        hr, hi = hk[:n2], hk[n2:]
        z = jnp.concatenate([yr * hr - yi * hi, yr * hi + yi * hr], axis=0).astype(BF16)
        bk = jnp.dot(fi_ref[...], z, preferred_element_type=F32)
        scr_b[pl.ds(pl.multiple_of(k1 * n2, n2), n2), :] = _pack_complex(bk[:n2], bk[n2:])
        return c

    lax.fori_loop(0, n1, stage2, 0, unroll=DFT_UNROLL)

    def stage3(j, c):
        bmat = _unpack_complex(scr_b[pl.ds(j, n1, stride=n2), :])
        y = jnp.dot(ei_ref[j], bmat, preferred_element_type=F32)
        u = u_ref[pl.ds(j, n1h, stride=n2), :]
        g = g_ref[pl.ds(j, n1h, stride=n2), :]
        o_ref[pl.ds(j, n1h, stride=n2), :] = g * (y + u * bias_ref[...])
        return c

    lax.fori_loop(0, n2, stage3, 0, unroll=DFT_UNROLL)


def long_conv_gated(u, u_blk, g, g_blk, spec, bias, tables):
    b, seq, _ = u.shape
    ch = bias.shape[0]
    n2 = _dft_n2(seq)
    n1h = seq // n2
    n1 = 2 * n1h
    e_fwd, f_fwd, f_inv, e_inv = tables
    kern = functools.partial(_longconv_kernel, n1=n1, n1h=n1h, n2=n2)
    one = pl.Buffered(1)
    return pl.pallas_call(
        kern,
        grid=(ch // LANES, b),
        in_specs=[
            pl.BlockSpec((None, seq, LANES), lambda c, i: (i, 0, u_blk + c), pipeline_mode=one),
            pl.BlockSpec((None, seq, LANES), lambda c, i: (i, 0, g_blk + c), pipeline_mode=one),
            pl.BlockSpec((n1, 2 * n2, LANES), lambda c, i: (0, 0, c), pipeline_mode=one),
            pl.BlockSpec((1, LANES), lambda c, i: (0, c)),
            _const_spec(e_fwd.shape),
            _const_spec(f_fwd.shape),
            _const_spec(f_inv.shape),
            _const_spec(e_inv.shape),
        ],
        out_specs=pl.BlockSpec((None, seq, LANES), lambda c, i: (i, 0, c)),
        out_shape=jax.ShapeDtypeStruct((b, seq, ch), F32),
        scratch_shapes=[pltpu.VMEM((n1 * n2, LANES), jnp.uint32), pltpu.VMEM((n1 * n2, LANES), jnp.uint32)],
        compiler_params=_cparams(("parallel", "parallel")),
        name="long_conv",
    )(u, g, spec, bias.reshape(1, ch).astype(F32), e_fwd, f_fwd, f_inv, e_inv)


def _dft_tables_complex(seq):
    n2 = _dft_n2(seq)
    n1 = seq // n2
    k1 = np.arange(n1, dtype=np.float64)[None, :, None]
    nn = (n2 * np.arange(n1, dtype=np.float64)[None, None, :] + np.arange(n2, dtype=np.float64)[:, None, None])
    ang = 2.0 * np.pi * k1 * nn / seq
    c, s = np.cos(ang), np.sin(ang)
    e_fwd = np.concatenate([np.concatenate([c, s], axis=2), np.concatenate([-s, c], axis=2)], axis=1)
    a2 = 2.0 * np.pi * np.outer(np.arange(n2), np.arange(n2)) / n2
    f_re = np.concatenate([np.cos(a2), np.sin(a2)], axis=1)
    return jnp.asarray(e_fwd, BF16), jnp.asarray(f_re, BF16)


def _seqdft_kernel(vr_ref, vi_ref, ef_ref, fr_ref, o_ref, scr, *, n1, n2, scale):
    def stage1(j, c):
        x = jnp.concatenate([vr_ref[pl.ds(j, n1, stride=n2), :], vi_ref[pl.ds(j, n1, stride=n2), :]], axis=0)
        a = jnp.dot(ef_ref[j], x.astype(BF16), preferred_element_type=F32)
        scr[pl.ds(pl.multiple_of(j * n1, n1), n1), :] = _pack_complex(a[:n1], a[n1:])
        return c

    lax.fori_loop(0, n2, stage1, 0, unroll=DFT_UNROLL)

    def stage2(k1, c):
        a = _unpack_complex(scr[pl.ds(k1, n2, stride=n1), :])
        o_ref[pl.ds(k1, n2, stride=n1), :] = jnp.dot(fr_ref[...], a, preferred_element_type=F32) * scale
        return c

    lax.fori_loop(0, n1, stage2, 0, unroll=DFT_UNROLL)


def seq_dft_real(v, ch, scale, tables):
    b, seq, _ = v.shape
    n2 = _dft_n2(seq)
    n1 = seq // n2
    e_fwd, f_re = tables
    nblk = ch // LANES
    kern = functools.partial(_seqdft_kernel, n1=n1, n2=n2, scale=scale)
    return pl.pallas_call(
        kern,
        grid=(nblk, b),
        in_specs=[
            pl.BlockSpec((None, seq, LANES), lambda c, i: (i, 0, c)),
            pl.BlockSpec((None, seq, LANES), lambda c, i: (i, 0, nblk + c)),
            _const_spec(e_fwd.shape),
            _const_spec(f_re.shape),
        ],
        out_specs=pl.BlockSpec((None, seq, LANES), lambda c, i: (i, 0, c)),
        out_shape=jax.ShapeDtypeStruct((b, seq, ch), F32),
        scratch_shapes=[pltpu.VMEM((n1 * n2, LANES), jnp.uint32)],
        compiler_params=_cparams(("parallel", "parallel")),
        name="seq_dft",
    )(v, v, e_fwd, f_re)


def fourier_channel_matrix():
    a = 2.0 * np.pi * np.outer(np.arange(FN_GROUP_DIM), np.arange(FN_GROUP_DIM)) / FN_GROUP_DIM
    eye = np.eye(FN_GROUPS)
    return jnp.asarray(np.concatenate([np.kron(eye, np.cos(a)), -np.kron(eye, np.sin(a))], axis=1), BF16)


def _shortconv_kernel(u_ref, w_ref, b_ref, o_ref):
    u = u_ref[...]
    n = u.shape[0]
    row = lax.broadcasted_iota(jnp.int32, u.shape, 0)
    prev = jnp.where(row == 0, 0.0, pltpu.roll(u, 1, axis=0))
    nxt = jnp.where(row == n - 1, 0.0, pltpu.roll(u, n - 1, axis=0))
    o_ref[...] = prev * w_ref[0:1, :] + u * w_ref[1:2, :] + nxt * w_ref[2:3, :] + b_ref[...]


def short_conv(u, w, bias):
    b, seq, ch = u.shape
    return pl.pallas_call(
        _shortconv_kernel,
        grid=(b, ch // LANES),
        in_specs=[
            pl.BlockSpec((None, seq, LANES), lambda i, c: (i, 0, c)),
            pl.BlockSpec((HY_SHORT, LANES), lambda i, c: (0, c)),
            pl.BlockSpec((1, LANES), lambda i, c: (0, c)),
        ],
        out_specs=pl.BlockSpec((None, seq, LANES), lambda i, c: (i, 0, c)),
        out_shape=jax.ShapeDtypeStruct((b, seq, ch), F32),
        compiler_params=_cparams(("parallel", "parallel")),
        name="short_conv",
    )(u, w.astype(F32), bias.reshape(1, ch).astype(F32))


def _filter_kernel(emb_ref, w1_ref, b1_ref, f1_ref, w2_ref, b2_ref, f2_ref, w3_ref, b3_ref, dec_ref, o_ref):
    z = jnp.dot(emb_ref[...].astype(BF16), w1_ref[...], preferred_element_type=F32) + b1_ref[...]
    z = jnp.sin(f1_ref[...] * z)
    z = jnp.dot(z.astype(BF16), w2_ref[...], preferred_element_type=F32) + b2_ref[...]
    z = jnp.sin(f2_ref[...] * z)
    h = jnp.dot(z.astype(BF16), w3_ref[...], preferred_element_type=F32) + b3_ref[...]
    o_ref[...] = h * dec_ref[...]


def hyena_filters(seq, hy_w1, hy_b1, hy_freq1, hy_w2, hy_b2, hy_freq2, hy_w3, hy_b3):
    t = jnp.linspace(0.0, 1.0, seq, dtype=F32)[:, None]
    ang = (2.0 * math.pi / seq) * jnp.arange(seq, dtype=F32)[:, None]
    bands = jnp.linspace(1e-4, HY_EMB_BANDS - 1, HY_EMB_BANDS, dtype=F32)[None, :]
    emb = jnp.concatenate([t, jnp.cos(bands * ang), -jnp.sin(bands * ang)], axis=-1)
    kdim = emb.shape[1]
    kpad = LANES - kdim
    emb = jnp.pad(emb, ((0, 0), (0, kpad)))
    w1 = jnp.pad(hy_w1, ((0, kpad), (0, 0))).astype(BF16)
    deltas = jnp.abs(jnp.linspace(math.log(HY_DECAY_TARGET) / HY_SLOW_DECAY,
                                  math.log(HY_DECAY_TARGET) / HY_FAST_DECAY, HY_WIDTH, dtype=F32))
    decay = jnp.tile(jnp.exp(-t * deltas), (1, 2 * HY_ORDER))
    fo = hy_w1.shape[1]
    nout = hy_w3.shape[1]
    tl = _tile(seq, 1024)
    row = lambda a: a.reshape(1, -1).astype(F32)
    full = lambda shape: pl.BlockSpec(shape, lambda i: (0, 0))
    return pl.pallas_call(
        _filter_kernel,
        grid=(seq // tl,),
        in_specs=[
            pl.BlockSpec((tl, LANES), lambda i: (i, 0)),
            full((LANES, fo)), full((1, fo)), full((1, fo)),
            full((fo, fo)), full((1, fo)), full((1, fo)),
            full((fo, nout)), full((1, nout)),
            pl.BlockSpec((tl, nout), lambda i: (i, 0)),
        ],
        out_specs=pl.BlockSpec((tl, nout), lambda i: (i, 0)),
        out_shape=jax.ShapeDtypeStruct((seq, nout), F32),
        compiler_params=_cparams(("parallel",)),
        name="hyena_filter",
    )(emb, w1, row(hy_b1), row(hy_freq1), hy_w2.astype(BF16), row(hy_b2), row(hy_freq2),
      hy_w3.astype(BF16), row(hy_b3), decay)


def hyena_spectra(seq, filt, tables):
    n2 = _dft_n2(seq)
    h = filt.reshape(seq, HY_ORDER, 2, HY_WIDTH)
    h_fwd, h_bwd = h[:, :, 0], h[:, :, 1]
    h_bwd = h_bwd.at[0].set(0.0)
    norm = jnp.sum(jnp.abs(h_fwd), axis=0) + jnp.sum(jnp.abs(h_bwd), axis=0)
    stacked = jnp.concatenate([h_fwd, h_bwd], axis=1).transpose(1, 0, 2)
    sp = dft_spectrum(stacked, tables)
    sf, sb = sp[:HY_ORDER], sp[HY_ORDER:]
    re = sf[:, :, :n2] + sb[:, :, :n2]
    im = sf[:, :, n2:] - sb[:, :, n2:]
    scale = (1.0 / (2 * seq)) / norm
    return (jnp.concatenate([re, im], axis=2) * scale[:, None, None, :]).astype(BF16)


def _merge_kernel(x_ref, yf_ref, yh_ref, ya_ref, g_ref, wf_ref, wh_ref, wa_ref, wo_ref,
                  gate_ref, ng_ref, sh_ref, sc_ref, xo_ref, ho_ref):
    d = x_ref.shape[-1]
    g = g_ref[0].astype(F32)
    yf = jnp.dot(yf_ref[0].astype(BF16), wf_ref[...], preferred_element_type=F32)
    yh = jnp.dot(yh_ref[0].astype(BF16), wh_ref[...], preferred_element_type=F32)
    ya = jnp.dot(ya_ref[0], wa_ref[...], preferred_element_type=F32)
    mix = g[:, 0:d] * yf + g[:, d:2 * d] * yh + g[:, 2 * d:3 * d] * ya
    x = x_ref[0] + gate_ref[0] * jnp.dot(mix.astype(BF16), wo_ref[...], preferred_element_type=F32)
    xo_ref[0] = x
    y = x * lax.rsqrt(jnp.mean(x * x, axis=-1, keepdims=True) + EPS) * ng_ref[...]
    ho_ref[0] = (y * (1.0 + sc_ref[0]) + sh_ref[0]).astype(ho_ref.dtype)


def merge_branches(x, yf, yh, ya, g, w_f, w_h, w_a, w_o, gate, norm_g, shift, scale):
    b, l, d = x.shape
    tl = _tile(l, 512)
    rows = lambda w: pl.BlockSpec((1, tl, w), lambda i, j: (i, j, 0))
    full = lambda a: pl.BlockSpec(a.shape, lambda i, j: (0, 0))
    per_b = pl.BlockSpec((1, 1, d), lambda i, j: (i, 0, 0))
    wf, wh, wa, wo = (w.astype(BF16) for w in (w_f, w_h, w_a, w_o))
    return pl.pallas_call(
        _merge_kernel,
        grid=(b, l // tl),
        in_specs=[rows(d), rows(yf.shape[-1]), rows(yh.shape[-1]), rows(ya.shape[-1]), rows(3 * d),
                  full(wf), full(wh), full(wa), full(wo),
                  per_b, pl.BlockSpec((1, d), lambda i, j: (0, 0)), per_b, per_b],
        out_specs=[rows(d), rows(d)],
        out_shape=[jax.ShapeDtypeStruct((b, l, d), F32), jax.ShapeDtypeStruct((b, l, d), BF16)],
        compiler_params=_cparams(("parallel", "parallel")),
        name="merge",
    )(x, yf, yh, ya, g, wf, wh, wa, wo, gate.reshape(b, 1, d), norm_g.reshape(1, d).astype(F32),
      shift.reshape(b, 1, d), scale.reshape(b, 1, d))


GLU_GROUP = 2 * LANES


def _glu_group_permutation():
    p = np.zeros((GLU_GROUP, GLU_GROUP), np.float32)
    j = np.arange(LANES)
    p[2 * j, j] = 1.0
    p[2 * j + 1, LANES + j] = 1.0
    return jnp.asarray(p, BF16)


def _moe_kernel(be_ref, act_ref, new_ref, rows_ref, w1_ref, b1_ref, w2_ref, b2_ref, p_ref, o_ref, w1s, w2s):
    i = pl.program_id(0)

    @pl.when(new_ref[i] > 0)
    def _():
        for q in range(w1s.shape[1] // GLU_GROUP):
            cols = slice(q * GLU_GROUP, (q + 1) * GLU_GROUP)
            w1s[:, cols] = jnp.dot(w1_ref[:, cols].astype(BF16), p_ref[...],
                                   preferred_element_type=F32).astype(BF16)
        w2s[...] = w2_ref[...].astype(BF16)

    @pl.when(act_ref[i] > 0)
    def _():
        u = jnp.dot(rows_ref[...], w1s[...], preferred_element_type=F32) + b1_ref[0]
        parts = []
        for q in range(u.shape[1] // GLU_GROUP):
            xg = jnp.minimum(u[:, q * GLU_GROUP:q * GLU_GROUP + LANES], SWIGLU_LIMIT)
            xl = jnp.clip(u[:, q * GLU_GROUP + LANES:(q + 1) * GLU_GROUP], -SWIGLU_LIMIT, SWIGLU_LIMIT)
            parts.append((xg * jax.nn.sigmoid(SWIGLU_ALPHA * xg) * (xl + 1.0)).astype(BF16))
        a = jnp.concatenate(parts, axis=1)
        y = jnp.dot(a, w2s[...], preferred_element_type=F32) + b2_ref[0]
        o_ref[...] = y.astype(o_ref.dtype)

    @pl.when(act_ref[i] == 0)
    def _():
        o_ref[...] = jnp.zeros(o_ref.shape, o_ref.dtype)


def moe_experts(rows, blk_exp, blk_act, blk_new, layer, w1_all, b1, w2_all, b2):
    r, d = rows.shape
    de = w2_all.shape[2]
    nblk = r // MOE_BLOCK
    grid_spec = pltpu.PrefetchScalarGridSpec(
        num_scalar_prefetch=3,
        grid=(nblk,),
        in_specs=[
            pl.BlockSpec((MOE_BLOCK, d), lambda i, be, act, new: (i, 0)),
            pl.BlockSpec((None, None, d, 2 * de), lambda i, be, act, new: (layer, be[i], 0, 0)),
            pl.BlockSpec((1, 1, 2 * de), lambda i, be, act, new: (be[i], 0, 0)),
            pl.BlockSpec((None, None, de, d), lambda i, be, act, new: (layer, be[i], 0, 0)),
            pl.BlockSpec((1, 1, d), lambda i, be, act, new: (be[i], 0, 0)),
            pl.BlockSpec((GLU_GROUP, GLU_GROUP), lambda i, be, act, new: (0, 0)),
        ],
        out_specs=pl.BlockSpec((MOE_BLOCK, d), lambda i, be, act, new: (i, 0)),
        scratch_shapes=[pltpu.VMEM((d, 2 * de), BF16), pltpu.VMEM((de, d), BF16)],
    )
    return pl.pallas_call(
        _moe_kernel,
        grid_spec=grid_spec,
        out_shape=jax.ShapeDtypeStruct((r, d), BF16),
        compiler_params=_cparams(("arbitrary",)),
        name="moe_experts",
    )(blk_exp, blk_act, blk_new, rows, w1_all, b1, w2_all, b2, _glu_group_permutation())


def _combine_kernel(y_ref, g_ref, x_ref, m_ref, o_ref):
    g = g_ref[...]
    acc = g[:, 0:1] * y_ref[0].astype(F32)
    for j in range(1, TOP_K):
        acc = acc + g[:, j:j + 1] * y_ref[j].astype(F32)
    o_ref[...] = x_ref[...] + m_ref[0] * acc


def moe_combine(y_sel, gate, resid, mod_blocks, tm):
    k, t, d = y_sel.shape
    return pl.pallas_call(
        _combine_kernel,
        grid=(t // tm,),
        in_specs=[pl.BlockSpec((k, tm, d), lambda i: (0, i, 0)),
                  pl.BlockSpec((tm, k), lambda i: (i, 0)),
                  pl.BlockSpec((tm, d), lambda i: (i, 0)),
                  pl.BlockSpec((1, 1, d), lambda i: (i, 0, 0))],
        out_specs=pl.BlockSpec((tm, d), lambda i: (i, 0)),
        out_shape=jax.ShapeDtypeStruct((t, d), F32),
        compiler_params=_cparams(("parallel",)),
        name="moe_combine",
    )(y_sel, gate, resid, mod_blocks)


def moe_ffn(h, resid, mod_blocks, tm, p):
    t_tok, d = h.shape
    wr = jnp.pad(p["w_router"], ((0, 0), (0, LANES - N_EXPERTS))).astype(BF16)
    br = jnp.pad(p["b_router"], (0, LANES - N_EXPERTS)).reshape(1, LANES).astype(F32)
    logits = matmul(h, wr, epi="bias", extra=(br,),
                    extra_specs=[pl.BlockSpec((1, LANES), lambda i, j: (0, 0))], name="mm_router")[:, :N_EXPERTS]
    top_v, top_i = lax.top_k(logits, TOP_K)
    gate = jax.nn.softmax(top_v, axis=-1)
    n_assign = t_tok * TOP_K
    flat_e = top_i.reshape(-1)
    experts = jnp.arange(N_EXPERTS, dtype=flat_e.dtype)[None, :]
    onehot = (flat_e[:, None] == experts).astype(jnp.int32)
    csum = jnp.cumsum(onehot, axis=0)
    counts = csum[-1]
    padded = (counts + MOE_BLOCK - 1) // MOE_BLOCK * MOE_BLOCK
    pad_end = jnp.cumsum(padded)
    pad_start = pad_end - padded
    dest = jnp.sum(onehot * (csum - 1 + pad_start[None, :]), axis=1)
    n_blocks = -(-n_assign // MOE_BLOCK) + N_EXPERTS
    n_rows = n_blocks * MOE_BLOCK
    row_tok = (jnp.arange(n_rows, dtype=jnp.int32) % t_tok).at[dest].set(
        jnp.arange(n_assign, dtype=jnp.int32) // TOP_K)
    blk_start = jnp.arange(n_blocks, dtype=jnp.int32) * MOE_BLOCK
    blk_exp = jnp.minimum(jnp.sum((blk_start[:, None] >= pad_end[None, :]).astype(jnp.int32), axis=1),
                          N_EXPERTS - 1)
    blk_act = (blk_start < pad_end[-1]).astype(jnp.int32)
    blk_new = jnp.concatenate([jnp.ones((1,), jnp.int32), (blk_exp[1:] != blk_exp[:-1]).astype(jnp.int32)])
    y_rows = moe_experts(h[row_tok], blk_exp, blk_act, blk_new, p["layer"], p["w1_all"], p["b1"],
                         p["w2_all"], p["b2"])
    y_sel = y_rows[dest.reshape(t_tok, TOP_K).T]
    return moe_combine(y_sel, gate, resid, mod_blocks, tm)


OFF_F = 0
OFF_HY = OFF_F + FN_WIDTH
OFF_Q = OFF_HY + (HY_ORDER + 1) * HY_WIDTH
OFF_K = OFF_Q + COL_QK
OFF_V = OFF_K + COL_QK
OFF_G = OFF_V + DA_WIDTH


def _token_mixer(x, h, mod_gate, norm2_g, mod_shift2, mod_scale2, p, lam, lam_init, rope, kv_extra, q_only_self):
    b, s, d = x.shape
    h2d = h.reshape(b * s, d)
    w = p["w_in"]
    cos, sin = rope

    w_fv = matmul(w[:, OFF_F:OFF_HY], fourier_channel_matrix(), out_dtype=BF16, name="mm_wfold")
    v_f = matmul(h2d, w_fv, name="mm_fproj").reshape(b, s, 2 * FN_WIDTH)
    y_f = seq_dft_real(v_f, FN_WIDTH, 1.0 / math.sqrt(s * FN_GROUP_DIM), _dft_tables_complex(s))

    z = matmul(h2d, w[:, OFF_HY:OFF_Q], name="mm_hproj").reshape(b, s, (HY_ORDER + 1) * HY_WIDTH)
    z = short_conv(z, p["hy_conv_w"], p["hy_conv_b"])
    tables = _dft_tables_real(s)
    filt = hyena_filters(s, p["hy_w1"], p["hy_b1"], p["hy_freq1"], p["hy_w2"], p["hy_b2"], p["hy_freq2"],
                         p["hy_w3"], p["hy_b3"])
    spec = hyena_spectra(s, filt, tables)
    cb = HY_WIDTH // LANES
    y_h = long_conv_gated(z, 0, z, cb, spec[0], p["hy_bias"][0], tables)
    y_h = long_conv_gated(y_h, 0, z, 2 * cb, spec[1], p["hy_bias"][1], tables)

    k_gain = p["k_norm_g"]
    k = qk_project(h2d, w[:, OFF_K:OFF_V], k_gain, cos, sin, s)
    v = matmul(h2d, w[:, OFF_V:OFF_G], out_dtype=BF16, name="mm_vproj")
    k3, v3 = k.reshape(b, s, COL_QK), v.reshape(b, s, DA_WIDTH)
    if q_only_self is None:
        y_a = None
    else:
        q = qk_project(h2d, w[:, OFF_Q:OFF_K], p["q_norm_g"] * (DA_QK_DIM ** -0.5 * math.log2(math.e)), cos, sin, s)
        if kv_extra is not None:
            k_all = jnp.concatenate([k3, kv_extra[0]], axis=1)
            v_all = jnp.concatenate([v3, kv_extra[1]], axis=1)
        else:
            k_all, v_all = k3, v3
        nk = k_all.shape[1]
        y_a = diff_attention(q, k_all.reshape(b * nk, COL_QK), v_all.reshape(b * nk, DA_WIDTH), lam,
                             p["subln_g"], 1.0 - lam_init, s, nk).reshape(b, s, DA_WIDTH)

    g = matmul(h2d, w[:, OFF_G:], out_dtype=BF16, epi="sigmoid", name="mm_gates").reshape(b, s, N_BRANCHES * d)
    x_new, h2 = merge_branches(x, y_f, y_h, y_a, g, p["w_f"], p["w_h"], p["w_a"], p["w_o"],
                               mod_gate, norm2_g, mod_shift2, mod_scale2)
    return x_new, h2, (k3, v3)


def _context_kv(h, p, rope):
    b, s, d = h.shape
    h2d = h.reshape(b * s, d)
    w = p["w_in"]
    k = qk_project(h2d, w[:, OFF_K:OFF_V], p["k_norm_g"], rope[0], rope[1], s)
    v = matmul(h2d, w[:, OFF_V:OFF_G], out_dtype=BF16, name="mm_vproj_ctx")
    return k.reshape(b, s, COL_QK), v.reshape(b, s, DA_WIDTH)


def _layer(l, x, xc, c, c_ctx, p, ctx_out):
    b, n_lat, d = x.shape
    n_ctx = xc.shape[1]
    lam_init = 0.8 - 0.6 * math.exp(-0.3 * l)
    lam = (jnp.exp(jnp.sum(p["lam_q"][0] * p["lam_k"][0]).astype(F32))
           - jnp.exp(jnp.sum(p["lam_q"][1] * p["lam_k"][1]).astype(F32)) + lam_init)

    cond = jnp.concatenate([c, c_ctx[None, :], jnp.zeros((16 - b - 1, d), F32)], axis=0)
    mod_all = matmul(jax.nn.silu(cond).astype(BF16), p["w_mod"].astype(BF16), epi="bias",
                     extra=(p["b_mod"].reshape(1, 6 * d).astype(F32),),
                     extra_specs=[pl.BlockSpec((1, 1024), lambda i, j: (0, j))], name="mm_mod")
    mod = [mod_all[:b, i * d:(i + 1) * d] for i in range(6)]
    mod_c = [jnp.broadcast_to(mod_all[b, i * d:(i + 1) * d], (b, d)) for i in range(6)]

    pw = dict(p)
    pw["w_in"] = p["w_in"].astype(BF16)

    no_rope = (jnp.ones((n_ctx, LANES), F32), jnp.zeros((n_ctx, LANES), F32))
    hc = normmod(xc, p["norm1_g"], mod_c[0], mod_c[1])
    h = normmod(x, p["norm1_g"], mod[0], mod[1])
    if ctx_out:
        xc_new, h2c, kv_c = _token_mixer(xc, hc, mod_c[2], p["norm2_g"], mod_c[3], mod_c[4], pw, lam, lam_init,
                                         no_rope, None, True)
    else:
        kv_c = _context_kv(hc, pw, no_rope)
    x_new, h2, _ = _token_mixer(x, h, mod[2], p["norm2_g"], mod[3], mod[4], pw, lam, lam_init,
                                rope_tables(n_lat), kv_c, True)

    n_exp, two_f = p["b_e1"].shape
    b1 = p["b_e1"].reshape(n_exp, two_f // GLU_GROUP, LANES, 2).transpose(0, 1, 3, 2).reshape(n_exp, 1, two_f)
    pe = {
        "w_router": p["w_router"], "b_router": p["b_router"], "layer": l,
        "w1_all": p["w_e1_all"], "b1": b1, "w2_all": p["w_e2_all"], "b2": p["b_e2"][:, None, :],
    }
    tm = 512
    lat_mod = jnp.repeat(mod[5], n_lat // tm, axis=0)
    if ctx_out:
        h_all = jnp.concatenate([h2c.reshape(b * n_ctx, d), h2.reshape(b * n_lat, d)], axis=0)
        resid = jnp.concatenate([xc_new.reshape(b * n_ctx, d), x_new.reshape(b * n_lat, d)], axis=0)
        mod_blocks = jnp.concatenate([jnp.tile(mod_c[5][:1], (b * n_ctx // tm, 1)), lat_mod], axis=0)
        out = moe_ffn(h_all, resid, mod_blocks[:, None, :], tm, pe)
        xc = out[:b * n_ctx].reshape(b, n_ctx, d)
        x = out[b * n_ctx:].reshape(b, n_lat, d)
    else:
        x = moe_ffn(h2.reshape(b * n_lat, d), x_new.reshape(b * n_lat, d), lat_mod[:, None, :], tm, pe)
        x = x.reshape(b, n_lat, d)
    return x, xc


_PARAM_NAMES = ("w_mod", "b_mod", "norm1_g", "norm2_g", "w_in", "hy_conv_w", "hy_conv_b", "hy_w1", "hy_b1",
                "hy_freq1", "hy_w2", "hy_b2", "hy_freq2", "hy_w3", "hy_b3", "hy_bias", "q_norm_g", "k_norm_g",
                "lam_q", "lam_k", "subln_g", "w_f", "w_h", "w_a", "w_o", "w_router", "b_router",
                "w_e1", "b_e1", "w_e2", "b_e2")


def kernel(x, c, ctx, c_ctx, w_mod, b_mod, norm1_g, norm2_g, w_in, hy_conv_w, hy_conv_b, hy_w1, hy_b1, hy_freq1,
           hy_w2, hy_b2, hy_freq2, hy_w3, hy_b3, hy_bias, q_norm_g, k_norm_g, lam_q, lam_k, subln_g, w_f, w_h,
           w_a, w_o, w_router, b_router, w_e1, b_e1, w_e2, b_e2):
    stacked = (w_mod, b_mod, norm1_g, norm2_g, w_in, hy_conv_w, hy_conv_b, hy_w1, hy_b1, hy_freq1, hy_w2, hy_b2,
               hy_freq2, hy_w3, hy_b3, hy_bias, q_norm_g, k_norm_g, lam_q, lam_k, subln_g, w_f, w_h, w_a, w_o,
               w_router, b_router, w_e1, b_e1, w_e2, b_e2)
    depth = w_mod.shape[0]
    xc = ctx
    for l in range(depth):
        p = {name: arr[l] for name, arr in zip(_PARAM_NAMES, stacked) if name not in ("w_e1", "w_e2")}
        p["w_e1_all"], p["w_e2_all"] = w_e1, w_e2
        x, xc = _layer(l, x, xc, c, c_ctx, p, l < depth - 1)
    return x
```

```python
import functools
import math

import numpy as np
import jax
import jax.numpy as jnp
from jax import lax
from jax.experimental import pallas as pl
from jax.experimental.pallas import tpu as pltpu

F32 = jnp.float32
BF16 = jnp.bfloat16

LANES = 128
VMEM_LIMIT = 56 * 1024 * 1024

GRID_W = 64
EPS = 1e-6
SUBLN_EPS = 1e-5
FN_GROUPS = 4
FN_GROUP_DIM = 64
FN_WIDTH = FN_GROUPS * FN_GROUP_DIM
HY_WIDTH = 256
HY_ORDER = 2
HY_SHORT = 3
HY_EMB_BANDS = 16
HY_DECAY_TARGET = 1e-2
HY_FAST_DECAY = 0.3
HY_SLOW_DECAY = 1.5
DA_HEADS = 4
DA_QK_DIM = 64
DA_V_DIM = 2 * DA_QK_DIM
DA_WIDTH = DA_HEADS * DA_V_DIM
ROPE_BASE = 10000.0
N_BRANCHES = 3
COL_QK = DA_HEADS * 2 * DA_QK_DIM
N_EXPERTS = 32
TOP_K = 4
SWIGLU_ALPHA = 1.702
SWIGLU_LIMIT = 7.0
MOE_BLOCK = 512
PROJ_GATE_CHUNK = 1024
DFT_MIN_N1 = 16
DFT_UNROLL = 8


def _dft_n2(seq):
    return min(LANES, seq // DFT_MIN_N1)


def _cparams(sem):
    return pltpu.CompilerParams(dimension_semantics=sem, vmem_limit_bytes=VMEM_LIMIT)


def _tile(n, pref):
    if n <= pref:
        return n
    for t in range(pref, 7, -1):
        if n % t == 0 and t % 8 == 0:
            return t
    return n


def _const_spec(shape):
    nd = len(shape)
    return pl.BlockSpec(shape, lambda *_: (0,) * nd, pipeline_mode=pl.Buffered(1))


def _qk_epilogue(acc, gm, gain, cos, sin):
    ms = jnp.dot((acc * acc).astype(BF16), gm, preferred_element_type=F32)
    y = acc * lax.rsqrt(ms + EPS) * gain
    n = y.shape[1]
    reps = n // LANES
    lane = lax.broadcasted_iota(jnp.int32, y.shape, 1)
    is_a = (lane % (DA_QK_DIM // 2)) < (DA_QK_DIM // 4)
    half = DA_QK_DIM // 4
    swapped = jnp.where(is_a, pltpu.roll(y, n - half, axis=1), pltpu.roll(y, half, axis=1))
    return y * jnp.tile(cos, (1, reps)) + swapped * jnp.tile(sin, (1, reps))


def _proj_kernel(x_ref, ng_ref, sh_ref, sc_ref, w_ref, gm_ref, qg_ref, kg_ref, cos_ref, sin_ref,
                 vf_ref, z_ref, q_ref, k_ref, v_ref, g_ref, *, cols):
    x = x_ref[0]
    y = x * lax.rsqrt(jnp.mean(x * x, axis=-1, keepdims=True) + EPS) * ng_ref[...]
    h = (y * (1.0 + sc_ref[0]) + sh_ref[0]).astype(BF16)

    def mm(name):
        a, b = cols[name]
        return jnp.dot(h, w_ref[:, a:b], preferred_element_type=F32)

    vf_ref[0] = mm("f")
    z_ref[0] = mm("hy")
    q_ref[0] = _qk_epilogue(mm("q"), gm_ref[...], qg_ref[...], cos_ref[...], sin_ref[...]).astype(q_ref.dtype)
    k_ref[0] = _qk_epilogue(mm("k"), gm_ref[...], kg_ref[...], cos_ref[...], sin_ref[...]).astype(k_ref.dtype)
    v_ref[0] = mm("v").astype(v_ref.dtype)
    a, b = cols["g"]
    for c0 in range(a, b, PROJ_GATE_CHUNK):
        acc = jnp.dot(h, w_ref[:, c0:c0 + PROJ_GATE_CHUNK], preferred_element_type=F32)
        g_ref[0, :, c0 - a:c0 - a + PROJ_GATE_CHUNK] = jax.nn.sigmoid(acc).astype(g_ref.dtype)


def project_all(x, norm_g, shift, scale, w_cat, cols, q_gain, k_gain, cos, sin):
    b, s, d = x.shape
    tm = _tile(s, 512)
    width = {name: stop - start for name, (start, stop) in cols.items()}
    gm = _group_mean_matrix(COL_QK, DA_QK_DIM)
    tile_gain = lambda g: jnp.tile(g.astype(F32), COL_QK // DA_QK_DIM).reshape(1, COL_QK)
    rows = lambda w: pl.BlockSpec((1, tm, w), lambda i, j: (i, j, 0))
    per_b = pl.BlockSpec((1, 1, d), lambda i, j: (i, 0, 0))
    full = lambda shape: pl.BlockSpec(shape, lambda i, j: (0, 0))
    outs = [("f", F32), ("hy", F32), ("q", BF16), ("k", BF16), ("v", BF16), ("g", BF16)]
    return pl.pallas_call(
        functools.partial(_proj_kernel, cols=cols),
        grid=(b, s // tm),
        in_specs=[rows(d), full((1, d)), per_b, per_b, _const_spec(w_cat.shape), full(gm.shape),
                  full((1, COL_QK)), full((1, COL_QK)),
                  pl.BlockSpec((tm, LANES), lambda i, j: (j, 0)), pl.BlockSpec((tm, LANES), lambda i, j: (j, 0))],
        out_specs=[rows(width[name]) for name, _ in outs],
        out_shape=[jax.ShapeDtypeStruct((b, s, width[name]), dt) for name, dt in outs],
        compiler_params=_cparams(("parallel", "parallel")),
        name="project_all",
    )(x, norm_g.reshape(1, d).astype(F32), shift.reshape(b, 1, d), scale.reshape(b, 1, d), w_cat, gm,
      tile_gain(q_gain), tile_gain(k_gain), cos, sin)


def _mm_kernel(a_ref, w_ref, *rest, epi):
    acc = jnp.dot(a_ref[...], w_ref[...], preferred_element_type=F32)
    if epi == "plain":
        (o_ref,) = rest
    elif epi == "bias":
        b_ref, o_ref = rest
        acc = acc + b_ref[...]
    else:
        raise ValueError(epi)
    o_ref[...] = acc.astype(o_ref.dtype)


def matmul(a, w, *, out_dtype=F32, epi="plain", extra=(), extra_specs=(), tm=512, tn=1024, name="mm"):
    m, k = a.shape
    k2, n = w.shape
    assert k == k2
    tm = _tile(m, tm)
    tn = _tile(n, tn)
    return pl.pallas_call(
        functools.partial(_mm_kernel, epi=epi),
        grid=(m // tm, n // tn),
        in_specs=[
            pl.BlockSpec((tm, k), lambda i, j: (i, 0)),
            pl.BlockSpec((k, tn), lambda i, j: (0, j)),
            *extra_specs,
        ],
        out_specs=pl.BlockSpec((tm, tn), lambda i, j: (i, j)),
        out_shape=jax.ShapeDtypeStruct((m, n), out_dtype),
        compiler_params=_cparams(("parallel", "parallel")),
        name=name,
    )(a, w, *extra)


def _group_mean_matrix(n, group):
    idx = np.arange(n)
    return jnp.asarray((idx[:, None] // group == idx[None, :] // group).astype(np.float32) / group, BF16)


def rope_tables(n_lat):
    rows = n_lat // GRID_W
    row = np.repeat(np.arange(rows), GRID_W).astype(np.float64)
    col = np.tile(np.arange(GRID_W), rows).astype(np.float64)
    n_freq = DA_QK_DIM // 4
    inv = ROPE_BASE ** (-np.arange(n_freq, dtype=np.float64) / n_freq)
    ang_r = row[:, None] * inv
    ang_c = col[:, None] * inv
    cos = np.concatenate([np.cos(ang_r), np.cos(ang_r), np.cos(ang_c), np.cos(ang_c)], axis=1)
    sin = np.concatenate([-np.sin(ang_r), np.sin(ang_r), -np.sin(ang_c), np.sin(ang_c)], axis=1)
    cos = np.tile(cos, (1, LANES // DA_QK_DIM))
    sin = np.tile(sin, (1, LANES // DA_QK_DIM))
    return jnp.asarray(cos, F32), jnp.asarray(sin, F32)


ATTN_TQ = 512
ATTN_TK = 768


def _attn_kernel(lam_ref, q_ref, k_ref, v_ref, g_ref, o_ref, qs_ref, s_ref, m_ref, acc_ref, *, tq, tk, nkc, out_scale):
    q = q_ref[...]
    lane = lax.broadcasted_iota(jnp.int32, q.shape, 1)
    zero = jnp.zeros_like(q)
    qs_ref[0:tq, :] = jnp.where(lane < DA_QK_DIM, q, zero)
    qs_ref[tq:2 * tq, :] = jnp.where(lane >= DA_QK_DIM, q, zero)
    m_ref[...] = jnp.full(m_ref.shape, -jnp.inf, F32)
    acc_ref[...] = jnp.zeros(acc_ref.shape, F32)

    def scores(j, slot):
        kj = k_ref[pl.ds(pl.multiple_of(j * tk, tk), tk), :]
        s_ref[slot] = lax.dot_general(qs_ref[...], kj, (((1,), (1,)), ((), ())), preferred_element_type=F32)

    def update(j, slot):
        s = s_ref[slot]
        vj = v_ref[pl.ds(pl.multiple_of(j * tk, tk), tk), :]
        m_prev = m_ref[...]
        m_next = jnp.maximum(m_prev, jnp.max(s, axis=1, keepdims=True))
        p = jnp.exp2(s - jnp.tile(m_next, (1, tk // LANES)))
        alpha = jnp.exp2(m_prev - m_next)
        m_ref[...] = m_next
        acc_ref[...] = acc_ref[...] * jnp.tile(alpha, (1, 2)) + jnp.dot(p.astype(BF16), vj,
                                                                          preferred_element_type=F32)

    scores(0, 0)
    pairs = (nkc - 1) // 2

    def body(i, c):
        j = 2 * i
        scores(j + 1, 1)
        update(j, 0)
        scores(j + 2, 0)
        update(j + 1, 1)
        return c

    lax.fori_loop(0, pairs, body, 0)
    if nkc % 2 == 0:
        scores(nkc - 1, 1)
        update(nkc - 2, 0)
        update(nkc - 1, 1)
    else:
        update(nkc - 1, 0)

    o1 = acc_ref[0:tq, 0:DA_V_DIM] / acc_ref[0:tq, DA_V_DIM:]
    o2 = acc_ref[tq:2 * tq, 0:DA_V_DIM] / acc_ref[tq:2 * tq, DA_V_DIM:]
    o = o1 - lam_ref[0, 0] * o2
    o = o * lax.rsqrt(jnp.mean(o * o, axis=-1, keepdims=True) + SUBLN_EPS)
    o_ref[...] = (o * g_ref[...] * out_scale).astype(o_ref.dtype)


def diff_attention(q, k, v, lam, subln_g, out_scale, nq, nk):
    b = q.shape[0] // nq
    tq = _tile(nq, ATTN_TQ)
    tk = next(t for t in (ATTN_TK, 256, 128) if nk % t == 0)
    nqb, nkc = nq // tq, nk // tk
    v_ext = jnp.concatenate([v.reshape(b * nk, DA_HEADS, DA_V_DIM),
                             jnp.ones((b * nk, DA_HEADS, DA_V_DIM), v.dtype)], axis=2)
    v_ext = v_ext.reshape(b * nk, 2 * DA_WIDTH)
    kern = functools.partial(_attn_kernel, tq=tq, tk=tk, nkc=nkc, out_scale=out_scale)
    return pl.pallas_call(
        kern,
        grid=(b, DA_HEADS, nqb),
        in_specs=[
            pl.BlockSpec(memory_space=pltpu.SMEM),
            pl.BlockSpec((tq, DA_V_DIM), lambda bi, h, qi: (bi * nqb + qi, h)),
            pl.BlockSpec((nk, DA_V_DIM), lambda bi, h, qi: (bi, h)),
            pl.BlockSpec((nk, 2 * DA_V_DIM), lambda bi, h, qi: (bi, h)),
            pl.BlockSpec((1, DA_V_DIM), lambda bi, h, qi: (0, 0)),
        ],
        out_specs=pl.BlockSpec((tq, DA_V_DIM), lambda bi, h, qi: (bi * nqb + qi, h)),
        out_shape=jax.ShapeDtypeStruct((b * nq, DA_WIDTH), BF16),
        scratch_shapes=[
            pltpu.VMEM((2 * tq, DA_V_DIM), BF16),
            pltpu.VMEM((2, 2 * tq, tk), F32),
            pltpu.VMEM((2 * tq, LANES), F32),
            pltpu.VMEM((2 * tq, 2 * DA_V_DIM), F32),
        ],
        compiler_params=_cparams(("parallel", "parallel", "parallel")),
        name="diff_attn",
    )(lam.reshape(1, 1).astype(F32), q, k, v_ext, subln_g.reshape(1, DA_V_DIM).astype(F32))


def _dft_tables_real(seq):
    n2 = _dft_n2(seq)
    n1h = seq // n2
    n1 = 2 * n1h
    n = n1 * n2
    k1 = np.arange(n1, dtype=np.float64)[None, :, None]
    nn = (n2 * np.arange(n1h, dtype=np.float64)[None, None, :] + np.arange(n2, dtype=np.float64)[:, None, None])
    ang = 2.0 * np.pi * k1 * nn / n
    e_fwd = np.concatenate([np.cos(ang), -np.sin(ang)], axis=1)
    e_inv = np.transpose(e_fwd, (0, 2, 1))
    a2 = 2.0 * np.pi * np.outer(np.arange(n2), np.arange(n2)) / n2
    c, s = np.cos(a2), np.sin(a2)
    f_fwd = np.block([[c, s], [-s, c]])
    f_inv = np.block([[c, -s], [s, c]])
    return tuple(jnp.asarray(t, BF16) for t in (e_fwd, f_fwd, f_inv, e_inv))


def _pack_complex(re, im):
    r = lax.bitcast_convert_type(re.astype(BF16).astype(F32), jnp.uint32)
    i = lax.bitcast_convert_type(im.astype(BF16).astype(F32), jnp.uint32)
    return r | (i >> 16)


def _unpack_complex(w):
    re = lax.bitcast_convert_type(w & jnp.uint32(0xFFFF0000), F32)
    im = lax.bitcast_convert_type(w << 16, F32)
    return jnp.concatenate([re, im], axis=0).astype(BF16)


def _spectrum_kernel(u_ref, ef_ref, ff_ref, o_ref, scr, *, n1, n1h, n2, kc):
    kk = pl.program_id(2)

    @pl.when(kk == 0)
    def _():
        def stage1(j, c):
            x = u_ref[pl.ds(j, n1h, stride=n2), :].astype(BF16)
            a = jnp.dot(ef_ref[j], x, preferred_element_type=F32)
            scr[pl.ds(pl.multiple_of(j * n1, n1), n1), :] = _pack_complex(a[:n1], a[n1:])
            return c

        lax.fori_loop(0, n2, stage1, 0, unroll=DFT_UNROLL)

    def stage2(t, c):
        a = _unpack_complex(scr[pl.ds(kk * kc + t, n2, stride=n1), :])
        o_ref[t] = jnp.dot(ff_ref[...], a, preferred_element_type=F32)
        return c

    lax.fori_loop(0, kc, stage2, 0, unroll=DFT_UNROLL)


def dft_spectrum(h, tables):
    s, seq, ch = h.shape
    n2 = _dft_n2(seq)
    n1h = seq // n2
    n1 = 2 * n1h
    e_fwd, f_fwd, _, _ = tables
    kc = min(n1, 16)
    kern = functools.partial(_spectrum_kernel, n1=n1, n1h=n1h, n2=n2, kc=kc)
    return pl.pallas_call(
        kern,
        grid=(s, ch // LANES, n1 // kc),
        in_specs=[
            pl.BlockSpec((None, seq, LANES), lambda i, c, k: (i, 0, c)),
            _const_spec(e_fwd.shape),
            _const_spec(f_fwd.shape),
        ],
        out_specs=pl.BlockSpec((None, kc, 2 * n2, LANES), lambda i, c, k: (i, k, 0, c)),
        out_shape=jax.ShapeDtypeStruct((s, n1, 2 * n2, ch), F32),
        scratch_shapes=[pltpu.VMEM((n1 * n2, LANES), jnp.uint32)],
        compiler_params=_cparams(("parallel", "parallel", "arbitrary")),
        name="dft_spectrum",
    )(h, e_fwd, f_fwd)


def _longconv_kernel(u_ref, g_ref, h_ref, bias_ref, ef_ref, ff_ref, fi_ref, ei_ref, o_ref, scr_a, scr_b,
                     *, n1, n1h, n2):
    def stage1(j, c):
        x = u_ref[pl.ds(j, n1h, stride=n2), :].astype(BF16)
        a = jnp.dot(ef_ref[j], x, preferred_element_type=F32)
        scr_a[pl.ds(pl.multiple_of(j * n1, n1), n1), :] = _pack_complex(a[:n1], a[n1:])
        return c

    lax.fori_loop(0, n2, stage1, 0, unroll=DFT_UNROLL)

    def stage2(k1, c):
        a = _unpack_complex(scr_a[pl.ds(k1, n2, stride=n1), :])
        y = jnp.dot(ff_ref[...], a, preferred_element_type=F32)
        hk = h_ref[k1].astype(F32)
        yr, yi = y[:n2], y[n2:]
        hr, hi = hk[:n2], hk[n2:]
        z = jnp.concatenate([yr * hr - yi * hi, yr * hi + yi * hr], axis=0).astype(BF16)
        bk = jnp.dot(fi_ref[...], z, preferred_element_type=F32)
        scr_b[pl.ds(pl.multiple_of(k1 * n2, n2), n2), :] = _pack_complex(bk[:n2], bk[n2:])
        return c

    lax.fori_loop(0, n1, stage2, 0, unroll=DFT_UNROLL)

    def stage3(j, c):
        bmat = _unpack_complex(scr_b[pl.ds(j, n1, stride=n2), :])
        y = jnp.dot(ei_ref[j], bmat, preferred_element_type=F32)
        u = u_ref[pl.ds(j, n1h, stride=n2), :]
        g = g_ref[pl.ds(j, n1h, stride=n2), :]
        o_ref[pl.ds(j, n1h, stride=n2), :] = g * (y + u * bias_ref[...])
        return c

    lax.fori_loop(0, n2, stage3, 0, unroll=DFT_UNROLL)


def long_conv_gated(u, u_blk, g, g_blk, spec, bias, tables):
    b, seq, _ = u.shape
    ch = bias.shape[0]
    n2 = _dft_n2(seq)
    n1h = seq // n2
    n1 = 2 * n1h
    e_fwd, f_fwd, f_inv, e_inv = tables
    kern = functools.partial(_longconv_kernel, n1=n1, n1h=n1h, n2=n2)
    one = pl.Buffered(1)
    return pl.pallas_call(
        kern,
        grid=(ch // LANES, b),
        in_specs=[
            pl.BlockSpec((None, seq, LANES), lambda c, i: (i, 0, u_blk + c), pipeline_mode=one),
            pl.BlockSpec((None, seq, LANES), lambda c, i: (i, 0, g_blk + c), pipeline_mode=one),
            pl.BlockSpec((n1, 2 * n2, LANES), lambda c, i: (0, 0, c), pipeline_mode=one),
            pl.BlockSpec((1, LANES), lambda c, i: (0, c)),
            _const_spec(e_fwd.shape),
            _const_spec(f_fwd.shape),
            _const_spec(f_inv.shape),
            _const_spec(e_inv.shape),
        ],
        out_specs=pl.BlockSpec((None, seq, LANES), lambda c, i: (i, 0, c)),
        out_shape=jax.ShapeDtypeStruct((b, seq, ch), F32),
        scratch_shapes=[pltpu.VMEM((n1 * n2, LANES), jnp.uint32), pltpu.VMEM((n1 * n2, LANES), jnp.uint32)],
        compiler_params=_cparams(("parallel", "parallel")),
        name="long_conv",
    )(u, g, spec, bias.reshape(1, ch).astype(F32), e_fwd, f_fwd, f_inv, e_inv)


def _dft_tables_complex(seq):
    n2 = _dft_n2(seq)
    n1 = seq // n2
    k1 = np.arange(n1, dtype=np.float64)[None, :, None]
    nn = (n2 * np.arange(n1, dtype=np.float64)[None, None, :] + np.arange(n2, dtype=np.float64)[:, None, None])
    ang = 2.0 * np.pi * k1 * nn / seq
    c, s = np.cos(ang), np.sin(ang)
    e_fwd = np.concatenate([np.concatenate([c, s], axis=2), np.concatenate([-s, c], axis=2)], axis=1)
    a2 = 2.0 * np.pi * np.outer(np.arange(n2), np.arange(n2)) / n2
    f_re = np.concatenate([np.cos(a2), np.sin(a2)], axis=1)
    return jnp.asarray(e_fwd, BF16), jnp.asarray(f_re, BF16)


def _seqdft_kernel(vr_ref, vi_ref, ef_ref, fr_ref, o_ref, scr, *, n1, n2, scale):
    def stage1(j, c):
        x = jnp.concatenate([vr_ref[pl.ds(j, n1, stride=n2), :], vi_ref[pl.ds(j, n1, stride=n2), :]], axis=0)
        a = jnp.dot(ef_ref[j], x.astype(BF16), preferred_element_type=F32)
        scr[pl.ds(pl.multiple_of(j * n1, n1), n1), :] = _pack_complex(a[:n1], a[n1:])
        return c

    lax.fori_loop(0, n2, stage1, 0, unroll=DFT_UNROLL)

    def stage2(k1, c):
        a = _unpack_complex(scr[pl.ds(k1, n2, stride=n1), :])
        o_ref[pl.ds(k1, n2, stride=n1), :] = jnp.dot(fr_ref[...], a, preferred_element_type=F32) * scale
        return c

    lax.fori_loop(0, n1, stage2, 0, unroll=DFT_UNROLL)


def seq_dft_real(v, ch, scale, tables):
    b, seq, _ = v.shape
    n2 = _dft_n2(seq)
    n1 = seq // n2
    e_fwd, f_re = tables
    nblk = ch // LANES
    kern = functools.partial(_seqdft_kernel, n1=n1, n2=n2, scale=scale)
    return pl.pallas_call(
        kern,
        grid=(nblk, b),
        in_specs=[
            pl.BlockSpec((None, seq, LANES), lambda c, i: (i, 0, c)),
            pl.BlockSpec((None, seq, LANES), lambda c, i: (i, 0, nblk + c)),
            _const_spec(e_fwd.shape),
            _const_spec(f_re.shape),
        ],
        out_specs=pl.BlockSpec((None, seq, LANES), lambda c, i: (i, 0, c)),
        out_shape=jax.ShapeDtypeStruct((b, seq, ch), F32),
        scratch_shapes=[pltpu.VMEM((n1 * n2, LANES), jnp.uint32)],
        compiler_params=_cparams(("parallel", "parallel")),
        name="seq_dft",
    )(v, v, e_fwd, f_re)


def fourier_channel_matrix():
    a = 2.0 * np.pi * np.outer(np.arange(FN_GROUP_DIM), np.arange(FN_GROUP_DIM)) / FN_GROUP_DIM
    eye = np.eye(FN_GROUPS)
    return jnp.asarray(np.concatenate([np.kron(eye, np.cos(a)), -np.kron(eye, np.sin(a))], axis=1), BF16)


def _shortconv_kernel(u_ref, w_ref, b_ref, o_ref):
    u = u_ref[...]
    n = u.shape[0]
    row = lax.broadcasted_iota(jnp.int32, u.shape, 0)
    prev = jnp.where(row == 0, 0.0, pltpu.roll(u, 1, axis=0))
    nxt = jnp.where(row == n - 1, 0.0, pltpu.roll(u, n - 1, axis=0))
    o_ref[...] = prev * w_ref[0:1, :] + u * w_ref[1:2, :] + nxt * w_ref[2:3, :] + b_ref[...]


def short_conv(u, w, bias):
    b, seq, ch = u.shape
    return pl.pallas_call(
        _shortconv_kernel,
        grid=(b, ch // LANES),
        in_specs=[
            pl.BlockSpec((None, seq, LANES), lambda i, c: (i, 0, c)),
            pl.BlockSpec((HY_SHORT, LANES), lambda i, c: (0, c)),
            pl.BlockSpec((1, LANES), lambda i, c: (0, c)),
        ],
        out_specs=pl.BlockSpec((None, seq, LANES), lambda i, c: (i, 0, c)),
        out_shape=jax.ShapeDtypeStruct((b, seq, ch), F32),
        compiler_params=_cparams(("parallel", "parallel")),
        name="short_conv",
    )(u, w.astype(F32), bias.reshape(1, ch).astype(F32))


def _filter_kernel(emb_ref, w1_ref, b1_ref, f1_ref, w2_ref, b2_ref, f2_ref, w3_ref, b3_ref, dec_ref, o_ref):
    z = jnp.dot(emb_ref[...].astype(BF16), w1_ref[...], preferred_element_type=F32) + b1_ref[...]
    z = jnp.sin(f1_ref[...] * z)
    z = jnp.dot(z.astype(BF16), w2_ref[...], preferred_element_type=F32) + b2_ref[...]
    z = jnp.sin(f2_ref[...] * z)
    h = jnp.dot(z.astype(BF16), w3_ref[...], preferred_element_type=F32) + b3_ref[...]
    o_ref[...] = h * dec_ref[...]


def hyena_filters(seq, hy_w1, hy_b1, hy_freq1, hy_w2, hy_b2, hy_freq2, hy_w3, hy_b3):
    t = jnp.linspace(0.0, 1.0, seq, dtype=F32)[:, None]
    ang = (2.0 * math.pi / seq) * jnp.arange(seq, dtype=F32)[:, None]
    bands = jnp.linspace(1e-4, HY_EMB_BANDS - 1, HY_EMB_BANDS, dtype=F32)[None, :]
    emb = jnp.concatenate([t, jnp.cos(bands * ang), -jnp.sin(bands * ang)], axis=-1)
    kdim = emb.shape[1]
    kpad = LANES - kdim
    emb = jnp.pad(emb, ((0, 0), (0, kpad)))
    w1 = jnp.pad(hy_w1, ((0, kpad), (0, 0))).astype(BF16)
    deltas = jnp.abs(jnp.linspace(math.log(HY_DECAY_TARGET) / HY_SLOW_DECAY,
                                  math.log(HY_DECAY_TARGET) / HY_FAST_DECAY, HY_WIDTH, dtype=F32))
    decay = jnp.tile(jnp.exp(-t * deltas), (1, 2 * HY_ORDER))
    fo = hy_w1.shape[1]
    nout = hy_w3.shape[1]
    tl = _tile(seq, 1024)
    row = lambda a: a.reshape(1, -1).astype(F32)
    full = lambda shape: pl.BlockSpec(shape, lambda i: (0, 0))
    return pl.pallas_call(
        _filter_kernel,
        grid=(seq // tl,),
        in_specs=[
            pl.BlockSpec((tl, LANES), lambda i: (i, 0)),
            full((LANES, fo)), full((1, fo)), full((1, fo)),
            full((fo, fo)), full((1, fo)), full((1, fo)),
            full((fo, nout)), full((1, nout)),
            pl.BlockSpec((tl, nout), lambda i: (i, 0)),
        ],
        out_specs=pl.BlockSpec((tl, nout), lambda i: (i, 0)),
        out_shape=jax.ShapeDtypeStruct((seq, nout), F32),
        compiler_params=_cparams(("parallel",)),
        name="hyena_filter",
    )(emb, w1, row(hy_b1), row(hy_freq1), hy_w2.astype(BF16), row(hy_b2), row(hy_freq2),
      hy_w3.astype(BF16), row(hy_b3), decay)


def hyena_spectra(seq, filt, tables):
    n2 = _dft_n2(seq)
    h = filt.reshape(seq, HY_ORDER, 2, HY_WIDTH)
    h_fwd, h_bwd = h[:, :, 0], h[:, :, 1]
    h_bwd = h_bwd.at[0].set(0.0)
    norm = jnp.sum(jnp.abs(h_fwd), axis=0) + jnp.sum(jnp.abs(h_bwd), axis=0)
    stacked = jnp.concatenate([h_fwd, h_bwd], axis=1).transpose(1, 0, 2)
    sp = dft_spectrum(stacked, tables)
    sf, sb = sp[:HY_ORDER], sp[HY_ORDER:]
    re = sf[:, :, :n2] + sb[:, :, :n2]
    im = sf[:, :, n2:] - sb[:, :, n2:]
    scale = (1.0 / (2 * seq)) / norm
    return (jnp.concatenate([re, im], axis=2) * scale[:, None, None, :]).astype(BF16)


def _merge_kernel(x_ref, yf_ref, yh_ref, ya_ref, g_ref, wf_ref, wh_ref, wa_ref, wo_ref,
                  gate_ref, ng_ref, sh_ref, sc_ref, xo_ref, ho_ref):
    d = x_ref.shape[-1]
    g = g_ref[0].astype(F32)
    yf = jnp.dot(yf_ref[0].astype(BF16), wf_ref[...], preferred_element_type=F32)
    yh = jnp.dot(yh_ref[0].astype(BF16), wh_ref[...], preferred_element_type=F32)
    ya = jnp.dot(ya_ref[0], wa_ref[...], preferred_element_type=F32)
    mix = g[:, 0:d] * yf + g[:, d:2 * d] * yh + g[:, 2 * d:3 * d] * ya
    x = x_ref[0] + gate_ref[0] * jnp.dot(mix.astype(BF16), wo_ref[...], preferred_element_type=F32)
    xo_ref[0] = x
    y = x * lax.rsqrt(jnp.mean(x * x, axis=-1, keepdims=True) + EPS) * ng_ref[...]
    ho_ref[0] = (y * (1.0 + sc_ref[0]) + sh_ref[0]).astype(ho_ref.dtype)


def merge_branches(x, yf, yh, ya, g, w_f, w_h, w_a, w_o, gate, norm_g, shift, scale):
    b, l, d = x.shape
    tl = _tile(l, 512)
    rows = lambda w: pl.BlockSpec((1, tl, w), lambda i, j: (i, j, 0))
    full = lambda a: pl.BlockSpec(a.shape, lambda i, j: (0, 0))
    per_b = pl.BlockSpec((1, 1, d), lambda i, j: (i, 0, 0))
    wf, wh, wa, wo = (w.astype(BF16) for w in (w_f, w_h, w_a, w_o))
    return pl.pallas_call(
        _merge_kernel,
        grid=(b, l // tl),
        in_specs=[rows(d), rows(yf.shape[-1]), rows(yh.shape[-1]), rows(ya.shape[-1]), rows(3 * d),
                  full(wf), full(wh), full(wa), full(wo),
                  per_b, pl.BlockSpec((1, d), lambda i, j: (0, 0)), per_b, per_b],
        out_specs=[rows(d), rows(d)],
        out_shape=[jax.ShapeDtypeStruct((b, l, d), F32), jax.ShapeDtypeStruct((b, l, d), BF16)],
        compiler_params=_cparams(("parallel", "parallel")),
        name="merge",
    )(x, yf, yh, ya, g, wf, wh, wa, wo, gate.reshape(b, 1, d), norm_g.reshape(1, d).astype(F32),
      shift.reshape(b, 1, d), scale.reshape(b, 1, d))


GLU_GROUP = 2 * LANES


def _glu_group_permutation():
    p = np.zeros((GLU_GROUP, GLU_GROUP), np.float32)
    j = np.arange(LANES)
    p[2 * j, j] = 1.0
    p[2 * j + 1, LANES + j] = 1.0
    return jnp.asarray(p, BF16)


def _moe_kernel(be_ref, act_ref, new_ref, rows_ref, w1_ref, b1_ref, w2_ref, b2_ref, p_ref, o_ref, w1s, w2s):
    i = pl.program_id(0)

    @pl.when(new_ref[i] > 0)
    def _():
        for q in range(w1s.shape[1] // GLU_GROUP):
            cols = slice(q * GLU_GROUP, (q + 1) * GLU_GROUP)
            w1s[:, cols] = jnp.dot(w1_ref[:, cols].astype(BF16), p_ref[...],
                                   preferred_element_type=F32).astype(BF16)
        w2s[...] = w2_ref[...].astype(BF16)

    @pl.when(act_ref[i] > 0)
    def _():
        u = jnp.dot(rows_ref[...], w1s[...], preferred_element_type=F32) + b1_ref[0]
        parts = []
        for q in range(u.shape[1] // GLU_GROUP):
            xg = jnp.minimum(u[:, q * GLU_GROUP:q * GLU_GROUP + LANES], SWIGLU_LIMIT)
            xl = jnp.clip(u[:, q * GLU_GROUP + LANES:(q + 1) * GLU_GROUP], -SWIGLU_LIMIT, SWIGLU_LIMIT)
            parts.append((xg * jax.nn.sigmoid(SWIGLU_ALPHA * xg) * (xl + 1.0)).astype(BF16))
        a = jnp.concatenate(parts, axis=1)
        y = jnp.dot(a, w2s[...], preferred_element_type=F32) + b2_ref[0]
        o_ref[...] = y.astype(o_ref.dtype)

    @pl.when(act_ref[i] == 0)
    def _():
        o_ref[...] = jnp.zeros(o_ref.shape, o_ref.dtype)


def moe_experts(rows, blk_exp, blk_act, blk_new, layer, w1_all, b1, w2_all, b2):
    r, d = rows.shape
    de = w2_all.shape[2]
    nblk = r // MOE_BLOCK
    grid_spec = pltpu.PrefetchScalarGridSpec(
        num_scalar_prefetch=3,
        grid=(nblk,),
        in_specs=[
            pl.BlockSpec((MOE_BLOCK, d), lambda i, be, act, new: (i, 0)),
            pl.BlockSpec((None, None, d, 2 * de), lambda i, be, act, new: (layer, be[i], 0, 0)),
            pl.BlockSpec((1, 1, 2 * de), lambda i, be, act, new: (be[i], 0, 0)),
            pl.BlockSpec((None, None, de, d), lambda i, be, act, new: (layer, be[i], 0, 0)),
            pl.BlockSpec((1, 1, d), lambda i, be, act, new: (be[i], 0, 0)),
            pl.BlockSpec((GLU_GROUP, GLU_GROUP), lambda i, be, act, new: (0, 0)),
        ],
        out_specs=pl.BlockSpec((MOE_BLOCK, d), lambda i, be, act, new: (i, 0)),
        scratch_shapes=[pltpu.VMEM((d, 2 * de), BF16), pltpu.VMEM((de, d), BF16)],
    )
    return pl.pallas_call(
        _moe_kernel,
        grid_spec=grid_spec,
        out_shape=jax.ShapeDtypeStruct((r, d), BF16),
        compiler_params=_cparams(("arbitrary",)),
        name="moe_experts",
    )(blk_exp, blk_act, blk_new, rows, w1_all, b1, w2_all, b2, _glu_group_permutation())


def _combine_kernel(y_ref, g_ref, x_ref, m_ref, o_ref):
    g = g_ref[...]
    acc = g[:, 0:1] * y_ref[0].astype(F32)
    for j in range(1, TOP_K):
        acc = acc + g[:, j:j + 1] * y_ref[j].astype(F32)
    o_ref[...] = x_ref[...] + m_ref[0] * acc


def moe_combine(y_sel, gate, resid, mod_blocks, tm):
    k, t, d = y_sel.shape
    return pl.pallas_call(
        _combine_kernel,
        grid=(t // tm,),
        in_specs=[pl.BlockSpec((k, tm, d), lambda i: (0, i, 0)),
                  pl.BlockSpec((tm, k), lambda i: (i, 0)),
                  pl.BlockSpec((tm, d), lambda i: (i, 0)),
                  pl.BlockSpec((1, 1, d), lambda i: (i, 0, 0))],
        out_specs=pl.BlockSpec((tm, d), lambda i: (i, 0)),
        out_shape=jax.ShapeDtypeStruct((t, d), F32),
        compiler_params=_cparams(("parallel",)),
        name="moe_combine",
    )(y_sel, gate, resid, mod_blocks)


def moe_ffn(h, resid, mod_blocks, tm, p):
    t_tok, d = h.shape
    wr = jnp.pad(p["w_router"], ((0, 0), (0, LANES - N_EXPERTS))).astype(BF16)
    br = jnp.pad(p["b_router"], (0, LANES - N_EXPERTS)).reshape(1, LANES).astype(F32)
    logits = matmul(h, wr, epi="bias", extra=(br,),
                    extra_specs=[pl.BlockSpec((1, LANES), lambda i, j: (0, 0))], name="mm_router")[:, :N_EXPERTS]
    top_v, top_i = lax.top_k(logits, TOP_K)
    gate = jax.nn.softmax(top_v, axis=-1)
    n_assign = t_tok * TOP_K
    flat_e = top_i.reshape(-1)
    experts = jnp.arange(N_EXPERTS, dtype=flat_e.dtype)[None, :]
    onehot = (flat_e[:, None] == experts).astype(jnp.int32)
    csum = jnp.cumsum(onehot, axis=0)
    counts = csum[-1]
    padded = (counts + MOE_BLOCK - 1) // MOE_BLOCK * MOE_BLOCK
    pad_end = jnp.cumsum(padded)
    pad_start = pad_end - padded
    dest = jnp.sum(onehot * (csum - 1 + pad_start[None, :]), axis=1)
    n_blocks = -(-n_assign // MOE_BLOCK) + N_EXPERTS
    n_rows = n_blocks * MOE_BLOCK
    row_tok = (jnp.arange(n_rows, dtype=jnp.int32) % t_tok).at[dest].set(
        jnp.arange(n_assign, dtype=jnp.int32) // TOP_K)
    blk_start = jnp.arange(n_blocks, dtype=jnp.int32) * MOE_BLOCK
    blk_exp = jnp.minimum(jnp.sum((blk_start[:, None] >= pad_end[None, :]).astype(jnp.int32), axis=1),
                          N_EXPERTS - 1)
    blk_act = (blk_start < pad_end[-1]).astype(jnp.int32)
    blk_new = jnp.concatenate([jnp.ones((1,), jnp.int32), (blk_exp[1:] != blk_exp[:-1]).astype(jnp.int32)])
    y_rows = moe_experts(h[row_tok], blk_exp, blk_act, blk_new, p["layer"], p["w1_all"], p["b1"],
                         p["w2_all"], p["b2"])
    y_sel = y_rows[dest.reshape(t_tok, TOP_K).T]
    return moe_combine(y_sel, gate, resid, mod_blocks, tm)


OFF_F = 0
OFF_HY = OFF_F + FN_WIDTH
OFF_Q = OFF_HY + (HY_ORDER + 1) * HY_WIDTH
OFF_K = OFF_Q + COL_QK
OFF_V = OFF_K + COL_QK
OFF_G = OFF_V + DA_WIDTH


def _projection_weights(w_in):
    w = w_in.astype(BF16)
    w_fv = matmul(w[:, OFF_F:OFF_HY], fourier_channel_matrix(), out_dtype=BF16, name="mm_wfold")
    w_cat = jnp.concatenate([w_fv, w[:, OFF_HY:]], axis=1)
    shift = w_fv.shape[1] - (OFF_HY - OFF_F)
    bounds = {"hy": (OFF_HY, OFF_Q), "q": (OFF_Q, OFF_K), "k": (OFF_K, OFF_V), "v": (OFF_V, OFF_G),
              "g": (OFF_G, w_in.shape[1])}
    cols = {"f": (0, w_fv.shape[1])}
    cols.update({name: (a + shift, b + shift) for name, (a, b) in bounds.items()})
    return w_cat, cols


def _project(x, mod_shift, mod_scale, p, rope):
    q_gain = p["q_norm_g"] * (DA_QK_DIM ** -0.5 * math.log2(math.e))
    return project_all(x, p["norm1_g"], mod_shift, mod_scale, p["w_cat"], p["cols"], q_gain, p["k_norm_g"],
                       rope[0], rope[1])


def _token_mixer(x, mod_shift1, mod_scale1, mod_gate, mod_shift2, mod_scale2, p, lam, lam_init, rope, kv_extra):
    b, s, d = x.shape
    v_f, z, q, k3, v3, g = _project(x, mod_shift1, mod_scale1, p, rope)

    y_f = seq_dft_real(v_f, FN_WIDTH, 1.0 / math.sqrt(s * FN_GROUP_DIM), _dft_tables_complex(s))

    z = short_conv(z, p["hy_conv_w"], p["hy_conv_b"])
    tables = _dft_tables_real(s)
    filt = hyena_filters(s, p["hy_w1"], p["hy_b1"], p["hy_freq1"], p["hy_w2"], p["hy_b2"], p["hy_freq2"],
                         p["hy_w3"], p["hy_b3"])
    spec = hyena_spectra(s, filt, tables)
    cb = HY_WIDTH // LANES
    y_h = long_conv_gated(z, 0, z, cb, spec[0], p["hy_bias"][0], tables)
    y_h = long_conv_gated(y_h, 0, z, 2 * cb, spec[1], p["hy_bias"][1], tables)

    if kv_extra is not None:
        k_all = jnp.concatenate([k3, kv_extra[0]], axis=1)
        v_all = jnp.concatenate([v3, kv_extra[1]], axis=1)
    else:
        k_all, v_all = k3, v3
    nk = k_all.shape[1]
    y_a = diff_attention(q.reshape(b * s, COL_QK), k_all.reshape(b * nk, COL_QK), v_all.reshape(b * nk, DA_WIDTH),
                         lam, p["subln_g"], 1.0 - lam_init, s, nk).reshape(b, s, DA_WIDTH)

    x_new, h2 = merge_branches(x, y_f, y_h, y_a, g, p["w_f"], p["w_h"], p["w_a"], p["w_o"],
                               mod_gate, p["norm2_g"], mod_shift2, mod_scale2)
    return x_new, h2, (k3, v3)


def _layer(l, x, xc, c, c_ctx, p, ctx_out):
    b, n_lat, d = x.shape
    n_ctx = xc.shape[1]
    lam_init = 0.8 - 0.6 * math.exp(-0.3 * l)
    lam = (jnp.exp(jnp.sum(p["lam_q"][0] * p["lam_k"][0]).astype(F32))
           - jnp.exp(jnp.sum(p["lam_q"][1] * p["lam_k"][1]).astype(F32)) + lam_init)

    cond = jnp.concatenate([c, c_ctx[None, :], jnp.zeros((16 - b - 1, d), F32)], axis=0)
    mod_all = matmul(jax.nn.silu(cond).astype(BF16), p["w_mod"].astype(BF16), epi="bias",
                     extra=(p["b_mod"].reshape(1, 6 * d).astype(F32),),
                     extra_specs=[pl.BlockSpec((1, 1024), lambda i, j: (0, j))], name="mm_mod")
    mod = [mod_all[:b, i * d:(i + 1) * d] for i in range(6)]
    mod_c = [jnp.broadcast_to(mod_all[b, i * d:(i + 1) * d], (b, d)) for i in range(6)]

    pw = dict(p)
    pw["w_cat"], pw["cols"] = _projection_weights(p["w_in"])

    no_rope = (jnp.ones((n_ctx, LANES), F32), jnp.zeros((n_ctx, LANES), F32))
    if ctx_out:
        xc_new, h2c, kv_c = _token_mixer(xc, mod_c[0], mod_c[1], mod_c[2], mod_c[3], mod_c[4], pw, lam, lam_init,
                                         no_rope, None)
    else:
        kv_c = _project(xc, mod_c[0], mod_c[1], pw, no_rope)[3:5]
    x_new, h2, _ = _token_mixer(x, mod[0], mod[1], mod[2], mod[3], mod[4], pw, lam, lam_init,
                                rope_tables(n_lat), kv_c)

    n_exp, two_f = p["b_e1"].shape
    b1 = p["b_e1"].reshape(n_exp, two_f // GLU_GROUP, LANES, 2).transpose(0, 1, 3, 2).reshape(n_exp, 1, two_f)
    pe = {
        "w_router": p["w_router"], "b_router": p["b_router"], "layer": l,
        "w1_all": p["w_e1_all"], "b1": b1, "w2_all": p["w_e2_all"], "b2": p["b_e2"][:, None, :],
    }
    tm = 512
    lat_mod = jnp.repeat(mod[5], n_lat // tm, axis=0)
    if ctx_out:
        h_all = jnp.concatenate([h2c.reshape(b * n_ctx, d), h2.reshape(b * n_lat, d)], axis=0)
        resid = jnp.concatenate([xc_new.reshape(b * n_ctx, d), x_new.reshape(b * n_lat, d)], axis=0)
        mod_blocks = jnp.concatenate([jnp.tile(mod_c[5][:1], (b * n_ctx // tm, 1)), lat_mod], axis=0)
        out = moe_ffn(h_all, resid, mod_blocks[:, None, :], tm, pe)
        xc = out[:b * n_ctx].reshape(b, n_ctx, d)
        x = out[b * n_ctx:].reshape(b, n_lat, d)
    else:
        x = moe_ffn(h2.reshape(b * n_lat, d), x_new.reshape(b * n_lat, d), lat_mod[:, None, :], tm, pe)
        x = x.reshape(b, n_lat, d)
    return x, xc


_PARAM_NAMES = ("w_mod", "b_mod", "norm1_g", "norm2_g", "w_in", "hy_conv_w", "hy_conv_b", "hy_w1", "hy_b1",
                "hy_freq1", "hy_w2", "hy_b2", "hy_freq2", "hy_w3", "hy_b3", "hy_bias", "q_norm_g", "k_norm_g",
                "lam_q", "lam_k", "subln_g", "w_f", "w_h", "w_a", "w_o", "w_router", "b_router",
                "w_e1", "b_e1", "w_e2", "b_e2")


def kernel(x, c, ctx, c_ctx, w_mod, b_mod, norm1_g, norm2_g, w_in, hy_conv_w, hy_conv_b, hy_w1, hy_b1, hy_freq1,
           hy_w2, hy_b2, hy_freq2, hy_w3, hy_b3, hy_bias, q_norm_g, k_norm_g, lam_q, lam_k, subln_g, w_f, w_h,
           w_a, w_o, w_router, b_router, w_e1, b_e1, w_e2, b_e2):
    stacked = (w_mod, b_mod, norm1_g, norm2_g, w_in, hy_conv_w, hy_conv_b, hy_w1, hy_b1, hy_freq1, hy_w2, hy_b2,
               hy_freq2, hy_w3, hy_b3, hy_bias, q_norm_g, k_norm_g, lam_q, lam_k, subln_g, w_f, w_h, w_a, w_o,
               w_router, b_router, w_e1, b_e1, w_e2, b_e2)
    depth = w_mod.shape[0]
    xc = ctx
    for l in range(depth):
        p = {name: arr[l] for name, arr in zip(_PARAM_NAMES, stacked) if name not in ("w_e1", "w_e2")}
        p["w_e1_all"], p["w_e2_all"] = w_e1, w_e2
        x, xc = _layer(l, x, xc, c, c_ctx, p, l < depth - 1)
    return x
```

```python
import functools
import math

import numpy as np
import jax
import jax.numpy as jnp
from jax import lax
from jax.experimental import pallas as pl
from jax.experimental.pallas import tpu as pltpu

F32 = jnp.float32
BF16 = jnp.bfloat16

LANES = 128
VMEM_LIMIT = 56 * 1024 * 1024

GRID_W = 64
EPS = 1e-6
SUBLN_EPS = 1e-5
FN_GROUPS = 4
FN_GROUP_DIM = 64
FN_WIDTH = FN_GROUPS * FN_GROUP_DIM
HY_WIDTH = 256
HY_ORDER = 2
HY_SHORT = 3
HY_EMB_BANDS = 16
HY_DECAY_TARGET = 1e-2
HY_FAST_DECAY = 0.3
HY_SLOW_DECAY = 1.5
DA_HEADS = 4
DA_QK_DIM = 64
DA_V_DIM = 2 * DA_QK_DIM
DA_WIDTH = DA_HEADS * DA_V_DIM
ROPE_BASE = 10000.0
N_BRANCHES = 3
COL_QK = DA_HEADS * 2 * DA_QK_DIM
N_EXPERTS = 32
TOP_K = 4
SWIGLU_ALPHA = 1.702
SWIGLU_LIMIT = 7.0
MOE_BLOCK = 512
PROJ_GATE_CHUNK = 1024
DFT_MIN_N1 = 16
DFT_UNROLL = 8


def _dft_n2(seq):
    return min(LANES, seq // DFT_MIN_N1)


def _cparams(sem):
    return pltpu.CompilerParams(dimension_semantics=sem, vmem_limit_bytes=VMEM_LIMIT)


def _tile(n, pref):
    if n <= pref:
        return n
    for t in range(pref, 7, -1):
        if n % t == 0 and t % 8 == 0:
            return t
    return n


def _const_spec(shape):
    nd = len(shape)
    return pl.BlockSpec(shape, lambda *_: (0,) * nd, pipeline_mode=pl.Buffered(1))


def _qk_epilogue(acc, gm, gain, cos, sin):
    ms = jnp.dot((acc * acc).astype(BF16), gm, preferred_element_type=F32)
    y = acc * lax.rsqrt(ms + EPS) * gain
    n = y.shape[1]
    reps = n // LANES
    lane = lax.broadcasted_iota(jnp.int32, y.shape, 1)
    is_a = (lane % (DA_QK_DIM // 2)) < (DA_QK_DIM // 4)
    half = DA_QK_DIM // 4
    swapped = jnp.where(is_a, pltpu.roll(y, n - half, axis=1), pltpu.roll(y, half, axis=1))
    return y * jnp.tile(cos, (1, reps)) + swapped * jnp.tile(sin, (1, reps))


def _proj_kernel(x_ref, ng_ref, sh_ref, sc_ref, w_ref, gm_ref, qg_ref, kg_ref, cos_ref, sin_ref,
                 vf_ref, z_ref, q_ref, k_ref, v_ref, g_ref, *, cols):
    x = x_ref[0]
    y = x * lax.rsqrt(jnp.mean(x * x, axis=-1, keepdims=True) + EPS) * ng_ref[...]
    h = (y * (1.0 + sc_ref[0]) + sh_ref[0]).astype(BF16)

    def mm(name):
        a, b = cols[name]
        return jnp.dot(h, w_ref[:, a:b], preferred_element_type=F32)

    vf_ref[0] = mm("f")
    z_ref[0] = mm("hy")
    q_ref[0] = _qk_epilogue(mm("q"), gm_ref[...], qg_ref[...], cos_ref[...], sin_ref[...]).astype(q_ref.dtype)
    k_ref[0] = _qk_epilogue(mm("k"), gm_ref[...], kg_ref[...], cos_ref[...], sin_ref[...]).astype(k_ref.dtype)
    v_ref[0] = mm("v").astype(v_ref.dtype)
    a, b = cols["g"]
    for c0 in range(a, b, PROJ_GATE_CHUNK):
        acc = jnp.dot(h, w_ref[:, c0:c0 + PROJ_GATE_CHUNK], preferred_element_type=F32)
        g_ref[0, :, c0 - a:c0 - a + PROJ_GATE_CHUNK] = jax.nn.sigmoid(acc).astype(g_ref.dtype)


def project_all(x, norm_g, shift, scale, w_cat, cols, q_gain, k_gain, cos, sin):
    b, s, d = x.shape
    tm = _tile(s, 512)
    width = {name: stop - start for name, (start, stop) in cols.items()}
    gm = _group_mean_matrix(COL_QK, DA_QK_DIM)
    tile_gain = lambda g: jnp.tile(g.astype(F32), COL_QK // DA_QK_DIM).reshape(1, COL_QK)
    rows = lambda w: pl.BlockSpec((1, tm, w), lambda i, j: (i, j, 0))
    per_b = pl.BlockSpec((1, 1, d), lambda i, j: (i, 0, 0))
    full = lambda shape: pl.BlockSpec(shape, lambda i, j: (0, 0))
    outs = [("f", F32), ("hy", F32), ("q", BF16), ("k", BF16), ("v", BF16), ("g", BF16)]
    return pl.pallas_call(
        functools.partial(_proj_kernel, cols=cols),
        grid=(b, s // tm),
        in_specs=[rows(d), full((1, d)), per_b, per_b, _const_spec(w_cat.shape), full(gm.shape),
                  full((1, COL_QK)), full((1, COL_QK)),
                  pl.BlockSpec((tm, LANES), lambda i, j: (j, 0)), pl.BlockSpec((tm, LANES), lambda i, j: (j, 0))],
        out_specs=[rows(width[name]) for name, _ in outs],
        out_shape=[jax.ShapeDtypeStruct((b, s, width[name]), dt) for name, dt in outs],
        compiler_params=_cparams(("parallel", "parallel")),
        name="project_all",
    )(x, norm_g.reshape(1, d).astype(F32), shift.reshape(b, 1, d), scale.reshape(b, 1, d), w_cat, gm,
      tile_gain(q_gain), tile_gain(k_gain), cos, sin)


def _mm_kernel(a_ref, w_ref, *rest, epi):
    acc = jnp.dot(a_ref[...], w_ref[...], preferred_element_type=F32)
    if epi == "plain":
        (o_ref,) = rest
    elif epi == "bias":
        b_ref, o_ref = rest
        acc = acc + b_ref[...]
    else:
        raise ValueError(epi)
    o_ref[...] = acc.astype(o_ref.dtype)


def matmul(a, w, *, out_dtype=F32, epi="plain", extra=(), extra_specs=(), tm=512, tn=1024, name="mm"):
    m, k = a.shape
    k2, n = w.shape
    assert k == k2
    tm = _tile(m, tm)
    tn = _tile(n, tn)
    return pl.pallas_call(
        functools.partial(_mm_kernel, epi=epi),
        grid=(m // tm, n // tn),
        in_specs=[
            pl.BlockSpec((tm, k), lambda i, j: (i, 0)),
            pl.BlockSpec((k, tn), lambda i, j: (0, j)),
            *extra_specs,
        ],
        out_specs=pl.BlockSpec((tm, tn), lambda i, j: (i, j)),
        out_shape=jax.ShapeDtypeStruct((m, n), out_dtype),
        compiler_params=_cparams(("parallel", "parallel")),
        name=name,
    )(a, w, *extra)


def _group_mean_matrix(n, group):
    idx = np.arange(n)
    return jnp.asarray((idx[:, None] // group == idx[None, :] // group).astype(np.float32) / group, BF16)


def rope_tables(n_lat):
    rows = n_lat // GRID_W
    row = np.repeat(np.arange(rows), GRID_W).astype(np.float64)
    col = np.tile(np.arange(GRID_W), rows).astype(np.float64)
    n_freq = DA_QK_DIM // 4
    inv = ROPE_BASE ** (-np.arange(n_freq, dtype=np.float64) / n_freq)
    ang_r = row[:, None] * inv
    ang_c = col[:, None] * inv
    cos = np.concatenate([np.cos(ang_r), np.cos(ang_r), np.cos(ang_c), np.cos(ang_c)], axis=1)
    sin = np.concatenate([-np.sin(ang_r), np.sin(ang_r), -np.sin(ang_c), np.sin(ang_c)], axis=1)
    cos = np.tile(cos, (1, LANES // DA_QK_DIM))
    sin = np.tile(sin, (1, LANES // DA_QK_DIM))
    return jnp.asarray(cos, F32), jnp.asarray(sin, F32)


ATTN_TQ = 512
ATTN_TK = 768


def _attn_kernel(lam_ref, q_ref, k_ref, v_ref, g_ref, o_ref, qs_ref, s_ref, m_ref, acc_ref, *, tq, tk, nkc, out_scale):
    q = q_ref[...]
    lane = lax.broadcasted_iota(jnp.int32, q.shape, 1)
    zero = jnp.zeros_like(q)
    qs_ref[0:tq, :] = jnp.where(lane < DA_QK_DIM, q, zero)
    qs_ref[tq:2 * tq, :] = jnp.where(lane >= DA_QK_DIM, q, zero)
    m_ref[...] = jnp.full(m_ref.shape, -jnp.inf, F32)
    acc_ref[...] = jnp.zeros(acc_ref.shape, F32)

    def scores(j, slot):
        kj = k_ref[pl.ds(pl.multiple_of(j * tk, tk), tk), :]
        s_ref[slot] = lax.dot_general(qs_ref[...], kj, (((1,), (1,)), ((), ())), preferred_element_type=F32)

    def update(j, slot):
        s = s_ref[slot]
        vj = v_ref[pl.ds(pl.multiple_of(j * tk, tk), tk), :]
        m_prev = m_ref[...]
        m_next = jnp.maximum(m_prev, jnp.max(s, axis=1, keepdims=True))
        p = jnp.exp2(s - jnp.tile(m_next, (1, tk // LANES)))
        alpha = jnp.exp2(m_prev - m_next)
        m_ref[...] = m_next
        acc_ref[...] = acc_ref[...] * jnp.tile(alpha, (1, 2)) + jnp.dot(p.astype(BF16), vj,
                                                                          preferred_element_type=F32)

    scores(0, 0)
    for j in range(nkc):
        if j + 1 < nkc:
            scores(j + 1, (j + 1) % 2)
        update(j, j % 2)

    o1 = acc_ref[0:tq, 0:DA_V_DIM] / acc_ref[0:tq, DA_V_DIM:]
    o2 = acc_ref[tq:2 * tq, 0:DA_V_DIM] / acc_ref[tq:2 * tq, DA_V_DIM:]
    o = o1 - lam_ref[0, 0] * o2
    o = o * lax.rsqrt(jnp.mean(o * o, axis=-1, keepdims=True) + SUBLN_EPS)
    o_ref[...] = (o * g_ref[...] * out_scale).astype(o_ref.dtype)


def diff_attention(q, k, v, lam, subln_g, out_scale, nq, nk):
    b = q.shape[0] // nq
    tq = _tile(nq, ATTN_TQ)
    tk = next(t for t in (ATTN_TK, 256, 128) if nk % t == 0)
    nqb, nkc = nq // tq, nk // tk
    v_ext = jnp.concatenate([v.reshape(b * nk, DA_HEADS, DA_V_DIM),
                             jnp.ones((b * nk, DA_HEADS, DA_V_DIM), v.dtype)], axis=2)
    v_ext = v_ext.reshape(b * nk, 2 * DA_WIDTH)
    kern = functools.partial(_attn_kernel, tq=tq, tk=tk, nkc=nkc, out_scale=out_scale)
    return pl.pallas_call(
        kern,
        grid=(b, DA_HEADS, nqb),
        in_specs=[
            pl.BlockSpec(memory_space=pltpu.SMEM),
            pl.BlockSpec((tq, DA_V_DIM), lambda bi, h, qi: (bi * nqb + qi, h)),
            pl.BlockSpec((nk, DA_V_DIM), lambda bi, h, qi: (bi, h)),
            pl.BlockSpec((nk, 2 * DA_V_DIM), lambda bi, h, qi: (bi, h)),
            pl.BlockSpec((1, DA_V_DIM), lambda bi, h, qi: (0, 0)),
        ],
        out_specs=pl.BlockSpec((tq, DA_V_DIM), lambda bi, h, qi: (bi * nqb + qi, h)),
        out_shape=jax.ShapeDtypeStruct((b * nq, DA_WIDTH), BF16),
        scratch_shapes=[
            pltpu.VMEM((2 * tq, DA_V_DIM), BF16),
            pltpu.VMEM((2, 2 * tq, tk), F32),
            pltpu.VMEM((2 * tq, LANES), F32),
            pltpu.VMEM((2 * tq, 2 * DA_V_DIM), F32),
        ],
        compiler_params=_cparams(("parallel", "parallel", "parallel")),
        name="diff_attn",
    )(lam.reshape(1, 1).astype(F32), q, k, v_ext, subln_g.reshape(1, DA_V_DIM).astype(F32))


def _dft_tables_real(seq):
    n2 = _dft_n2(seq)
    n1h = seq // n2
    n1 = 2 * n1h
    n = n1 * n2
    k1 = np.arange(n1, dtype=np.float64)[None, :, None]
    nn = (n2 * np.arange(n1h, dtype=np.float64)[None, None, :] + np.arange(n2, dtype=np.float64)[:, None, None])
    ang = 2.0 * np.pi * k1 * nn / n
    e_fwd = np.concatenate([np.cos(ang), -np.sin(ang)], axis=1)
    e_inv = np.transpose(e_fwd, (0, 2, 1))
    a2 = 2.0 * np.pi * np.outer(np.arange(n2), np.arange(n2)) / n2
    c, s = np.cos(a2), np.sin(a2)
    f_fwd = np.block([[c, s], [-s, c]])
    f_inv = np.block([[c, -s], [s, c]])
    return tuple(jnp.asarray(t, BF16) for t in (e_fwd, f_fwd, f_inv, e_inv))


def _pack_complex(re, im):
    r = lax.bitcast_convert_type(re.astype(BF16).astype(F32), jnp.uint32)
    i = lax.bitcast_convert_type(im.astype(BF16).astype(F32), jnp.uint32)
    return r | (i >> 16)


def _unpack_complex(w):
    re = lax.bitcast_convert_type(w & jnp.uint32(0xFFFF0000), F32)
    im = lax.bitcast_convert_type(w << 16, F32)
    return jnp.concatenate([re, im], axis=0).astype(BF16)


def _spectrum_kernel(u_ref, ef_ref, ff_ref, o_ref, scr, *, n1, n1h, n2, kc):
    kk = pl.program_id(2)

    @pl.when(kk == 0)
    def _():
        def stage1(j, c):
            x = u_ref[pl.ds(j, n1h, stride=n2), :].astype(BF16)
            a = jnp.dot(ef_ref[j], x, preferred_element_type=F32)
            scr[pl.ds(pl.multiple_of(j * n1, n1), n1), :] = _pack_complex(a[:n1], a[n1:])
            return c

        lax.fori_loop(0, n2, stage1, 0, unroll=DFT_UNROLL)

    def stage2(t, c):
        a = _unpack_complex(scr[pl.ds(kk * kc + t, n2, stride=n1), :])
        o_ref[t] = jnp.dot(ff_ref[...], a, preferred_element_type=F32)
        return c

    lax.fori_loop(0, kc, stage2, 0, unroll=DFT_UNROLL)


def dft_spectrum(h, tables):
    s, seq, ch = h.shape
    n2 = _dft_n2(seq)
    n1h = seq // n2
    n1 = 2 * n1h
    e_fwd, f_fwd, _, _ = tables
    kc = min(n1, 16)
    kern = functools.partial(_spectrum_kernel, n1=n1, n1h=n1h, n2=n2, kc=kc)
    return pl.pallas_call(
        kern,
        grid=(s, ch // LANES, n1 // kc),
        in_specs=[
            pl.BlockSpec((None, seq, LANES), lambda i, c, k: (i, 0, c)),
            _const_spec(e_fwd.shape),
            _const_spec(f_fwd.shape),
        ],
        out_specs=pl.BlockSpec((None, kc, 2 * n2, LANES), lambda i, c, k: (i, k, 0, c)),
        out_shape=jax.ShapeDtypeStruct((s, n1, 2 * n2, ch), F32),
        scratch_shapes=[pltpu.VMEM((n1 * n2, LANES), jnp.uint32)],
        compiler_params=_cparams(("parallel", "parallel", "arbitrary")),
        name="dft_spectrum",
    )(h, e_fwd, f_fwd)


def _longconv_kernel(u_ref, g_ref, h_ref, bias_ref, ef_ref, ff_ref, fi_ref, ei_ref, o_ref, scr_a, scr_b,
                     *, n1, n1h, n2):
    def stage1(j, c):
        x = u_ref[pl.ds(j, n1h, stride=n2), :].astype(BF16)
        a = jnp.dot(ef_ref[j], x, preferred_element_type=F32)
        scr_a[pl.ds(pl.multiple_of(j * n1, n1), n1), :] = _pack_complex(a[:n1], a[n1:])
        return c

    lax.fori_loop(0, n2, stage1, 0, unroll=DFT_UNROLL)

    def stage2(k1, c):
        a = _unpack_complex(scr_a[pl.ds(k1, n2, stride=n1), :])
        y = jnp.dot(ff_ref[...], a, preferred_element_type=F32)
        hk = h_ref[k1].astype(F32)
        yr, yi = y[:n2], y[n2:]
        hr, hi = hk[:n2], hk[n2:]
        z = jnp.concatenate([yr * hr - yi * hi, yr * hi + yi * hr], axis=0).astype(BF16)
        bk = jnp.dot(fi_ref[...], z, preferred_element_type=F32)
        scr_b[pl.ds(pl.multiple_of(k1 * n2, n2), n2), :] = _pack_complex(bk[:n2], bk[n2:])
        return c

    lax.fori_loop(0, n1, stage2, 0, unroll=DFT_UNROLL)

    def stage3(j, c):
        bmat = _unpack_complex(scr_b[pl.ds(j, n1, stride=n2), :])
        y = jnp.dot(ei_ref[j], bmat, preferred_element_type=F32)
        u = u_ref[pl.ds(j, n1h, stride=n2), :]
        g = g_ref[pl.ds(j, n1h, stride=n2), :]
        o_ref[pl.ds(j, n1h, stride=n2), :] = g * (y + u * bias_ref[...])
        return c

    lax.fori_loop(0, n2, stage3, 0, unroll=DFT_UNROLL)


def long_conv_gated(u, u_blk, g, g_blk, spec, bias, tables):
    b, seq, _ = u.shape
    ch = bias.shape[0]
    n2 = _dft_n2(seq)
    n1h = seq // n2
    n1 = 2 * n1h
    e_fwd, f_fwd, f_inv, e_inv = tables
    kern = functools.partial(_longconv_kernel, n1=n1, n1h=n1h, n2=n2)
    one = pl.Buffered(1)
    return pl.pallas_call(
        kern,
        grid=(ch // LANES, b),
        in_specs=[
            pl.BlockSpec((None, seq, LANES), lambda c, i: (i, 0, u_blk + c), pipeline_mode=one),
            pl.BlockSpec((None, seq, LANES), lambda c, i: (i, 0, g_blk + c), pipeline_mode=one),
            pl.BlockSpec((n1, 2 * n2, LANES), lambda c, i: (0, 0, c), pipeline_mode=one),
            pl.BlockSpec((1, LANES), lambda c, i: (0, c)),
            _const_spec(e_fwd.shape),
            _const_spec(f_fwd.shape),
            _const_spec(f_inv.shape),
            _const_spec(e_inv.shape),
        ],
        out_specs=pl.BlockSpec((None, seq, LANES), lambda c, i: (i, 0, c)),
        out_shape=jax.ShapeDtypeStruct((b, seq, ch), F32),
        scratch_shapes=[pltpu.VMEM((n1 * n2, LANES), jnp.uint32), pltpu.VMEM((n1 * n2, LANES), jnp.uint32)],
        compiler_params=_cparams(("parallel", "parallel")),
        name="long_conv",
    )(u, g, spec, bias.reshape(1, ch).astype(F32), e_fwd, f_fwd, f_inv, e_inv)


def _dft_tables_complex(seq):
    n2 = _dft_n2(seq)
    n1 = seq // n2
    k1 = np.arange(n1, dtype=np.float64)[None, :, None]
    nn = (n2 * np.arange(n1, dtype=np.float64)[None, None, :] + np.arange(n2, dtype=np.float64)[:, None, None])
    ang = 2.0 * np.pi * k1 * nn / seq
    c, s = np.cos(ang), np.sin(ang)
    e_fwd = np.concatenate([np.concatenate([c, s], axis=2), np.concatenate([-s, c], axis=2)], axis=1)
    a2 = 2.0 * np.pi * np.outer(np.arange(n2), np.arange(n2)) / n2
    f_re = np.concatenate([np.cos(a2), np.sin(a2)], axis=1)
    return jnp.asarray(e_fwd, BF16), jnp.asarray(f_re, BF16)


def _seqdft_kernel(vr_ref, vi_ref, ef_ref, fr_ref, o_ref, scr, *, n1, n2, scale):
    def stage1(j, c):
        x = jnp.concatenate([vr_ref[pl.ds(j, n1, stride=n2), :], vi_ref[pl.ds(j, n1, stride=n2), :]], axis=0)
        a = jnp.dot(ef_ref[j], x.astype(BF16), preferred_element_type=F32)
        scr[pl.ds(pl.multiple_of(j * n1, n1), n1), :] = _pack_complex(a[:n1], a[n1:])
        return c

    lax.fori_loop(0, n2, stage1, 0, unroll=DFT_UNROLL)

    def stage2(k1, c):
        a = _unpack_complex(scr[pl.ds(k1, n2, stride=n1), :])
        o_ref[pl.ds(k1, n2, stride=n1), :] = jnp.dot(fr_ref[...], a, preferred_element_type=F32) * scale
        return c

    lax.fori_loop(0, n1, stage2, 0, unroll=DFT_UNROLL)


def seq_dft_real(v, ch, scale, tables):
    b, seq, _ = v.shape
    n2 = _dft_n2(seq)
    n1 = seq // n2
    e_fwd, f_re = tables
    nblk = ch // LANES
    kern = functools.partial(_seqdft_kernel, n1=n1, n2=n2, scale=scale)
    return pl.pallas_call(
        kern,
        grid=(nblk, b),
        in_specs=[
            pl.BlockSpec((None, seq, LANES), lambda c, i: (i, 0, c)),
            pl.BlockSpec((None, seq, LANES), lambda c, i: (i, 0, nblk + c)),
            _const_spec(e_fwd.shape),
            _const_spec(f_re.shape),
        ],
        out_specs=pl.BlockSpec((None, seq, LANES), lambda c, i: (i, 0, c)),
        out_shape=jax.ShapeDtypeStruct((b, seq, ch), F32),
        scratch_shapes=[pltpu.VMEM((n1 * n2, LANES), jnp.uint32)],
        compiler_params=_cparams(("parallel", "parallel")),
        name="seq_dft",
    )(v, v, e_fwd, f_re)


def fourier_channel_matrix():
    a = 2.0 * np.pi * np.outer(np.arange(FN_GROUP_DIM), np.arange(FN_GROUP_DIM)) / FN_GROUP_DIM
    eye = np.eye(FN_GROUPS)
    return jnp.asarray(np.concatenate([np.kron(eye, np.cos(a)), -np.kron(eye, np.sin(a))], axis=1), BF16)


def _shortconv_kernel(u_ref, w_ref, b_ref, o_ref):
    u = u_ref[...]
    n = u.shape[0]
    row = lax.broadcasted_iota(jnp.int32, u.shape, 0)
    prev = jnp.where(row == 0, 0.0, pltpu.roll(u, 1, axis=0))
    nxt = jnp.where(row == n - 1, 0.0, pltpu.roll(u, n - 1, axis=0))
    o_ref[...] = prev * w_ref[0:1, :] + u * w_ref[1:2, :] + nxt * w_ref[2:3, :] + b_ref[...]


def short_conv(u, w, bias):
    b, seq, ch = u.shape
    return pl.pallas_call(
        _shortconv_kernel,
        grid=(b, ch // LANES),
        in_specs=[
            pl.BlockSpec((None, seq, LANES), lambda i, c: (i, 0, c)),
            pl.BlockSpec((HY_SHORT, LANES), lambda i, c: (0, c)),
            pl.BlockSpec((1, LANES), lambda i, c: (0, c)),
        ],
        out_specs=pl.BlockSpec((None, seq, LANES), lambda i, c: (i, 0, c)),
        out_shape=jax.ShapeDtypeStruct((b, seq, ch), F32),
        compiler_params=_cparams(("parallel", "parallel")),
        name="short_conv",
    )(u, w.astype(F32), bias.reshape(1, ch).astype(F32))


def _filter_kernel(emb_ref, w1_ref, b1_ref, f1_ref, w2_ref, b2_ref, f2_ref, w3_ref, b3_ref, dec_ref, o_ref):
    z = jnp.dot(emb_ref[...].astype(BF16), w1_ref[...], preferred_element_type=F32) + b1_ref[...]
    z = jnp.sin(f1_ref[...] * z)
    z = jnp.dot(z.astype(BF16), w2_ref[...], preferred_element_type=F32) + b2_ref[...]
    z = jnp.sin(f2_ref[...] * z)
    h = jnp.dot(z.astype(BF16), w3_ref[...], preferred_element_type=F32) + b3_ref[...]
    o_ref[...] = h * dec_ref[...]


def hyena_filters(seq, hy_w1, hy_b1, hy_freq1, hy_w2, hy_b2, hy_freq2, hy_w3, hy_b3):
    t = jnp.linspace(0.0, 1.0, seq, dtype=F32)[:, None]
    ang = (2.0 * math.pi / seq) * jnp.arange(seq, dtype=F32)[:, None]
    bands = jnp.linspace(1e-4, HY_EMB_BANDS - 1, HY_EMB_BANDS, dtype=F32)[None, :]
    emb = jnp.concatenate([t, jnp.cos(bands * ang), -jnp.sin(bands * ang)], axis=-1)
    kdim = emb.shape[1]
    kpad = LANES - kdim
    emb = jnp.pad(emb, ((0, 0), (0, kpad)))
    w1 = jnp.pad(hy_w1, ((0, kpad), (0, 0))).astype(BF16)
    deltas = jnp.abs(jnp.linspace(math.log(HY_DECAY_TARGET) / HY_SLOW_DECAY,
                                  math.log(HY_DECAY_TARGET) / HY_FAST_DECAY, HY_WIDTH, dtype=F32))
    decay = jnp.tile(jnp.exp(-t * deltas), (1, 2 * HY_ORDER))
    fo = hy_w1.shape[1]
    nout = hy_w3.shape[1]
    tl = _tile(seq, 1024)
    row = lambda a: a.reshape(1, -1).astype(F32)
    full = lambda shape: pl.BlockSpec(shape, lambda i: (0, 0))
    return pl.pallas_call(
        _filter_kernel,
        grid=(seq // tl,),
        in_specs=[
            pl.BlockSpec((tl, LANES), lambda i: (i, 0)),
            full((LANES, fo)), full((1, fo)), full((1, fo)),
            full((fo, fo)), full((1, fo)), full((1, fo)),
            full((fo, nout)), full((1, nout)),
            pl.BlockSpec((tl, nout), lambda i: (i, 0)),
        ],
        out_specs=pl.BlockSpec((tl, nout), lambda i: (i, 0)),
        out_shape=jax.ShapeDtypeStruct((seq, nout), F32),
        compiler_params=_cparams(("parallel",)),
        name="hyena_filter",
    )(emb, w1, row(hy_b1), row(hy_freq1), hy_w2.astype(BF16), row(hy_b2), row(hy_freq2),
      hy_w3.astype(BF16), row(hy_b3), decay)


def hyena_spectra(seq, filt, tables):
    n2 = _dft_n2(seq)
    h = filt.reshape(seq, HY_ORDER, 2, HY_WIDTH)
    h_fwd, h_bwd = h[:, :, 0], h[:, :, 1]
    h_bwd = h_bwd.at[0].set(0.0)
    norm = jnp.sum(jnp.abs(h_fwd), axis=0) + jnp.sum(jnp.abs(h_bwd), axis=0)
    stacked = jnp.concatenate([h_fwd, h_bwd], axis=1).transpose(1, 0, 2)
    sp = dft_spectrum(stacked, tables)
    sf, sb = sp[:HY_ORDER], sp[HY_ORDER:]
    re = sf[:, :, :n2] + sb[:, :, :n2]
    im = sf[:, :, n2:] - sb[:, :, n2:]
    scale = (1.0 / (2 * seq)) / norm
    return (jnp.concatenate([re, im], axis=2) * scale[:, None, None, :]).astype(BF16)


def _merge_kernel(x_ref, yf_ref, yh_ref, ya_ref, g_ref, wf_ref, wh_ref, wa_ref, wo_ref,
                  gate_ref, ng_ref, sh_ref, sc_ref, xo_ref, ho_ref):
    d = x_ref.shape[-1]
    g = g_ref[0].astype(F32)
    yf = jnp.dot(yf_ref[0].astype(BF16), wf_ref[...], preferred_element_type=F32)
    yh = jnp.dot(yh_ref[0].astype(BF16), wh_ref[...], preferred_element_type=F32)
    ya = jnp.dot(ya_ref[0], wa_ref[...], preferred_element_type=F32)
    mix = g[:, 0:d] * yf + g[:, d:2 * d] * yh + g[:, 2 * d:3 * d] * ya
    x = x_ref[0] + gate_ref[0] * jnp.dot(mix.astype(BF16), wo_ref[...], preferred_element_type=F32)
    xo_ref[0] = x
    y = x * lax.rsqrt(jnp.mean(x * x, axis=-1, keepdims=True) + EPS) * ng_ref[...]
    ho_ref[0] = (y * (1.0 + sc_ref[0]) + sh_ref[0]).astype(ho_ref.dtype)


def merge_branches(x, yf, yh, ya, g, w_f, w_h, w_a, w_o, gate, norm_g, shift, scale):
    b, l, d = x.shape
    tl = _tile(l, 512)
    rows = lambda w: pl.BlockSpec((1, tl, w), lambda i, j: (i, j, 0))
    full = lambda a: pl.BlockSpec(a.shape, lambda i, j: (0, 0))
    per_b = pl.BlockSpec((1, 1, d), lambda i, j: (i, 0, 0))
    wf, wh, wa, wo = (w.astype(BF16) for w in (w_f, w_h, w_a, w_o))
    return pl.pallas_call(
        _merge_kernel,
        grid=(b, l // tl),
        in_specs=[rows(d), rows(yf.shape[-1]), rows(yh.shape[-1]), rows(ya.shape[-1]), rows(3 * d),
                  full(wf), full(wh), full(wa), full(wo),
                  per_b, pl.BlockSpec((1, d), lambda i, j: (0, 0)), per_b, per_b],
        out_specs=[rows(d), rows(d)],
        out_shape=[jax.ShapeDtypeStruct((b, l, d), F32), jax.ShapeDtypeStruct((b, l, d), BF16)],
        compiler_params=_cparams(("parallel", "parallel")),
        name="merge",
    )(x, yf, yh, ya, g, wf, wh, wa, wo, gate.reshape(b, 1, d), norm_g.reshape(1, d).astype(F32),
      shift.reshape(b, 1, d), scale.reshape(b, 1, d))


GLU_GROUP = 2 * LANES


def _glu_group_permutation():
    p = np.zeros((GLU_GROUP, GLU_GROUP), np.float32)
    j = np.arange(LANES)
    p[2 * j, j] = 1.0
    p[2 * j + 1, LANES + j] = 1.0
    return jnp.asarray(p, BF16)


def _moe_kernel(be_ref, act_ref, new_ref, rows_ref, w1_ref, b1_ref, w2_ref, b2_ref, p_ref, o_ref, w1s, w2s):
    i = pl.program_id(0)

    @pl.when(new_ref[i] > 0)
    def _():
        for q in range(w1s.shape[1] // GLU_GROUP):
            cols = slice(q * GLU_GROUP, (q + 1) * GLU_GROUP)
            w1s[:, cols] = jnp.dot(w1_ref[:, cols].astype(BF16), p_ref[...],
                                   preferred_element_type=F32).astype(BF16)
        w2s[...] = w2_ref[...].astype(BF16)

    @pl.when(act_ref[i] > 0)
    def _():
        u = jnp.dot(rows_ref[...], w1s[...], preferred_element_type=F32) + b1_ref[0]
        parts = []
        for q in range(u.shape[1] // GLU_GROUP):
            xg = jnp.minimum(u[:, q * GLU_GROUP:q * GLU_GROUP + LANES], SWIGLU_LIMIT)
            xl = jnp.clip(u[:, q * GLU_GROUP + LANES:(q + 1) * GLU_GROUP], -SWIGLU_LIMIT, SWIGLU_LIMIT)
            parts.append((xg * jax.nn.sigmoid(SWIGLU_ALPHA * xg) * (xl + 1.0)).astype(BF16))
        a = jnp.concatenate(parts, axis=1)
        y = jnp.dot(a, w2s[...], preferred_element_type=F32) + b2_ref[0]
        o_ref[...] = y.astype(o_ref.dtype)

    @pl.when(act_ref[i] == 0)
    def _():
        o_ref[...] = jnp.zeros(o_ref.shape, o_ref.dtype)


def moe_experts(rows, blk_exp, blk_act, blk_new, layer, w1_all, b1, w2_all, b2):
    r, d = rows.shape
    de = w2_all.shape[2]
    nblk = r // MOE_BLOCK
    grid_spec = pltpu.PrefetchScalarGridSpec(
        num_scalar_prefetch=3,
        grid=(nblk,),
        in_specs=[
            pl.BlockSpec((MOE_BLOCK, d), lambda i, be, act, new: (i, 0)),
            pl.BlockSpec((None, None, d, 2 * de), lambda i, be, act, new: (layer, be[i], 0, 0)),
            pl.BlockSpec((1, 1, 2 * de), lambda i, be, act, new: (be[i], 0, 0)),
            pl.BlockSpec((None, None, de, d), lambda i, be, act, new: (layer, be[i], 0, 0)),
            pl.BlockSpec((1, 1, d), lambda i, be, act, new: (be[i], 0, 0)),
            pl.BlockSpec((GLU_GROUP, GLU_GROUP), lambda i, be, act, new: (0, 0)),
        ],
        out_specs=pl.BlockSpec((MOE_BLOCK, d), lambda i, be, act, new: (i, 0)),
        scratch_shapes=[pltpu.VMEM((d, 2 * de), BF16), pltpu.VMEM((de, d), BF16)],
    )
    return pl.pallas_call(
        _moe_kernel,
        grid_spec=grid_spec,
        out_shape=jax.ShapeDtypeStruct((r, d), BF16),
        compiler_params=_cparams(("arbitrary",)),
        name="moe_experts",
    )(blk_exp, blk_act, blk_new, rows, w1_all, b1, w2_all, b2, _glu_group_permutation())


def _combine_kernel(y_ref, g_ref, x_ref, m_ref, o_ref):
    g = g_ref[...]
    acc = g[:, 0:1] * y_ref[0].astype(F32)
    for j in range(1, TOP_K):
        acc = acc + g[:, j:j + 1] * y_ref[j].astype(F32)
    o_ref[...] = x_ref[...] + m_ref[0] * acc


def moe_combine(y_sel, gate, resid, mod_blocks, tm):
    k, t, d = y_sel.shape
    return pl.pallas_call(
        _combine_kernel,
        grid=(t // tm,),
        in_specs=[pl.BlockSpec((k, tm, d), lambda i: (0, i, 0)),
                  pl.BlockSpec((tm, k), lambda i: (i, 0)),
                  pl.BlockSpec((tm, d), lambda i: (i, 0)),
                  pl.BlockSpec((1, 1, d), lambda i: (i, 0, 0))],
        out_specs=pl.BlockSpec((tm, d), lambda i: (i, 0)),
        out_shape=jax.ShapeDtypeStruct((t, d), F32),
        compiler_params=_cparams(("parallel",)),
        name="moe_combine",
    )(y_sel, gate, resid, mod_blocks)


def moe_ffn(h, resid, mod_blocks, tm, p):
    t_tok, d = h.shape
    wr = jnp.pad(p["w_router"], ((0, 0), (0, LANES - N_EXPERTS))).astype(BF16)
    br = jnp.pad(p["b_router"], (0, LANES - N_EXPERTS)).reshape(1, LANES).astype(F32)
    logits = matmul(h, wr, epi="bias", extra=(br,),
                    extra_specs=[pl.BlockSpec((1, LANES), lambda i, j: (0, 0))], name="mm_router")[:, :N_EXPERTS]
    top_v, top_i = lax.top_k(logits, TOP_K)
    gate = jax.nn.softmax(top_v, axis=-1)
    n_assign = t_tok * TOP_K
    flat_e = top_i.reshape(-1)
    experts = jnp.arange(N_EXPERTS, dtype=flat_e.dtype)[None, :]
    onehot = (flat_e[:, None] == experts).astype(jnp.int32)
    csum = jnp.cumsum(onehot, axis=0)
    counts = csum[-1]
    padded = (counts + MOE_BLOCK - 1) // MOE_BLOCK * MOE_BLOCK
    pad_end = jnp.cumsum(padded)
    pad_start = pad_end - padded
    dest = jnp.sum(onehot * (csum - 1 + pad_start[None, :]), axis=1)
    n_blocks = -(-n_assign // MOE_BLOCK) + N_EXPERTS
    n_rows = n_blocks * MOE_BLOCK
    filled = jnp.zeros((n_rows,), jnp.int32).at[dest].add(jnp.arange(n_assign, dtype=jnp.int32) // TOP_K + 1)
    row_tok = jnp.where(filled > 0, filled - 1, jnp.arange(n_rows, dtype=jnp.int32) % t_tok)
    blk_start = jnp.arange(n_blocks, dtype=jnp.int32) * MOE_BLOCK
    blk_exp = jnp.minimum(jnp.sum((blk_start[:, None] >= pad_end[None, :]).astype(jnp.int32), axis=1),
                          N_EXPERTS - 1)
    blk_act = (blk_start < pad_end[-1]).astype(jnp.int32)
    blk_new = jnp.concatenate([jnp.ones((1,), jnp.int32), (blk_exp[1:] != blk_exp[:-1]).astype(jnp.int32)])
    y_rows = moe_experts(h[row_tok], blk_exp, blk_act, blk_new, p["layer"], p["w1_all"], p["b1"],
                         p["w2_all"], p["b2"])
    y_sel = y_rows[dest.reshape(t_tok, TOP_K).T]
    return moe_combine(y_sel, gate, resid, mod_blocks, tm)


OFF_F = 0
OFF_HY = OFF_F + FN_WIDTH
OFF_Q = OFF_HY + (HY_ORDER + 1) * HY_WIDTH
OFF_K = OFF_Q + COL_QK
OFF_V = OFF_K + COL_QK
OFF_G = OFF_V + DA_WIDTH


def _projection_weights(w_in):
    w = w_in.astype(BF16)
    w_fv = matmul(w[:, OFF_F:OFF_HY], fourier_channel_matrix(), out_dtype=BF16, name="mm_wfold")
    w_cat = jnp.concatenate([w_fv, w[:, OFF_HY:]], axis=1)
    shift = w_fv.shape[1] - (OFF_HY - OFF_F)
    bounds = {"hy": (OFF_HY, OFF_Q), "q": (OFF_Q, OFF_K), "k": (OFF_K, OFF_V), "v": (OFF_V, OFF_G),
              "g": (OFF_G, w_in.shape[1])}
    cols = {"f": (0, w_fv.shape[1])}
    cols.update({name: (a + shift, b + shift) for name, (a, b) in bounds.items()})
    return w_cat, cols


def _project(x, mod_shift, mod_scale, p, rope):
    q_gain = p["q_norm_g"] * (DA_QK_DIM ** -0.5 * math.log2(math.e))
    return project_all(x, p["norm1_g"], mod_shift, mod_scale, p["w_cat"], p["cols"], q_gain, p["k_norm_g"],
                       rope[0], rope[1])


def _token_mixer(x, mod_shift1, mod_scale1, mod_gate, mod_shift2, mod_scale2, p, lam, lam_init, rope, kv_extra):
    b, s, d = x.shape
    v_f, z, q, k3, v3, g = _project(x, mod_shift1, mod_scale1, p, rope)

    y_f = seq_dft_real(v_f, FN_WIDTH, 1.0 / math.sqrt(s * FN_GROUP_DIM), _dft_tables_complex(s))

    z = short_conv(z, p["hy_conv_w"], p["hy_conv_b"])
    tables = _dft_tables_real(s)
    filt = hyena_filters(s, p["hy_w1"], p["hy_b1"], p["hy_freq1"], p["hy_w2"], p["hy_b2"], p["hy_freq2"],
                         p["hy_w3"], p["hy_b3"])
    spec = hyena_spectra(s, filt, tables)
    cb = HY_WIDTH // LANES
    y_h = long_conv_gated(z, 0, z, cb, spec[0], p["hy_bias"][0], tables)
    y_h = long_conv_gated(y_h, 0, z, 2 * cb, spec[1], p["hy_bias"][1], tables)

    if kv_extra is not None:
        k_all = jnp.concatenate([k3, kv_extra[0]], axis=1)
        v_all = jnp.concatenate([v3, kv_extra[1]], axis=1)
    else:
        k_all, v_all = k3, v3
    nk = k_all.shape[1]
    y_a = diff_attention(q.reshape(b * s, COL_QK), k_all.reshape(b * nk, COL_QK), v_all.reshape(b * nk, DA_WIDTH),
                         lam, p["subln_g"], 1.0 - lam_init, s, nk).reshape(b, s, DA_WIDTH)

    x_new, h2 = merge_branches(x, y_f, y_h, y_a, g, p["w_f"], p["w_h"], p["w_a"], p["w_o"],
                               mod_gate, p["norm2_g"], mod_shift2, mod_scale2)
    return x_new, h2, (k3, v3)


def _layer(l, x, xc, c, c_ctx, p, ctx_out):
    b, n_lat, d = x.shape
    n_ctx = xc.shape[1]
    lam_init = 0.8 - 0.6 * math.exp(-0.3 * l)
    lam = (jnp.exp(jnp.sum(p["lam_q"][0] * p["lam_k"][0]).astype(F32))
           - jnp.exp(jnp.sum(p["lam_q"][1] * p["lam_k"][1]).astype(F32)) + lam_init)

    cond = jnp.concatenate([c, c_ctx[None, :], jnp.zeros((16 - b - 1, d), F32)], axis=0)
    mod_all = matmul(jax.nn.silu(cond).astype(BF16), p["w_mod"].astype(BF16), epi="bias",
                     extra=(p["b_mod"].reshape(1, 6 * d).astype(F32),),
                     extra_specs=[pl.BlockSpec((1, 1024), lambda i, j: (0, j))], name="mm_mod")
    mod = [mod_all[:b, i * d:(i + 1) * d] for i in range(6)]
    mod_c = [jnp.broadcast_to(mod_all[b, i * d:(i + 1) * d], (b, d)) for i in range(6)]

    pw = dict(p)
    pw["w_cat"], pw["cols"] = _projection_weights(p["w_in"])

    no_rope = (jnp.ones((n_ctx, LANES), F32), jnp.zeros((n_ctx, LANES), F32))
    if ctx_out:
        xc_new, h2c, kv_c = _token_mixer(xc, mod_c[0], mod_c[1], mod_c[2], mod_c[3], mod_c[4], pw, lam, lam_init,
                                         no_rope, None)
    else:
        kv_c = _project(xc, mod_c[0], mod_c[1], pw, no_rope)[3:5]
    x_new, h2, _ = _token_mixer(x, mod[0], mod[1], mod[2], mod[3], mod[4], pw, lam, lam_init,
                                rope_tables(n_lat), kv_c)

    n_exp, two_f = p["b_e1"].shape
    b1 = p["b_e1"].reshape(n_exp, two_f // GLU_GROUP, LANES, 2).transpose(0, 1, 3, 2).reshape(n_exp, 1, two_f)
    pe = {
        "w_router": p["w_router"], "b_router": p["b_router"], "layer": l,
        "w1_all": p["w_e1_all"], "b1": b1, "w2_all": p["w_e2_all"], "b2": p["b_e2"][:, None, :],
    }
    tm = 512
    lat_mod = jnp.repeat(mod[5], n_lat // tm, axis=0)
    if ctx_out:
        h_all = jnp.concatenate([h2c.reshape(b * n_ctx, d), h2.reshape(b * n_lat, d)], axis=0)
        resid = jnp.concatenate([xc_new.reshape(b * n_ctx, d), x_new.reshape(b * n_lat, d)], axis=0)
        mod_blocks = jnp.concatenate([jnp.tile(mod_c[5][:1], (b * n_ctx // tm, 1)), lat_mod], axis=0)
        out = moe_ffn(h_all, resid, mod_blocks[:, None, :], tm, pe)
        xc = out[:b * n_ctx].reshape(b, n_ctx, d)
        x = out[b * n_ctx:].reshape(b, n_lat, d)
    else:
        x = moe_ffn(h2.reshape(b * n_lat, d), x_new.reshape(b * n_lat, d), lat_mod[:, None, :], tm, pe)
        x = x.reshape(b, n_lat, d)
    return x, xc


_PARAM_NAMES = ("w_mod", "b_mod", "norm1_g", "norm2_g", "w_in", "hy_conv_w", "hy_conv_b", "hy_w1", "hy_b1",
                "hy_freq1", "hy_w2", "hy_b2", "hy_freq2", "hy_w3", "hy_b3", "hy_bias", "q_norm_g", "k_norm_g",
                "lam_q", "lam_k", "subln_g", "w_f", "w_h", "w_a", "w_o", "w_router", "b_router",
                "w_e1", "b_e1", "w_e2", "b_e2")


def kernel(x, c, ctx, c_ctx, w_mod, b_mod, norm1_g, norm2_g, w_in, hy_conv_w, hy_conv_b, hy_w1, hy_b1, hy_freq1,
           hy_w2, hy_b2, hy_freq2, hy_w3, hy_b3, hy_bias, q_norm_g, k_norm_g, lam_q, lam_k, subln_g, w_f, w_h,
           w_a, w_o, w_router, b_router, w_e1, b_e1, w_e2, b_e2):
    stacked = (w_mod, b_mod, norm1_g, norm2_g, w_in, hy_conv_w, hy_conv_b, hy_w1, hy_b1, hy_freq1, hy_w2, hy_b2,
               hy_freq2, hy_w3, hy_b3, hy_bias, q_norm_g, k_norm_g, lam_q, lam_k, subln_g, w_f, w_h, w_a, w_o,
               w_router, b_router, w_e1, b_e1, w_e2, b_e2)
    depth = w_mod.shape[0]
    xc = ctx
    for l in range(depth):
        p = {name: arr[l] for name, arr in zip(_PARAM_NAMES, stacked) if name not in ("w_e1", "w_e2")}
        p["w_e1_all"], p["w_e2_all"] = w_e1, w_e2
        x, xc = _layer(l, x, xc, c, c_ctx, p, l < depth - 1)
    return x
```

```python
import functools
import math

import numpy as np
import jax
import jax.numpy as jnp
from jax import lax
from jax.experimental import pallas as pl
from jax.experimental.pallas import tpu as pltpu

F32 = jnp.float32
BF16 = jnp.bfloat16

LANES = 128
VMEM_LIMIT = 56 * 1024 * 1024

GRID_W = 64
EPS = 1e-6
SUBLN_EPS = 1e-5
FN_GROUPS = 4
FN_GROUP_DIM = 64
FN_WIDTH = FN_GROUPS * FN_GROUP_DIM
HY_WIDTH = 256
HY_ORDER = 2
HY_SHORT = 3
HY_EMB_BANDS = 16
HY_DECAY_TARGET = 1e-2
HY_FAST_DECAY = 0.3
HY_SLOW_DECAY = 1.5
DA_HEADS = 4
DA_QK_DIM = 64
DA_V_DIM = 2 * DA_QK_DIM
DA_WIDTH = DA_HEADS * DA_V_DIM
ROPE_BASE = 10000.0
N_BRANCHES = 3
COL_QK = DA_HEADS * 2 * DA_QK_DIM
N_EXPERTS = 32
TOP_K = 4
SWIGLU_ALPHA = 1.702
SWIGLU_LIMIT = 7.0
MOE_BLOCK = 512
PROJ_GATE_CHUNK = 1024
DFT_MIN_N1 = 16
DFT_UNROLL = 8


def _dft_n2(seq):
    return min(LANES, seq // DFT_MIN_N1)


def _cparams(sem):
    return pltpu.CompilerParams(dimension_semantics=sem, vmem_limit_bytes=VMEM_LIMIT)


def _tile(n, pref):
    if n <= pref:
        return n
    for t in range(pref, 7, -1):
        if n % t == 0 and t % 8 == 0:
            return t
    return n


def _const_spec(shape):
    nd = len(shape)
    return pl.BlockSpec(shape, lambda *_: (0,) * nd, pipeline_mode=pl.Buffered(1))


def _qk_epilogue(acc, gm, gain, cos, sin):
    ms = jnp.dot((acc * acc).astype(BF16), gm, preferred_element_type=F32)
    y = acc * lax.rsqrt(ms + EPS) * gain
    n = y.shape[1]
    reps = n // LANES
    lane = lax.broadcasted_iota(jnp.int32, y.shape, 1)
    is_a = (lane % (DA_QK_DIM // 2)) < (DA_QK_DIM // 4)
    half = DA_QK_DIM // 4
    swapped = jnp.where(is_a, pltpu.roll(y, n - half, axis=1), pltpu.roll(y, half, axis=1))
    return y * jnp.tile(cos, (1, reps)) + swapped * jnp.tile(sin, (1, reps))


def _proj_kernel(x_ref, ng_ref, sh_ref, sc_ref, w_ref, gm_ref, qg_ref, kg_ref, cos_ref, sin_ref,
                 vf_ref, z_ref, q_ref, k_ref, v_ref, g_ref, *, cols):
    x = x_ref[0]
    y = x * lax.rsqrt(jnp.mean(x * x, axis=-1, keepdims=True) + EPS) * ng_ref[...]
    h = (y * (1.0 + sc_ref[0]) + sh_ref[0]).astype(BF16)

    def mm(name):
        a, b = cols[name]
        return jnp.dot(h, w_ref[:, a:b], preferred_element_type=F32)

    vf_ref[0] = mm("f")
    z_ref[0] = mm("hy")
    q_ref[0] = _qk_epilogue(mm("q"), gm_ref[...], qg_ref[...], cos_ref[...], sin_ref[...]).astype(q_ref.dtype)
    k_ref[0] = _qk_epilogue(mm("k"), gm_ref[...], kg_ref[...], cos_ref[...], sin_ref[...]).astype(k_ref.dtype)
    vv = mm("v").astype(v_ref.dtype)
    ones = jnp.ones((vv.shape[0], DA_V_DIM), v_ref.dtype)
    for hd in range(DA_HEADS):
        v_ref[0, :, 2 * hd * DA_V_DIM:(2 * hd + 1) * DA_V_DIM] = vv[:, hd * DA_V_DIM:(hd + 1) * DA_V_DIM]
        v_ref[0, :, (2 * hd + 1) * DA_V_DIM:(2 * hd + 2) * DA_V_DIM] = ones
    a, b = cols["g"]
    for c0 in range(a, b, PROJ_GATE_CHUNK):
        acc = jnp.dot(h, w_ref[:, c0:c0 + PROJ_GATE_CHUNK], preferred_element_type=F32)
        g_ref[0, :, c0 - a:c0 - a + PROJ_GATE_CHUNK] = jax.nn.sigmoid(acc).astype(g_ref.dtype)


def project_all(x, norm_g, shift, scale, w_cat, cols, q_gain, k_gain, cos, sin):
    b, s, d = x.shape
    tm = _tile(s, 512)
    width = {name: stop - start for name, (start, stop) in cols.items()}
    width["v"] *= 2
    gm = _group_mean_matrix(COL_QK, DA_QK_DIM)
    tile_gain = lambda g: jnp.tile(g.astype(F32), COL_QK // DA_QK_DIM).reshape(1, COL_QK)
    rows = lambda w: pl.BlockSpec((1, tm, w), lambda i, j: (i, j, 0))
    per_b = pl.BlockSpec((1, 1, d), lambda i, j: (i, 0, 0))
    full = lambda shape: pl.BlockSpec(shape, lambda i, j: (0, 0))
    outs = [("f", F32), ("hy", F32), ("q", BF16), ("k", BF16), ("v", BF16), ("g", BF16)]
    return pl.pallas_call(
        functools.partial(_proj_kernel, cols=cols),
        grid=(b, s // tm),
        in_specs=[rows(d), full((1, d)), per_b, per_b, _const_spec(w_cat.shape), full(gm.shape),
                  full((1, COL_QK)), full((1, COL_QK)),
                  pl.BlockSpec((tm, LANES), lambda i, j: (j, 0)), pl.BlockSpec((tm, LANES), lambda i, j: (j, 0))],
        out_specs=[rows(width[name]) for name, _ in outs],
        out_shape=[jax.ShapeDtypeStruct((b, s, width[name]), dt) for name, dt in outs],
        compiler_params=_cparams(("parallel", "parallel")),
        name="project_all",
    )(x, norm_g.reshape(1, d).astype(F32), shift.reshape(b, 1, d), scale.reshape(b, 1, d), w_cat, gm,
      tile_gain(q_gain), tile_gain(k_gain), cos, sin)


def _mm_kernel(a_ref, w_ref, *rest, epi):
    acc = jnp.dot(a_ref[...], w_ref[...], preferred_element_type=F32)
    if epi == "plain":
        (o_ref,) = rest
    elif epi == "bias":
        b_ref, o_ref = rest
        acc = acc + b_ref[...]
    else:
        raise ValueError(epi)
    o_ref[...] = acc.astype(o_ref.dtype)


def matmul(a, w, *, out_dtype=F32, epi="plain", extra=(), extra_specs=(), tm=512, tn=1024, name="mm"):
    m, k = a.shape
    k2, n = w.shape
    assert k == k2
    tm = _tile(m, tm)
    tn = _tile(n, tn)
    return pl.pallas_call(
        functools.partial(_mm_kernel, epi=epi),
        grid=(m // tm, n // tn),
        in_specs=[
            pl.BlockSpec((tm, k), lambda i, j: (i, 0)),
            pl.BlockSpec((k, tn), lambda i, j: (0, j)),
            *extra_specs,
        ],
        out_specs=pl.BlockSpec((tm, tn), lambda i, j: (i, j)),
        out_shape=jax.ShapeDtypeStruct((m, n), out_dtype),
        compiler_params=_cparams(("parallel", "parallel")),
        name=name,
    )(a, w, *extra)


def _group_mean_matrix(n, group):
    idx = np.arange(n)
    return jnp.asarray((idx[:, None] // group == idx[None, :] // group).astype(np.float32) / group, BF16)


def rope_tables(n_lat):
    rows = n_lat // GRID_W
    row = np.repeat(np.arange(rows), GRID_W).astype(np.float64)
    col = np.tile(np.arange(GRID_W), rows).astype(np.float64)
    n_freq = DA_QK_DIM // 4
    inv = ROPE_BASE ** (-np.arange(n_freq, dtype=np.float64) / n_freq)
    ang_r = row[:, None] * inv
    ang_c = col[:, None] * inv
    cos = np.concatenate([np.cos(ang_r), np.cos(ang_r), np.cos(ang_c), np.cos(ang_c)], axis=1)
    sin = np.concatenate([-np.sin(ang_r), np.sin(ang_r), -np.sin(ang_c), np.sin(ang_c)], axis=1)
    cos = np.tile(cos, (1, LANES // DA_QK_DIM))
    sin = np.tile(sin, (1, LANES // DA_QK_DIM))
    return jnp.asarray(cos, F32), jnp.asarray(sin, F32)


ATTN_TQ = 512
ATTN_TK = 768


def _attn_kernel(lam_ref, q_ref, k_ref, v_ref, g_ref, o_ref, qs_ref, s_ref, m_ref, acc_ref, *, tq, tk, nkc, out_scale):
    q = q_ref[...]
    lane = lax.broadcasted_iota(jnp.int32, q.shape, 1)
    zero = jnp.zeros_like(q)
    qs_ref[0:tq, :] = jnp.where(lane < DA_QK_DIM, q, zero)
    qs_ref[tq:2 * tq, :] = jnp.where(lane >= DA_QK_DIM, q, zero)
    m_ref[...] = jnp.full(m_ref.shape, -jnp.inf, F32)
    acc_ref[...] = jnp.zeros(acc_ref.shape, F32)

    def scores(j, slot):
        kj = k_ref[pl.ds(pl.multiple_of(j * tk, tk), tk), :]
        s_ref[slot] = lax.dot_general(qs_ref[...], kj, (((1,), (1,)), ((), ())), preferred_element_type=F32)

    def update(j, slot):
        s = s_ref[slot]
        vj = v_ref[pl.ds(pl.multiple_of(j * tk, tk), tk), :]
        m_prev = m_ref[...]
        m_next = jnp.maximum(m_prev, jnp.max(s, axis=1, keepdims=True))
        p = jnp.exp2(s - jnp.tile(m_next, (1, tk // LANES)))
        alpha = jnp.exp2(m_prev - m_next)
        m_ref[...] = m_next
        acc_ref[...] = acc_ref[...] * jnp.tile(alpha, (1, 2)) + jnp.dot(p.astype(BF16), vj,
                                                                          preferred_element_type=F32)

    scores(0, 0)
    for j in range(nkc):
        if j + 1 < nkc:
            scores(j + 1, (j + 1) % 2)
        update(j, j % 2)

    o1 = acc_ref[0:tq, 0:DA_V_DIM] / acc_ref[0:tq, DA_V_DIM:]
    o2 = acc_ref[tq:2 * tq, 0:DA_V_DIM] / acc_ref[tq:2 * tq, DA_V_DIM:]
    o = o1 - lam_ref[0, 0] * o2
    o = o * lax.rsqrt(jnp.mean(o * o, axis=-1, keepdims=True) + SUBLN_EPS)
    o_ref[...] = (o * g_ref[...] * out_scale).astype(o_ref.dtype)


def diff_attention(q, k, v_ext, lam, subln_g, out_scale, nq, nk):
    b = q.shape[0] // nq
    tq = _tile(nq, ATTN_TQ)
    tk = next(t for t in (ATTN_TK, 256, 128) if nk % t == 0)
    nqb, nkc = nq // tq, nk // tk
    kern = functools.partial(_attn_kernel, tq=tq, tk=tk, nkc=nkc, out_scale=out_scale)
    return pl.pallas_call(
        kern,
        grid=(b, DA_HEADS, nqb),
        in_specs=[
            pl.BlockSpec(memory_space=pltpu.SMEM),
            pl.BlockSpec((tq, DA_V_DIM), lambda bi, h, qi: (bi * nqb + qi, h)),
            pl.BlockSpec((nk, DA_V_DIM), lambda bi, h, qi: (bi, h)),
            pl.BlockSpec((nk, 2 * DA_V_DIM), lambda bi, h, qi: (bi, h)),
            pl.BlockSpec((1, DA_V_DIM), lambda bi, h, qi: (0, 0)),
        ],
        out_specs=pl.BlockSpec((tq, DA_V_DIM), lambda bi, h, qi: (bi * nqb + qi, h)),
        out_shape=jax.ShapeDtypeStruct((b * nq, DA_WIDTH), BF16),
        scratch_shapes=[
            pltpu.VMEM((2 * tq, DA_V_DIM), BF16),
            pltpu.VMEM((2, 2 * tq, tk), F32),
            pltpu.VMEM((2 * tq, LANES), F32),
            pltpu.VMEM((2 * tq, 2 * DA_V_DIM), F32),
        ],
        compiler_params=_cparams(("parallel", "parallel", "parallel")),
        name="diff_attn",
    )(lam.reshape(1, 1).astype(F32), q, k, v_ext, subln_g.reshape(1, DA_V_DIM).astype(F32))


def _dft_tables_real(seq):
    n2 = _dft_n2(seq)
    n1h = seq // n2
    n1 = 2 * n1h
    n = n1 * n2
    k1 = np.arange(n1, dtype=np.float64)[None, :, None]
    nn = (n2 * np.arange(n1h, dtype=np.float64)[None, None, :] + np.arange(n2, dtype=np.float64)[:, None, None])
    ang = 2.0 * np.pi * k1 * nn / n
    e_fwd = np.concatenate([np.cos(ang), -np.sin(ang)], axis=1)
    e_inv = np.transpose(e_fwd, (0, 2, 1))
    a2 = 2.0 * np.pi * np.outer(np.arange(n2), np.arange(n2)) / n2
    c, s = np.cos(a2), np.sin(a2)
    f_fwd = np.block([[c, s], [-s, c]])
    f_inv = np.block([[c, -s], [s, c]])
    f_spec = np.block([[c, s, c, s], [-s, c, s, -c]])
    return tuple(jnp.asarray(t, BF16) for t in (e_fwd, f_fwd, f_inv, e_inv, f_spec))


def _pack_complex(re, im):
    r = lax.bitcast_convert_type(re.astype(BF16).astype(F32), jnp.uint32)
    i = lax.bitcast_convert_type(im.astype(BF16).astype(F32), jnp.uint32)
    return r | (i >> 16)


def _unpack_complex(w):
    re = lax.bitcast_convert_type(w & jnp.uint32(0xFFFF0000), F32)
    im = lax.bitcast_convert_type(w << 16, F32)
    return jnp.concatenate([re, im], axis=0).astype(BF16)


def _spectrum_kernel(hf_ref, hb_ref, sc_ref, ef_ref, fs_ref, o_ref, scr_f, scr_b, *, n1, n1h, n2, kc):
    kk = pl.program_id(2)

    @pl.when(kk == 0)
    def _():
        def stage1(j, c):
            xf = hf_ref[pl.ds(j, n1h, stride=n2), :]
            xb = hb_ref[pl.ds(j, n1h, stride=n2), :]
            row = lax.broadcasted_iota(jnp.int32, xb.shape, 0)
            xb = jnp.where((row == 0) & (j == 0), 0.0, xb)
            af = jnp.dot(ef_ref[j], xf.astype(BF16), preferred_element_type=F32)
            ab = jnp.dot(ef_ref[j], xb.astype(BF16), preferred_element_type=F32)
            r0 = pl.multiple_of(j * n1, n1)
            scr_f[pl.ds(r0, n1), :] = _pack_complex(af[:n1], af[n1:])
            scr_b[pl.ds(r0, n1), :] = _pack_complex(ab[:n1], ab[n1:])
            return c

        lax.fori_loop(0, n2, stage1, 0, unroll=DFT_UNROLL)

    def stage2(t, c):
        k1 = kk * kc + t
        a = jnp.concatenate([_unpack_complex(scr_f[pl.ds(k1, n2, stride=n1), :]),
                             _unpack_complex(scr_b[pl.ds(k1, n2, stride=n1), :])], axis=0)
        o_ref[t] = (jnp.dot(fs_ref[...], a, preferred_element_type=F32) * sc_ref[...]).astype(o_ref.dtype)
        return c

    lax.fori_loop(0, kc, stage2, 0, unroll=DFT_UNROLL)


def filter_spectra(filt, scale, tables):
    seq = filt.shape[0]
    n_order, _, ch = scale.shape
    n2 = _dft_n2(seq)
    n1h = seq // n2
    n1 = 2 * n1h
    e_fwd, f_spec = tables[0], tables[4]
    kc = min(n1, 16)
    cb = ch // LANES
    kern = functools.partial(_spectrum_kernel, n1=n1, n1h=n1h, n2=n2, kc=kc)
    return pl.pallas_call(
        kern,
        grid=(n_order, cb, n1 // kc),
        in_specs=[
            pl.BlockSpec((seq, LANES), lambda o, c, k: (0, (2 * o) * cb + c)),
            pl.BlockSpec((seq, LANES), lambda o, c, k: (0, (2 * o + 1) * cb + c)),
            pl.BlockSpec((None, 1, LANES), lambda o, c, k: (o, 0, c)),
            _const_spec(e_fwd.shape),
            _const_spec(f_spec.shape),
        ],
        out_specs=pl.BlockSpec((None, kc, 2 * n2, LANES), lambda o, c, k: (o, k, 0, c)),
        out_shape=jax.ShapeDtypeStruct((n_order, n1, 2 * n2, ch), BF16),
        scratch_shapes=[pltpu.VMEM((n1 * n2, LANES), jnp.uint32), pltpu.VMEM((n1 * n2, LANES), jnp.uint32)],
        compiler_params=_cparams(("parallel", "parallel", "arbitrary")),
        name="filter_spectra",
    )(filt, filt, scale, e_fwd, f_spec)


def _longconv_kernel(u_ref, g_ref, h_ref, bias_ref, ef_ref, ff_ref, fi_ref, ei_ref, o_ref, scr_a, scr_b,
                     *, n1, n1h, n2):
    def stage1(j, c):
        x = u_ref[pl.ds(j, n1h, stride=n2), :].astype(BF16)
        a = jnp.dot(ef_ref[j], x, preferred_element_type=F32)
        scr_a[pl.ds(pl.multiple_of(j * n1, n1), n1), :] = _pack_complex(a[:n1], a[n1:])
        return c

    lax.fori_loop(0, n2, stage1, 0, unroll=DFT_UNROLL)

    def stage2(k1, c):
        a = _unpack_complex(scr_a[pl.ds(k1, n2, stride=n1), :])
        y = jnp.dot(ff_ref[...], a, preferred_element_type=F32)
        hk = h_ref[k1].astype(F32)
        yr, yi = y[:n2], y[n2:]
        hr, hi = hk[:n2], hk[n2:]
        z = jnp.concatenate([yr * hr - yi * hi, yr * hi + yi * hr], axis=0).astype(BF16)
        bk = jnp.dot(fi_ref[...], z, preferred_element_type=F32)
        scr_b[pl.ds(pl.multiple_of(k1 * n2, n2), n2), :] = _pack_complex(bk[:n2], bk[n2:])
        return c

    lax.fori_loop(0, n1, stage2, 0, unroll=DFT_UNROLL)

    def stage3(j, c):
        bmat = _unpack_complex(scr_b[pl.ds(j, n1, stride=n2), :])
        y = jnp.dot(ei_ref[j], bmat, preferred_element_type=F32)
        u = u_ref[pl.ds(j, n1h, stride=n2), :]
        g = g_ref[pl.ds(j, n1h, stride=n2), :]
        o_ref[pl.ds(j, n1h, stride=n2), :] = g * (y + u * bias_ref[...])
        return c

    lax.fori_loop(0, n2, stage3, 0, unroll=DFT_UNROLL)


def long_conv_gated(u, u_blk, g, g_blk, spec, bias, tables):
    b, seq, _ = u.shape
    ch = bias.shape[0]
    n2 = _dft_n2(seq)
    n1h = seq // n2
    n1 = 2 * n1h
    e_fwd, f_fwd, f_inv, e_inv = tables[:4]
    kern = functools.partial(_longconv_kernel, n1=n1, n1h=n1h, n2=n2)
    one = pl.Buffered(1)
    return pl.pallas_call(
        kern,
        grid=(ch // LANES, b),
        in_specs=[
            pl.BlockSpec((None, seq, LANES), lambda c, i: (i, 0, u_blk + c), pipeline_mode=one),
            pl.BlockSpec((None, seq, LANES), lambda c, i: (i, 0, g_blk + c), pipeline_mode=one),
            pl.BlockSpec((n1, 2 * n2, LANES), lambda c, i: (0, 0, c), pipeline_mode=one),
            pl.BlockSpec((1, LANES), lambda c, i: (0, c)),
            _const_spec(e_fwd.shape),
            _const_spec(f_fwd.shape),
            _const_spec(f_inv.shape),
            _const_spec(e_inv.shape),
        ],
        out_specs=pl.BlockSpec((None, seq, LANES), lambda c, i: (i, 0, c)),
        out_shape=jax.ShapeDtypeStruct((b, seq, ch), F32),
        scratch_shapes=[pltpu.VMEM((n1 * n2, LANES), jnp.uint32), pltpu.VMEM((n1 * n2, LANES), jnp.uint32)],
        compiler_params=_cparams(("parallel", "parallel")),
        name="long_conv",
    )(u, g, spec, bias.reshape(1, ch).astype(F32), e_fwd, f_fwd, f_inv, e_inv)


def _dft_tables_complex(seq):
    n2 = _dft_n2(seq)
    n1 = seq // n2
    k1 = np.arange(n1, dtype=np.float64)[None, :, None]
    nn = (n2 * np.arange(n1, dtype=np.float64)[None, None, :] + np.arange(n2, dtype=np.float64)[:, None, None])
    ang = 2.0 * np.pi * k1 * nn / seq
    c, s = np.cos(ang), np.sin(ang)
    e_fwd = np.concatenate([np.concatenate([c, s], axis=2), np.concatenate([-s, c], axis=2)], axis=1)
    a2 = 2.0 * np.pi * np.outer(np.arange(n2), np.arange(n2)) / n2
    f_re = np.concatenate([np.cos(a2), np.sin(a2)], axis=1)
    return jnp.asarray(e_fwd, BF16), jnp.asarray(f_re, BF16)


def _seqdft_kernel(vr_ref, vi_ref, ef_ref, fr_ref, o_ref, scr, *, n1, n2, scale):
    def stage1(j, c):
        x = jnp.concatenate([vr_ref[pl.ds(j, n1, stride=n2), :], vi_ref[pl.ds(j, n1, stride=n2), :]], axis=0)
        a = jnp.dot(ef_ref[j], x.astype(BF16), preferred_element_type=F32)
        scr[pl.ds(pl.multiple_of(j * n1, n1), n1), :] = _pack_complex(a[:n1], a[n1:])
        return c

    lax.fori_loop(0, n2, stage1, 0, unroll=DFT_UNROLL)

    def stage2(k1, c):
        a = _unpack_complex(scr[pl.ds(k1, n2, stride=n1), :])
        o_ref[pl.ds(k1, n2, stride=n1), :] = jnp.dot(fr_ref[...], a, preferred_element_type=F32) * scale
        return c

    lax.fori_loop(0, n1, stage2, 0, unroll=DFT_UNROLL)


def seq_dft_real(v, ch, scale, tables):
    b, seq, _ = v.shape
    n2 = _dft_n2(seq)
    n1 = seq // n2
    e_fwd, f_re = tables
    nblk = ch // LANES
    kern = functools.partial(_seqdft_kernel, n1=n1, n2=n2, scale=scale)
    return pl.pallas_call(
        kern,
        grid=(nblk, b),
        in_specs=[
            pl.BlockSpec((None, seq, LANES), lambda c, i: (i, 0, c)),
            pl.BlockSpec((None, seq, LANES), lambda c, i: (i, 0, nblk + c)),
            _const_spec(e_fwd.shape),
            _const_spec(f_re.shape),
        ],
        out_specs=pl.BlockSpec((None, seq, LANES), lambda c, i: (i, 0, c)),
        out_shape=jax.ShapeDtypeStruct((b, seq, ch), F32),
        scratch_shapes=[pltpu.VMEM((n1 * n2, LANES), jnp.uint32)],
        compiler_params=_cparams(("parallel", "parallel")),
        name="seq_dft",
    )(v, v, e_fwd, f_re)


def fourier_channel_matrix():
    a = 2.0 * np.pi * np.outer(np.arange(FN_GROUP_DIM), np.arange(FN_GROUP_DIM)) / FN_GROUP_DIM
    eye = np.eye(FN_GROUPS)
    return jnp.asarray(np.concatenate([np.kron(eye, np.cos(a)), -np.kron(eye, np.sin(a))], axis=1), BF16)


def _shortconv_kernel(u_ref, w_ref, b_ref, o_ref):
    u = u_ref[...]
    n = u.shape[0]
    row = lax.broadcasted_iota(jnp.int32, u.shape, 0)
    prev = jnp.where(row == 0, 0.0, pltpu.roll(u, 1, axis=0))
    nxt = jnp.where(row == n - 1, 0.0, pltpu.roll(u, n - 1, axis=0))
    o_ref[...] = prev * w_ref[0:1, :] + u * w_ref[1:2, :] + nxt * w_ref[2:3, :] + b_ref[...]


def short_conv(u, w, bias):
    b, seq, ch = u.shape
    return pl.pallas_call(
        _shortconv_kernel,
        grid=(b, ch // LANES),
        in_specs=[
            pl.BlockSpec((None, seq, LANES), lambda i, c: (i, 0, c)),
            pl.BlockSpec((HY_SHORT, LANES), lambda i, c: (0, c)),
            pl.BlockSpec((1, LANES), lambda i, c: (0, c)),
        ],
        out_specs=pl.BlockSpec((None, seq, LANES), lambda i, c: (i, 0, c)),
        out_shape=jax.ShapeDtypeStruct((b, seq, ch), F32),
        compiler_params=_cparams(("parallel", "parallel")),
        name="short_conv",
    )(u, w.astype(F32), bias.reshape(1, ch).astype(F32))


def _filter_kernel(emb_ref, w1_ref, b1_ref, f1_ref, w2_ref, b2_ref, f2_ref, w3_ref, b3_ref, dec_ref, o_ref):
    z = jnp.dot(emb_ref[...].astype(BF16), w1_ref[...], preferred_element_type=F32) + b1_ref[...]
    z = jnp.sin(f1_ref[...] * z)
    z = jnp.dot(z.astype(BF16), w2_ref[...], preferred_element_type=F32) + b2_ref[...]
    z = jnp.sin(f2_ref[...] * z)
    h = jnp.dot(z.astype(BF16), w3_ref[...], preferred_element_type=F32) + b3_ref[...]
    o_ref[...] = h * dec_ref[...]


def hyena_filters(seq, hy_w1, hy_b1, hy_freq1, hy_w2, hy_b2, hy_freq2, hy_w3, hy_b3):
    t = jnp.linspace(0.0, 1.0, seq, dtype=F32)[:, None]
    ang = (2.0 * math.pi / seq) * jnp.arange(seq, dtype=F32)[:, None]
    bands = jnp.linspace(1e-4, HY_EMB_BANDS - 1, HY_EMB_BANDS, dtype=F32)[None, :]
    emb = jnp.concatenate([t, jnp.cos(bands * ang), -jnp.sin(bands * ang)], axis=-1)
    kdim = emb.shape[1]
    kpad = LANES - kdim
    emb = jnp.pad(emb, ((0, 0), (0, kpad)))
    w1 = jnp.pad(hy_w1, ((0, kpad), (0, 0))).astype(BF16)
    deltas = jnp.abs(jnp.linspace(math.log(HY_DECAY_TARGET) / HY_SLOW_DECAY,
                                  math.log(HY_DECAY_TARGET) / HY_FAST_DECAY, HY_WIDTH, dtype=F32))
    decay = jnp.tile(jnp.exp(-t * deltas), (1, 2 * HY_ORDER))
    fo = hy_w1.shape[1]
    nout = hy_w3.shape[1]
    tl = _tile(seq, 1024)
    row = lambda a: a.reshape(1, -1).astype(F32)
    full = lambda shape: pl.BlockSpec(shape, lambda i: (0, 0))
    return pl.pallas_call(
        _filter_kernel,
        grid=(seq // tl,),
        in_specs=[
            pl.BlockSpec((tl, LANES), lambda i: (i, 0)),
            full((LANES, fo)), full((1, fo)), full((1, fo)),
            full((fo, fo)), full((1, fo)), full((1, fo)),
            full((fo, nout)), full((1, nout)),
            pl.BlockSpec((tl, nout), lambda i: (i, 0)),
        ],
        out_specs=pl.BlockSpec((tl, nout), lambda i: (i, 0)),
        out_shape=jax.ShapeDtypeStruct((seq, nout), F32),
        compiler_params=_cparams(("parallel",)),
        name="hyena_filter",
    )(emb, w1, row(hy_b1), row(hy_freq1), hy_w2.astype(BF16), row(hy_b2), row(hy_freq2),
      hy_w3.astype(BF16), row(hy_b3), decay)


def hyena_spectra(seq, filt, tables):
    mag = jnp.abs(filt).reshape(seq, HY_ORDER, 2, HY_WIDTH)
    norm = jnp.sum(mag[:, :, 0], axis=0) + jnp.sum(mag[1:, :, 1], axis=0)
    scale = (1.0 / (2 * seq)) / norm
    return filter_spectra(filt, scale[:, None, :], tables)


def _merge_kernel(x_ref, yf_ref, yh_ref, ya_ref, g_ref, wf_ref, wh_ref, wa_ref, wo_ref,
                  gate_ref, ng_ref, sh_ref, sc_ref, *rest):
    xo_ref, ho_ref = rest[-2:]
    d = x_ref.shape[-1]
    g = g_ref[0].astype(F32)
    yf = jnp.dot(yf_ref[0].astype(BF16), wf_ref[...], preferred_element_type=F32)
    yh = jnp.dot(yh_ref[0].astype(BF16), wh_ref[...], preferred_element_type=F32)
    ya = jnp.dot(ya_ref[0], wa_ref[...], preferred_element_type=F32)
    mix = g[:, 0:d] * yf + g[:, d:2 * d] * yh + g[:, 2 * d:3 * d] * ya
    x = x_ref[0] + gate_ref[0] * jnp.dot(mix.astype(BF16), wo_ref[...], preferred_element_type=F32)
    xo_ref[...] = x
    y = x * lax.rsqrt(jnp.mean(x * x, axis=-1, keepdims=True) + EPS) * ng_ref[...]
    ho_ref[...] = (y * (1.0 + sc_ref[0]) + sh_ref[0]).astype(ho_ref.dtype)


def merge_branches(x, yf, yh, ya, g, w_f, w_h, w_a, w_o, gate, norm_g, shift, scale, total_rows, row_offset, prior):
    b, l, d = x.shape
    tl = _tile(l, 512)
    assert row_offset % tl == 0
    rows = lambda w: pl.BlockSpec((1, tl, w), lambda i, j: (i, j, 0))
    full = lambda a: pl.BlockSpec(a.shape, lambda i, j: (0, 0))
    per_b = pl.BlockSpec((1, 1, d), lambda i, j: (i, 0, 0))
    out_rows = pl.BlockSpec((tl, d), lambda i, j: (row_offset // tl + i * (l // tl) + j, 0))
    wf, wh, wa, wo = (w.astype(BF16) for w in (w_f, w_h, w_a, w_o))
    in_specs = [rows(d), rows(yf.shape[-1]), rows(yh.shape[-1]), rows(ya.shape[-1]), rows(3 * d),
                full(wf), full(wh), full(wa), full(wo),
                per_b, pl.BlockSpec((1, d), lambda i, j: (0, 0)), per_b, per_b]
    args = [x, yf, yh, ya, g, wf, wh, wa, wo, gate.reshape(b, 1, d), norm_g.reshape(1, d).astype(F32),
            shift.reshape(b, 1, d), scale.reshape(b, 1, d)]
    aliases = {}
    if prior is not None:
        aliases = {len(args): 0, len(args) + 1: 1}
        in_specs += [pl.BlockSpec(memory_space=pl.ANY), pl.BlockSpec(memory_space=pl.ANY)]
        args += list(prior)
    return pl.pallas_call(
        _merge_kernel,
        grid=(b, l // tl),
        in_specs=in_specs,
        out_specs=[out_rows, out_rows],
        out_shape=[jax.ShapeDtypeStruct((total_rows, d), F32), jax.ShapeDtypeStruct((total_rows, d), BF16)],
        input_output_aliases=aliases,
        compiler_params=_cparams(("parallel", "parallel")),
        name="merge",
    )(*args)


GLU_GROUP = 2 * LANES


def _glu_group_permutation():
    p = np.zeros((GLU_GROUP, GLU_GROUP), np.float32)
    j = np.arange(LANES)
    p[2 * j, j] = 1.0
    p[2 * j + 1, LANES + j] = 1.0
    return jnp.asarray(p, BF16)


def _moe_kernel(be_ref, act_ref, new_ref, rows_ref, w1_ref, b1_ref, w2_ref, b2_ref, p_ref, o_ref, w1s, w2s):
    i = pl.program_id(0)

    @pl.when(new_ref[i] > 0)
    def _():
        for q in range(w1s.shape[1] // GLU_GROUP):
            cols = slice(q * GLU_GROUP, (q + 1) * GLU_GROUP)
            w1s[:, cols] = jnp.dot(w1_ref[:, cols].astype(BF16), p_ref[...],
                                   preferred_element_type=F32).astype(BF16)
        w2s[...] = w2_ref[...].astype(BF16)

    @pl.when(act_ref[i] > 0)
    def _():
        u = jnp.dot(rows_ref[...], w1s[...], preferred_element_type=F32) + b1_ref[0]
        parts = []
        for q in range(u.shape[1] // GLU_GROUP):
            xg = jnp.minimum(u[:, q * GLU_GROUP:q * GLU_GROUP + LANES], SWIGLU_LIMIT)
            xl = jnp.clip(u[:, q * GLU_GROUP + LANES:(q + 1) * GLU_GROUP], -SWIGLU_LIMIT, SWIGLU_LIMIT)
            parts.append((xg * jax.nn.sigmoid(SWIGLU_ALPHA * xg) * (xl + 1.0)).astype(BF16))
        a = jnp.concatenate(parts, axis=1)
        y = jnp.dot(a, w2s[...], preferred_element_type=F32) + b2_ref[0]
        o_ref[...] = y.astype(o_ref.dtype)

    @pl.when(act_ref[i] == 0)
    def _():
        o_ref[...] = jnp.zeros(o_ref.shape, o_ref.dtype)


def moe_experts(rows, blk_exp, blk_act, blk_new, layer, w1_all, b1, w2_all, b2):
    r, d = rows.shape
    de = w2_all.shape[2]
    nblk = r // MOE_BLOCK
    grid_spec = pltpu.PrefetchScalarGridSpec(
        num_scalar_prefetch=3,
        grid=(nblk,),
        in_specs=[
            pl.BlockSpec((MOE_BLOCK, d), lambda i, be, act, new: (i, 0)),
            pl.BlockSpec((None, None, d, 2 * de), lambda i, be, act, new: (layer, be[i], 0, 0)),
            pl.BlockSpec((1, 1, 2 * de), lambda i, be, act, new: (be[i], 0, 0)),
            pl.BlockSpec((None, None, de, d), lambda i, be, act, new: (layer, be[i], 0, 0)),
            pl.BlockSpec((1, 1, d), lambda i, be, act, new: (be[i], 0, 0)),
            pl.BlockSpec((GLU_GROUP, GLU_GROUP), lambda i, be, act, new: (0, 0)),
        ],
        out_specs=pl.BlockSpec((MOE_BLOCK, d), lambda i, be, act, new: (i, 0)),
        scratch_shapes=[pltpu.VMEM((d, 2 * de), BF16), pltpu.VMEM((de, d), BF16)],
    )
    return pl.pallas_call(
        _moe_kernel,
        grid_spec=grid_spec,
        out_shape=jax.ShapeDtypeStruct((r, d), BF16),
        compiler_params=_cparams(("arbitrary",)),
        name="moe_experts",
    )(blk_exp, blk_act, blk_new, rows, w1_all, b1, w2_all, b2, _glu_group_permutation())


def _combine_kernel(y_ref, g_ref, x_ref, m_ref, *o_refs, split):
    g = g_ref[...]
    acc = g[:, 0:1] * y_ref[0].astype(F32)
    for j in range(1, TOP_K):
        acc = acc + g[:, j:j + 1] * y_ref[j].astype(F32)
    val = x_ref[...] + m_ref[0] * acc
    if split == 0:
        o_refs[0][...] = val
    else:
        i = pl.program_id(0)

        @pl.when(i < split)
        def _():
            o_refs[0][...] = val

        @pl.when(i >= split)
        def _():
            o_refs[1][...] = val


def moe_combine(y_sel, gate, resid, mod_blocks, tm, split):
    k, t, d = y_sel.shape
    nb = t // tm
    if split == 0:
        out_specs = [pl.BlockSpec((tm, d), lambda i: (i, 0))]
        out_shape = [jax.ShapeDtypeStruct((t, d), F32)]
    else:
        out_specs = [pl.BlockSpec((tm, d), lambda i: (jnp.minimum(i, split - 1), 0)),
                     pl.BlockSpec((tm, d), lambda i: (jnp.maximum(i - split, 0), 0))]
        out_shape = [jax.ShapeDtypeStruct((split * tm, d), F32), jax.ShapeDtypeStruct(((nb - split) * tm, d), F32)]
    return pl.pallas_call(
        functools.partial(_combine_kernel, split=split),
        grid=(nb,),
        in_specs=[pl.BlockSpec((k, tm, d), lambda i: (0, i, 0)),
                  pl.BlockSpec((tm, k), lambda i: (i, 0)),
                  pl.BlockSpec((tm, d), lambda i: (i, 0)),
                  pl.BlockSpec((1, 1, d), lambda i: (i, 0, 0))],
        out_specs=out_specs,
        out_shape=out_shape,
        compiler_params=_cparams(("arbitrary",)),
        name="moe_combine",
    )(y_sel, gate, resid, mod_blocks)


def moe_ffn(h, resid, mod_blocks, tm, split, p):
    t_tok, d = h.shape
    wr = jnp.pad(p["w_router"], ((0, 0), (0, LANES - N_EXPERTS))).astype(BF16)
    br = jnp.pad(p["b_router"], (0, LANES - N_EXPERTS)).reshape(1, LANES).astype(F32)
    logits = matmul(h, wr, epi="bias", extra=(br,),
                    extra_specs=[pl.BlockSpec((1, LANES), lambda i, j: (0, 0))], name="mm_router")[:, :N_EXPERTS]
    top_v, top_i = lax.top_k(logits, TOP_K)
    gate = jax.nn.softmax(top_v, axis=-1)
    n_assign = t_tok * TOP_K
    flat_e = top_i.reshape(-1)
    experts = jnp.arange(N_EXPERTS, dtype=flat_e.dtype)[None, :]
    onehot = (flat_e[:, None] == experts).astype(jnp.int32)
    csum = jnp.cumsum(onehot, axis=0)
    counts = csum[-1]
    padded = (counts + MOE_BLOCK - 1) // MOE_BLOCK * MOE_BLOCK
    pad_end = jnp.cumsum(padded)
    pad_start = pad_end - padded
    dest = jnp.sum(onehot * (csum - 1 + pad_start[None, :]), axis=1)
    n_blocks = -(-n_assign // MOE_BLOCK) + N_EXPERTS
    n_rows = n_blocks * MOE_BLOCK
    filled = jnp.zeros((n_rows,), jnp.int32).at[dest].add(jnp.arange(n_assign, dtype=jnp.int32) // TOP_K + 1)
    row_tok = jnp.where(filled > 0, filled - 1, jnp.arange(n_rows, dtype=jnp.int32) % t_tok)
    blk_start = jnp.arange(n_blocks, dtype=jnp.int32) * MOE_BLOCK
    blk_exp = jnp.minimum(jnp.sum((blk_start[:, None] >= pad_end[None, :]).astype(jnp.int32), axis=1),
                          N_EXPERTS - 1)
    blk_act = (blk_start < pad_end[-1]).astype(jnp.int32)
    blk_new = jnp.concatenate([jnp.ones((1,), jnp.int32), (blk_exp[1:] != blk_exp[:-1]).astype(jnp.int32)])
    y_rows = moe_experts(h[row_tok], blk_exp, blk_act, blk_new, p["layer"], p["w1_all"], p["b1"],
                         p["w2_all"], p["b2"])
    y_sel = y_rows[dest.reshape(t_tok, TOP_K).T]
    return moe_combine(y_sel, gate, resid, mod_blocks, tm, split)


OFF_F = 0
OFF_HY = OFF_F + FN_WIDTH
OFF_Q = OFF_HY + (HY_ORDER + 1) * HY_WIDTH
OFF_K = OFF_Q + COL_QK
OFF_V = OFF_K + COL_QK
OFF_G = OFF_V + DA_WIDTH


def _projection_weights(w_in):
    w = w_in.astype(BF16)
    w_fv = matmul(w[:, OFF_F:OFF_HY], fourier_channel_matrix(), out_dtype=BF16, name="mm_wfold")
    w_cat = jnp.concatenate([w_fv, w[:, OFF_HY:]], axis=1)
    shift = w_fv.shape[1] - (OFF_HY - OFF_F)
    bounds = {"hy": (OFF_HY, OFF_Q), "q": (OFF_Q, OFF_K), "k": (OFF_K, OFF_V), "v": (OFF_V, OFF_G),
              "g": (OFF_G, w_in.shape[1])}
    cols = {"f": (0, w_fv.shape[1])}
    cols.update({name: (a + shift, b + shift) for name, (a, b) in bounds.items()})
    return w_cat, cols


def _project(x, mod_shift, mod_scale, p, rope):
    q_gain = p["q_norm_g"] * (DA_QK_DIM ** -0.5 * math.log2(math.e))
    return project_all(x, p["norm1_g"], mod_shift, mod_scale, p["w_cat"], p["cols"], q_gain, p["k_norm_g"],
                       rope[0], rope[1])


def _token_mixer(x, mod_shift1, mod_scale1, mod_gate, mod_shift2, mod_scale2, p, lam, lam_init, rope, kv_extra,
                 rows_out):
    b, s, d = x.shape
    v_f, z, q, k3, v3, g = _project(x, mod_shift1, mod_scale1, p, rope)

    y_f = seq_dft_real(v_f, FN_WIDTH, 1.0 / math.sqrt(s * FN_GROUP_DIM), _dft_tables_complex(s))

    z = short_conv(z, p["hy_conv_w"], p["hy_conv_b"])
    tables = _dft_tables_real(s)
    filt = hyena_filters(s, p["hy_w1"], p["hy_b1"], p["hy_freq1"], p["hy_w2"], p["hy_b2"], p["hy_freq2"],
                         p["hy_w3"], p["hy_b3"])
    spec = hyena_spectra(s, filt, tables)
    cb = HY_WIDTH // LANES
    y_h = long_conv_gated(z, 0, z, cb, spec[0], p["hy_bias"][0], tables)
    y_h = long_conv_gated(y_h, 0, z, 2 * cb, spec[1], p["hy_bias"][1], tables)

    if kv_extra is not None:
        k_all = jnp.concatenate([k3, kv_extra[0]], axis=1)
        v_all = jnp.concatenate([v3, kv_extra[1]], axis=1)
    else:
        k_all, v_all = k3, v3
    nk = k_all.shape[1]
    y_a = diff_attention(q.reshape(b * s, COL_QK), k_all.reshape(b * nk, COL_QK),
                         v_all.reshape(b * nk, 2 * DA_WIDTH), lam, p["subln_g"], 1.0 - lam_init, s, nk)
    y_a = y_a.reshape(b, s, DA_WIDTH)

    total_rows, row_offset, prior = rows_out
    buffers = merge_branches(x, y_f, y_h, y_a, g, p["w_f"], p["w_h"], p["w_a"], p["w_o"],
                             mod_gate, p["norm2_g"], mod_shift2, mod_scale2, total_rows, row_offset, prior)
    return buffers, (k3, v3)


def _layer(l, x, xc, c, c_ctx, p, ctx_out):
    b, n_lat, d = x.shape
    n_ctx = xc.shape[1]
    lam_init = 0.8 - 0.6 * math.exp(-0.3 * l)
    lam = (jnp.exp(jnp.sum(p["lam_q"][0] * p["lam_k"][0]).astype(F32))
           - jnp.exp(jnp.sum(p["lam_q"][1] * p["lam_k"][1]).astype(F32)) + lam_init)

    cond = jnp.concatenate([c, c_ctx[None, :], jnp.zeros((16 - b - 1, d), F32)], axis=0)
    mod_all = matmul(jax.nn.silu(cond).astype(BF16), p["w_mod"].astype(BF16), epi="bias",
                     extra=(p["b_mod"].reshape(1, 6 * d).astype(F32),),
                     extra_specs=[pl.BlockSpec((1, 1024), lambda i, j: (0, j))], name="mm_mod")
    mod = [mod_all[:b, i * d:(i + 1) * d] for i in range(6)]
    mod_c = [jnp.broadcast_to(mod_all[b, i * d:(i + 1) * d], (b, d)) for i in range(6)]

    pw = dict(p)
    pw["w_cat"], pw["cols"] = _projection_weights(p["w_in"])

    no_rope = (jnp.ones((n_ctx, LANES), F32), jnp.zeros((n_ctx, LANES), F32))
    rows_c = b * n_ctx if ctx_out else 0
    total_rows = rows_c + b * n_lat
    prior = None
    if ctx_out:
        prior, kv_c = _token_mixer(xc, mod_c[0], mod_c[1], mod_c[2], mod_c[3], mod_c[4], pw, lam, lam_init,
                                   no_rope, None, (total_rows, 0, None))
    else:
        kv_c = _project(xc, mod_c[0], mod_c[1], pw, no_rope)[3:5]
    (resid, h_all), _ = _token_mixer(x, mod[0], mod[1], mod[2], mod[3], mod[4], pw, lam, lam_init,
                                     rope_tables(n_lat), kv_c, (total_rows, rows_c, prior))

    n_exp, two_f = p["b_e1"].shape
    b1 = p["b_e1"].reshape(n_exp, two_f // GLU_GROUP, LANES, 2).transpose(0, 1, 3, 2).reshape(n_exp, 1, two_f)
    pe = {
        "w_router": p["w_router"], "b_router": p["b_router"], "layer": l,
        "w1_all": p["w_e1_all"], "b1": b1, "w2_all": p["w_e2_all"], "b2": p["b_e2"][:, None, :],
    }
    tm = 512
    assert rows_c % tm == 0 and n_lat % tm == 0
    split = rows_c // tm
    mod_blocks = jnp.concatenate([jnp.tile(mod_c[5][:1], (split, 1)), jnp.repeat(mod[5], n_lat // tm, axis=0)], axis=0)
    outs = moe_ffn(h_all, resid, mod_blocks[:, None, :], tm, split, pe)
    if ctx_out:
        xc = outs[0].reshape(b, n_ctx, d)
    x = outs[-1].reshape(b, n_lat, d)
    return x, xc


_PARAM_NAMES = ("w_mod", "b_mod", "norm1_g", "norm2_g", "w_in", "hy_conv_w", "hy_conv_b", "hy_w1", "hy_b1",
                "hy_freq1", "hy_w2", "hy_b2", "hy_freq2", "hy_w3", "hy_b3", "hy_bias", "q_norm_g", "k_norm_g",
                "lam_q", "lam_k", "subln_g", "w_f", "w_h", "w_a", "w_o", "w_router", "b_router",
                "w_e1", "b_e1", "w_e2", "b_e2")


def kernel(x, c, ctx, c_ctx, w_mod, b_mod, norm1_g, norm2_g, w_in, hy_conv_w, hy_conv_b, hy_w1, hy_b1, hy_freq1,
           hy_w2, hy_b2, hy_freq2, hy_w3, hy_b3, hy_bias, q_norm_g, k_norm_g, lam_q, lam_k, subln_g, w_f, w_h,
           w_a, w_o, w_router, b_router, w_e1, b_e1, w_e2, b_e2):
    stacked = (w_mod, b_mod, norm1_g, norm2_g, w_in, hy_conv_w, hy_conv_b, hy_w1, hy_b1, hy_freq1, hy_w2, hy_b2,
               hy_freq2, hy_w3, hy_b3, hy_bias, q_norm_g, k_norm_g, lam_q, lam_k, subln_g, w_f, w_h, w_a, w_o,
               w_router, b_router, w_e1, b_e1, w_e2, b_e2)
    depth = w_mod.shape[0]
    xc = ctx
    for l in range(depth):
        p = {name: arr[l] for name, arr in zip(_PARAM_NAMES, stacked) if name not in ("w_e1", "w_e2")}
        p["w_e1_all"], p["w_e2_all"] = w_e1, w_e2
        x, xc = _layer(l, x, xc, c, c_ctx, p, l < depth - 1)
    return x
```

```python
import functools
import math

import numpy as np
import jax
import jax.numpy as jnp
from jax import lax
from jax.experimental import pallas as pl
from jax.experimental.pallas import tpu as pltpu

F32 = jnp.float32
BF16 = jnp.bfloat16

LANES = 128
VMEM_LIMIT = 56 * 1024 * 1024

GRID_W = 64
EPS = 1e-6
SUBLN_EPS = 1e-5
FN_GROUPS = 4
FN_GROUP_DIM = 64
FN_WIDTH = FN_GROUPS * FN_GROUP_DIM
HY_WIDTH = 256
HY_ORDER = 2
HY_SHORT = 3
HY_EMB_BANDS = 16
HY_DECAY_TARGET = 1e-2
HY_FAST_DECAY = 0.3
HY_SLOW_DECAY = 1.5
DA_HEADS = 4
DA_QK_DIM = 64
DA_V_DIM = 2 * DA_QK_DIM
DA_WIDTH = DA_HEADS * DA_V_DIM
ROPE_BASE = 10000.0
N_BRANCHES = 3
COL_QK = DA_HEADS * 2 * DA_QK_DIM
N_EXPERTS = 32
TOP_K = 4
SWIGLU_ALPHA = 1.702
SWIGLU_LIMIT = 7.0
MOE_BLOCK = 512
PROJ_GATE_CHUNK = 1024
DFT_MIN_N1 = 16
DFT_UNROLL = 8


def _dft_n2(seq):
    return min(LANES, seq // DFT_MIN_N1)


def _cparams(sem):
    return pltpu.CompilerParams(dimension_semantics=sem, vmem_limit_bytes=VMEM_LIMIT)


def _tile(n, pref):
    if n <= pref:
        return n
    for t in range(pref, 7, -1):
        if n % t == 0 and t % 8 == 0:
            return t
    return n


def _const_spec(shape):
    nd = len(shape)
    return pl.BlockSpec(shape, lambda *_: (0,) * nd, pipeline_mode=pl.Buffered(1))


def _qk_epilogue(acc, gm, gain, cos, sin):
    ms = jnp.dot((acc * acc).astype(BF16), gm, preferred_element_type=F32)
    y = acc * lax.rsqrt(ms + EPS) * gain
    n = y.shape[1]
    reps = n // LANES
    lane = lax.broadcasted_iota(jnp.int32, y.shape, 1)
    is_a = (lane % (DA_QK_DIM // 2)) < (DA_QK_DIM // 4)
    half = DA_QK_DIM // 4
    swapped = jnp.where(is_a, pltpu.roll(y, n - half, axis=1), pltpu.roll(y, half, axis=1))
    return y * jnp.tile(cos, (1, reps)) + swapped * jnp.tile(sin, (1, reps))


def _proj_kernel(x_ref, ng_ref, sh_ref, sc_ref, w_ref, gm_ref, qg_ref, kg_ref, cos_ref, sin_ref,
                 vf_ref, z_ref, q_ref, k_ref, v_ref, g_ref, *, cols):
    x = x_ref[0]
    y = x * lax.rsqrt(jnp.mean(x * x, axis=-1, keepdims=True) + EPS) * ng_ref[...]
    h = (y * (1.0 + sc_ref[0]) + sh_ref[0]).astype(BF16)

    def mm(name):
        a, b = cols[name]
        return jnp.dot(h, w_ref[:, a:b], preferred_element_type=F32)

    vf_ref[0] = mm("f")
    z_ref[0] = mm("hy")
    q_ref[0] = _qk_epilogue(mm("q"), gm_ref[...], qg_ref[...], cos_ref[...], sin_ref[...]).astype(q_ref.dtype)
    k_ref[0] = _qk_epilogue(mm("k"), gm_ref[...], kg_ref[...], cos_ref[...], sin_ref[...]).astype(k_ref.dtype)
    vv = mm("v").astype(v_ref.dtype)
    ones = jnp.ones((vv.shape[0], DA_V_DIM), v_ref.dtype)
    for hd in range(DA_HEADS):
        v_ref[0, :, 2 * hd * DA_V_DIM:(2 * hd + 1) * DA_V_DIM] = vv[:, hd * DA_V_DIM:(hd + 1) * DA_V_DIM]
        v_ref[0, :, (2 * hd + 1) * DA_V_DIM:(2 * hd + 2) * DA_V_DIM] = ones
    a, b = cols["g"]
    for c0 in range(a, b, PROJ_GATE_CHUNK):
        acc = jnp.dot(h, w_ref[:, c0:c0 + PROJ_GATE_CHUNK], preferred_element_type=F32)
        g_ref[0, :, c0 - a:c0 - a + PROJ_GATE_CHUNK] = jax.nn.sigmoid(acc).astype(g_ref.dtype)


def project_all(x, norm_g, shift, scale, w_cat, cols, q_gain, k_gain, cos, sin):
    b, s, d = x.shape
    tm = _tile(s, 512)
    width = {name: stop - start for name, (start, stop) in cols.items()}
    width["v"] *= 2
    gm = _group_mean_matrix(COL_QK, DA_QK_DIM)
    tile_gain = lambda g: jnp.tile(g.astype(F32), COL_QK // DA_QK_DIM).reshape(1, COL_QK)
    rows = lambda w: pl.BlockSpec((1, tm, w), lambda i, j: (i, j, 0))
    per_b = pl.BlockSpec((1, 1, d), lambda i, j: (i, 0, 0))
    full = lambda shape: pl.BlockSpec(shape, lambda i, j: (0, 0))
    outs = [("f", F32), ("hy", F32), ("q", BF16), ("k", BF16), ("v", BF16), ("g", BF16)]
    return pl.pallas_call(
        functools.partial(_proj_kernel, cols=cols),
        grid=(b, s // tm),
        in_specs=[rows(d), full((1, d)), per_b, per_b, _const_spec(w_cat.shape), full(gm.shape),
                  full((1, COL_QK)), full((1, COL_QK)),
                  pl.BlockSpec((tm, LANES), lambda i, j: (j, 0)), pl.BlockSpec((tm, LANES), lambda i, j: (j, 0))],
        out_specs=[rows(width[name]) for name, _ in outs],
        out_shape=[jax.ShapeDtypeStruct((b, s, width[name]), dt) for name, dt in outs],
        compiler_params=_cparams(("parallel", "parallel")),
        name="project_all",
    )(x, norm_g.reshape(1, d).astype(F32), shift.reshape(b, 1, d), scale.reshape(b, 1, d), w_cat, gm,
      tile_gain(q_gain), tile_gain(k_gain), cos, sin)


def _mm_kernel(a_ref, w_ref, *rest, epi):
    acc = jnp.dot(a_ref[...], w_ref[...], preferred_element_type=F32)
    if epi == "plain":
        (o_ref,) = rest
    elif epi == "bias":
        b_ref, o_ref = rest
        acc = acc + b_ref[...]
    else:
        raise ValueError(epi)
    o_ref[...] = acc.astype(o_ref.dtype)


def matmul(a, w, *, out_dtype=F32, epi="plain", extra=(), extra_specs=(), tm=512, tn=1024, name="mm"):
    m, k = a.shape
    k2, n = w.shape
    assert k == k2
    tm = _tile(m, tm)
    tn = _tile(n, tn)
    return pl.pallas_call(
        functools.partial(_mm_kernel, epi=epi),
        grid=(m // tm, n // tn),
        in_specs=[
            pl.BlockSpec((tm, k), lambda i, j: (i, 0)),
            pl.BlockSpec((k, tn), lambda i, j: (0, j)),
            *extra_specs,
        ],
        out_specs=pl.BlockSpec((tm, tn), lambda i, j: (i, j)),
        out_shape=jax.ShapeDtypeStruct((m, n), out_dtype),
        compiler_params=_cparams(("parallel", "parallel")),
        name=name,
    )(a, w, *extra)


def _group_mean_matrix(n, group):
    idx = np.arange(n)
    return jnp.asarray((idx[:, None] // group == idx[None, :] // group).astype(np.float32) / group, BF16)


def rope_tables(n_lat):
    rows = n_lat // GRID_W
    row = np.repeat(np.arange(rows), GRID_W).astype(np.float64)
    col = np.tile(np.arange(GRID_W), rows).astype(np.float64)
    n_freq = DA_QK_DIM // 4
    inv = ROPE_BASE ** (-np.arange(n_freq, dtype=np.float64) / n_freq)
    ang_r = row[:, None] * inv
    ang_c = col[:, None] * inv
    cos = np.concatenate([np.cos(ang_r), np.cos(ang_r), np.cos(ang_c), np.cos(ang_c)], axis=1)
    sin = np.concatenate([-np.sin(ang_r), np.sin(ang_r), -np.sin(ang_c), np.sin(ang_c)], axis=1)
    cos = np.tile(cos, (1, LANES // DA_QK_DIM))
    sin = np.tile(sin, (1, LANES // DA_QK_DIM))
    return jnp.asarray(cos, F32), jnp.asarray(sin, F32)


ATTN_TQ = 512
ATTN_TK = 768


def _attn_kernel(lam_ref, q_ref, k_ref, v_ref, g_ref, o_ref, qs_ref, s_ref, m_ref, acc_ref,
                 *, tq, tk, nkc, nqb, out_scale):
    def stack(qi, qslot):
        q = q_ref[pl.ds(pl.multiple_of(qi * tq, tq), tq), :]
        lane = lax.broadcasted_iota(jnp.int32, q.shape, 1)
        zero = jnp.zeros_like(q)
        qs_ref[qslot, 0:tq, :] = jnp.where(lane < DA_QK_DIM, q, zero)
        qs_ref[qslot, tq:2 * tq, :] = jnp.where(lane >= DA_QK_DIM, q, zero)

    def scores(qslot, j, slot):
        kj = k_ref[j * tk:(j + 1) * tk, :]
        s_ref[slot] = lax.dot_general(qs_ref[qslot], kj, (((1,), (1,)), ((), ())), preferred_element_type=F32)

    def update(j, slot):
        s = s_ref[slot]
        vj = v_ref[j * tk:(j + 1) * tk, :]
        m_prev = m_ref[...]
        m_next = jnp.maximum(m_prev, jnp.max(s, axis=1, keepdims=True))
        p = jnp.exp2(s - jnp.tile(m_next, (1, tk // LANES)))
        alpha = jnp.exp2(m_prev - m_next)
        m_ref[...] = m_next
        acc_ref[...] = acc_ref[...] * jnp.tile(alpha, (1, 2)) + jnp.dot(p.astype(BF16), vj,
                                                                          preferred_element_type=F32)

    def reset():
        m_ref[...] = jnp.full(m_ref.shape, -jnp.inf, F32)
        acc_ref[...] = jnp.zeros(acc_ref.shape, F32)

    stack(0, 0)
    scores(0, 0, 0)
    reset()

    def q_block(qi, local):
        qslot = local % 2
        t0 = local * nkc
        for j in range(nkc):
            nslot = (t0 + j + 1) % 2
            if j + 1 < nkc:
                scores(qslot, j + 1, nslot)
            else:
                nxt = jnp.minimum(qi + 1, nqb - 1)
                stack(nxt, 1 - qslot)
                scores(1 - qslot, 0, nslot)
            update(j, (t0 + j) % 2)
        o1 = acc_ref[0:tq, 0:DA_V_DIM] / acc_ref[0:tq, DA_V_DIM:]
        o2 = acc_ref[tq:2 * tq, 0:DA_V_DIM] / acc_ref[tq:2 * tq, DA_V_DIM:]
        o = o1 - lam_ref[0, 0] * o2
        o = o * lax.rsqrt(jnp.mean(o * o, axis=-1, keepdims=True) + SUBLN_EPS)
        o_ref[pl.ds(pl.multiple_of(qi * tq, tq), tq), :] = (o * g_ref[...] * out_scale).astype(o_ref.dtype)
        reset()

    group = 2
    if nqb % group == 0 and nqb > group:
        def q_group(i, c):
            for local in range(group):
                q_block(i * group + local, local)
            return c

        lax.fori_loop(0, nqb // group, q_group, 0)
    else:
        for qi in range(nqb):
            q_block(qi, qi)


def diff_attention(q, k, v_ext, lam, subln_g, out_scale, nq, nk):
    b = q.shape[0] // nq
    tq = _tile(nq, ATTN_TQ)
    tk = next(t for t in (ATTN_TK, 256, 128) if nk % t == 0)
    nqb, nkc = nq // tq, nk // tk
    kern = functools.partial(_attn_kernel, tq=tq, tk=tk, nkc=nkc, nqb=nqb, out_scale=out_scale)
    return pl.pallas_call(
        kern,
        grid=(b, DA_HEADS),
        in_specs=[
            pl.BlockSpec(memory_space=pltpu.SMEM),
            pl.BlockSpec((nq, DA_V_DIM), lambda bi, h: (bi, h)),
            pl.BlockSpec((nk, DA_V_DIM), lambda bi, h: (bi, h)),
            pl.BlockSpec((nk, 2 * DA_V_DIM), lambda bi, h: (bi, h)),
            pl.BlockSpec((1, DA_V_DIM), lambda bi, h: (0, 0)),
        ],
        out_specs=pl.BlockSpec((nq, DA_V_DIM), lambda bi, h: (bi, h)),
        out_shape=jax.ShapeDtypeStruct((b * nq, DA_WIDTH), BF16),
        scratch_shapes=[
            pltpu.VMEM((2, 2 * tq, DA_V_DIM), BF16),
            pltpu.VMEM((2, 2 * tq, tk), F32),
            pltpu.VMEM((2 * tq, LANES), F32),
            pltpu.VMEM((2 * tq, 2 * DA_V_DIM), F32),
        ],
        compiler_params=_cparams(("parallel", "parallel")),
        name="diff_attn",
    )(lam.reshape(1, 1).astype(F32), q, k, v_ext, subln_g.reshape(1, DA_V_DIM).astype(F32))


def _dft_tables_real(seq):
    n2 = _dft_n2(seq)
    n1h = seq // n2
    n1 = 2 * n1h
    n = n1 * n2
    k1 = np.arange(n1, dtype=np.float64)[None, :, None]
    nn = (n2 * np.arange(n1h, dtype=np.float64)[None, None, :] + np.arange(n2, dtype=np.float64)[:, None, None])
    ang = 2.0 * np.pi * k1 * nn / n
    e_fwd = np.concatenate([np.cos(ang), -np.sin(ang)], axis=1)
    e_inv = np.transpose(e_fwd, (0, 2, 1))
    a2 = 2.0 * np.pi * np.outer(np.arange(n2), np.arange(n2)) / n2
    c, s = np.cos(a2), np.sin(a2)
    f_fwd = np.block([[c, s], [-s, c]])
    f_inv = np.block([[c, -s], [s, c]])
    f_spec = np.block([[c, s, c, s], [-s, c, s, -c]])
    return tuple(jnp.asarray(t, BF16) for t in (e_fwd, f_fwd, f_inv, e_inv, f_spec))


def _pack_complex(re, im):
    r = lax.bitcast_convert_type(re.astype(BF16).astype(F32), jnp.uint32)
    i = lax.bitcast_convert_type(im.astype(BF16).astype(F32), jnp.uint32)
    return r | (i >> 16)


def _unpack_complex(w):
    re = lax.bitcast_convert_type(w & jnp.uint32(0xFFFF0000), F32)
    im = lax.bitcast_convert_type(w << 16, F32)
    return jnp.concatenate([re, im], axis=0).astype(BF16)


def _spectrum_kernel(hf_ref, hb_ref, sc_ref, ef_ref, fs_ref, o_ref, scr_f, scr_b, *, n1, n1h, n2, kc):
    kk = pl.program_id(2)

    @pl.when(kk == 0)
    def _():
        def stage1(j, c):
            xf = hf_ref[pl.ds(j, n1h, stride=n2), :]
            xb = hb_ref[pl.ds(j, n1h, stride=n2), :]
            row = lax.broadcasted_iota(jnp.int32, xb.shape, 0)
            xb = jnp.where((row == 0) & (j == 0), 0.0, xb)
            af = jnp.dot(ef_ref[j], xf.astype(BF16), preferred_element_type=F32)
            ab = jnp.dot(ef_ref[j], xb.astype(BF16), preferred_element_type=F32)
            r0 = pl.multiple_of(j * n1, n1)
            scr_f[pl.ds(r0, n1), :] = _pack_complex(af[:n1], af[n1:])
            scr_b[pl.ds(r0, n1), :] = _pack_complex(ab[:n1], ab[n1:])
            return c

        lax.fori_loop(0, n2, stage1, 0, unroll=DFT_UNROLL)

    def stage2(t, c):
        k1 = kk * kc + t
        a = jnp.concatenate([_unpack_complex(scr_f[pl.ds(k1, n2, stride=n1), :]),
                             _unpack_complex(scr_b[pl.ds(k1, n2, stride=n1), :])], axis=0)
        o_ref[t] = (jnp.dot(fs_ref[...], a, preferred_element_type=F32) * sc_ref[...]).astype(o_ref.dtype)
        return c

    lax.fori_loop(0, kc, stage2, 0, unroll=DFT_UNROLL)


def filter_spectra(filt, scale, tables):
    seq = filt.shape[0]
    n_order, _, ch = scale.shape
    n2 = _dft_n2(seq)
    n1h = seq // n2
    n1 = 2 * n1h
    e_fwd, f_spec = tables[0], tables[4]
    kc = min(n1, 16)
    cb = ch // LANES
    kern = functools.partial(_spectrum_kernel, n1=n1, n1h=n1h, n2=n2, kc=kc)
    return pl.pallas_call(
        kern,
        grid=(n_order, cb, n1 // kc),
        in_specs=[
            pl.BlockSpec((seq, LANES), lambda o, c, k: (0, (2 * o) * cb + c)),
            pl.BlockSpec((seq, LANES), lambda o, c, k: (0, (2 * o + 1) * cb + c)),
            pl.BlockSpec((None, 1, LANES), lambda o, c, k: (o, 0, c)),
            _const_spec(e_fwd.shape),
            _const_spec(f_spec.shape),
        ],
        out_specs=pl.BlockSpec((None, kc, 2 * n2, LANES), lambda o, c, k: (o, k, 0, c)),
        out_shape=jax.ShapeDtypeStruct((n_order, n1, 2 * n2, ch), BF16),
        scratch_shapes=[pltpu.VMEM((n1 * n2, LANES), jnp.uint32), pltpu.VMEM((n1 * n2, LANES), jnp.uint32)],
        compiler_params=_cparams(("parallel", "parallel", "arbitrary")),
        name="filter_spectra",
    )(filt, filt, scale, e_fwd, f_spec)


def _longconv_kernel(u_ref, g_ref, h_ref, bias_ref, ef_ref, ff_ref, fi_ref, ei_ref, o_ref, scr_a, scr_b,
                     *, n1, n1h, n2):
    def stage1(j, c):
        x = u_ref[pl.ds(j, n1h, stride=n2), :].astype(BF16)
        a = jnp.dot(ef_ref[j], x, preferred_element_type=F32)
        scr_a[pl.ds(pl.multiple_of(j * n1, n1), n1), :] = _pack_complex(a[:n1], a[n1:])
        return c

    lax.fori_loop(0, n2, stage1, 0, unroll=DFT_UNROLL)

    def stage2(k1, c):
        a = _unpack_complex(scr_a[pl.ds(k1, n2, stride=n1), :])
        y = jnp.dot(ff_ref[...], a, preferred_element_type=F32)
        hk = h_ref[k1].astype(F32)
        yr, yi = y[:n2], y[n2:]
        hr, hi = hk[:n2], hk[n2:]
        z = jnp.concatenate([yr * hr - yi * hi, yr * hi + yi * hr], axis=0).astype(BF16)
        bk = jnp.dot(fi_ref[...], z, preferred_element_type=F32)
        scr_b[pl.ds(pl.multiple_of(k1 * n2, n2), n2), :] = _pack_complex(bk[:n2], bk[n2:])
        return c

    lax.fori_loop(0, n1, stage2, 0, unroll=DFT_UNROLL)

    def stage3(j, c):
        bmat = _unpack_complex(scr_b[pl.ds(j, n1, stride=n2), :])
        y = jnp.dot(ei_ref[j], bmat, preferred_element_type=F32)
        u = u_ref[pl.ds(j, n1h, stride=n2), :]
        g = g_ref[pl.ds(j, n1h, stride=n2), :]
        o_ref[pl.ds(j, n1h, stride=n2), :] = g * (y + u * bias_ref[...])
        return c

    lax.fori_loop(0, n2, stage3, 0, unroll=DFT_UNROLL)


def long_conv_gated(u, u_blk, g, g_blk, spec, bias, tables):
    b, seq, _ = u.shape
    ch = bias.shape[0]
    n2 = _dft_n2(seq)
    n1h = seq // n2
    n1 = 2 * n1h
    e_fwd, f_fwd, f_inv, e_inv = tables[:4]
    kern = functools.partial(_longconv_kernel, n1=n1, n1h=n1h, n2=n2)
    one = pl.Buffered(1)
    return pl.pallas_call(
        kern,
        grid=(ch // LANES, b),
        in_specs=[
            pl.BlockSpec((None, seq, LANES), lambda c, i: (i, 0, u_blk + c), pipeline_mode=one),
            pl.BlockSpec((None, seq, LANES), lambda c, i: (i, 0, g_blk + c), pipeline_mode=one),
            pl.BlockSpec((n1, 2 * n2, LANES), lambda c, i: (0, 0, c), pipeline_mode=one),
            pl.BlockSpec((1, LANES), lambda c, i: (0, c)),
            _const_spec(e_fwd.shape),
            _const_spec(f_fwd.shape),
            _const_spec(f_inv.shape),
            _const_spec(e_inv.shape),
        ],
        out_specs=pl.BlockSpec((None, seq, LANES), lambda c, i: (i, 0, c)),
        out_shape=jax.ShapeDtypeStruct((b, seq, ch), F32),
        scratch_shapes=[pltpu.VMEM((n1 * n2, LANES), jnp.uint32), pltpu.VMEM((n1 * n2, LANES), jnp.uint32)],
        compiler_params=_cparams(("parallel", "parallel")),
        name="long_conv",
    )(u, g, spec, bias.reshape(1, ch).astype(F32), e_fwd, f_fwd, f_inv, e_inv)


def _dft_tables_complex(seq):
    n2 = _dft_n2(seq)
    n1 = seq // n2
    k1 = np.arange(n1, dtype=np.float64)[None, :, None]
    nn = (n2 * np.arange(n1, dtype=np.float64)[None, None, :] + np.arange(n2, dtype=np.float64)[:, None, None])
    ang = 2.0 * np.pi * k1 * nn / seq
    c, s = np.cos(ang), np.sin(ang)
    e_fwd = np.concatenate([np.concatenate([c, s], axis=2), np.concatenate([-s, c], axis=2)], axis=1)
    a2 = 2.0 * np.pi * np.outer(np.arange(n2), np.arange(n2)) / n2
    f_re = np.concatenate([np.cos(a2), np.sin(a2)], axis=1)
    return jnp.asarray(e_fwd, BF16), jnp.asarray(f_re, BF16)


def _seqdft_kernel(vr_ref, vi_ref, ef_ref, fr_ref, o_ref, scr, *, n1, n2, scale):
    def stage1(j, c):
        x = jnp.concatenate([vr_ref[pl.ds(j, n1, stride=n2), :], vi_ref[pl.ds(j, n1, stride=n2), :]], axis=0)
        a = jnp.dot(ef_ref[j], x.astype(BF16), preferred_element_type=F32)
        scr[pl.ds(pl.multiple_of(j * n1, n1), n1), :] = _pack_complex(a[:n1], a[n1:])
        return c

    lax.fori_loop(0, n2, stage1, 0, unroll=DFT_UNROLL)

    def stage2(k1, c):
        a = _unpack_complex(scr[pl.ds(k1, n2, stride=n1), :])
        o_ref[pl.ds(k1, n2, stride=n1), :] = jnp.dot(fr_ref[...], a, preferred_element_type=F32) * scale
        return c

    lax.fori_loop(0, n1, stage2, 0, unroll=DFT_UNROLL)


def seq_dft_real(v, ch, scale, tables):
    b, seq, _ = v.shape
    n2 = _dft_n2(seq)
    n1 = seq // n2
    e_fwd, f_re = tables
    nblk = ch // LANES
    kern = functools.partial(_seqdft_kernel, n1=n1, n2=n2, scale=scale)
    return pl.pallas_call(
        kern,
        grid=(nblk, b),
        in_specs=[
            pl.BlockSpec((None, seq, LANES), lambda c, i: (i, 0, c)),
            pl.BlockSpec((None, seq, LANES), lambda c, i: (i, 0, nblk + c)),
            _const_spec(e_fwd.shape),
            _const_spec(f_re.shape),
        ],
        out_specs=pl.BlockSpec((None, seq, LANES), lambda c, i: (i, 0, c)),
        out_shape=jax.ShapeDtypeStruct((b, seq, ch), F32),
        scratch_shapes=[pltpu.VMEM((n1 * n2, LANES), jnp.uint32)],
        compiler_params=_cparams(("parallel", "parallel")),
        name="seq_dft",
    )(v, v, e_fwd, f_re)


def fourier_channel_matrix():
    a = 2.0 * np.pi * np.outer(np.arange(FN_GROUP_DIM), np.arange(FN_GROUP_DIM)) / FN_GROUP_DIM
    eye = np.eye(FN_GROUPS)
    return jnp.asarray(np.concatenate([np.kron(eye, np.cos(a)), -np.kron(eye, np.sin(a))], axis=1), BF16)


def _shortconv_kernel(u_ref, w_ref, b_ref, o_ref):
    u = u_ref[...]
    n = u.shape[0]
    row = lax.broadcasted_iota(jnp.int32, u.shape, 0)
    prev = jnp.where(row == 0, 0.0, pltpu.roll(u, 1, axis=0))
    nxt = jnp.where(row == n - 1, 0.0, pltpu.roll(u, n - 1, axis=0))
    o_ref[...] = prev * w_ref[0:1, :] + u * w_ref[1:2, :] + nxt * w_ref[2:3, :] + b_ref[...]


def short_conv(u, w, bias):
    b, seq, ch = u.shape
    return pl.pallas_call(
        _shortconv_kernel,
        grid=(b, ch // LANES),
        in_specs=[
            pl.BlockSpec((None, seq, LANES), lambda i, c: (i, 0, c)),
            pl.BlockSpec((HY_SHORT, LANES), lambda i, c: (0, c)),
            pl.BlockSpec((1, LANES), lambda i, c: (0, c)),
        ],
        out_specs=pl.BlockSpec((None, seq, LANES), lambda i, c: (i, 0, c)),
        out_shape=jax.ShapeDtypeStruct((b, seq, ch), F32),
        compiler_params=_cparams(("parallel", "parallel")),
        name="short_conv",
    )(u, w.astype(F32), bias.reshape(1, ch).astype(F32))


def _filter_kernel(emb_ref, w1_ref, b1_ref, f1_ref, w2_ref, b2_ref, f2_ref, w3_ref, b3_ref, dec_ref, o_ref, mag_ref):
    z = jnp.dot(emb_ref[...].astype(BF16), w1_ref[...], preferred_element_type=F32) + b1_ref[...]
    z = jnp.sin(f1_ref[...] * z)
    z = jnp.dot(z.astype(BF16), w2_ref[...], preferred_element_type=F32) + b2_ref[...]
    z = jnp.sin(f2_ref[...] * z)
    h = jnp.dot(z.astype(BF16), w3_ref[...], preferred_element_type=F32) + b3_ref[...]
    filt = h * dec_ref[...]
    o_ref[...] = filt
    mag_ref[0] = jnp.sum(jnp.abs(filt), axis=0, keepdims=True)


def hyena_filters(seq, hy_w1, hy_b1, hy_freq1, hy_w2, hy_b2, hy_freq2, hy_w3, hy_b3):
    t = jnp.linspace(0.0, 1.0, seq, dtype=F32)[:, None]
    ang = (2.0 * math.pi / seq) * jnp.arange(seq, dtype=F32)[:, None]
    bands = jnp.linspace(1e-4, HY_EMB_BANDS - 1, HY_EMB_BANDS, dtype=F32)[None, :]
    emb = jnp.concatenate([t, jnp.cos(bands * ang), -jnp.sin(bands * ang)], axis=-1)
    kdim = emb.shape[1]
    kpad = LANES - kdim
    emb = jnp.pad(emb, ((0, 0), (0, kpad)))
    w1 = jnp.pad(hy_w1, ((0, kpad), (0, 0))).astype(BF16)
    deltas = jnp.abs(jnp.linspace(math.log(HY_DECAY_TARGET) / HY_SLOW_DECAY,
                                  math.log(HY_DECAY_TARGET) / HY_FAST_DECAY, HY_WIDTH, dtype=F32))
    decay = jnp.tile(jnp.exp(-t * deltas), (1, 2 * HY_ORDER))
    fo = hy_w1.shape[1]
    nout = hy_w3.shape[1]
    tl = _tile(seq, 1024)
    row = lambda a: a.reshape(1, -1).astype(F32)
    full = lambda shape: pl.BlockSpec(shape, lambda i: (0, 0))
    return pl.pallas_call(
        _filter_kernel,
        grid=(seq // tl,),
        in_specs=[
            pl.BlockSpec((tl, LANES), lambda i: (i, 0)),
            full((LANES, fo)), full((1, fo)), full((1, fo)),
            full((fo, fo)), full((1, fo)), full((1, fo)),
            full((fo, nout)), full((1, nout)),
            pl.BlockSpec((tl, nout), lambda i: (i, 0)),
        ],
        out_specs=[pl.BlockSpec((tl, nout), lambda i: (i, 0)), pl.BlockSpec((1, 1, nout), lambda i: (i, 0, 0))],
        out_shape=[jax.ShapeDtypeStruct((seq, nout), F32), jax.ShapeDtypeStruct((seq // tl, 1, nout), F32)],
        compiler_params=_cparams(("parallel",)),
        name="hyena_filter",
    )(emb, w1, row(hy_b1), row(hy_freq1), hy_w2.astype(BF16), row(hy_b2), row(hy_freq2),
      hy_w3.astype(BF16), row(hy_b3), decay)


def hyena_spectra(seq, filt, mag_blocks, tables):
    mag = jnp.sum(mag_blocks, axis=(0, 1)).reshape(HY_ORDER, 2, HY_WIDTH)
    lag0_bwd = jnp.abs(filt[0]).reshape(HY_ORDER, 2, HY_WIDTH)[:, 1]
    norm = mag[:, 0] + mag[:, 1] - lag0_bwd
    scale = (1.0 / (2 * seq)) / norm
    return filter_spectra(filt, scale[:, None, :], tables)


def _merge_kernel(x_ref, yf_ref, yh_ref, ya_ref, g_ref, wf_ref, wh_ref, wa_ref, wo_ref,
                  gate_ref, ng_ref, sh_ref, sc_ref, *rest):
    xo_ref, ho_ref = rest[-2:]
    d = x_ref.shape[-1]
    g = g_ref[0].astype(F32)
    yf = jnp.dot(yf_ref[0].astype(BF16), wf_ref[...], preferred_element_type=F32)
    yh = jnp.dot(yh_ref[0].astype(BF16), wh_ref[...], preferred_element_type=F32)
    ya = jnp.dot(ya_ref[0], wa_ref[...], preferred_element_type=F32)
    mix = g[:, 0:d] * yf + g[:, d:2 * d] * yh + g[:, 2 * d:3 * d] * ya
    x = x_ref[0] + gate_ref[0] * jnp.dot(mix.astype(BF16), wo_ref[...], preferred_element_type=F32)
    xo_ref[...] = x
    y = x * lax.rsqrt(jnp.mean(x * x, axis=-1, keepdims=True) + EPS) * ng_ref[...]
    ho_ref[...] = (y * (1.0 + sc_ref[0]) + sh_ref[0]).astype(ho_ref.dtype)


def merge_branches(x, yf, yh, ya, g, w_f, w_h, w_a, w_o, gate, norm_g, shift, scale, total_rows, row_offset, prior):
    b, l, d = x.shape
    tl = _tile(l, 512)
    assert row_offset % tl == 0
    rows = lambda w: pl.BlockSpec((1, tl, w), lambda i, j: (i, j, 0))
    full = lambda a: pl.BlockSpec(a.shape, lambda i, j: (0, 0))
    per_b = pl.BlockSpec((1, 1, d), lambda i, j: (i, 0, 0))
    out_rows = pl.BlockSpec((tl, d), lambda i, j: (row_offset // tl + i * (l // tl) + j, 0))
    wf, wh, wa, wo = (w.astype(BF16) for w in (w_f, w_h, w_a, w_o))
    in_specs = [rows(d), rows(yf.shape[-1]), rows(yh.shape[-1]), rows(ya.shape[-1]), rows(3 * d),
                full(wf), full(wh), full(wa), full(wo),
                per_b, pl.BlockSpec((1, d), lambda i, j: (0, 0)), per_b, per_b]
    args = [x, yf, yh, ya, g, wf, wh, wa, wo, gate.reshape(b, 1, d), norm_g.reshape(1, d).astype(F32),
            shift.reshape(b, 1, d), scale.reshape(b, 1, d)]
    aliases = {}
    if prior is not None:
        aliases = {len(args): 0, len(args) + 1: 1}
        in_specs += [pl.BlockSpec(memory_space=pl.ANY), pl.BlockSpec(memory_space=pl.ANY)]
        args += list(prior)
    return pl.pallas_call(
        _merge_kernel,
        grid=(b, l // tl),
        in_specs=in_specs,
        out_specs=[out_rows, out_rows],
        out_shape=[jax.ShapeDtypeStruct((total_rows, d), F32), jax.ShapeDtypeStruct((total_rows, d), BF16)],
        input_output_aliases=aliases,
        compiler_params=_cparams(("parallel", "parallel")),
        name="merge",
    )(*args)


GLU_GROUP = 2 * LANES


def _glu_group_permutation():
    p = np.zeros((GLU_GROUP, GLU_GROUP), np.float32)
    j = np.arange(LANES)
    p[2 * j, j] = 1.0
    p[2 * j + 1, LANES + j] = 1.0
    return jnp.asarray(p, BF16)


def _moe_kernel(be_ref, act_ref, new_ref, rows_ref, w1_ref, b1_ref, w2_ref, b2_ref, p_ref, o_ref, w1s, w2s):
    i = pl.program_id(0)

    @pl.when(new_ref[i] > 0)
    def _():
        for q in range(w1s.shape[1] // GLU_GROUP):
            cols = slice(q * GLU_GROUP, (q + 1) * GLU_GROUP)
            w1s[:, cols] = jnp.dot(w1_ref[:, cols].astype(BF16), p_ref[...],
                                   preferred_element_type=F32).astype(BF16)
        w2s[...] = w2_ref[...].astype(BF16)

    @pl.when(act_ref[i] > 0)
    def _():
        u = jnp.dot(rows_ref[...], w1s[...], preferred_element_type=F32) + b1_ref[0]
        parts = []
        for q in range(u.shape[1] // GLU_GROUP):
            xg = jnp.minimum(u[:, q * GLU_GROUP:q * GLU_GROUP + LANES], SWIGLU_LIMIT)
            xl = jnp.clip(u[:, q * GLU_GROUP + LANES:(q + 1) * GLU_GROUP], -SWIGLU_LIMIT, SWIGLU_LIMIT)
            parts.append((xg * jax.nn.sigmoid(SWIGLU_ALPHA * xg) * (xl + 1.0)).astype(BF16))
        a = jnp.concatenate(parts, axis=1)
        y = jnp.dot(a, w2s[...], preferred_element_type=F32) + b2_ref[0]
        o_ref[...] = y.astype(o_ref.dtype)

    @pl.when(act_ref[i] == 0)
    def _():
        o_ref[...] = jnp.zeros(o_ref.shape, o_ref.dtype)


def moe_experts(rows, blk_exp, blk_act, blk_new, layer, w1_all, b1, w2_all, b2):
    r, d = rows.shape
    de = w2_all.shape[2]
    nblk = r // MOE_BLOCK
    grid_spec = pltpu.PrefetchScalarGridSpec(
        num_scalar_prefetch=3,
        grid=(nblk,),
        in_specs=[
            pl.BlockSpec((MOE_BLOCK, d), lambda i, be, act, new: (i, 0)),
            pl.BlockSpec((None, None, d, 2 * de), lambda i, be, act, new: (layer, be[i], 0, 0)),
            pl.BlockSpec((1, 1, 2 * de), lambda i, be, act, new: (be[i], 0, 0)),
            pl.BlockSpec((None, None, de, d), lambda i, be, act, new: (layer, be[i], 0, 0)),
            pl.BlockSpec((1, 1, d), lambda i, be, act, new: (be[i], 0, 0)),
            pl.BlockSpec((GLU_GROUP, GLU_GROUP), lambda i, be, act, new: (0, 0)),
        ],
        out_specs=pl.BlockSpec((MOE_BLOCK, d), lambda i, be, act, new: (i, 0)),
        scratch_shapes=[pltpu.VMEM((d, 2 * de), BF16), pltpu.VMEM((de, d), BF16)],
    )
    return pl.pallas_call(
        _moe_kernel,
        grid_spec=grid_spec,
        out_shape=jax.ShapeDtypeStruct((r, d), BF16),
        compiler_params=_cparams(("arbitrary",)),
        name="moe_experts",
    )(blk_exp, blk_act, blk_new, rows, w1_all, b1, w2_all, b2, _glu_group_permutation())


def _combine_kernel(y_ref, g_ref, x_ref, m_ref, *o_refs, split):
    g = g_ref[...]
    acc = g[:, 0:1] * y_ref[0].astype(F32)
    for j in range(1, TOP_K):
        acc = acc + g[:, j:j + 1] * y_ref[j].astype(F32)
    val = x_ref[...] + m_ref[0] * acc
    if split == 0:
        o_refs[0][...] = val
    else:
        i = pl.program_id(0)

        @pl.when(i < split)
        def _():
            o_refs[0][...] = val

        @pl.when(i >= split)
        def _():
            o_refs[1][...] = val


def moe_combine(y_sel, gate, resid, mod_blocks, tm, split):
    k, t, d = y_sel.shape
    nb = t // tm
    if split == 0:
        out_specs = [pl.BlockSpec((tm, d), lambda i: (i, 0))]
        out_shape = [jax.ShapeDtypeStruct((t, d), F32)]
    else:
        out_specs = [pl.BlockSpec((tm, d), lambda i: (jnp.minimum(i, split - 1), 0)),
                     pl.BlockSpec((tm, d), lambda i: (jnp.maximum(i - split, 0), 0))]
        out_shape = [jax.ShapeDtypeStruct((split * tm, d), F32), jax.ShapeDtypeStruct(((nb - split) * tm, d), F32)]
    return pl.pallas_call(
        functools.partial(_combine_kernel, split=split),
        grid=(nb,),
        in_specs=[pl.BlockSpec((k, tm, d), lambda i: (0, i, 0)),
                  pl.BlockSpec((tm, k), lambda i: (i, 0)),
                  pl.BlockSpec((tm, d), lambda i: (i, 0)),
                  pl.BlockSpec((1, 1, d), lambda i: (i, 0, 0))],
        out_specs=out_specs,
        out_shape=out_shape,
        compiler_params=_cparams(("arbitrary",)),
        name="moe_combine",
    )(y_sel, gate, resid, mod_blocks)


def moe_ffn(h, resid, mod_blocks, tm, split, p):
    t_tok, d = h.shape
    wr = jnp.pad(p["w_router"], ((0, 0), (0, LANES - N_EXPERTS))).astype(BF16)
    br = jnp.pad(p["b_router"], (0, LANES - N_EXPERTS)).reshape(1, LANES).astype(F32)
    logits = matmul(h, wr, epi="bias", extra=(br,),
                    extra_specs=[pl.BlockSpec((1, LANES), lambda i, j: (0, 0))], name="mm_router")[:, :N_EXPERTS]
    top_v, top_i = lax.top_k(logits, TOP_K)
    gate = jax.nn.softmax(top_v, axis=-1)
    n_assign = t_tok * TOP_K
    flat_e = top_i.reshape(-1)
    experts = jnp.arange(N_EXPERTS, dtype=flat_e.dtype)[None, :]
    onehot = (flat_e[:, None] == experts).astype(jnp.int32)
    csum = jnp.cumsum(onehot, axis=0)
    counts = csum[-1]
    padded = (counts + MOE_BLOCK - 1) // MOE_BLOCK * MOE_BLOCK
    pad_end = jnp.cumsum(padded)
    pad_start = pad_end - padded
    dest = jnp.sum(onehot * (csum - 1 + pad_start[None, :]), axis=1)
    n_blocks = -(-n_assign // MOE_BLOCK) + N_EXPERTS
    n_rows = n_blocks * MOE_BLOCK
    filled = jnp.zeros((n_rows,), jnp.int32).at[dest].add(jnp.arange(n_assign, dtype=jnp.int32) // TOP_K + 1)
    row_tok = jnp.where(filled > 0, filled - 1, jnp.arange(n_rows, dtype=jnp.int32) % t_tok)
    blk_start = jnp.arange(n_blocks, dtype=jnp.int32) * MOE_BLOCK
    blk_exp = jnp.minimum(jnp.sum((blk_start[:, None] >= pad_end[None, :]).astype(jnp.int32), axis=1),
                          N_EXPERTS - 1)
    blk_act = (blk_start < pad_end[-1]).astype(jnp.int32)
    blk_new = jnp.concatenate([jnp.ones((1,), jnp.int32), (blk_exp[1:] != blk_exp[:-1]).astype(jnp.int32)])
    y_rows = moe_experts(h[row_tok], blk_exp, blk_act, blk_new, p["layer"], p["w1_all"], p["b1"],
                         p["w2_all"], p["b2"])
    y_sel = y_rows[dest.reshape(t_tok, TOP_K).T]
    return moe_combine(y_sel, gate, resid, mod_blocks, tm, split)


OFF_F = 0
OFF_HY = OFF_F + FN_WIDTH
OFF_Q = OFF_HY + (HY_ORDER + 1) * HY_WIDTH
OFF_K = OFF_Q + COL_QK
OFF_V = OFF_K + COL_QK
OFF_G = OFF_V + DA_WIDTH


def _projection_weights(w_in):
    w = w_in.astype(BF16)
    w_fv = matmul(w[:, OFF_F:OFF_HY], fourier_channel_matrix(), out_dtype=BF16, name="mm_wfold")
    w_cat = jnp.concatenate([w_fv, w[:, OFF_HY:]], axis=1)
    shift = w_fv.shape[1] - (OFF_HY - OFF_F)
    bounds = {"hy": (OFF_HY, OFF_Q), "q": (OFF_Q, OFF_K), "k": (OFF_K, OFF_V), "v": (OFF_V, OFF_G),
              "g": (OFF_G, w_in.shape[1])}
    cols = {"f": (0, w_fv.shape[1])}
    cols.update({name: (a + shift, b + shift) for name, (a, b) in bounds.items()})
    return w_cat, cols


def _project(x, mod_shift, mod_scale, p, rope):
    q_gain = p["q_norm_g"] * (DA_QK_DIM ** -0.5 * math.log2(math.e))
    return project_all(x, p["norm1_g"], mod_shift, mod_scale, p["w_cat"], p["cols"], q_gain, p["k_norm_g"],
                       rope[0], rope[1])


def _token_mixer(x, mod_shift1, mod_scale1, mod_gate, mod_shift2, mod_scale2, p, lam, lam_init, rope, kv_extra,
                 rows_out):
    b, s, d = x.shape
    v_f, z, q, k3, v3, g = _project(x, mod_shift1, mod_scale1, p, rope)

    y_f = seq_dft_real(v_f, FN_WIDTH, 1.0 / math.sqrt(s * FN_GROUP_DIM), _dft_tables_complex(s))

    z = short_conv(z, p["hy_conv_w"], p["hy_conv_b"])
    tables = _dft_tables_real(s)
    filt, mag_blocks = hyena_filters(s, p["hy_w1"], p["hy_b1"], p["hy_freq1"], p["hy_w2"], p["hy_b2"], p["hy_freq2"],
                         p["hy_w3"], p["hy_b3"])
    spec = hyena_spectra(s, filt, mag_blocks, tables)
    cb = HY_WIDTH // LANES
    y_h = long_conv_gated(z, 0, z, cb, spec[0], p["hy_bias"][0], tables)
    y_h = long_conv_gated(y_h, 0, z, 2 * cb, spec[1], p["hy_bias"][1], tables)

    if kv_extra is not None:
        k_all = jnp.concatenate([k3, kv_extra[0]], axis=1)
        v_all = jnp.concatenate([v3, kv_extra[1]], axis=1)
    else:
        k_all, v_all = k3, v3
    nk = k_all.shape[1]
    y_a = diff_attention(q.reshape(b * s, COL_QK), k_all.reshape(b * nk, COL_QK),
                         v_all.reshape(b * nk, 2 * DA_WIDTH), lam, p["subln_g"], 1.0 - lam_init, s, nk)
    y_a = y_a.reshape(b, s, DA_WIDTH)

    total_rows, row_offset, prior = rows_out
    buffers = merge_branches(x, y_f, y_h, y_a, g, p["w_f"], p["w_h"], p["w_a"], p["w_o"],
                             mod_gate, p["norm2_g"], mod_shift2, mod_scale2, total_rows, row_offset, prior)
    return buffers, (k3, v3)


def _layer(l, x, xc, c, c_ctx, p, ctx_out):
    b, n_lat, d = x.shape
    n_ctx = xc.shape[1]
    lam_init = 0.8 - 0.6 * math.exp(-0.3 * l)
    lam = (jnp.exp(jnp.sum(p["lam_q"][0] * p["lam_k"][0]).astype(F32))
           - jnp.exp(jnp.sum(p["lam_q"][1] * p["lam_k"][1]).astype(F32)) + lam_init)

    cond = jnp.concatenate([c, c_ctx[None, :], jnp.zeros((16 - b - 1, d), F32)], axis=0)
    mod_all = matmul(jax.nn.silu(cond).astype(BF16), p["w_mod"].astype(BF16), epi="bias",
                     extra=(p["b_mod"].reshape(1, 6 * d).astype(F32),),
                     extra_specs=[pl.BlockSpec((1, 1024), lambda i, j: (0, j))], name="mm_mod")
    mod = [mod_all[:b, i * d:(i + 1) * d] for i in range(6)]
    mod_c = [jnp.broadcast_to(mod_all[b, i * d:(i + 1) * d], (b, d)) for i in range(6)]

    pw = dict(p)
    pw["w_cat"], pw["cols"] = _projection_weights(p["w_in"])

    no_rope = (jnp.ones((n_ctx, LANES), F32), jnp.zeros((n_ctx, LANES), F32))
    rows_c = b * n_ctx if ctx_out else 0
    total_rows = rows_c + b * n_lat
    prior = None
    if ctx_out:
        prior, kv_c = _token_mixer(xc, mod_c[0], mod_c[1], mod_c[2], mod_c[3], mod_c[4], pw, lam, lam_init,
                                   no_rope, None, (total_rows, 0, None))
    else:
        kv_c = _project(xc, mod_c[0], mod_c[1], pw, no_rope)[3:5]
    (resid, h_all), _ = _token_mixer(x, mod[0], mod[1], mod[2], mod[3], mod[4], pw, lam, lam_init,
                                     rope_tables(n_lat), kv_c, (total_rows, rows_c, prior))

    n_exp, two_f = p["b_e1"].shape
    b1 = p["b_e1"].reshape(n_exp, two_f // GLU_GROUP, LANES, 2).transpose(0, 1, 3, 2).reshape(n_exp, 1, two_f)
    pe = {
        "w_router": p["w_router"], "b_router": p["b_router"], "layer": l,
        "w1_all": p["w_e1_all"], "b1": b1, "w2_all": p["w_e2_all"], "b2": p["b_e2"][:, None, :],
    }
    tm = 512
    assert rows_c % tm == 0 and n_lat % tm == 0
    split = rows_c // tm
    mod_blocks = jnp.concatenate([jnp.tile(mod_c[5][:1], (split, 1)), jnp.repeat(mod[5], n_lat // tm, axis=0)], axis=0)
    outs = moe_ffn(h_all, resid, mod_blocks[:, None, :], tm, split, pe)
    if ctx_out:
        xc = outs[0].reshape(b, n_ctx, d)
    x = outs[-1].reshape(b, n_lat, d)
    return x, xc


_PARAM_NAMES = ("w_mod", "b_mod", "norm1_g", "norm2_g", "w_in", "hy_conv_w", "hy_conv_b", "hy_w1", "hy_b1",
                "hy_freq1", "hy_w2", "hy_b2", "hy_freq2", "hy_w3", "hy_b3", "hy_bias", "q_norm_g", "k_norm_g",
                "lam_q", "lam_k", "subln_g", "w_f", "w_h", "w_a", "w_o", "w_router", "b_router",
                "w_e1", "b_e1", "w_e2", "b_e2")


def kernel(x, c, ctx, c_ctx, w_mod, b_mod, norm1_g, norm2_g, w_in, hy_conv_w, hy_conv_b, hy_w1, hy_b1, hy_freq1,
           hy_w2, hy_b2, hy_freq2, hy_w3, hy_b3, hy_bias, q_norm_g, k_norm_g, lam_q, lam_k, subln_g, w_f, w_h,
           w_a, w_o, w_router, b_router, w_e1, b_e1, w_e2, b_e2):
    stacked = (w_mod, b_mod, norm1_g, norm2_g, w_in, hy_conv_w, hy_conv_b, hy_w1, hy_b1, hy_freq1, hy_w2, hy_b2,
               hy_freq2, hy_w3, hy_b3, hy_bias, q_norm_g, k_norm_g, lam_q, lam_k, subln_g, w_f, w_h, w_a, w_o,
               w_router, b_router, w_e1, b_e1, w_e2, b_e2)
    depth = w_mod.shape[0]
    xc = ctx
    for l in range(depth):
        p = {name: arr[l] for name, arr in zip(_PARAM_NAMES, stacked) if name not in ("w_e1", "w_e2")}
        p["w_e1_all"], p["w_e2_all"] = w_e1, w_e2
        x, xc = _layer(l, x, xc, c, c_ctx, p, l < depth - 1)
    return x
```

```python
import functools
import math

import numpy as np
import jax
import jax.numpy as jnp
from jax import lax
from jax.experimental import pallas as pl
from jax.experimental.pallas import tpu as pltpu

F32 = jnp.float32
BF16 = jnp.bfloat16

LANES = 128
VMEM_LIMIT = 56 * 1024 * 1024

GRID_W = 64
EPS = 1e-6
SUBLN_EPS = 1e-5
FN_GROUPS = 4
FN_GROUP_DIM = 64
FN_WIDTH = FN_GROUPS * FN_GROUP_DIM
HY_WIDTH = 256
HY_ORDER = 2
HY_SHORT = 3
HY_EMB_BANDS = 16
HY_DECAY_TARGET = 1e-2
HY_FAST_DECAY = 0.3
HY_SLOW_DECAY = 1.5
DA_HEADS = 4
DA_QK_DIM = 64
DA_V_DIM = 2 * DA_QK_DIM
DA_WIDTH = DA_HEADS * DA_V_DIM
ROPE_BASE = 10000.0
N_BRANCHES = 3
COL_QK = DA_HEADS * 2 * DA_QK_DIM
N_EXPERTS = 32
TOP_K = 4
SWIGLU_ALPHA = 1.702
SWIGLU_LIMIT = 7.0
MOE_BLOCK = 512
PROJ_GATE_CHUNK = 1024
DFT_MIN_N1 = 16
DFT_UNROLL = 8


def _dft_n2(seq):
    return min(LANES, seq // DFT_MIN_N1)


def _cparams(sem):
    return pltpu.CompilerParams(dimension_semantics=sem, vmem_limit_bytes=VMEM_LIMIT)


def _tile(n, pref):
    if n <= pref:
        return n
    for t in range(pref, 7, -1):
        if n % t == 0 and t % 8 == 0:
            return t
    return n


def _const_spec(shape):
    nd = len(shape)
    return pl.BlockSpec(shape, lambda *_: (0,) * nd, pipeline_mode=pl.Buffered(1))


def _qk_epilogue(acc, gm, gain, cos, sin):
    ms = jnp.dot((acc * acc).astype(BF16), gm, preferred_element_type=F32)
    y = acc * lax.rsqrt(ms + EPS) * gain
    n = y.shape[1]
    reps = n // LANES
    lane = lax.broadcasted_iota(jnp.int32, y.shape, 1)
    is_a = (lane % (DA_QK_DIM // 2)) < (DA_QK_DIM // 4)
    half = DA_QK_DIM // 4
    swapped = jnp.where(is_a, pltpu.roll(y, n - half, axis=1), pltpu.roll(y, half, axis=1))
    return y * jnp.tile(cos, (1, reps)) + swapped * jnp.tile(sin, (1, reps))


def _proj_kernel(x_ref, ng_ref, sh_ref, sc_ref, w_ref, gm_ref, qg_ref, kg_ref, cos_ref, sin_ref,
                 vf_ref, z_ref, q_ref, k_ref, v_ref, g_ref, *, cols):
    x = x_ref[0]
    y = x * lax.rsqrt(jnp.mean(x * x, axis=-1, keepdims=True) + EPS) * ng_ref[...]
    h = (y * (1.0 + sc_ref[0]) + sh_ref[0]).astype(BF16)

    def mm(name):
        a, b = cols[name]
        return jnp.dot(h, w_ref[:, a:b], preferred_element_type=F32)

    vf_ref[0] = mm("f")
    z_ref[0] = mm("hy")
    q_ref[0] = _qk_epilogue(mm("q"), gm_ref[...], qg_ref[...], cos_ref[...], sin_ref[...]).astype(q_ref.dtype)
    k_ref[0] = _qk_epilogue(mm("k"), gm_ref[...], kg_ref[...], cos_ref[...], sin_ref[...]).astype(k_ref.dtype)
    vv = mm("v").astype(v_ref.dtype)
    ones = jnp.ones((vv.shape[0], DA_V_DIM), v_ref.dtype)
    for hd in range(DA_HEADS):
        v_ref[0, :, 2 * hd * DA_V_DIM:(2 * hd + 1) * DA_V_DIM] = vv[:, hd * DA_V_DIM:(hd + 1) * DA_V_DIM]
        v_ref[0, :, (2 * hd + 1) * DA_V_DIM:(2 * hd + 2) * DA_V_DIM] = ones
    a, b = cols["g"]
    for c0 in range(a, b, PROJ_GATE_CHUNK):
        acc = jnp.dot(h, w_ref[:, c0:c0 + PROJ_GATE_CHUNK], preferred_element_type=F32)
        g_ref[0, :, c0 - a:c0 - a + PROJ_GATE_CHUNK] = jax.nn.sigmoid(acc).astype(g_ref.dtype)


def project_all(x, norm_g, shift, scale, w_cat, cols, q_gain, k_gain, cos, sin):
    b, s, d = x.shape
    tm = _tile(s, 512)
    width = {name: stop - start for name, (start, stop) in cols.items()}
    width["v"] *= 2
    gm = _group_mean_matrix(COL_QK, DA_QK_DIM)
    tile_gain = lambda g: jnp.tile(g.astype(F32), COL_QK // DA_QK_DIM).reshape(1, COL_QK)
    rows = lambda w: pl.BlockSpec((1, tm, w), lambda i, j: (i, j, 0))
    per_b = pl.BlockSpec((1, 1, d), lambda i, j: (i, 0, 0))
    full = lambda shape: pl.BlockSpec(shape, lambda i, j: (0, 0))
    outs = [("f", F32), ("hy", F32), ("q", BF16), ("k", BF16), ("v", BF16), ("g", BF16)]
    return pl.pallas_call(
        functools.partial(_proj_kernel, cols=cols),
        grid=(b, s // tm),
        in_specs=[rows(d), full((1, d)), per_b, per_b, _const_spec(w_cat.shape), full(gm.shape),
                  full((1, COL_QK)), full((1, COL_QK)),
                  pl.BlockSpec((tm, LANES), lambda i, j: (j, 0)), pl.BlockSpec((tm, LANES), lambda i, j: (j, 0))],
        out_specs=[rows(width[name]) for name, _ in outs],
        out_shape=[jax.ShapeDtypeStruct((b, s, width[name]), dt) for name, dt in outs],
        compiler_params=_cparams(("parallel", "parallel")),
        name="project_all",
    )(x, norm_g.reshape(1, d).astype(F32), shift.reshape(b, 1, d), scale.reshape(b, 1, d), w_cat, gm,
      tile_gain(q_gain), tile_gain(k_gain), cos, sin)


def _mm_kernel(a_ref, w_ref, *rest, epi):
    acc = jnp.dot(a_ref[...], w_ref[...], preferred_element_type=F32)
    if epi == "plain":
        (o_ref,) = rest
    elif epi == "bias":
        b_ref, o_ref = rest
        acc = acc + b_ref[...]
    else:
        raise ValueError(epi)
    o_ref[...] = acc.astype(o_ref.dtype)


def matmul(a, w, *, out_dtype=F32, epi="plain", extra=(), extra_specs=(), tm=512, tn=1024, name="mm"):
    m, k = a.shape
    k2, n = w.shape
    assert k == k2
    tm = _tile(m, tm)
    tn = _tile(n, tn)
    return pl.pallas_call(
        functools.partial(_mm_kernel, epi=epi),
        grid=(m // tm, n // tn),
        in_specs=[
            pl.BlockSpec((tm, k), lambda i, j: (i, 0)),
            pl.BlockSpec((k, tn), lambda i, j: (0, j)),
            *extra_specs,
        ],
        out_specs=pl.BlockSpec((tm, tn), lambda i, j: (i, j)),
        out_shape=jax.ShapeDtypeStruct((m, n), out_dtype),
        compiler_params=_cparams(("parallel", "parallel")),
        name=name,
    )(a, w, *extra)


def _group_mean_matrix(n, group):
    idx = np.arange(n)
    return jnp.asarray((idx[:, None] // group == idx[None, :] // group).astype(np.float32) / group, BF16)


def rope_tables(n_lat):
    rows = n_lat // GRID_W
    row = np.repeat(np.arange(rows), GRID_W).astype(np.float64)
    col = np.tile(np.arange(GRID_W), rows).astype(np.float64)
    n_freq = DA_QK_DIM // 4
    inv = ROPE_BASE ** (-np.arange(n_freq, dtype=np.float64) / n_freq)
    ang_r = row[:, None] * inv
    ang_c = col[:, None] * inv
    cos = np.concatenate([np.cos(ang_r), np.cos(ang_r), np.cos(ang_c), np.cos(ang_c)], axis=1)
    sin = np.concatenate([-np.sin(ang_r), np.sin(ang_r), -np.sin(ang_c), np.sin(ang_c)], axis=1)
    cos = np.tile(cos, (1, LANES // DA_QK_DIM))
    sin = np.tile(sin, (1, LANES // DA_QK_DIM))
    return jnp.asarray(cos, F32), jnp.asarray(sin, F32)


ATTN_TQ = 512
ATTN_TK = 768


def _attn_kernel(lam_ref, q_ref, k_ref, v_ref, g_ref, o_ref, qs_ref, s_ref, m_ref, acc_ref,
                 *, tq, tk, nkc, nqb, out_scale):
    def stack(qi, qslot):
        q = q_ref[pl.ds(pl.multiple_of(qi * tq, tq), tq), :]
        lane = lax.broadcasted_iota(jnp.int32, q.shape, 1)
        zero = jnp.zeros_like(q)
        qs_ref[qslot, 0:tq, :] = jnp.where(lane < DA_QK_DIM, q, zero)
        qs_ref[qslot, tq:2 * tq, :] = jnp.where(lane >= DA_QK_DIM, q, zero)

    def scores(qslot, j, slot):
        kj = k_ref[j * tk:(j + 1) * tk, :]
        s_ref[slot] = lax.dot_general(qs_ref[qslot], kj, (((1,), (1,)), ((), ())), preferred_element_type=F32)

    def update(j, slot):
        s = s_ref[slot]
        vj = v_ref[j * tk:(j + 1) * tk, :]
        m_prev = m_ref[...]
        m_next = jnp.maximum(m_prev, jnp.max(s, axis=1, keepdims=True))
        p = jnp.exp2(s - jnp.tile(m_next, (1, tk // LANES)))
        alpha = jnp.exp2(m_prev - m_next)
        m_ref[...] = m_next
        acc_ref[...] = acc_ref[...] * jnp.tile(alpha, (1, 2)) + jnp.dot(p.astype(BF16), vj,
                                                                          preferred_element_type=F32)

    def reset():
        m_ref[...] = jnp.full(m_ref.shape, -jnp.inf, F32)
        acc_ref[...] = jnp.zeros(acc_ref.shape, F32)

    stack(0, 0)
    scores(0, 0, 0)
    reset()

    def q_block(qi, local):
        qslot = local % 2
        t0 = local * nkc
        for j in range(nkc):
            nslot = (t0 + j + 1) % 2
            if j + 1 < nkc:
                scores(qslot, j + 1, nslot)
            else:
                nxt = jnp.minimum(qi + 1, nqb - 1)
                stack(nxt, 1 - qslot)
                scores(1 - qslot, 0, nslot)
            update(j, (t0 + j) % 2)
        o1 = acc_ref[0:tq, 0:DA_V_DIM] / acc_ref[0:tq, DA_V_DIM:]
        o2 = acc_ref[tq:2 * tq, 0:DA_V_DIM] / acc_ref[tq:2 * tq, DA_V_DIM:]
        o = o1 - lam_ref[0, 0] * o2
        o = o * lax.rsqrt(jnp.mean(o * o, axis=-1, keepdims=True) + SUBLN_EPS)
        o_ref[pl.ds(pl.multiple_of(qi * tq, tq), tq), :] = (o * g_ref[...] * out_scale).astype(o_ref.dtype)
        reset()

    group = 2
    if nqb % group == 0 and nqb > group:
        def q_group(i, c):
            for local in range(group):
                q_block(i * group + local, local)
            return c

        lax.fori_loop(0, nqb // group, q_group, 0)
    else:
        for qi in range(nqb):
            q_block(qi, qi)


def diff_attention(q, k, v_ext, lam, subln_g, out_scale, nq, nk):
    b = q.shape[0] // nq
    tq = _tile(nq, ATTN_TQ)
    tk = next(t for t in (ATTN_TK, 256, 128) if nk % t == 0)
    nqb, nkc = nq // tq, nk // tk
    kern = functools.partial(_attn_kernel, tq=tq, tk=tk, nkc=nkc, nqb=nqb, out_scale=out_scale)
    return pl.pallas_call(
        kern,
        grid=(b, DA_HEADS),
        in_specs=[
            pl.BlockSpec(memory_space=pltpu.SMEM),
            pl.BlockSpec((nq, DA_V_DIM), lambda bi, h: (bi, h)),
            pl.BlockSpec((nk, DA_V_DIM), lambda bi, h: (bi, h)),
            pl.BlockSpec((nk, 2 * DA_V_DIM), lambda bi, h: (bi, h)),
            pl.BlockSpec((1, DA_V_DIM), lambda bi, h: (0, 0)),
        ],
        out_specs=pl.BlockSpec((nq, DA_V_DIM), lambda bi, h: (bi, h)),
        out_shape=jax.ShapeDtypeStruct((b * nq, DA_WIDTH), BF16),
        scratch_shapes=[
            pltpu.VMEM((2, 2 * tq, DA_V_DIM), BF16),
            pltpu.VMEM((2, 2 * tq, tk), F32),
            pltpu.VMEM((2 * tq, LANES), F32),
            pltpu.VMEM((2 * tq, 2 * DA_V_DIM), F32),
        ],
        compiler_params=_cparams(("parallel", "parallel")),
        name="diff_attn",
    )(lam.reshape(1, 1).astype(F32), q, k, v_ext, subln_g.reshape(1, DA_V_DIM).astype(F32))


def _dft_tables_real(seq):
    n2 = _dft_n2(seq)
    n1h = seq // n2
    n1 = 2 * n1h
    n = n1 * n2
    k1 = np.arange(n1, dtype=np.float64)[None, :, None]
    nn = (n2 * np.arange(n1h, dtype=np.float64)[None, None, :] + np.arange(n2, dtype=np.float64)[:, None, None])
    ang = 2.0 * np.pi * k1 * nn / n
    e_fwd = np.concatenate([np.cos(ang), -np.sin(ang)], axis=1)
    e_inv = np.transpose(e_fwd, (0, 2, 1))
    a2 = 2.0 * np.pi * np.outer(np.arange(n2), np.arange(n2)) / n2
    c, s = np.cos(a2), np.sin(a2)
    f_fwd = np.block([[c, s], [-s, c]])
    f_inv = np.block([[c, -s], [s, c]])
    f_spec = np.block([[c, s, c, s], [-s, c, s, -c]])
    return tuple(jnp.asarray(t, BF16) for t in (e_fwd, f_fwd, f_inv, e_inv, f_spec))


def _pack_complex(re, im):
    r = lax.bitcast_convert_type(re.astype(BF16).astype(F32), jnp.uint32)
    i = lax.bitcast_convert_type(im.astype(BF16).astype(F32), jnp.uint32)
    return r | (i >> 16)


def _unpack_complex(w):
    re = lax.bitcast_convert_type(w & jnp.uint32(0xFFFF0000), F32)
    im = lax.bitcast_convert_type(w << 16, F32)
    return jnp.concatenate([re, im], axis=0).astype(BF16)


def _spectrum_kernel(hf_ref, hb_ref, sc_ref, ef_ref, fs_ref, o_ref, scr_f, scr_b, *, n1, n1h, n2, kc):
    kk = pl.program_id(2)

    @pl.when(kk == 0)
    def _():
        def stage1(j, c):
            xf = hf_ref[pl.ds(j, n1h, stride=n2), :]
            xb = hb_ref[pl.ds(j, n1h, stride=n2), :]
            row = lax.broadcasted_iota(jnp.int32, xb.shape, 0)
            xb = jnp.where((row == 0) & (j == 0), 0.0, xb)
            af = jnp.dot(ef_ref[j], xf.astype(BF16), preferred_element_type=F32)
            ab = jnp.dot(ef_ref[j], xb.astype(BF16), preferred_element_type=F32)
            r0 = pl.multiple_of(j * n1, n1)
            scr_f[pl.ds(r0, n1), :] = _pack_complex(af[:n1], af[n1:])
            scr_b[pl.ds(r0, n1), :] = _pack_complex(ab[:n1], ab[n1:])
            return c

        lax.fori_loop(0, n2, stage1, 0, unroll=DFT_UNROLL)

    def stage2(t, c):
        k1 = kk * kc + t
        a = jnp.concatenate([_unpack_complex(scr_f[pl.ds(k1, n2, stride=n1), :]),
                             _unpack_complex(scr_b[pl.ds(k1, n2, stride=n1), :])], axis=0)
        o_ref[t] = (jnp.dot(fs_ref[...], a, preferred_element_type=F32) * sc_ref[...]).astype(o_ref.dtype)
        return c

    lax.fori_loop(0, kc, stage2, 0, unroll=DFT_UNROLL)


def filter_spectra(filt, scale, tables):
    seq = filt.shape[0]
    n_order, _, ch = scale.shape
    n2 = _dft_n2(seq)
    n1h = seq // n2
    n1 = 2 * n1h
    e_fwd, f_spec = tables[0], tables[4]
    kc = min(n1, 16)
    cb = ch // LANES
    kern = functools.partial(_spectrum_kernel, n1=n1, n1h=n1h, n2=n2, kc=kc)
    return pl.pallas_call(
        kern,
        grid=(n_order, cb, n1 // kc),
        in_specs=[
            pl.BlockSpec((seq, LANES), lambda o, c, k: (0, (2 * o) * cb + c)),
            pl.BlockSpec((seq, LANES), lambda o, c, k: (0, (2 * o + 1) * cb + c)),
            pl.BlockSpec((None, 1, LANES), lambda o, c, k: (o, 0, c)),
            _const_spec(e_fwd.shape),
            _const_spec(f_spec.shape),
        ],
        out_specs=pl.BlockSpec((None, kc, 2 * n2, LANES), lambda o, c, k: (o, k, 0, c)),
        out_shape=jax.ShapeDtypeStruct((n_order, n1, 2 * n2, ch), BF16),
        scratch_shapes=[pltpu.VMEM((n1 * n2, LANES), jnp.uint32), pltpu.VMEM((n1 * n2, LANES), jnp.uint32)],
        compiler_params=_cparams(("parallel", "parallel", "arbitrary")),
        name="filter_spectra",
    )(filt, filt, scale, e_fwd, f_spec)


def _longconv_kernel(u_ref, g_ref, h_ref, bias_ref, ef_ref, ff_ref, fi_ref, ei_ref, o_ref, scr_a, scr_b,
                     *, n1, n1h, n2):
    def stage1(j, c):
        x = u_ref[pl.ds(j, n1h, stride=n2), :].astype(BF16)
        a = jnp.dot(ef_ref[j], x, preferred_element_type=F32)
        scr_a[pl.ds(pl.multiple_of(j * n1, n1), n1), :] = _pack_complex(a[:n1], a[n1:])
        return c

    lax.fori_loop(0, n2, stage1, 0, unroll=DFT_UNROLL)

    def stage2(k1, c):
        a = _unpack_complex(scr_a[pl.ds(k1, n2, stride=n1), :])
        y = jnp.dot(ff_ref[...], a, preferred_element_type=F32)
        hk = h_ref[k1].astype(F32)
        yr, yi = y[:n2], y[n2:]
        hr, hi = hk[:n2], hk[n2:]
        z = jnp.concatenate([yr * hr - yi * hi, yr * hi + yi * hr], axis=0).astype(BF16)
        bk = jnp.dot(fi_ref[...], z, preferred_element_type=F32)
        scr_b[pl.ds(pl.multiple_of(k1 * n2, n2), n2), :] = _pack_complex(bk[:n2], bk[n2:])
        return c

    lax.fori_loop(0, n1, stage2, 0, unroll=DFT_UNROLL)

    def stage3(j, c):
        bmat = _unpack_complex(scr_b[pl.ds(j, n1, stride=n2), :])
        y = jnp.dot(ei_ref[j], bmat, preferred_element_type=F32)
        u = u_ref[pl.ds(j, n1h, stride=n2), :]
        g = g_ref[pl.ds(j, n1h, stride=n2), :]
        o_ref[pl.ds(j, n1h, stride=n2), :] = g * (y + u * bias_ref[...])
        return c

    lax.fori_loop(0, n2, stage3, 0, unroll=DFT_UNROLL)


def long_conv_gated(u, u_blk, g, g_blk, spec, bias, tables):
    b, seq, _ = u.shape
    ch = bias.shape[0]
    n2 = _dft_n2(seq)
    n1h = seq // n2
    n1 = 2 * n1h
    e_fwd, f_fwd, f_inv, e_inv = tables[:4]
    kern = functools.partial(_longconv_kernel, n1=n1, n1h=n1h, n2=n2)
    one = pl.Buffered(1)
    return pl.pallas_call(
        kern,
        grid=(ch // LANES, b),
        in_specs=[
            pl.BlockSpec((None, seq, LANES), lambda c, i: (i, 0, u_blk + c), pipeline_mode=one),
            pl.BlockSpec((None, seq, LANES), lambda c, i: (i, 0, g_blk + c), pipeline_mode=one),
            pl.BlockSpec((n1, 2 * n2, LANES), lambda c, i: (0, 0, c), pipeline_mode=one),
            pl.BlockSpec((1, LANES), lambda c, i: (0, c)),
            _const_spec(e_fwd.shape),
            _const_spec(f_fwd.shape),
            _const_spec(f_inv.shape),
            _const_spec(e_inv.shape),
        ],
        out_specs=pl.BlockSpec((None, seq, LANES), lambda c, i: (i, 0, c)),
        out_shape=jax.ShapeDtypeStruct((b, seq, ch), F32),
        scratch_shapes=[pltpu.VMEM((n1 * n2, LANES), jnp.uint32), pltpu.VMEM((n1 * n2, LANES), jnp.uint32)],
        compiler_params=_cparams(("parallel", "parallel")),
        name="long_conv",
    )(u, g, spec, bias.reshape(1, ch).astype(F32), e_fwd, f_fwd, f_inv, e_inv)


def _dft_tables_complex(seq):
    n2 = _dft_n2(seq)
    n1 = seq // n2
    k1 = np.arange(n1, dtype=np.float64)[None, :, None]
    nn = (n2 * np.arange(n1, dtype=np.float64)[None, None, :] + np.arange(n2, dtype=np.float64)[:, None, None])
    ang = 2.0 * np.pi * k1 * nn / seq
    c, s = np.cos(ang), np.sin(ang)
    e_fwd = np.concatenate([np.concatenate([c, s], axis=2), np.concatenate([-s, c], axis=2)], axis=1)
    a2 = 2.0 * np.pi * np.outer(np.arange(n2), np.arange(n2)) / n2
    f_re = np.concatenate([np.cos(a2), np.sin(a2)], axis=1)
    return jnp.asarray(e_fwd, BF16), jnp.asarray(f_re, BF16)


def _seqdft_kernel(vr_ref, vi_ref, ef_ref, fr_ref, o_ref, scr, *, n1, n2, scale):
    def stage1(j, c):
        x = jnp.concatenate([vr_ref[pl.ds(j, n1, stride=n2), :], vi_ref[pl.ds(j, n1, stride=n2), :]], axis=0)
        a = jnp.dot(ef_ref[j], x.astype(BF16), preferred_element_type=F32)
        scr[pl.ds(pl.multiple_of(j * n1, n1), n1), :] = _pack_complex(a[:n1], a[n1:])
        return c

    lax.fori_loop(0, n2, stage1, 0, unroll=DFT_UNROLL)

    def stage2(k1, c):
        a = _unpack_complex(scr[pl.ds(k1, n2, stride=n1), :])
        o_ref[pl.ds(k1, n2, stride=n1), :] = jnp.dot(fr_ref[...], a, preferred_element_type=F32) * scale
        return c

    lax.fori_loop(0, n1, stage2, 0, unroll=DFT_UNROLL)


def seq_dft_real(v, ch, scale, tables):
    b, seq, _ = v.shape
    n2 = _dft_n2(seq)
    n1 = seq // n2
    e_fwd, f_re = tables
    nblk = ch // LANES
    kern = functools.partial(_seqdft_kernel, n1=n1, n2=n2, scale=scale)
    return pl.pallas_call(
        kern,
        grid=(nblk, b),
        in_specs=[
            pl.BlockSpec((None, seq, LANES), lambda c, i: (i, 0, c)),
            pl.BlockSpec((None, seq, LANES), lambda c, i: (i, 0, nblk + c)),
            _const_spec(e_fwd.shape),
            _const_spec(f_re.shape),
        ],
        out_specs=pl.BlockSpec((None, seq, LANES), lambda c, i: (i, 0, c)),
        out_shape=jax.ShapeDtypeStruct((b, seq, ch), F32),
        scratch_shapes=[pltpu.VMEM((n1 * n2, LANES), jnp.uint32)],
        compiler_params=_cparams(("parallel", "parallel")),
        name="seq_dft",
    )(v, v, e_fwd, f_re)


def fourier_channel_matrix():
    a = 2.0 * np.pi * np.outer(np.arange(FN_GROUP_DIM), np.arange(FN_GROUP_DIM)) / FN_GROUP_DIM
    eye = np.eye(FN_GROUPS)
    return jnp.asarray(np.concatenate([np.kron(eye, np.cos(a)), -np.kron(eye, np.sin(a))], axis=1), BF16)


def _shortconv_kernel(u_ref, w_ref, b_ref, o_ref):
    u = u_ref[...]
    n = u.shape[0]
    row = lax.broadcasted_iota(jnp.int32, u.shape, 0)
    prev = jnp.where(row == 0, 0.0, pltpu.roll(u, 1, axis=0))
    nxt = jnp.where(row == n - 1, 0.0, pltpu.roll(u, n - 1, axis=0))
    o_ref[...] = prev * w_ref[0:1, :] + u * w_ref[1:2, :] + nxt * w_ref[2:3, :] + b_ref[...]


def short_conv(u, w, bias):
    b, seq, ch = u.shape
    return pl.pallas_call(
        _shortconv_kernel,
        grid=(b, ch // LANES),
        in_specs=[
            pl.BlockSpec((None, seq, LANES), lambda i, c: (i, 0, c)),
            pl.BlockSpec((HY_SHORT, LANES), lambda i, c: (0, c)),
            pl.BlockSpec((1, LANES), lambda i, c: (0, c)),
        ],
        out_specs=pl.BlockSpec((None, seq, LANES), lambda i, c: (i, 0, c)),
        out_shape=jax.ShapeDtypeStruct((b, seq, ch), F32),
        compiler_params=_cparams(("parallel", "parallel")),
        name="short_conv",
    )(u, w.astype(F32), bias.reshape(1, ch).astype(F32))


def _filter_kernel(emb_ref, w1_ref, b1_ref, f1_ref, w2_ref, b2_ref, f2_ref, w3_ref, b3_ref, dec_ref, o_ref, mag_ref):
    z = jnp.dot(emb_ref[...].astype(BF16), w1_ref[...], preferred_element_type=F32) + b1_ref[...]
    z = jnp.sin(f1_ref[...] * z)
    z = jnp.dot(z.astype(BF16), w2_ref[...], preferred_element_type=F32) + b2_ref[...]
    z = jnp.sin(f2_ref[...] * z)
    h = jnp.dot(z.astype(BF16), w3_ref[...], preferred_element_type=F32) + b3_ref[...]
    filt = h * dec_ref[...]
    o_ref[...] = filt
    mag_ref[0] = jnp.sum(jnp.abs(filt), axis=0, keepdims=True)


def hyena_filters(seq, hy_w1, hy_b1, hy_freq1, hy_w2, hy_b2, hy_freq2, hy_w3, hy_b3):
    t = jnp.linspace(0.0, 1.0, seq, dtype=F32)[:, None]
    ang = (2.0 * math.pi / seq) * jnp.arange(seq, dtype=F32)[:, None]
    bands = jnp.linspace(1e-4, HY_EMB_BANDS - 1, HY_EMB_BANDS, dtype=F32)[None, :]
    emb = jnp.concatenate([t, jnp.cos(bands * ang), -jnp.sin(bands * ang)], axis=-1)
    kdim = emb.shape[1]
    kpad = LANES - kdim
    emb = jnp.pad(emb, ((0, 0), (0, kpad)))
    w1 = jnp.pad(hy_w1, ((0, kpad), (0, 0))).astype(BF16)
    deltas = jnp.abs(jnp.linspace(math.log(HY_DECAY_TARGET) / HY_SLOW_DECAY,
                                  math.log(HY_DECAY_TARGET) / HY_FAST_DECAY, HY_WIDTH, dtype=F32))
    decay = jnp.tile(jnp.exp(-t * deltas), (1, 2 * HY_ORDER))
    fo = hy_w1.shape[1]
    nout = hy_w3.shape[1]
    tl = _tile(seq, 1024)
    row = lambda a: a.reshape(1, -1).astype(F32)
    full = lambda shape: pl.BlockSpec(shape, lambda i: (0, 0))
    return pl.pallas_call(
        _filter_kernel,
        grid=(seq // tl,),
        in_specs=[
            pl.BlockSpec((tl, LANES), lambda i: (i, 0)),
            full((LANES, fo)), full((1, fo)), full((1, fo)),
            full((fo, fo)), full((1, fo)), full((1, fo)),
            full((fo, nout)), full((1, nout)),
            pl.BlockSpec((tl, nout), lambda i: (i, 0)),
        ],
        out_specs=[pl.BlockSpec((tl, nout), lambda i: (i, 0)), pl.BlockSpec((1, 1, nout), lambda i: (i, 0, 0))],
        out_shape=[jax.ShapeDtypeStruct((seq, nout), F32), jax.ShapeDtypeStruct((seq // tl, 1, nout), F32)],
        compiler_params=_cparams(("parallel",)),
        name="hyena_filter",
    )(emb, w1, row(hy_b1), row(hy_freq1), hy_w2.astype(BF16), row(hy_b2), row(hy_freq2),
      hy_w3.astype(BF16), row(hy_b3), decay)


def hyena_spectra(seq, filt, mag_blocks, tables):
    mag = jnp.sum(mag_blocks, axis=(0, 1)).reshape(HY_ORDER, 2, HY_WIDTH)
    lag0_bwd = jnp.abs(filt[0]).reshape(HY_ORDER, 2, HY_WIDTH)[:, 1]
    norm = mag[:, 0] + mag[:, 1] - lag0_bwd
    scale = (1.0 / (2 * seq)) / norm
    return filter_spectra(filt, scale[:, None, :], tables)


def _merge_kernel(x_ref, yf_ref, yh_ref, ya_ref, g_ref, wf_ref, wh_ref, wa_ref, wo_ref,
                  gate_ref, ng_ref, sh_ref, sc_ref, *rest):
    xo_ref, ho_ref = rest[-2:]
    d = x_ref.shape[-1]
    g = g_ref[0].astype(F32)
    yf = jnp.dot(yf_ref[0].astype(BF16), wf_ref[...], preferred_element_type=F32)
    yh = jnp.dot(yh_ref[0].astype(BF16), wh_ref[...], preferred_element_type=F32)
    ya = jnp.dot(ya_ref[0], wa_ref[...], preferred_element_type=F32)
    mix = g[:, 0:d] * yf + g[:, d:2 * d] * yh + g[:, 2 * d:3 * d] * ya
    x = x_ref[0] + gate_ref[0] * jnp.dot(mix.astype(BF16), wo_ref[...], preferred_element_type=F32)
    xo_ref[...] = x
    y = x * lax.rsqrt(jnp.mean(x * x, axis=-1, keepdims=True) + EPS) * ng_ref[...]
    ho_ref[...] = (y * (1.0 + sc_ref[0]) + sh_ref[0]).astype(ho_ref.dtype)


def merge_branches(x, yf, yh, ya, g, w_f, w_h, w_a, w_o, gate, norm_g, shift, scale, total_rows, row_offset, prior):
    b, l, d = x.shape
    tl = _tile(l, 512)
    assert row_offset % tl == 0
    rows = lambda w: pl.BlockSpec((1, tl, w), lambda i, j: (i, j, 0))
    full = lambda a: pl.BlockSpec(a.shape, lambda i, j: (0, 0))
    per_b = pl.BlockSpec((1, 1, d), lambda i, j: (i, 0, 0))
    out_rows = pl.BlockSpec((tl, d), lambda i, j: (row_offset // tl + i * (l // tl) + j, 0))
    wf, wh, wa, wo = (w.astype(BF16) for w in (w_f, w_h, w_a, w_o))
    in_specs = [rows(d), rows(yf.shape[-1]), rows(yh.shape[-1]), rows(ya.shape[-1]), rows(3 * d),
                full(wf), full(wh), full(wa), full(wo),
                per_b, pl.BlockSpec((1, d), lambda i, j: (0, 0)), per_b, per_b]
    args = [x, yf, yh, ya, g, wf, wh, wa, wo, gate.reshape(b, 1, d), norm_g.reshape(1, d).astype(F32),
            shift.reshape(b, 1, d), scale.reshape(b, 1, d)]
    aliases = {}
    if prior is not None:
        aliases = {len(args): 0, len(args) + 1: 1}
        in_specs += [pl.BlockSpec(memory_space=pl.ANY), pl.BlockSpec(memory_space=pl.ANY)]
        args += list(prior)
    return pl.pallas_call(
        _merge_kernel,
        grid=(b, l // tl),
        in_specs=in_specs,
        out_specs=[out_rows, out_rows],
        out_shape=[jax.ShapeDtypeStruct((total_rows, d), F32), jax.ShapeDtypeStruct((total_rows, d), BF16)],
        input_output_aliases=aliases,
        compiler_params=_cparams(("parallel", "parallel")),
        name="merge",
    )(*args)


GLU_GROUP = 2 * LANES


def _glu_group_permutation():
    p = np.zeros((GLU_GROUP, GLU_GROUP), np.float32)
    j = np.arange(LANES)
    p[2 * j, j] = 1.0
    p[2 * j + 1, LANES + j] = 1.0
    return jnp.asarray(p, BF16)


def _moe_kernel(be_ref, act_ref, new_ref, rows_ref, w1_ref, b1_ref, w2_ref, b2_ref, p_ref, o_ref, w1s, w2s):
    i = pl.program_id(0)

    @pl.when(new_ref[i] > 0)
    def _():
        for q in range(w1s.shape[1] // GLU_GROUP):
            cols = slice(q * GLU_GROUP, (q + 1) * GLU_GROUP)
            w1s[:, cols] = jnp.dot(w1_ref[:, cols].astype(BF16), p_ref[...],
                                   preferred_element_type=F32).astype(BF16)
        w2s[...] = w2_ref[...].astype(BF16)

    @pl.when(act_ref[i] > 0)
    def _():
        u = jnp.dot(rows_ref[...], w1s[...], preferred_element_type=F32) + b1_ref[0]
        parts = []
        for q in range(u.shape[1] // GLU_GROUP):
            xg = jnp.minimum(u[:, q * GLU_GROUP:q * GLU_GROUP + LANES], SWIGLU_LIMIT)
            xl = jnp.clip(u[:, q * GLU_GROUP + LANES:(q + 1) * GLU_GROUP], -SWIGLU_LIMIT, SWIGLU_LIMIT)
            parts.append((xg * jax.nn.sigmoid(SWIGLU_ALPHA * xg) * (xl + 1.0)).astype(BF16))
        a = jnp.concatenate(parts, axis=1)
        y = jnp.dot(a, w2s[...], preferred_element_type=F32) + b2_ref[0]
        o_ref[...] = y.astype(o_ref.dtype)

    @pl.when(act_ref[i] == 0)
    def _():
        o_ref[...] = jnp.zeros(o_ref.shape, o_ref.dtype)


def moe_experts(rows, blk_exp, blk_act, blk_new, layer, w1_all, b1, w2_all, b2):
    r, d = rows.shape
    de = w2_all.shape[2]
    nblk = r // MOE_BLOCK
    grid_spec = pltpu.PrefetchScalarGridSpec(
        num_scalar_prefetch=3,
        grid=(nblk,),
        in_specs=[
            pl.BlockSpec((MOE_BLOCK, d), lambda i, be, act, new: (i, 0)),
            pl.BlockSpec((None, None, d, 2 * de), lambda i, be, act, new: (layer, be[i], 0, 0)),
            pl.BlockSpec((1, 1, 2 * de), lambda i, be, act, new: (be[i], 0, 0)),
            pl.BlockSpec((None, None, de, d), lambda i, be, act, new: (layer, be[i], 0, 0)),
            pl.BlockSpec((1, 1, d), lambda i, be, act, new: (be[i], 0, 0)),
            pl.BlockSpec((GLU_GROUP, GLU_GROUP), lambda i, be, act, new: (0, 0)),
        ],
        out_specs=pl.BlockSpec((MOE_BLOCK, d), lambda i, be, act, new: (i, 0)),
        scratch_shapes=[pltpu.VMEM((d, 2 * de), BF16), pltpu.VMEM((de, d), BF16)],
    )
    return pl.pallas_call(
        _moe_kernel,
        grid_spec=grid_spec,
        out_shape=jax.ShapeDtypeStruct((r, d), BF16),
        compiler_params=_cparams(("arbitrary",)),
        name="moe_experts",
    )(blk_exp, blk_act, blk_new, rows, w1_all, b1, w2_all, b2, _glu_group_permutation())


def _combine_kernel(y_ref, g_ref, x_ref, m_ref, *o_refs, split):
    g = g_ref[...]
    acc = g[:, 0:1] * y_ref[0].astype(F32)
    for j in range(1, TOP_K):
        acc = acc + g[:, j:j + 1] * y_ref[j].astype(F32)
    val = x_ref[...] + m_ref[0] * acc
    if split == 0:
        o_refs[0][...] = val
    else:
        i = pl.program_id(0)

        @pl.when(i < split)
        def _():
            o_refs[0][...] = val

        @pl.when(i >= split)
        def _():
            o_refs[1][...] = val


def moe_combine(y_sel, gate, resid, mod_blocks, tm, split):
    k, t, d = y_sel.shape
    nb = t // tm
    if split == 0:
        out_specs = [pl.BlockSpec((tm, d), lambda i: (i, 0))]
        out_shape = [jax.ShapeDtypeStruct((t, d), F32)]
    else:
        out_specs = [pl.BlockSpec((tm, d), lambda i: (jnp.minimum(i, split - 1), 0)),
                     pl.BlockSpec((tm, d), lambda i: (jnp.maximum(i - split, 0), 0))]
        out_shape = [jax.ShapeDtypeStruct((split * tm, d), F32), jax.ShapeDtypeStruct(((nb - split) * tm, d), F32)]
    return pl.pallas_call(
        functools.partial(_combine_kernel, split=split),
        grid=(nb,),
        in_specs=[pl.BlockSpec((k, tm, d), lambda i: (0, i, 0)),
                  pl.BlockSpec((tm, k), lambda i: (i, 0)),
                  pl.BlockSpec((tm, d), lambda i: (i, 0)),
                  pl.BlockSpec((1, 1, d), lambda i: (i, 0, 0))],
        out_specs=out_specs,
        out_shape=out_shape,
        compiler_params=_cparams(("arbitrary",)),
        name="moe_combine",
    )(y_sel, gate, resid, mod_blocks)


ROUTER_TM = 512


def _router_kernel(h_ref, w_ref, b_ref, tri_ref, e_ref, g_ref, r_ref, cnt_ref, carry):
    @pl.when(pl.program_id(0) == 0)
    def _():
        carry[...] = jnp.zeros(carry.shape, F32)

    logits = jnp.dot(h_ref[...], w_ref[...], preferred_element_type=F32) + b_ref[...]
    slot_lane = lax.broadcasted_iota(jnp.int32, logits.shape, 1)
    lane = slot_lane.astype(F32)
    rem = logits
    tops, idxs, hots = [], [], []
    for _ in range(TOP_K):
        m = jnp.max(rem, axis=1, keepdims=True)
        idx = jnp.min(jnp.where(rem == m, lane, float(LANES)), axis=1, keepdims=True)
        hot = lane == idx
        tops.append(m)
        idxs.append(idx)
        hots.append(hot)
        rem = jnp.where(hot, -jnp.inf, rem)
    ex = [jnp.exp(t - tops[0]) for t in tops]
    den = functools.reduce(lambda a, b: a + b, ex)
    sel = functools.reduce(lambda a, b: a + b, [hot.astype(F32) for hot in hots])
    before = jnp.dot(tri_ref[...], sel.astype(BF16), preferred_element_type=F32) + carry[...]
    e_out = jnp.zeros(logits.shape, F32)
    g_out = jnp.zeros(logits.shape, F32)
    r_out = jnp.zeros(logits.shape, F32)
    for j in range(TOP_K):
        slot = slot_lane == j
        e_out = jnp.where(slot, idxs[j], e_out)
        g_out = jnp.where(slot, ex[j] / den, g_out)
        r_out = jnp.where(slot, jnp.sum(jnp.where(hots[j], before, 0.0), axis=1, keepdims=True), r_out)
    e_ref[...] = e_out.astype(jnp.int32)
    g_ref[...] = g_out
    r_ref[...] = r_out.astype(jnp.int32)
    carry[...] = carry[...] + jnp.sum(sel, axis=0, keepdims=True)
    cnt_ref[...] = carry[...]


def route(h, w_router, b_router):
    t_tok, d = h.shape
    tm = _tile(t_tok, ROUTER_TM)
    wr = jnp.pad(w_router, ((0, 0), (0, LANES - N_EXPERTS))).astype(BF16)
    br = jnp.pad(b_router.astype(F32), (0, LANES - N_EXPERTS), constant_values=-jnp.inf).reshape(1, LANES)
    tri = jnp.asarray(np.tril(np.ones((tm, tm), np.float32), -1), BF16)
    full = lambda shape: pl.BlockSpec(shape, lambda i: (0, 0))
    rows = pl.BlockSpec((tm, LANES), lambda i: (i, 0))
    e, g, r, cnt = pl.pallas_call(
        _router_kernel,
        grid=(t_tok // tm,),
        in_specs=[pl.BlockSpec((tm, d), lambda i: (i, 0)), full((d, LANES)), full((1, LANES)), full((tm, tm))],
        out_specs=[rows, rows, rows, full((1, LANES))],
        out_shape=[jax.ShapeDtypeStruct((t_tok, LANES), jnp.int32), jax.ShapeDtypeStruct((t_tok, LANES), F32),
                   jax.ShapeDtypeStruct((t_tok, LANES), jnp.int32), jax.ShapeDtypeStruct((1, LANES), F32)],
        scratch_shapes=[pltpu.VMEM((1, LANES), F32)],
        compiler_params=_cparams(("arbitrary",)),
        name="router",
    )(h, wr, br, tri)
    return e[:, :TOP_K], g[:, :TOP_K], r[:, :TOP_K], cnt[0, :N_EXPERTS].astype(jnp.int32)


def moe_ffn(h, resid, mod_blocks, tm, split, p):
    t_tok, d = h.shape
    top_i, gate, rank, counts = route(h, p["w_router"], p["b_router"])
    n_assign = t_tok * TOP_K
    padded = (counts + MOE_BLOCK - 1) // MOE_BLOCK * MOE_BLOCK
    pad_end = jnp.cumsum(padded)
    pad_start = pad_end - padded
    experts = jnp.arange(N_EXPERTS, dtype=jnp.int32)
    dest = rank + jnp.sum(jnp.where(top_i[..., None] == experts, pad_start, 0), axis=-1)
    dest = dest.reshape(-1)
    n_blocks = -(-n_assign // MOE_BLOCK) + N_EXPERTS
    n_rows = n_blocks * MOE_BLOCK
    filled = jnp.zeros((n_rows,), jnp.int32).at[dest].add(jnp.arange(n_assign, dtype=jnp.int32) // TOP_K + 1)
    row_tok = jnp.where(filled > 0, filled - 1, jnp.arange(n_rows, dtype=jnp.int32) % t_tok)
    blk_start = jnp.arange(n_blocks, dtype=jnp.int32) * MOE_BLOCK
    blk_exp = jnp.minimum(jnp.sum((blk_start[:, None] >= pad_end[None, :]).astype(jnp.int32), axis=1),
                          N_EXPERTS - 1)
    blk_act = (blk_start < pad_end[-1]).astype(jnp.int32)
    blk_new = jnp.concatenate([jnp.ones((1,), jnp.int32), (blk_exp[1:] != blk_exp[:-1]).astype(jnp.int32)])
    y_rows = moe_experts(h[row_tok], blk_exp, blk_act, blk_new, p["layer"], p["w1_all"], p["b1"],
                         p["w2_all"], p["b2"])
    y_sel = y_rows[dest.reshape(t_tok, TOP_K).T]
    return moe_combine(y_sel, gate, resid, mod_blocks, tm, split)


OFF_F = 0
OFF_HY = OFF_F + FN_WIDTH
OFF_Q = OFF_HY + (HY_ORDER + 1) * HY_WIDTH
OFF_K = OFF_Q + COL_QK
OFF_V = OFF_K + COL_QK
OFF_G = OFF_V + DA_WIDTH


def _projection_weights(w_in):
    w = w_in.astype(BF16)
    w_fv = matmul(w[:, OFF_F:OFF_HY], fourier_channel_matrix(), out_dtype=BF16, name="mm_wfold")
    w_cat = jnp.concatenate([w_fv, w[:, OFF_HY:]], axis=1)
    shift = w_fv.shape[1] - (OFF_HY - OFF_F)
    bounds = {"hy": (OFF_HY, OFF_Q), "q": (OFF_Q, OFF_K), "k": (OFF_K, OFF_V), "v": (OFF_V, OFF_G),
              "g": (OFF_G, w_in.shape[1])}
    cols = {"f": (0, w_fv.shape[1])}
    cols.update({name: (a + shift, b + shift) for name, (a, b) in bounds.items()})
    return w_cat, cols


def _project(x, mod_shift, mod_scale, p, rope):
    q_gain = p["q_norm_g"] * (DA_QK_DIM ** -0.5 * math.log2(math.e))
    return project_all(x, p["norm1_g"], mod_shift, mod_scale, p["w_cat"], p["cols"], q_gain, p["k_norm_g"],
                       rope[0], rope[1])


def _token_mixer(x, mod_shift1, mod_scale1, mod_gate, mod_shift2, mod_scale2, p, lam, lam_init, rope, kv_extra,
                 rows_out):
    b, s, d = x.shape
    v_f, z, q, k3, v3, g = _project(x, mod_shift1, mod_scale1, p, rope)

    y_f = seq_dft_real(v_f, FN_WIDTH, 1.0 / math.sqrt(s * FN_GROUP_DIM), _dft_tables_complex(s))

    z = short_conv(z, p["hy_conv_w"], p["hy_conv_b"])
    tables = _dft_tables_real(s)
    filt, mag_blocks = hyena_filters(s, p["hy_w1"], p["hy_b1"], p["hy_freq1"], p["hy_w2"], p["hy_b2"], p["hy_freq2"],
                         p["hy_w3"], p["hy_b3"])
    spec = hyena_spectra(s, filt, mag_blocks, tables)
    cb = HY_WIDTH // LANES
    y_h = long_conv_gated(z, 0, z, cb, spec[0], p["hy_bias"][0], tables)
    y_h = long_conv_gated(y_h, 0, z, 2 * cb, spec[1], p["hy_bias"][1], tables)

    if kv_extra is not None:
        k_all = jnp.concatenate([k3, kv_extra[0]], axis=1)
        v_all = jnp.concatenate([v3, kv_extra[1]], axis=1)
    else:
        k_all, v_all = k3, v3
    nk = k_all.shape[1]
    y_a = diff_attention(q.reshape(b * s, COL_QK), k_all.reshape(b * nk, COL_QK),
                         v_all.reshape(b * nk, 2 * DA_WIDTH), lam, p["subln_g"], 1.0 - lam_init, s, nk)
    y_a = y_a.reshape(b, s, DA_WIDTH)

    total_rows, row_offset, prior = rows_out
    buffers = merge_branches(x, y_f, y_h, y_a, g, p["w_f"], p["w_h"], p["w_a"], p["w_o"],
                             mod_gate, p["norm2_g"], mod_shift2, mod_scale2, total_rows, row_offset, prior)
    return buffers, (k3, v3)


def _layer(l, x, xc, c, c_ctx, p, ctx_out):
    b, n_lat, d = x.shape
    n_ctx = xc.shape[1]
    lam_init = 0.8 - 0.6 * math.exp(-0.3 * l)
    lam = (jnp.exp(jnp.sum(p["lam_q"][0] * p["lam_k"][0]).astype(F32))
           - jnp.exp(jnp.sum(p["lam_q"][1] * p["lam_k"][1]).astype(F32)) + lam_init)

    cond = jnp.concatenate([c, c_ctx[None, :], jnp.zeros((16 - b - 1, d), F32)], axis=0)
    mod_all = matmul(jax.nn.silu(cond).astype(BF16), p["w_mod"].astype(BF16), epi="bias",
                     extra=(p["b_mod"].reshape(1, 6 * d).astype(F32),),
                     extra_specs=[pl.BlockSpec((1, 1024), lambda i, j: (0, j))], name="mm_mod")
    mod = [mod_all[:b, i * d:(i + 1) * d] for i in range(6)]
    mod_c = [jnp.broadcast_to(mod_all[b, i * d:(i + 1) * d], (b, d)) for i in range(6)]

    pw = dict(p)
    pw["w_cat"], pw["cols"] = _projection_weights(p["w_in"])

    no_rope = (jnp.ones((n_ctx, LANES), F32), jnp.zeros((n_ctx, LANES), F32))
    rows_c = b * n_ctx if ctx_out else 0
    total_rows = rows_c + b * n_lat
    prior = None
    if ctx_out:
        prior, kv_c = _token_mixer(xc, mod_c[0], mod_c[1], mod_c[2], mod_c[3], mod_c[4], pw, lam, lam_init,
                                   no_rope, None, (total_rows, 0, None))
    else:
        kv_c = _project(xc, mod_c[0], mod_c[1], pw, no_rope)[3:5]
    (resid, h_all), _ = _token_mixer(x, mod[0], mod[1], mod[2], mod[3], mod[4], pw, lam, lam_init,
                                     rope_tables(n_lat), kv_c, (total_rows, rows_c, prior))

    n_exp, two_f = p["b_e1"].shape
    b1 = p["b_e1"].reshape(n_exp, two_f // GLU_GROUP, LANES, 2).transpose(0, 1, 3, 2).reshape(n_exp, 1, two_f)
    pe = {
        "w_router": p["w_router"], "b_router": p["b_router"], "layer": l,
        "w1_all": p["w_e1_all"], "b1": b1, "w2_all": p["w_e2_all"], "b2": p["b_e2"][:, None, :],
    }
    tm = 512
    assert rows_c % tm == 0 and n_lat % tm == 0
    split = rows_c // tm
    mod_blocks = jnp.concatenate([jnp.tile(mod_c[5][:1], (split, 1)), jnp.repeat(mod[5], n_lat // tm, axis=0)], axis=0)
    outs = moe_ffn(h_all, resid, mod_blocks[:, None, :], tm, split, pe)
    if ctx_out:
        xc = outs[0].reshape(b, n_ctx, d)
    x = outs[-1].reshape(b, n_lat, d)
    return x, xc


_PARAM_NAMES = ("w_mod", "b_mod", "norm1_g", "norm2_g", "w_in", "hy_conv_w", "hy_conv_b", "hy_w1", "hy_b1",
                "hy_freq1", "hy_w2", "hy_b2", "hy_freq2", "hy_w3", "hy_b3", "hy_bias", "q_norm_g", "k_norm_g",
                "lam_q", "lam_k", "subln_g", "w_f", "w_h", "w_a", "w_o", "w_router", "b_router",
                "w_e1", "b_e1", "w_e2", "b_e2")


def kernel(x, c, ctx, c_ctx, w_mod, b_mod, norm1_g, norm2_g, w_in, hy_conv_w, hy_conv_b, hy_w1, hy_b1, hy_freq1,
           hy_w2, hy_b2, hy_freq2, hy_w3, hy_b3, hy_bias, q_norm_g, k_norm_g, lam_q, lam_k, subln_g, w_f, w_h,
           w_a, w_o, w_router, b_router, w_e1, b_e1, w_e2, b_e2):
    stacked = (w_mod, b_mod, norm1_g, norm2_g, w_in, hy_conv_w, hy_conv_b, hy_w1, hy_b1, hy_freq1, hy_w2, hy_b2,
               hy_freq2, hy_w3, hy_b3, hy_bias, q_norm_g, k_norm_g, lam_q, lam_k, subln_g, w_f, w_h, w_a, w_o,
               w_router, b_router, w_e1, b_e1, w_e2, b_e2)
    depth = w_mod.shape[0]
    xc = ctx
    for l in range(depth):
        p = {name: arr[l] for name, arr in zip(_PARAM_NAMES, stacked) if name not in ("w_e1", "w_e2")}
        p["w_e1_all"], p["w_e2_all"] = w_e1, w_e2
        x, xc = _layer(l, x, xc, c, c_ctx, p, l < depth - 1)
    return x
```

```python
import functools
import math

import numpy as np
import jax
import jax.numpy as jnp
from jax import lax
from jax.experimental import pallas as pl
from jax.experimental.pallas import tpu as pltpu

F32 = jnp.float32
BF16 = jnp.bfloat16

LANES = 128
VMEM_LIMIT = 56 * 1024 * 1024

GRID_W = 64
EPS = 1e-6
SUBLN_EPS = 1e-5
FN_GROUPS = 4
FN_GROUP_DIM = 64
FN_WIDTH = FN_GROUPS * FN_GROUP_DIM
HY_WIDTH = 256
HY_ORDER = 2
HY_SHORT = 3
HY_EMB_BANDS = 16
HY_DECAY_TARGET = 1e-2
HY_FAST_DECAY = 0.3
HY_SLOW_DECAY = 1.5
DA_HEADS = 4
DA_QK_DIM = 64
DA_V_DIM = 2 * DA_QK_DIM
DA_WIDTH = DA_HEADS * DA_V_DIM
ROPE_BASE = 10000.0
N_BRANCHES = 3
COL_QK = DA_HEADS * 2 * DA_QK_DIM
N_EXPERTS = 32
TOP_K = 4
SWIGLU_ALPHA = 1.702
SWIGLU_LIMIT = 7.0
MOE_BLOCK = 512
PROJ_GATE_CHUNK = 1024
DFT_MIN_N1 = 16
DFT_UNROLL = 8


def _dft_n2(seq):
    return min(LANES, seq // DFT_MIN_N1)


def _cparams(sem):
    return pltpu.CompilerParams(dimension_semantics=sem, vmem_limit_bytes=VMEM_LIMIT)


def _tile(n, pref):
    if n <= pref:
        return n
    for t in range(pref, 7, -1):
        if n % t == 0 and t % 8 == 0:
            return t
    return n


def _const_spec(shape):
    nd = len(shape)
    return pl.BlockSpec(shape, lambda *_: (0,) * nd, pipeline_mode=pl.Buffered(1))


def _qk_epilogue(acc, gm, gain, cos, sin):
    ms = jnp.dot((acc * acc).astype(BF16), gm, preferred_element_type=F32)
    y = acc * lax.rsqrt(ms + EPS) * gain
    n = y.shape[1]
    reps = n // LANES
    lane = lax.broadcasted_iota(jnp.int32, y.shape, 1)
    is_a = (lane % (DA_QK_DIM // 2)) < (DA_QK_DIM // 4)
    half = DA_QK_DIM // 4
    swapped = jnp.where(is_a, pltpu.roll(y, n - half, axis=1), pltpu.roll(y, half, axis=1))
    return y * jnp.tile(cos, (1, reps)) + swapped * jnp.tile(sin, (1, reps))


def _proj_kernel(x_ref, ng_ref, sh_ref, sc_ref, w_ref, gm_ref, qg_ref, kg_ref, cos_ref, sin_ref,
                 vf_ref, z_ref, q_ref, k_ref, v_ref, g_ref, *, cols):
    x = x_ref[0]
    y = x * lax.rsqrt(jnp.mean(x * x, axis=-1, keepdims=True) + EPS) * ng_ref[...]
    h = (y * (1.0 + sc_ref[0]) + sh_ref[0]).astype(BF16)

    def mm(name):
        a, b = cols[name]
        return jnp.dot(h, w_ref[:, a:b], preferred_element_type=F32)

    vf_ref[0] = mm("f")
    z_ref[0] = mm("hy")
    q_ref[0] = _qk_epilogue(mm("q"), gm_ref[...], qg_ref[...], cos_ref[...], sin_ref[...]).astype(q_ref.dtype)
    k_ref[0] = _qk_epilogue(mm("k"), gm_ref[...], kg_ref[...], cos_ref[...], sin_ref[...]).astype(k_ref.dtype)
    vv = mm("v").astype(v_ref.dtype)
    ones = jnp.ones((vv.shape[0], DA_V_DIM), v_ref.dtype)
    for hd in range(DA_HEADS):
        v_ref[0, :, 2 * hd * DA_V_DIM:(2 * hd + 1) * DA_V_DIM] = vv[:, hd * DA_V_DIM:(hd + 1) * DA_V_DIM]
        v_ref[0, :, (2 * hd + 1) * DA_V_DIM:(2 * hd + 2) * DA_V_DIM] = ones
    a, b = cols["g"]
    for c0 in range(a, b, PROJ_GATE_CHUNK):
        acc = jnp.dot(h, w_ref[:, c0:c0 + PROJ_GATE_CHUNK], preferred_element_type=F32)
        g_ref[0, :, c0 - a:c0 - a + PROJ_GATE_CHUNK] = jax.nn.sigmoid(acc).astype(g_ref.dtype)


def project_all(x, norm_g, shift, scale, w_cat, cols, q_gain, k_gain, cos, sin):
    b, s, d = x.shape
    tm = _tile(s, 512)
    width = {name: stop - start for name, (start, stop) in cols.items()}
    width["v"] *= 2
    gm = _group_mean_matrix(COL_QK, DA_QK_DIM)
    tile_gain = lambda g: jnp.tile(g.astype(F32), COL_QK // DA_QK_DIM).reshape(1, COL_QK)
    rows = lambda w: pl.BlockSpec((1, tm, w), lambda i, j: (i, j, 0))
    per_b = pl.BlockSpec((1, 1, d), lambda i, j: (i, 0, 0))
    full = lambda shape: pl.BlockSpec(shape, lambda i, j: (0, 0))
    outs = [("f", F32), ("hy", F32), ("q", BF16), ("k", BF16), ("v", BF16), ("g", BF16)]
    return pl.pallas_call(
        functools.partial(_proj_kernel, cols=cols),
        grid=(b, s // tm),
        in_specs=[rows(d), full((1, d)), per_b, per_b, _const_spec(w_cat.shape), full(gm.shape),
                  full((1, COL_QK)), full((1, COL_QK)),
                  pl.BlockSpec((tm, LANES), lambda i, j: (j, 0)), pl.BlockSpec((tm, LANES), lambda i, j: (j, 0))],
        out_specs=[rows(width[name]) for name, _ in outs],
        out_shape=[jax.ShapeDtypeStruct((b, s, width[name]), dt) for name, dt in outs],
        compiler_params=_cparams(("parallel", "parallel")),
        name="project_all",
    )(x, norm_g.reshape(1, d).astype(F32), shift.reshape(b, 1, d), scale.reshape(b, 1, d), w_cat, gm,
      tile_gain(q_gain), tile_gain(k_gain), cos, sin)


def _mm_kernel(a_ref, w_ref, *rest, epi):
    acc = jnp.dot(a_ref[...], w_ref[...], preferred_element_type=F32)
    if epi == "plain":
        (o_ref,) = rest
    elif epi == "bias":
        b_ref, o_ref = rest
        acc = acc + b_ref[...]
    else:
        raise ValueError(epi)
    o_ref[...] = acc.astype(o_ref.dtype)


def matmul(a, w, *, out_dtype=F32, epi="plain", extra=(), extra_specs=(), tm=512, tn=1024, name="mm"):
    m, k = a.shape
    k2, n = w.shape
    assert k == k2
    tm = _tile(m, tm)
    tn = _tile(n, tn)
    return pl.pallas_call(
        functools.partial(_mm_kernel, epi=epi),
        grid=(m // tm, n // tn),
        in_specs=[
            pl.BlockSpec((tm, k), lambda i, j: (i, 0)),
            pl.BlockSpec((k, tn), lambda i, j: (0, j)),
            *extra_specs,
        ],
        out_specs=pl.BlockSpec((tm, tn), lambda i, j: (i, j)),
        out_shape=jax.ShapeDtypeStruct((m, n), out_dtype),
        compiler_params=_cparams(("parallel", "parallel")),
        name=name,
    )(a, w, *extra)


def _group_mean_matrix(n, group):
    idx = np.arange(n)
    return jnp.asarray((idx[:, None] // group == idx[None, :] // group).astype(np.float32) / group, BF16)


def rope_tables(n_lat):
    rows = n_lat // GRID_W
    row = np.repeat(np.arange(rows), GRID_W).astype(np.float64)
    col = np.tile(np.arange(GRID_W), rows).astype(np.float64)
    n_freq = DA_QK_DIM // 4
    inv = ROPE_BASE ** (-np.arange(n_freq, dtype=np.float64) / n_freq)
    ang_r = row[:, None] * inv
    ang_c = col[:, None] * inv
    cos = np.concatenate([np.cos(ang_r), np.cos(ang_r), np.cos(ang_c), np.cos(ang_c)], axis=1)
    sin = np.concatenate([-np.sin(ang_r), np.sin(ang_r), -np.sin(ang_c), np.sin(ang_c)], axis=1)
    cos = np.tile(cos, (1, LANES // DA_QK_DIM))
    sin = np.tile(sin, (1, LANES // DA_QK_DIM))
    return jnp.asarray(cos, F32), jnp.asarray(sin, F32)


ATTN_TQ = 512
ATTN_TK = 768


def _attn_kernel(lam_ref, q_ref, k_ref, v_ref, g_ref, o_ref, qs_ref, s_ref, m_ref, acc_ref,
                 *, tq, tk, nkc, nqb, out_scale):
    def stack(qi, qslot):
        q = q_ref[pl.ds(pl.multiple_of(qi * tq, tq), tq), :]
        lane = lax.broadcasted_iota(jnp.int32, q.shape, 1)
        zero = jnp.zeros_like(q)
        qs_ref[qslot, 0:tq, :] = jnp.where(lane < DA_QK_DIM, q, zero)
        qs_ref[qslot, tq:2 * tq, :] = jnp.where(lane >= DA_QK_DIM, q, zero)

    def scores(qslot, j, slot):
        kj = k_ref[j * tk:(j + 1) * tk, :]
        s_ref[slot] = lax.dot_general(qs_ref[qslot], kj, (((1,), (1,)), ((), ())), preferred_element_type=F32)

    def update(j, slot):
        s = s_ref[slot]
        vj = v_ref[j * tk:(j + 1) * tk, :]
        m_prev = m_ref[...]
        m_next = jnp.maximum(m_prev, jnp.max(s, axis=1, keepdims=True))
        p = jnp.exp2(s - jnp.tile(m_next, (1, tk // LANES)))
        alpha = jnp.exp2(m_prev - m_next)
        m_ref[...] = m_next
        acc_ref[...] = acc_ref[...] * jnp.tile(alpha, (1, 2)) + jnp.dot(p.astype(BF16), vj,
                                                                          preferred_element_type=F32)

    def reset():
        m_ref[...] = jnp.full(m_ref.shape, -jnp.inf, F32)
        acc_ref[...] = jnp.zeros(acc_ref.shape, F32)

    stack(0, 0)
    scores(0, 0, 0)
    reset()

    def q_block(qi, local):
        qslot = local % 2
        t0 = local * nkc
        for j in range(nkc):
            nslot = (t0 + j + 1) % 2
            if j + 1 < nkc:
                scores(qslot, j + 1, nslot)
            else:
                nxt = jnp.minimum(qi + 1, nqb - 1)
                stack(nxt, 1 - qslot)
                scores(1 - qslot, 0, nslot)
            update(j, (t0 + j) % 2)
        o1 = acc_ref[0:tq, 0:DA_V_DIM] / acc_ref[0:tq, DA_V_DIM:]
        o2 = acc_ref[tq:2 * tq, 0:DA_V_DIM] / acc_ref[tq:2 * tq, DA_V_DIM:]
        o = o1 - lam_ref[0, 0] * o2
        o = o * lax.rsqrt(jnp.mean(o * o, axis=-1, keepdims=True) + SUBLN_EPS)
        o_ref[pl.ds(pl.multiple_of(qi * tq, tq), tq), :] = (o * g_ref[...] * out_scale).astype(o_ref.dtype)
        reset()

    group = 2
    if nqb % group == 0 and nqb > group:
        def q_group(i, c):
            for local in range(group):
                q_block(i * group + local, local)
            return c

        lax.fori_loop(0, nqb // group, q_group, 0)
    else:
        for qi in range(nqb):
            q_block(qi, qi)


def diff_attention(q, k, v_ext, lam, subln_g, out_scale, nq, nk):
    b = q.shape[0] // nq
    tq = _tile(nq, ATTN_TQ)
    tk = next(t for t in (ATTN_TK, 256, 128) if nk % t == 0)
    nqb, nkc = nq // tq, nk // tk
    kern = functools.partial(_attn_kernel, tq=tq, tk=tk, nkc=nkc, nqb=nqb, out_scale=out_scale)
    return pl.pallas_call(
        kern,
        grid=(b, DA_HEADS),
        in_specs=[
            pl.BlockSpec(memory_space=pltpu.SMEM),
            pl.BlockSpec((nq, DA_V_DIM), lambda bi, h: (bi, h)),
            pl.BlockSpec((nk, DA_V_DIM), lambda bi, h: (bi, h)),
            pl.BlockSpec((nk, 2 * DA_V_DIM), lambda bi, h: (bi, h)),
            pl.BlockSpec((1, DA_V_DIM), lambda bi, h: (0, 0)),
        ],
        out_specs=pl.BlockSpec((nq, DA_V_DIM), lambda bi, h: (bi, h)),
        out_shape=jax.ShapeDtypeStruct((b * nq, DA_WIDTH), BF16),
        scratch_shapes=[
            pltpu.VMEM((2, 2 * tq, DA_V_DIM), BF16),
            pltpu.VMEM((2, 2 * tq, tk), F32),
            pltpu.VMEM((2 * tq, LANES), F32),
            pltpu.VMEM((2 * tq, 2 * DA_V_DIM), F32),
        ],
        compiler_params=_cparams(("parallel", "parallel")),
        name="diff_attn",
    )(lam.reshape(1, 1).astype(F32), q, k, v_ext, subln_g.reshape(1, DA_V_DIM).astype(F32))


def _dft_tables_real(seq):
    n2 = _dft_n2(seq)
    n1h = seq // n2
    n1 = 2 * n1h
    n = n1 * n2
    k1 = np.arange(n1, dtype=np.float64)[None, :, None]
    nn = (n2 * np.arange(n1h, dtype=np.float64)[None, None, :] + np.arange(n2, dtype=np.float64)[:, None, None])
    ang = 2.0 * np.pi * k1 * nn / n
    e_fwd = np.concatenate([np.cos(ang), -np.sin(ang)], axis=1)
    e_inv = np.transpose(e_fwd, (0, 2, 1))
    a2 = 2.0 * np.pi * np.outer(np.arange(n2), np.arange(n2)) / n2
    c, s = np.cos(a2), np.sin(a2)
    f_fwd = np.block([[c, s], [-s, c]])
    f_inv = np.block([[c, -s], [s, c]])
    f_spec = np.block([[c, s, c, s], [-s, c, s, -c]])
    return tuple(jnp.asarray(t, BF16) for t in (e_fwd, f_fwd, f_inv, e_inv, f_spec))


def _pack_complex(re, im):
    r = lax.bitcast_convert_type(re.astype(BF16).astype(F32), jnp.uint32)
    i = lax.bitcast_convert_type(im.astype(BF16).astype(F32), jnp.uint32)
    return r | (i >> 16)


def _unpack_complex(w):
    re = lax.bitcast_convert_type(w & jnp.uint32(0xFFFF0000), F32)
    im = lax.bitcast_convert_type(w << 16, F32)
    return jnp.concatenate([re, im], axis=0).astype(BF16)


def _spectrum_kernel(hf_ref, hb_ref, sc_ref, ef_ref, fs_ref, o_ref, scr_f, scr_b, *, n1, n1h, n2, kc):
    kk = pl.program_id(2)

    @pl.when(kk == 0)
    def _():
        def stage1(j, c):
            xf = hf_ref[pl.ds(j, n1h, stride=n2), :]
            xb = hb_ref[pl.ds(j, n1h, stride=n2), :]
            row = lax.broadcasted_iota(jnp.int32, xb.shape, 0)
            xb = jnp.where((row == 0) & (j == 0), 0.0, xb)
            af = jnp.dot(ef_ref[j], xf.astype(BF16), preferred_element_type=F32)
            ab = jnp.dot(ef_ref[j], xb.astype(BF16), preferred_element_type=F32)
            r0 = pl.multiple_of(j * n1, n1)
            scr_f[pl.ds(r0, n1), :] = _pack_complex(af[:n1], af[n1:])
            scr_b[pl.ds(r0, n1), :] = _pack_complex(ab[:n1], ab[n1:])
            return c

        lax.fori_loop(0, n2, stage1, 0, unroll=DFT_UNROLL)

    def stage2(t, c):
        k1 = kk * kc + t
        a = jnp.concatenate([_unpack_complex(scr_f[pl.ds(k1, n2, stride=n1), :]),
                             _unpack_complex(scr_b[pl.ds(k1, n2, stride=n1), :])], axis=0)
        o_ref[t] = (jnp.dot(fs_ref[...], a, preferred_element_type=F32) * sc_ref[...]).astype(o_ref.dtype)
        return c

    lax.fori_loop(0, kc, stage2, 0, unroll=DFT_UNROLL)


def filter_spectra(filt, scale, tables):
    seq = filt.shape[0]
    n_order, _, ch = scale.shape
    n2 = _dft_n2(seq)
    n1h = seq // n2
    n1 = 2 * n1h
    e_fwd, f_spec = tables[0], tables[4]
    kc = min(n1, 16)
    cb = ch // LANES
    kern = functools.partial(_spectrum_kernel, n1=n1, n1h=n1h, n2=n2, kc=kc)
    return pl.pallas_call(
        kern,
        grid=(n_order, cb, n1 // kc),
        in_specs=[
            pl.BlockSpec((seq, LANES), lambda o, c, k: (0, (2 * o) * cb + c)),
            pl.BlockSpec((seq, LANES), lambda o, c, k: (0, (2 * o + 1) * cb + c)),
            pl.BlockSpec((None, 1, LANES), lambda o, c, k: (o, 0, c)),
            _const_spec(e_fwd.shape),
            _const_spec(f_spec.shape),
        ],
        out_specs=pl.BlockSpec((None, kc, 2 * n2, LANES), lambda o, c, k: (o, k, 0, c)),
        out_shape=jax.ShapeDtypeStruct((n_order, n1, 2 * n2, ch), BF16),
        scratch_shapes=[pltpu.VMEM((n1 * n2, LANES), jnp.uint32), pltpu.VMEM((n1 * n2, LANES), jnp.uint32)],
        compiler_params=_cparams(("parallel", "parallel", "arbitrary")),
        name="filter_spectra",
    )(filt, filt, scale, e_fwd, f_spec)


def _longconv_kernel(u_ref, g_ref, h_ref, bias_ref, ef_ref, ff_ref, fi_ref, ei_ref, o_ref, scr_a, scr_b,
                     *, n1, n1h, n2):
    def stage1(j, c):
        x = u_ref[pl.ds(j, n1h, stride=n2), :].astype(BF16)
        a = jnp.dot(ef_ref[j], x, preferred_element_type=F32)
        scr_a[pl.ds(pl.multiple_of(j * n1, n1), n1), :] = _pack_complex(a[:n1], a[n1:])
        return c

    lax.fori_loop(0, n2, stage1, 0, unroll=DFT_UNROLL)

    def stage2(k1, c):
        a = _unpack_complex(scr_a[pl.ds(k1, n2, stride=n1), :])
        y = jnp.dot(ff_ref[...], a, preferred_element_type=F32)
        hk = h_ref[k1].astype(F32)
        yr, yi = y[:n2], y[n2:]
        hr, hi = hk[:n2], hk[n2:]
        z = jnp.concatenate([yr * hr - yi * hi, yr * hi + yi * hr], axis=0).astype(BF16)
        bk = jnp.dot(fi_ref[...], z, preferred_element_type=F32)
        scr_b[pl.ds(pl.multiple_of(k1 * n2, n2), n2), :] = _pack_complex(bk[:n2], bk[n2:])
        return c

    lax.fori_loop(0, n1, stage2, 0, unroll=DFT_UNROLL)

    def stage3(j, c):
        bmat = _unpack_complex(scr_b[pl.ds(j, n1, stride=n2), :])
        y = jnp.dot(ei_ref[j], bmat, preferred_element_type=F32)
        u = u_ref[pl.ds(j, n1h, stride=n2), :]
        g = g_ref[pl.ds(j, n1h, stride=n2), :]
        o_ref[pl.ds(j, n1h, stride=n2), :] = g * (y + u * bias_ref[...])
        return c

    lax.fori_loop(0, n2, stage3, 0, unroll=DFT_UNROLL)


def long_conv_gated(u, u_blk, g, g_blk, spec, bias, tables):
    b, seq, _ = u.shape
    ch = bias.shape[0]
    n2 = _dft_n2(seq)
    n1h = seq // n2
    n1 = 2 * n1h
    e_fwd, f_fwd, f_inv, e_inv = tables[:4]
    kern = functools.partial(_longconv_kernel, n1=n1, n1h=n1h, n2=n2)
    one = pl.Buffered(1)
    return pl.pallas_call(
        kern,
        grid=(ch // LANES, b),
        in_specs=[
            pl.BlockSpec((None, seq, LANES), lambda c, i: (i, 0, u_blk + c), pipeline_mode=one),
            pl.BlockSpec((None, seq, LANES), lambda c, i: (i, 0, g_blk + c), pipeline_mode=one),
            pl.BlockSpec((n1, 2 * n2, LANES), lambda c, i: (0, 0, c), pipeline_mode=one),
            pl.BlockSpec((1, LANES), lambda c, i: (0, c)),
            _const_spec(e_fwd.shape),
            _const_spec(f_fwd.shape),
            _const_spec(f_inv.shape),
            _const_spec(e_inv.shape),
        ],
        out_specs=pl.BlockSpec((None, seq, LANES), lambda c, i: (i, 0, c)),
        out_shape=jax.ShapeDtypeStruct((b, seq, ch), F32),
        scratch_shapes=[pltpu.VMEM((n1 * n2, LANES), jnp.uint32), pltpu.VMEM((n1 * n2, LANES), jnp.uint32)],
        compiler_params=_cparams(("parallel", "parallel")),
        name="long_conv",
    )(u, g, spec, bias.reshape(1, ch).astype(F32), e_fwd, f_fwd, f_inv, e_inv)


def _dft_tables_complex(seq):
    n2 = _dft_n2(seq)
    n1 = seq // n2
    k1 = np.arange(n1, dtype=np.float64)[None, :, None]
    nn = (n2 * np.arange(n1, dtype=np.float64)[None, None, :] + np.arange(n2, dtype=np.float64)[:, None, None])
    ang = 2.0 * np.pi * k1 * nn / seq
    c, s = np.cos(ang), np.sin(ang)
    e_fwd = np.concatenate([np.concatenate([c, s], axis=2), np.concatenate([-s, c], axis=2)], axis=1)
    a2 = 2.0 * np.pi * np.outer(np.arange(n2), np.arange(n2)) / n2
    f_re = np.concatenate([np.cos(a2), np.sin(a2)], axis=1)
    return jnp.asarray(e_fwd, BF16), jnp.asarray(f_re, BF16)


def _seqdft_kernel(vr_ref, vi_ref, ef_ref, fr_ref, o_ref, scr, *, n1, n2, scale):
    def stage1(j, c):
        x = jnp.concatenate([vr_ref[pl.ds(j, n1, stride=n2), :], vi_ref[pl.ds(j, n1, stride=n2), :]], axis=0)
        a = jnp.dot(ef_ref[j], x.astype(BF16), preferred_element_type=F32)
        scr[pl.ds(pl.multiple_of(j * n1, n1), n1), :] = _pack_complex(a[:n1], a[n1:])
        return c

    lax.fori_loop(0, n2, stage1, 0, unroll=DFT_UNROLL)

    def stage2(k1, c):
        a = _unpack_complex(scr[pl.ds(k1, n2, stride=n1), :])
        o_ref[pl.ds(k1, n2, stride=n1), :] = jnp.dot(fr_ref[...], a, preferred_element_type=F32) * scale
        return c

    lax.fori_loop(0, n1, stage2, 0, unroll=DFT_UNROLL)


def seq_dft_real(v, ch, scale, tables):
    b, seq, _ = v.shape
    n2 = _dft_n2(seq)
    n1 = seq // n2
    e_fwd, f_re = tables
    nblk = ch // LANES
    kern = functools.partial(_seqdft_kernel, n1=n1, n2=n2, scale=scale)
    return pl.pallas_call(
        kern,
        grid=(nblk, b),
        in_specs=[
            pl.BlockSpec((None, seq, LANES), lambda c, i: (i, 0, c)),
            pl.BlockSpec((None, seq, LANES), lambda c, i: (i, 0, nblk + c)),
            _const_spec(e_fwd.shape),
            _const_spec(f_re.shape),
        ],
        out_specs=pl.BlockSpec((None, seq, LANES), lambda c, i: (i, 0, c)),
        out_shape=jax.ShapeDtypeStruct((b, seq, ch), F32),
        scratch_shapes=[pltpu.VMEM((n1 * n2, LANES), jnp.uint32)],
        compiler_params=_cparams(("parallel", "parallel")),
        name="seq_dft",
    )(v, v, e_fwd, f_re)


def fourier_channel_matrix():
    a = 2.0 * np.pi * np.outer(np.arange(FN_GROUP_DIM), np.arange(FN_GROUP_DIM)) / FN_GROUP_DIM
    eye = np.eye(FN_GROUPS)
    return jnp.asarray(np.concatenate([np.kron(eye, np.cos(a)), -np.kron(eye, np.sin(a))], axis=1), BF16)


def _shortconv_kernel(u_ref, w_ref, b_ref, o_ref):
    u = u_ref[...]
    n = u.shape[0]
    row = lax.broadcasted_iota(jnp.int32, u.shape, 0)
    prev = jnp.where(row == 0, 0.0, pltpu.roll(u, 1, axis=0))
    nxt = jnp.where(row == n - 1, 0.0, pltpu.roll(u, n - 1, axis=0))
    o_ref[...] = prev * w_ref[0:1, :] + u * w_ref[1:2, :] + nxt * w_ref[2:3, :] + b_ref[...]


def short_conv(u, w, bias):
    b, seq, ch = u.shape
    return pl.pallas_call(
        _shortconv_kernel,
        grid=(b, ch // LANES),
        in_specs=[
            pl.BlockSpec((None, seq, LANES), lambda i, c: (i, 0, c)),
            pl.BlockSpec((HY_SHORT, LANES), lambda i, c: (0, c)),
            pl.BlockSpec((1, LANES), lambda i, c: (0, c)),
        ],
        out_specs=pl.BlockSpec((None, seq, LANES), lambda i, c: (i, 0, c)),
        out_shape=jax.ShapeDtypeStruct((b, seq, ch), F32),
        compiler_params=_cparams(("parallel", "parallel")),
        name="short_conv",
    )(u, w.astype(F32), bias.reshape(1, ch).astype(F32))


def _filter_kernel(emb_ref, w1_ref, b1_ref, f1_ref, w2_ref, b2_ref, f2_ref, w3_ref, b3_ref, dec_ref, o_ref, mag_ref):
    z = jnp.dot(emb_ref[...].astype(BF16), w1_ref[...], preferred_element_type=F32) + b1_ref[...]
    z = jnp.sin(f1_ref[...] * z)
    z = jnp.dot(z.astype(BF16), w2_ref[...], preferred_element_type=F32) + b2_ref[...]
    z = jnp.sin(f2_ref[...] * z)
    h = jnp.dot(z.astype(BF16), w3_ref[...], preferred_element_type=F32) + b3_ref[...]
    filt = h * dec_ref[...]
    o_ref[...] = filt
    mag_ref[0] = jnp.sum(jnp.abs(filt), axis=0, keepdims=True)


def hyena_filters(seq, hy_w1, hy_b1, hy_freq1, hy_w2, hy_b2, hy_freq2, hy_w3, hy_b3):
    t = jnp.linspace(0.0, 1.0, seq, dtype=F32)[:, None]
    ang = (2.0 * math.pi / seq) * jnp.arange(seq, dtype=F32)[:, None]
    bands = jnp.linspace(1e-4, HY_EMB_BANDS - 1, HY_EMB_BANDS, dtype=F32)[None, :]
    emb = jnp.concatenate([t, jnp.cos(bands * ang), -jnp.sin(bands * ang)], axis=-1)
    kdim = emb.shape[1]
    kpad = LANES - kdim
    emb = jnp.pad(emb, ((0, 0), (0, kpad)))
    w1 = jnp.pad(hy_w1, ((0, kpad), (0, 0))).astype(BF16)
    deltas = jnp.abs(jnp.linspace(math.log(HY_DECAY_TARGET) / HY_SLOW_DECAY,
                                  math.log(HY_DECAY_TARGET) / HY_FAST_DECAY, HY_WIDTH, dtype=F32))
    decay = jnp.tile(jnp.exp(-t * deltas), (1, 2 * HY_ORDER))
    fo = hy_w1.shape[1]
    nout = hy_w3.shape[1]
    tl = _tile(seq, 1024)
    row = lambda a: a.reshape(1, -1).astype(F32)
    full = lambda shape: pl.BlockSpec(shape, lambda i: (0, 0))
    return pl.pallas_call(
        _filter_kernel,
        grid=(seq // tl,),
        in_specs=[
            pl.BlockSpec((tl, LANES), lambda i: (i, 0)),
            full((LANES, fo)), full((1, fo)), full((1, fo)),
            full((fo, fo)), full((1, fo)), full((1, fo)),
            full((fo, nout)), full((1, nout)),
            pl.BlockSpec((tl, nout), lambda i: (i, 0)),
        ],
        out_specs=[pl.BlockSpec((tl, nout), lambda i: (i, 0)), pl.BlockSpec((1, 1, nout), lambda i: (i, 0, 0))],
        out_shape=[jax.ShapeDtypeStruct((seq, nout), F32), jax.ShapeDtypeStruct((seq // tl, 1, nout), F32)],
        compiler_params=_cparams(("parallel",)),
        name="hyena_filter",
    )(emb, w1, row(hy_b1), row(hy_freq1), hy_w2.astype(BF16), row(hy_b2), row(hy_freq2),
      hy_w3.astype(BF16), row(hy_b3), decay)


def hyena_spectra(seq, filt, mag_blocks, tables):
    mag = jnp.sum(mag_blocks, axis=(0, 1)).reshape(HY_ORDER, 2, HY_WIDTH)
    lag0_bwd = jnp.abs(filt[0]).reshape(HY_ORDER, 2, HY_WIDTH)[:, 1]
    norm = mag[:, 0] + mag[:, 1] - lag0_bwd
    scale = (1.0 / (2 * seq)) / norm
    return filter_spectra(filt, scale[:, None, :], tables)


def _merge_kernel(x_ref, yf_ref, yh_ref, ya_ref, g_ref, wf_ref, wh_ref, wa_ref, wo_ref,
                  gate_ref, ng_ref, sh_ref, sc_ref, *rest):
    xo_ref, ho_ref = rest[-2:]
    d = x_ref.shape[-1]
    g = g_ref[0].astype(F32)
    yf = jnp.dot(yf_ref[0].astype(BF16), wf_ref[...], preferred_element_type=F32)
    yh = jnp.dot(yh_ref[0].astype(BF16), wh_ref[...], preferred_element_type=F32)
    ya = jnp.dot(ya_ref[0], wa_ref[...], preferred_element_type=F32)
    mix = g[:, 0:d] * yf + g[:, d:2 * d] * yh + g[:, 2 * d:3 * d] * ya
    x = x_ref[0] + gate_ref[0] * jnp.dot(mix.astype(BF16), wo_ref[...], preferred_element_type=F32)
    xo_ref[...] = x
    y = x * lax.rsqrt(jnp.mean(x * x, axis=-1, keepdims=True) + EPS) * ng_ref[...]
    ho_ref[...] = (y * (1.0 + sc_ref[0]) + sh_ref[0]).astype(ho_ref.dtype)


def merge_branches(x, yf, yh, ya, g, w_f, w_h, w_a, w_o, gate, norm_g, shift, scale, total_rows, row_offset, prior):
    b, l, d = x.shape
    tl = _tile(l, 512)
    assert row_offset % tl == 0
    rows = lambda w: pl.BlockSpec((1, tl, w), lambda i, j: (i, j, 0))
    full = lambda a: pl.BlockSpec(a.shape, lambda i, j: (0, 0))
    per_b = pl.BlockSpec((1, 1, d), lambda i, j: (i, 0, 0))
    out_rows = pl.BlockSpec((tl, d), lambda i, j: (row_offset // tl + i * (l // tl) + j, 0))
    wf, wh, wa, wo = (w.astype(BF16) for w in (w_f, w_h, w_a, w_o))
    in_specs = [rows(d), rows(yf.shape[-1]), rows(yh.shape[-1]), rows(ya.shape[-1]), rows(3 * d),
                full(wf), full(wh), full(wa), full(wo),
                per_b, pl.BlockSpec((1, d), lambda i, j: (0, 0)), per_b, per_b]
    args = [x, yf, yh, ya, g, wf, wh, wa, wo, gate.reshape(b, 1, d), norm_g.reshape(1, d).astype(F32),
            shift.reshape(b, 1, d), scale.reshape(b, 1, d)]
    aliases = {}
    if prior is not None:
        aliases = {len(args): 0, len(args) + 1: 1}
        in_specs += [pl.BlockSpec(memory_space=pl.ANY), pl.BlockSpec(memory_space=pl.ANY)]
        args += list(prior)
    return pl.pallas_call(
        _merge_kernel,
        grid=(b, l // tl),
        in_specs=in_specs,
        out_specs=[out_rows, out_rows],
        out_shape=[jax.ShapeDtypeStruct((total_rows, d), F32), jax.ShapeDtypeStruct((total_rows, d), BF16)],
        input_output_aliases=aliases,
        compiler_params=_cparams(("parallel", "parallel")),
        name="merge",
    )(*args)


GLU_GROUP = 2 * LANES


def _glu_group_permutation():
    p = np.zeros((GLU_GROUP, GLU_GROUP), np.float32)
    j = np.arange(LANES)
    p[2 * j, j] = 1.0
    p[2 * j + 1, LANES + j] = 1.0
    return jnp.asarray(p, BF16)


def _moe_kernel(be_ref, act_ref, new_ref, rows_ref, w1_ref, b1_ref, w2_ref, b2_ref, p_ref, o_ref, w1s, w2s):
    i = pl.program_id(0)

    @pl.when(new_ref[i] > 0)
    def _():
        for q in range(w1s.shape[1] // GLU_GROUP):
            cols = slice(q * GLU_GROUP, (q + 1) * GLU_GROUP)
            w1s[:, cols] = jnp.dot(w1_ref[:, cols].astype(BF16), p_ref[...],
                                   preferred_element_type=F32).astype(BF16)
        w2s[...] = w2_ref[...].astype(BF16)

    @pl.when(act_ref[i] > 0)
    def _():
        u = jnp.dot(rows_ref[...], w1s[...], preferred_element_type=F32) + b1_ref[0]
        parts = []
        for q in range(u.shape[1] // GLU_GROUP):
            xg = jnp.minimum(u[:, q * GLU_GROUP:q * GLU_GROUP + LANES], SWIGLU_LIMIT)
            xl = jnp.clip(u[:, q * GLU_GROUP + LANES:(q + 1) * GLU_GROUP], -SWIGLU_LIMIT, SWIGLU_LIMIT)
            parts.append((xg * jax.nn.sigmoid(SWIGLU_ALPHA * xg) * (xl + 1.0)).astype(BF16))
        a = jnp.concatenate(parts, axis=1)
        y = jnp.dot(a, w2s[...], preferred_element_type=F32) + b2_ref[0]
        o_ref[...] = y.astype(o_ref.dtype)

    @pl.when(act_ref[i] == 0)
    def _():
        o_ref[...] = jnp.zeros(o_ref.shape, o_ref.dtype)


def moe_experts(rows, blk_exp, blk_act, blk_new, layer, w1_all, b1, w2_all, b2):
    r, d = rows.shape
    de = w2_all.shape[2]
    nblk = r // MOE_BLOCK
    grid_spec = pltpu.PrefetchScalarGridSpec(
        num_scalar_prefetch=3,
        grid=(nblk,),
        in_specs=[
            pl.BlockSpec((MOE_BLOCK, d), lambda i, be, act, new: (i, 0)),
            pl.BlockSpec((None, None, d, 2 * de), lambda i, be, act, new: (layer, be[i], 0, 0)),
            pl.BlockSpec((1, 1, 2 * de), lambda i, be, act, new: (be[i], 0, 0)),
            pl.BlockSpec((None, None, de, d), lambda i, be, act, new: (layer, be[i], 0, 0)),
            pl.BlockSpec((1, 1, d), lambda i, be, act, new: (be[i], 0, 0)),
            pl.BlockSpec((GLU_GROUP, GLU_GROUP), lambda i, be, act, new: (0, 0)),
        ],
        out_specs=pl.BlockSpec((MOE_BLOCK, d), lambda i, be, act, new: (i, 0)),
        scratch_shapes=[pltpu.VMEM((d, 2 * de), BF16), pltpu.VMEM((de, d), BF16)],
    )
    return pl.pallas_call(
        _moe_kernel,
        grid_spec=grid_spec,
        out_shape=jax.ShapeDtypeStruct((r, d), BF16),
        compiler_params=_cparams(("arbitrary",)),
        name="moe_experts",
    )(blk_exp, blk_act, blk_new, rows, w1_all, b1, w2_all, b2, _glu_group_permutation())


def _combine_kernel(y_ref, g_ref, x_ref, m_ref, *o_refs, split):
    g = g_ref[...]
    acc = g[:, 0:1] * y_ref[0].astype(F32)
    for j in range(1, TOP_K):
        acc = acc + g[:, j:j + 1] * y_ref[j].astype(F32)
    val = x_ref[...] + m_ref[0] * acc
    if split == 0:
        o_refs[0][...] = val
    else:
        i = pl.program_id(0)

        @pl.when(i < split)
        def _():
            o_refs[0][...] = val

        @pl.when(i >= split)
        def _():
            o_refs[1][...] = val


def moe_combine(y_sel, gate, resid, mod_blocks, tm, split):
    k, t, d = y_sel.shape
    nb = t // tm
    if split == 0:
        out_specs = [pl.BlockSpec((tm, d), lambda i: (i, 0))]
        out_shape = [jax.ShapeDtypeStruct((t, d), F32)]
    else:
        out_specs = [pl.BlockSpec((tm, d), lambda i: (jnp.minimum(i, split - 1), 0)),
                     pl.BlockSpec((tm, d), lambda i: (jnp.maximum(i - split, 0), 0))]
        out_shape = [jax.ShapeDtypeStruct((split * tm, d), F32), jax.ShapeDtypeStruct(((nb - split) * tm, d), F32)]
    return pl.pallas_call(
        functools.partial(_combine_kernel, split=split),
        grid=(nb,),
        in_specs=[pl.BlockSpec((k, tm, d), lambda i: (0, i, 0)),
                  pl.BlockSpec((tm, k), lambda i: (i, 0)),
                  pl.BlockSpec((tm, d), lambda i: (i, 0)),
                  pl.BlockSpec((1, 1, d), lambda i: (i, 0, 0))],
        out_specs=out_specs,
        out_shape=out_shape,
        compiler_params=_cparams(("arbitrary",)),
        name="moe_combine",
    )(y_sel, gate, resid, mod_blocks)


ROUTER_TM = 512


def _router_kernel(h_ref, w_ref, b_ref, tri_ref, e_ref, g_ref, r_ref, cnt_ref, carry):
    @pl.when(pl.program_id(0) == 0)
    def _():
        carry[...] = jnp.zeros(carry.shape, F32)

    logits = jnp.dot(h_ref[...], w_ref[...], preferred_element_type=F32) + b_ref[...]
    slot_lane = lax.broadcasted_iota(jnp.int32, logits.shape, 1)
    lane = slot_lane.astype(F32)
    rem = logits
    tops, idxs, hots = [], [], []
    for _ in range(TOP_K):
        m = jnp.max(rem, axis=1, keepdims=True)
        idx = jnp.min(jnp.where(rem == m, lane, float(LANES)), axis=1, keepdims=True)
        hot = lane == idx
        tops.append(m)
        idxs.append(idx)
        hots.append(hot)
        rem = jnp.where(hot, -jnp.inf, rem)
    ex = [jnp.exp(t - tops[0]) for t in tops]
    den = functools.reduce(lambda a, b: a + b, ex)
    sel = functools.reduce(lambda a, b: a + b, [hot.astype(F32) for hot in hots])
    before = jnp.dot(tri_ref[...], sel.astype(BF16), preferred_element_type=F32) + carry[...]
    e_out = jnp.zeros(logits.shape, F32)
    g_out = jnp.zeros(logits.shape, F32)
    r_out = jnp.zeros(logits.shape, F32)
    for j in range(TOP_K):
        slot = slot_lane == j
        e_out = jnp.where(slot, idxs[j], e_out)
        g_out = jnp.where(slot, ex[j] / den, g_out)
        r_out = jnp.where(slot, jnp.sum(jnp.where(hots[j], before, 0.0), axis=1, keepdims=True), r_out)
    e_ref[...] = e_out.astype(jnp.int32)
    g_ref[...] = g_out
    r_ref[...] = r_out.astype(jnp.int32)
    carry[...] = carry[...] + jnp.sum(sel, axis=0, keepdims=True)
    cnt_ref[...] = carry[...]


def route(h, w_router, b_router):
    t_tok, d = h.shape
    tm = _tile(t_tok, ROUTER_TM)
    wr = jnp.pad(w_router, ((0, 0), (0, LANES - N_EXPERTS))).astype(BF16)
    br = jnp.pad(b_router.astype(F32), (0, LANES - N_EXPERTS), constant_values=-jnp.inf).reshape(1, LANES)
    tri = jnp.asarray(np.tril(np.ones((tm, tm), np.float32), -1), BF16)
    full = lambda shape: pl.BlockSpec(shape, lambda i: (0, 0))
    rows = pl.BlockSpec((tm, LANES), lambda i: (i, 0))
    e, g, r, cnt = pl.pallas_call(
        _router_kernel,
        grid=(t_tok // tm,),
        in_specs=[pl.BlockSpec((tm, d), lambda i: (i, 0)), full((d, LANES)), full((1, LANES)), full((tm, tm))],
        out_specs=[rows, rows, rows, full((1, LANES))],
        out_shape=[jax.ShapeDtypeStruct((t_tok, LANES), jnp.int32), jax.ShapeDtypeStruct((t_tok, LANES), F32),
                   jax.ShapeDtypeStruct((t_tok, LANES), jnp.int32), jax.ShapeDtypeStruct((1, LANES), F32)],
        scratch_shapes=[pltpu.VMEM((1, LANES), F32)],
        compiler_params=_cparams(("arbitrary",)),
        name="router",
    )(h, wr, br, tri)
    return e[:, :TOP_K], g[:, :TOP_K], r[:, :TOP_K], cnt[0, :N_EXPERTS].astype(jnp.int32)


def moe_ffn(h, resid, mod_blocks, tm, split, p, between):
    t_tok, d = h.shape
    top_i, gate, rank, counts = route(h, p["w_router"], p["b_router"])
    n_assign = t_tok * TOP_K
    padded = (counts + MOE_BLOCK - 1) // MOE_BLOCK * MOE_BLOCK
    pad_end = jnp.cumsum(padded)
    pad_start = pad_end - padded
    experts = jnp.arange(N_EXPERTS, dtype=jnp.int32)
    dest = rank + jnp.sum(jnp.where(top_i[..., None] == experts, pad_start, 0), axis=-1)
    dest = dest.reshape(-1)
    n_blocks = -(-n_assign // MOE_BLOCK) + N_EXPERTS
    n_rows = n_blocks * MOE_BLOCK
    filled = jnp.zeros((n_rows,), jnp.int32).at[dest].add(jnp.arange(n_assign, dtype=jnp.int32) // TOP_K + 1)
    row_tok = jnp.where(filled > 0, filled - 1, jnp.arange(n_rows, dtype=jnp.int32) % t_tok)
    blk_start = jnp.arange(n_blocks, dtype=jnp.int32) * MOE_BLOCK
    blk_exp = jnp.minimum(jnp.sum((blk_start[:, None] >= pad_end[None, :]).astype(jnp.int32), axis=1),
                          N_EXPERTS - 1)
    blk_act = (blk_start < pad_end[-1]).astype(jnp.int32)
    blk_new = jnp.concatenate([jnp.ones((1,), jnp.int32), (blk_exp[1:] != blk_exp[:-1]).astype(jnp.int32)])
    rows = h[row_tok]
    extra = between() if between is not None else None
    y_rows = moe_experts(rows, blk_exp, blk_act, blk_new, p["layer"], p["w1_all"], p["b1"], p["w2_all"], p["b2"])
    y_sel = y_rows[dest.reshape(t_tok, TOP_K).T]
    return moe_combine(y_sel, gate, resid, mod_blocks, tm, split), extra


OFF_F = 0
OFF_HY = OFF_F + FN_WIDTH
OFF_Q = OFF_HY + (HY_ORDER + 1) * HY_WIDTH
OFF_K = OFF_Q + COL_QK
OFF_V = OFF_K + COL_QK
OFF_G = OFF_V + DA_WIDTH


def _projection_weights(w_in):
    w = w_in.astype(BF16)
    w_fv = matmul(w[:, OFF_F:OFF_HY], fourier_channel_matrix(), out_dtype=BF16, name="mm_wfold")
    w_cat = jnp.concatenate([w_fv, w[:, OFF_HY:]], axis=1)
    shift = w_fv.shape[1] - (OFF_HY - OFF_F)
    bounds = {"hy": (OFF_HY, OFF_Q), "q": (OFF_Q, OFF_K), "k": (OFF_K, OFF_V), "v": (OFF_V, OFF_G),
              "g": (OFF_G, w_in.shape[1])}
    cols = {"f": (0, w_fv.shape[1])}
    cols.update({name: (a + shift, b + shift) for name, (a, b) in bounds.items()})
    return w_cat, cols


def _project(x, mod_shift, mod_scale, p, rope):
    q_gain = p["q_norm_g"] * (DA_QK_DIM ** -0.5 * math.log2(math.e))
    return project_all(x, p["norm1_g"], mod_shift, mod_scale, p["w_cat"], p["cols"], q_gain, p["k_norm_g"],
                       rope[0], rope[1])


def _token_mixer(x, mod_shift1, mod_scale1, mod_gate, mod_shift2, mod_scale2, p, lam, lam_init, rope, kv_extra,
                 rows_out):
    b, s, d = x.shape
    v_f, z, q, k3, v3, g = _project(x, mod_shift1, mod_scale1, p, rope)

    y_f = seq_dft_real(v_f, FN_WIDTH, 1.0 / math.sqrt(s * FN_GROUP_DIM), _dft_tables_complex(s))

    z = short_conv(z, p["hy_conv_w"], p["hy_conv_b"])
    tables = _dft_tables_real(s)
    spec = p["spectra"][s]
    cb = HY_WIDTH // LANES
    y_h = long_conv_gated(z, 0, z, cb, spec[0], p["hy_bias"][0], tables)
    y_h = long_conv_gated(y_h, 0, z, 2 * cb, spec[1], p["hy_bias"][1], tables)

    if kv_extra is not None:
        k_all = jnp.concatenate([k3, kv_extra[0]], axis=1)
        v_all = jnp.concatenate([v3, kv_extra[1]], axis=1)
    else:
        k_all, v_all = k3, v3
    nk = k_all.shape[1]
    y_a = diff_attention(q.reshape(b * s, COL_QK), k_all.reshape(b * nk, COL_QK),
                         v_all.reshape(b * nk, 2 * DA_WIDTH), lam, p["subln_g"], 1.0 - lam_init, s, nk)
    y_a = y_a.reshape(b, s, DA_WIDTH)

    total_rows, row_offset, prior = rows_out
    buffers = merge_branches(x, y_f, y_h, y_a, g, p["w_f"], p["w_h"], p["w_a"], p["w_o"],
                             mod_gate, p["norm2_g"], mod_shift2, mod_scale2, total_rows, row_offset, prior)
    return buffers, (k3, v3)


def _layer_setup(p, c, c_ctx, seqs):
    b, d = c.shape
    cond = jnp.concatenate([c, c_ctx[None, :], jnp.zeros((16 - b - 1, d), F32)], axis=0)
    mod_all = matmul(jax.nn.silu(cond).astype(BF16), p["w_mod"].astype(BF16), epi="bias",
                     extra=(p["b_mod"].reshape(1, 6 * d).astype(F32),),
                     extra_specs=[pl.BlockSpec((1, 1024), lambda i, j: (0, j))], name="mm_mod")
    pre = {"mod_all": mod_all, "spectra": {}}
    pre["w_cat"], pre["cols"] = _projection_weights(p["w_in"])
    for s in seqs:
        filt, mag_blocks = hyena_filters(s, p["hy_w1"], p["hy_b1"], p["hy_freq1"], p["hy_w2"], p["hy_b2"],
                                         p["hy_freq2"], p["hy_w3"], p["hy_b3"])
        pre["spectra"][s] = hyena_spectra(s, filt, mag_blocks, _dft_tables_real(s))
    return pre


def _layer(l, x, xc, p, ctx_out, pre, next_setup):
    b, n_lat, d = x.shape
    n_ctx = xc.shape[1]
    lam_init = 0.8 - 0.6 * math.exp(-0.3 * l)
    lam = (jnp.exp(jnp.sum(p["lam_q"][0] * p["lam_k"][0]).astype(F32))
           - jnp.exp(jnp.sum(p["lam_q"][1] * p["lam_k"][1]).astype(F32)) + lam_init)

    mod_all = pre["mod_all"]
    mod = [mod_all[:b, i * d:(i + 1) * d] for i in range(6)]
    mod_c = [jnp.broadcast_to(mod_all[b, i * d:(i + 1) * d], (b, d)) for i in range(6)]

    pw = dict(p)
    pw.update(pre)

    no_rope = (jnp.ones((n_ctx, LANES), F32), jnp.zeros((n_ctx, LANES), F32))
    rows_c = b * n_ctx if ctx_out else 0
    total_rows = rows_c + b * n_lat
    prior = None
    if ctx_out:
        prior, kv_c = _token_mixer(xc, mod_c[0], mod_c[1], mod_c[2], mod_c[3], mod_c[4], pw, lam, lam_init,
                                   no_rope, None, (total_rows, 0, None))
    else:
        kv_c = _project(xc, mod_c[0], mod_c[1], pw, no_rope)[3:5]
    (resid, h_all), _ = _token_mixer(x, mod[0], mod[1], mod[2], mod[3], mod[4], pw, lam, lam_init,
                                     rope_tables(n_lat), kv_c, (total_rows, rows_c, prior))

    n_exp, two_f = p["b_e1"].shape
    b1 = p["b_e1"].reshape(n_exp, two_f // GLU_GROUP, LANES, 2).transpose(0, 1, 3, 2).reshape(n_exp, 1, two_f)
    pe = {
        "w_router": p["w_router"], "b_router": p["b_router"], "layer": l,
        "w1_all": p["w_e1_all"], "b1": b1, "w2_all": p["w_e2_all"], "b2": p["b_e2"][:, None, :],
    }
    tm = 512
    assert rows_c % tm == 0 and n_lat % tm == 0
    split = rows_c // tm
    mod_blocks = jnp.concatenate([jnp.tile(mod_c[5][:1], (split, 1)), jnp.repeat(mod[5], n_lat // tm, axis=0)], axis=0)
    outs, next_pre = moe_ffn(h_all, resid, mod_blocks[:, None, :], tm, split, pe, next_setup)
    if ctx_out:
        xc = outs[0].reshape(b, n_ctx, d)
    x = outs[-1].reshape(b, n_lat, d)
    return x, xc, next_pre


_PARAM_NAMES = ("w_mod", "b_mod", "norm1_g", "norm2_g", "w_in", "hy_conv_w", "hy_conv_b", "hy_w1", "hy_b1",
                "hy_freq1", "hy_w2", "hy_b2", "hy_freq2", "hy_w3", "hy_b3", "hy_bias", "q_norm_g", "k_norm_g",
                "lam_q", "lam_k", "subln_g", "w_f", "w_h", "w_a", "w_o", "w_router", "b_router",
                "w_e1", "b_e1", "w_e2", "b_e2")


def kernel(x, c, ctx, c_ctx, w_mod, b_mod, norm1_g, norm2_g, w_in, hy_conv_w, hy_conv_b, hy_w1, hy_b1, hy_freq1,
           hy_w2, hy_b2, hy_freq2, hy_w3, hy_b3, hy_bias, q_norm_g, k_norm_g, lam_q, lam_k, subln_g, w_f, w_h,
           w_a, w_o, w_router, b_router, w_e1, b_e1, w_e2, b_e2):
    stacked = (w_mod, b_mod, norm1_g, norm2_g, w_in, hy_conv_w, hy_conv_b, hy_w1, hy_b1, hy_freq1, hy_w2, hy_b2,
               hy_freq2, hy_w3, hy_b3, hy_bias, q_norm_g, k_norm_g, lam_q, lam_k, subln_g, w_f, w_h, w_a, w_o,
               w_router, b_router, w_e1, b_e1, w_e2, b_e2)
    depth = w_mod.shape[0]
    n_lat, n_ctx = x.shape[1], ctx.shape[1]

    def layer_params(l):
        p = {name: arr[l] for name, arr in zip(_PARAM_NAMES, stacked) if name not in ("w_e1", "w_e2")}
        p["w_e1_all"], p["w_e2_all"] = w_e1, w_e2
        return p

    def setup(l):
        seqs = (n_lat, n_ctx) if l < depth - 1 else (n_lat,)
        return _layer_setup(layer_params(l), c, c_ctx, seqs)

    xc = ctx
    pre = setup(0)
    for l in range(depth):
        next_setup = functools.partial(setup, l + 1) if l + 1 < depth else None
        x, xc, pre = _layer(l, x, xc, layer_params(l), l < depth - 1, pre, next_setup)
    return x
```

```python
import functools
import math

import numpy as np
import jax
import jax.numpy as jnp
from jax import lax
from jax.experimental import pallas as pl
from jax.experimental.pallas import tpu as pltpu

F32 = jnp.float32
BF16 = jnp.bfloat16

LANES = 128
VMEM_LIMIT = 56 * 1024 * 1024

GRID_W = 64
EPS = 1e-6
SUBLN_EPS = 1e-5
FN_GROUPS = 4
FN_GROUP_DIM = 64
FN_WIDTH = FN_GROUPS * FN_GROUP_DIM
HY_WIDTH = 256
HY_ORDER = 2
HY_SHORT = 3
HY_EMB_BANDS = 16
HY_DECAY_TARGET = 1e-2
HY_FAST_DECAY = 0.3
HY_SLOW_DECAY = 1.5
DA_HEADS = 4
DA_QK_DIM = 64
DA_V_DIM = 2 * DA_QK_DIM
DA_WIDTH = DA_HEADS * DA_V_DIM
ROPE_BASE = 10000.0
N_BRANCHES = 3
COL_QK = DA_HEADS * 2 * DA_QK_DIM
N_EXPERTS = 32
TOP_K = 4
SWIGLU_ALPHA = 1.702
SWIGLU_LIMIT = 7.0
MOE_BLOCK = 512
PROJ_GATE_CHUNK = 1024
BF16_ROWS = 16
DFT_MIN_N1 = BF16_ROWS
DFT_UNROLL = 16
SPECTRUM_SLABS = 16
ROW_TILE = 512
FILTER_ROWS = 1024
MM_TM, MM_TN = 512, 1024


def _dft_n2(seq):
    return min(LANES, seq // DFT_MIN_N1)


def _cparams(sem):
    return pltpu.CompilerParams(dimension_semantics=sem, vmem_limit_bytes=VMEM_LIMIT)


def _tile(n, pref):
    if n <= pref:
        return n
    for t in range(pref, 7, -1):
        if n % t == 0 and t % 8 == 0:
            return t
    return n


def _const_spec(shape):
    nd = len(shape)
    return pl.BlockSpec(shape, lambda *_: (0,) * nd, pipeline_mode=pl.Buffered(1))


def _qk_epilogue(acc, gm, gain, cos, sin):
    ms = jnp.dot((acc * acc).astype(BF16), gm, preferred_element_type=F32)
    y = acc * lax.rsqrt(ms + EPS) * gain
    n = y.shape[1]
    reps = n // LANES
    lane = lax.broadcasted_iota(jnp.int32, y.shape, 1)
    is_a = (lane % (DA_QK_DIM // 2)) < (DA_QK_DIM // 4)
    half = DA_QK_DIM // 4
    swapped = jnp.where(is_a, pltpu.roll(y, n - half, axis=1), pltpu.roll(y, half, axis=1))
    return y * jnp.tile(cos, (1, reps)) + swapped * jnp.tile(sin, (1, reps))


def _proj_kernel(x_ref, ng_ref, sh_ref, sc_ref, w_ref, gm_ref, qg_ref, kg_ref, cos_ref, sin_ref,
                 vf_ref, z_ref, q_ref, k_ref, v_ref, g_ref, *, cols):
    x = x_ref[0]
    y = x * lax.rsqrt(jnp.mean(x * x, axis=-1, keepdims=True) + EPS) * ng_ref[...]
    h = (y * (1.0 + sc_ref[0]) + sh_ref[0]).astype(BF16)

    def mm(name):
        a, b = cols[name]
        return jnp.dot(h, w_ref[:, a:b], preferred_element_type=F32)

    vf_ref[0] = mm("f")
    z_ref[0] = mm("hy")
    q_ref[0] = _qk_epilogue(mm("q"), gm_ref[...], qg_ref[...], cos_ref[...], sin_ref[...]).astype(q_ref.dtype)
    k_ref[0] = _qk_epilogue(mm("k"), gm_ref[...], kg_ref[...], cos_ref[...], sin_ref[...]).astype(k_ref.dtype)
    vv = mm("v").astype(v_ref.dtype)
    ones = jnp.ones((vv.shape[0], DA_V_DIM), v_ref.dtype)
    for hd in range(DA_HEADS):
        v_ref[0, :, 2 * hd * DA_V_DIM:(2 * hd + 1) * DA_V_DIM] = vv[:, hd * DA_V_DIM:(hd + 1) * DA_V_DIM]
        v_ref[0, :, (2 * hd + 1) * DA_V_DIM:(2 * hd + 2) * DA_V_DIM] = ones
    a, b = cols["g"]
    for c0 in range(a, b, PROJ_GATE_CHUNK):
        acc = jnp.dot(h, w_ref[:, c0:c0 + PROJ_GATE_CHUNK], preferred_element_type=F32)
        g_ref[0, :, c0 - a:c0 - a + PROJ_GATE_CHUNK] = jax.nn.sigmoid(acc).astype(g_ref.dtype)


def project_all(x, norm_g, shift, scale, w_cat, cols, q_gain, k_gain, cos, sin):
    b, s, d = x.shape
    tm = _tile(s, ROW_TILE)
    width = {name: stop - start for name, (start, stop) in cols.items()}
    width["v"] *= 2
    gm = _group_mean_matrix(COL_QK, DA_QK_DIM)
    tile_gain = lambda g: jnp.tile(g.astype(F32), COL_QK // DA_QK_DIM).reshape(1, COL_QK)
    rows = lambda w: pl.BlockSpec((1, tm, w), lambda i, j: (i, j, 0))
    per_b = pl.BlockSpec((1, 1, d), lambda i, j: (i, 0, 0))
    full = lambda shape: pl.BlockSpec(shape, lambda i, j: (0, 0))
    outs = [("f", F32), ("hy", F32), ("q", BF16), ("k", BF16), ("v", BF16), ("g", BF16)]
    return pl.pallas_call(
        functools.partial(_proj_kernel, cols=cols),
        grid=(b, s // tm),
        in_specs=[rows(d), full((1, d)), per_b, per_b, _const_spec(w_cat.shape), full(gm.shape),
                  full((1, COL_QK)), full((1, COL_QK)),
                  pl.BlockSpec((tm, LANES), lambda i, j: (j, 0)), pl.BlockSpec((tm, LANES), lambda i, j: (j, 0))],
        out_specs=[rows(width[name]) for name, _ in outs],
        out_shape=[jax.ShapeDtypeStruct((b, s, width[name]), dt) for name, dt in outs],
        compiler_params=_cparams(("parallel", "parallel")),
        name="project_all",
    )(x, norm_g.reshape(1, d).astype(F32), shift.reshape(b, 1, d), scale.reshape(b, 1, d), w_cat, gm,
      tile_gain(q_gain), tile_gain(k_gain), cos, sin)


def _mm_kernel(a_ref, w_ref, *rest, epi):
    acc = jnp.dot(a_ref[...], w_ref[...], preferred_element_type=F32)
    if epi == "plain":
        (o_ref,) = rest
    elif epi == "bias":
        b_ref, o_ref = rest
        acc = acc + b_ref[...]
    else:
        raise ValueError(epi)
    o_ref[...] = acc.astype(o_ref.dtype)


def matmul(a, w, *, out_dtype=F32, epi="plain", extra=(), extra_specs=(), tm=MM_TM, tn=MM_TN, name="mm"):
    m, k = a.shape
    k2, n = w.shape
    assert k == k2
    tm = _tile(m, tm)
    tn = _tile(n, tn)
    return pl.pallas_call(
        functools.partial(_mm_kernel, epi=epi),
        grid=(m // tm, n // tn),
        in_specs=[
            pl.BlockSpec((tm, k), lambda i, j: (i, 0)),
            pl.BlockSpec((k, tn), lambda i, j: (0, j)),
            *extra_specs,
        ],
        out_specs=pl.BlockSpec((tm, tn), lambda i, j: (i, j)),
        out_shape=jax.ShapeDtypeStruct((m, n), out_dtype),
        compiler_params=_cparams(("parallel", "parallel")),
        name=name,
    )(a, w, *extra)


def _group_mean_matrix(n, group):
    idx = np.arange(n)
    return jnp.asarray((idx[:, None] // group == idx[None, :] // group).astype(np.float32) / group, BF16)


def rope_tables(n_lat):
    rows = n_lat // GRID_W
    row = np.repeat(np.arange(rows), GRID_W).astype(np.float64)
    col = np.tile(np.arange(GRID_W), rows).astype(np.float64)
    n_freq = DA_QK_DIM // 4
    inv = ROPE_BASE ** (-np.arange(n_freq, dtype=np.float64) / n_freq)
    ang_r = row[:, None] * inv
    ang_c = col[:, None] * inv
    cos = np.concatenate([np.cos(ang_r), np.cos(ang_r), np.cos(ang_c), np.cos(ang_c)], axis=1)
    sin = np.concatenate([-np.sin(ang_r), np.sin(ang_r), -np.sin(ang_c), np.sin(ang_c)], axis=1)
    cos = np.tile(cos, (1, LANES // DA_QK_DIM))
    sin = np.tile(sin, (1, LANES // DA_QK_DIM))
    return jnp.asarray(cos, F32), jnp.asarray(sin, F32)


ATTN_TQ = 512
ATTN_TK = 768


def _attn_kernel(lam_ref, q_ref, k_ref, v_ref, g_ref, o_ref, qs_ref, s_ref, m_ref, acc_ref,
                 *, tq, tk, nkc, nqb, out_scale):
    def stack(qi, qslot):
        q = q_ref[pl.ds(pl.multiple_of(qi * tq, tq), tq), :]
        lane = lax.broadcasted_iota(jnp.int32, q.shape, 1)
        zero = jnp.zeros_like(q)
        qs_ref[qslot, 0:tq, :] = jnp.where(lane < DA_QK_DIM, q, zero)
        qs_ref[qslot, tq:2 * tq, :] = jnp.where(lane >= DA_QK_DIM, q, zero)

    def scores(qslot, j, slot):
        kj = k_ref[j * tk:(j + 1) * tk, :]
        s_ref[slot] = lax.dot_general(qs_ref[qslot], kj, (((1,), (1,)), ((), ())), preferred_element_type=F32)

    def update(j, slot):
        s = s_ref[slot]
        vj = v_ref[j * tk:(j + 1) * tk, :]
        m_prev = m_ref[...]
        m_next = jnp.maximum(m_prev, jnp.max(s, axis=1, keepdims=True))
        p = jnp.exp2(s - jnp.tile(m_next, (1, tk // LANES)))
        alpha = jnp.exp2(m_prev - m_next)
        m_ref[...] = m_next
        acc_ref[...] = acc_ref[...] * jnp.tile(alpha, (1, 2)) + jnp.dot(p.astype(BF16), vj,
                                                                          preferred_element_type=F32)

    def reset():
        m_ref[...] = jnp.full(m_ref.shape, -jnp.inf, F32)
        acc_ref[...] = jnp.zeros(acc_ref.shape, F32)

    stack(0, 0)
    scores(0, 0, 0)
    reset()

    def q_block(qi, local):
        qslot = local % 2
        t0 = local * nkc
        for j in range(nkc):
            nslot = (t0 + j + 1) % 2
            if j + 1 < nkc:
                scores(qslot, j + 1, nslot)
            else:
                nxt = jnp.minimum(qi + 1, nqb - 1)
                stack(nxt, 1 - qslot)
                scores(1 - qslot, 0, nslot)
            update(j, (t0 + j) % 2)
        o1 = acc_ref[0:tq, 0:DA_V_DIM] / acc_ref[0:tq, DA_V_DIM:]
        o2 = acc_ref[tq:2 * tq, 0:DA_V_DIM] / acc_ref[tq:2 * tq, DA_V_DIM:]
        o = o1 - lam_ref[0, 0] * o2
        o = o * lax.rsqrt(jnp.mean(o * o, axis=-1, keepdims=True) + SUBLN_EPS)
        o_ref[pl.ds(pl.multiple_of(qi * tq, tq), tq), :] = (o * g_ref[...] * out_scale).astype(o_ref.dtype)
        reset()

    group = 2
    if nqb % group == 0 and nqb > group:
        def q_group(i, c):
            for local in range(group):
                q_block(i * group + local, local)
            return c

        lax.fori_loop(0, nqb // group, q_group, 0)
    else:
        for qi in range(nqb):
            q_block(qi, qi)


def diff_attention(q, k, v_ext, lam, subln_g, out_scale, nq, nk):
    b = q.shape[0] // nq
    tq = _tile(nq, ATTN_TQ)
    tk = next(t for t in (ATTN_TK, 256, 128) if nk % t == 0)
    nqb, nkc = nq // tq, nk // tk
    kern = functools.partial(_attn_kernel, tq=tq, tk=tk, nkc=nkc, nqb=nqb, out_scale=out_scale)
    return pl.pallas_call(
        kern,
        grid=(b, DA_HEADS),
        in_specs=[
            pl.BlockSpec(memory_space=pltpu.SMEM),
            pl.BlockSpec((nq, DA_V_DIM), lambda bi, h: (bi, h)),
            pl.BlockSpec((nk, DA_V_DIM), lambda bi, h: (bi, h)),
            pl.BlockSpec((nk, 2 * DA_V_DIM), lambda bi, h: (bi, h)),
            pl.BlockSpec((1, DA_V_DIM), lambda bi, h: (0, 0)),
        ],
        out_specs=pl.BlockSpec((nq, DA_V_DIM), lambda bi, h: (bi, h)),
        out_shape=jax.ShapeDtypeStruct((b * nq, DA_WIDTH), BF16),
        scratch_shapes=[
            pltpu.VMEM((2, 2 * tq, DA_V_DIM), BF16),
            pltpu.VMEM((2, 2 * tq, tk), F32),
            pltpu.VMEM((2 * tq, LANES), F32),
            pltpu.VMEM((2 * tq, 2 * DA_V_DIM), F32),
        ],
        compiler_params=_cparams(("parallel", "parallel")),
        name="diff_attn",
    )(lam.reshape(1, 1).astype(F32), q, k, v_ext, subln_g.reshape(1, DA_V_DIM).astype(F32))


def _dft_tables_real(seq):
    n2 = _dft_n2(seq)
    n1h = seq // n2
    n1 = 2 * n1h
    n = n1 * n2
    k1 = np.arange(n1, dtype=np.float64)[None, :, None]
    nn = (n2 * np.arange(n1h, dtype=np.float64)[None, None, :] + np.arange(n2, dtype=np.float64)[:, None, None])
    ang = 2.0 * np.pi * k1 * nn / n
    e_fwd = np.concatenate([np.cos(ang), -np.sin(ang)], axis=1)
    e_inv = np.transpose(e_fwd, (0, 2, 1))
    a2 = 2.0 * np.pi * np.outer(np.arange(n2), np.arange(n2)) / n2
    c, s = np.cos(a2), np.sin(a2)
    f_fwd = np.block([[c, s], [-s, c]])
    f_inv = np.block([[c, -s], [s, c]])
    f_spec = np.block([[c, s, c, s], [-s, c, s, -c]])
    return tuple(jnp.asarray(t, BF16) for t in (e_fwd, f_fwd, f_inv, e_inv, f_spec))


def _pack_complex(re, im):
    r = lax.bitcast_convert_type(re.astype(BF16).astype(F32), jnp.uint32)
    i = lax.bitcast_convert_type(im.astype(BF16).astype(F32), jnp.uint32)
    return r | (i >> 16)


def _unpack_complex(w):
    re = lax.bitcast_convert_type(w & jnp.uint32(0xFFFF0000), F32)
    im = lax.bitcast_convert_type(w << 16, F32)
    return jnp.concatenate([re, im], axis=0).astype(BF16)


def _spectrum_kernel(hf_ref, hb_ref, sc_ref, ef_ref, fs_ref, o_ref, scr_f, scr_b, *, n1, n1h, n2, kc):
    kk = pl.program_id(2)

    @pl.when(kk == 0)
    def _():
        def stage1(j, c):
            xf = hf_ref[pl.ds(j, n1h, stride=n2), :]
            xb = hb_ref[pl.ds(j, n1h, stride=n2), :]
            row = lax.broadcasted_iota(jnp.int32, xb.shape, 0)
            xb = jnp.where((row == 0) & (j == 0), 0.0, xb)
            af = jnp.dot(ef_ref[j], xf.astype(BF16), preferred_element_type=F32)
            ab = jnp.dot(ef_ref[j], xb.astype(BF16), preferred_element_type=F32)
            r0 = pl.multiple_of(j * n1, n1)
            scr_f[pl.ds(r0, n1), :] = _pack_complex(af[:n1], af[n1:])
            scr_b[pl.ds(r0, n1), :] = _pack_complex(ab[:n1], ab[n1:])
            return c

        lax.fori_loop(0, n2, stage1, 0, unroll=DFT_UNROLL)

    def stage2(t, c):
        k1 = kk * kc + t
        a = jnp.concatenate([_unpack_complex(scr_f[pl.ds(k1, n2, stride=n1), :]),
                             _unpack_complex(scr_b[pl.ds(k1, n2, stride=n1), :])], axis=0)
        o_ref[t] = (jnp.dot(fs_ref[...], a, preferred_element_type=F32) * sc_ref[...]).astype(o_ref.dtype)
        return c

    lax.fori_loop(0, kc, stage2, 0, unroll=DFT_UNROLL)


def filter_spectra(filt, scale, tables):
    seq = filt.shape[0]
    n_order, _, ch = scale.shape
    n2 = _dft_n2(seq)
    n1h = seq // n2
    n1 = 2 * n1h
    e_fwd, f_spec = tables[0], tables[4]
    kc = min(n1, SPECTRUM_SLABS)
    cb = ch // LANES
    kern = functools.partial(_spectrum_kernel, n1=n1, n1h=n1h, n2=n2, kc=kc)
    return pl.pallas_call(
        kern,
        grid=(n_order, cb, n1 // kc),
        in_specs=[
            pl.BlockSpec((seq, LANES), lambda o, c, k: (0, (2 * o) * cb + c)),
            pl.BlockSpec((seq, LANES), lambda o, c, k: (0, (2 * o + 1) * cb + c)),
            pl.BlockSpec((None, 1, LANES), lambda o, c, k: (o, 0, c)),
            _const_spec(e_fwd.shape),
            _const_spec(f_spec.shape),
        ],
        out_specs=pl.BlockSpec((None, kc, 2 * n2, LANES), lambda o, c, k: (o, k, 0, c)),
        out_shape=jax.ShapeDtypeStruct((n_order, n1, 2 * n2, ch), BF16),
        scratch_shapes=[pltpu.VMEM((n1 * n2, LANES), jnp.uint32), pltpu.VMEM((n1 * n2, LANES), jnp.uint32)],
        compiler_params=_cparams(("parallel", "parallel", "arbitrary")),
        name="filter_spectra",
    )(filt, filt, scale, e_fwd, f_spec)


def _longconv_kernel(u_ref, g_ref, h_ref, bias_ref, ef_ref, ff_ref, fi_ref, ei_ref, o_ref, scr_a, scr_b,
                     *, n1, n1h, n2):
    def stage1(j, c):
        x = u_ref[pl.ds(j, n1h, stride=n2), :].astype(BF16)
        a = jnp.dot(ef_ref[j], x, preferred_element_type=F32)
        scr_a[pl.ds(pl.multiple_of(j * n1, n1), n1), :] = _pack_complex(a[:n1], a[n1:])
        return c

    lax.fori_loop(0, n2, stage1, 0, unroll=DFT_UNROLL)

    def stage2(k1, c):
        a = _unpack_complex(scr_a[pl.ds(k1, n2, stride=n1), :])
        y = jnp.dot(ff_ref[...], a, preferred_element_type=F32)
        hk = h_ref[k1].astype(F32)
        yr, yi = y[:n2], y[n2:]
        hr, hi = hk[:n2], hk[n2:]
        z = jnp.concatenate([yr * hr - yi * hi, yr * hi + yi * hr], axis=0).astype(BF16)
        bk = jnp.dot(fi_ref[...], z, preferred_element_type=F32)
        scr_b[pl.ds(pl.multiple_of(k1 * n2, n2), n2), :] = _pack_complex(bk[:n2], bk[n2:])
        return c

    lax.fori_loop(0, n1, stage2, 0, unroll=DFT_UNROLL)

    def stage3(j, c):
        bmat = _unpack_complex(scr_b[pl.ds(j, n1, stride=n2), :])
        y = jnp.dot(ei_ref[j], bmat, preferred_element_type=F32)
        u = u_ref[pl.ds(j, n1h, stride=n2), :]
        g = g_ref[pl.ds(j, n1h, stride=n2), :]
        o_ref[pl.ds(j, n1h, stride=n2), :] = g * (y + u * bias_ref[...])
        return c

    lax.fori_loop(0, n2, stage3, 0, unroll=DFT_UNROLL)


def long_conv_gated(u, u_blk, g, g_blk, spec, bias, tables):
    b, seq, _ = u.shape
    ch = bias.shape[0]
    n2 = _dft_n2(seq)
    n1h = seq // n2
    n1 = 2 * n1h
    e_fwd, f_fwd, f_inv, e_inv = tables[:4]
    kern = functools.partial(_longconv_kernel, n1=n1, n1h=n1h, n2=n2)
    one = pl.Buffered(1)
    return pl.pallas_call(
        kern,
        grid=(ch // LANES, b),
        in_specs=[
            pl.BlockSpec((None, seq, LANES), lambda c, i: (i, 0, u_blk + c), pipeline_mode=one),
            pl.BlockSpec((None, seq, LANES), lambda c, i: (i, 0, g_blk + c), pipeline_mode=one),
            pl.BlockSpec((n1, 2 * n2, LANES), lambda c, i: (0, 0, c), pipeline_mode=one),
            pl.BlockSpec((1, LANES), lambda c, i: (0, c)),
            _const_spec(e_fwd.shape),
            _const_spec(f_fwd.shape),
            _const_spec(f_inv.shape),
            _const_spec(e_inv.shape),
        ],
        out_specs=pl.BlockSpec((None, seq, LANES), lambda c, i: (i, 0, c)),
        out_shape=jax.ShapeDtypeStruct((b, seq, ch), F32),
        scratch_shapes=[pltpu.VMEM((n1 * n2, LANES), jnp.uint32), pltpu.VMEM((n1 * n2, LANES), jnp.uint32)],
        compiler_params=_cparams(("parallel", "parallel")),
        name="long_conv",
    )(u, g, spec, bias.reshape(1, ch).astype(F32), e_fwd, f_fwd, f_inv, e_inv)


def _dft_tables_complex(seq):
    n2 = _dft_n2(seq)
    n1 = seq // n2
    k1 = np.arange(n1, dtype=np.float64)[None, :, None]
    nn = (n2 * np.arange(n1, dtype=np.float64)[None, None, :] + np.arange(n2, dtype=np.float64)[:, None, None])
    ang = 2.0 * np.pi * k1 * nn / seq
    c, s = np.cos(ang), np.sin(ang)
    e_fwd = np.concatenate([np.concatenate([c, s], axis=2), np.concatenate([-s, c], axis=2)], axis=1)
    a2 = 2.0 * np.pi * np.outer(np.arange(n2), np.arange(n2)) / n2
    f_re = np.concatenate([np.cos(a2), np.sin(a2)], axis=1)
    return jnp.asarray(e_fwd, BF16), jnp.asarray(f_re, BF16)


def _seqdft_kernel(vr_ref, vi_ref, ef_ref, fr_ref, o_ref, scr, *, n1, n2, scale):
    def stage1(j, c):
        x = jnp.concatenate([vr_ref[pl.ds(j, n1, stride=n2), :], vi_ref[pl.ds(j, n1, stride=n2), :]], axis=0)
        a = jnp.dot(ef_ref[j], x.astype(BF16), preferred_element_type=F32)
        scr[pl.ds(pl.multiple_of(j * n1, n1), n1), :] = _pack_complex(a[:n1], a[n1:])
        return c

    lax.fori_loop(0, n2, stage1, 0, unroll=DFT_UNROLL)

    def stage2(k1, c):
        a = _unpack_complex(scr[pl.ds(k1, n2, stride=n1), :])
        o_ref[pl.ds(k1, n2, stride=n1), :] = jnp.dot(fr_ref[...], a, preferred_element_type=F32) * scale
        return c

    lax.fori_loop(0, n1, stage2, 0, unroll=DFT_UNROLL)


def seq_dft_real(v, ch, scale, tables):
    b, seq, _ = v.shape
    n2 = _dft_n2(seq)
    n1 = seq // n2
    e_fwd, f_re = tables
    nblk = ch // LANES
    kern = functools.partial(_seqdft_kernel, n1=n1, n2=n2, scale=scale)
    return pl.pallas_call(
        kern,
        grid=(nblk, b),
        in_specs=[
            pl.BlockSpec((None, seq, LANES), lambda c, i: (i, 0, c)),
            pl.BlockSpec((None, seq, LANES), lambda c, i: (i, 0, nblk + c)),
            _const_spec(e_fwd.shape),
            _const_spec(f_re.shape),
        ],
        out_specs=pl.BlockSpec((None, seq, LANES), lambda c, i: (i, 0, c)),
        out_shape=jax.ShapeDtypeStruct((b, seq, ch), F32),
        scratch_shapes=[pltpu.VMEM((n1 * n2, LANES), jnp.uint32)],
        compiler_params=_cparams(("parallel", "parallel")),
        name="seq_dft",
    )(v, v, e_fwd, f_re)


def fourier_channel_matrix():
    a = 2.0 * np.pi * np.outer(np.arange(FN_GROUP_DIM), np.arange(FN_GROUP_DIM)) / FN_GROUP_DIM
    eye = np.eye(FN_GROUPS)
    return jnp.asarray(np.concatenate([np.kron(eye, np.cos(a)), -np.kron(eye, np.sin(a))], axis=1), BF16)


def _shortconv_kernel(u_ref, w_ref, b_ref, o_ref):
    u = u_ref[...]
    n = u.shape[0]
    row = lax.broadcasted_iota(jnp.int32, u.shape, 0)
    prev = jnp.where(row == 0, 0.0, pltpu.roll(u, 1, axis=0))
    nxt = jnp.where(row == n - 1, 0.0, pltpu.roll(u, n - 1, axis=0))
    o_ref[...] = prev * w_ref[0:1, :] + u * w_ref[1:2, :] + nxt * w_ref[2:3, :] + b_ref[...]


def short_conv(u, w, bias):
    b, seq, ch = u.shape
    return pl.pallas_call(
        _shortconv_kernel,
        grid=(b, ch // LANES),
        in_specs=[
            pl.BlockSpec((None, seq, LANES), lambda i, c: (i, 0, c)),
            pl.BlockSpec((HY_SHORT, LANES), lambda i, c: (0, c)),
            pl.BlockSpec((1, LANES), lambda i, c: (0, c)),
        ],
        out_specs=pl.BlockSpec((None, seq, LANES), lambda i, c: (i, 0, c)),
        out_shape=jax.ShapeDtypeStruct((b, seq, ch), F32),
        compiler_params=_cparams(("parallel", "parallel")),
        name="short_conv",
    )(u, w.astype(F32), bias.reshape(1, ch).astype(F32))


def _filter_kernel(emb_ref, w1_ref, b1_ref, f1_ref, w2_ref, b2_ref, f2_ref, w3_ref, b3_ref, dec_ref, o_ref, mag_ref):
    z = jnp.dot(emb_ref[...].astype(BF16), w1_ref[...], preferred_element_type=F32) + b1_ref[...]
    z = jnp.sin(f1_ref[...] * z)
    z = jnp.dot(z.astype(BF16), w2_ref[...], preferred_element_type=F32) + b2_ref[...]
    z = jnp.sin(f2_ref[...] * z)
    h = jnp.dot(z.astype(BF16), w3_ref[...], preferred_element_type=F32) + b3_ref[...]
    filt = h * dec_ref[...]
    o_ref[...] = filt
    mag_ref[0] = jnp.sum(jnp.abs(filt), axis=0, keepdims=True)


def hyena_filters(seq, hy_w1, hy_b1, hy_freq1, hy_w2, hy_b2, hy_freq2, hy_w3, hy_b3):
    t = jnp.linspace(0.0, 1.0, seq, dtype=F32)[:, None]
    ang = (2.0 * math.pi / seq) * jnp.arange(seq, dtype=F32)[:, None]
    bands = jnp.linspace(1e-4, HY_EMB_BANDS - 1, HY_EMB_BANDS, dtype=F32)[None, :]
    emb = jnp.concatenate([t, jnp.cos(bands * ang), -jnp.sin(bands * ang)], axis=-1)
    kdim = emb.shape[1]
    kpad = LANES - kdim
    emb = jnp.pad(emb, ((0, 0), (0, kpad)))
    w1 = jnp.pad(hy_w1, ((0, kpad), (0, 0))).astype(BF16)
    deltas = jnp.abs(jnp.linspace(math.log(HY_DECAY_TARGET) / HY_SLOW_DECAY,
                                  math.log(HY_DECAY_TARGET) / HY_FAST_DECAY, HY_WIDTH, dtype=F32))
    decay = jnp.tile(jnp.exp(-t * deltas), (1, 2 * HY_ORDER))
    fo = hy_w1.shape[1]
    nout = hy_w3.shape[1]
    tl = _tile(seq, FILTER_ROWS)
    row = lambda a: a.reshape(1, -1).astype(F32)
    full = lambda shape: pl.BlockSpec(shape, lambda i: (0, 0))
    return pl.pallas_call(
        _filter_kernel,
        grid=(seq // tl,),
        in_specs=[
            pl.BlockSpec((tl, LANES), lambda i: (i, 0)),
            full((LANES, fo)), full((1, fo)), full((1, fo)),
            full((fo, fo)), full((1, fo)), full((1, fo)),
            full((fo, nout)), full((1, nout)),
            pl.BlockSpec((tl, nout), lambda i: (i, 0)),
        ],
        out_specs=[pl.BlockSpec((tl, nout), lambda i: (i, 0)), pl.BlockSpec((1, 1, nout), lambda i: (i, 0, 0))],
        out_shape=[jax.ShapeDtypeStruct((seq, nout), F32), jax.ShapeDtypeStruct((seq // tl, 1, nout), F32)],
        compiler_params=_cparams(("parallel",)),
        name="hyena_filter",
    )(emb, w1, row(hy_b1), row(hy_freq1), hy_w2.astype(BF16), row(hy_b2), row(hy_freq2),
      hy_w3.astype(BF16), row(hy_b3), decay)


def hyena_spectra(seq, filt, mag_blocks, tables):
    mag = jnp.sum(mag_blocks, axis=(0, 1)).reshape(HY_ORDER, 2, HY_WIDTH)
    lag0_bwd = jnp.abs(filt[0]).reshape(HY_ORDER, 2, HY_WIDTH)[:, 1]
    norm = mag[:, 0] + mag[:, 1] - lag0_bwd
    scale = (1.0 / (2 * seq)) / norm
    return filter_spectra(filt, scale[:, None, :], tables)


def _merge_kernel(x_ref, yf_ref, yh_ref, ya_ref, g_ref, wf_ref, wh_ref, wa_ref, wo_ref,
                  gate_ref, ng_ref, sh_ref, sc_ref, *rest):
    xo_ref, ho_ref = rest[-2:]
    d = x_ref.shape[-1]
    g = g_ref[0].astype(F32)
    yf = jnp.dot(yf_ref[0].astype(BF16), wf_ref[...], preferred_element_type=F32)
    yh = jnp.dot(yh_ref[0].astype(BF16), wh_ref[...], preferred_element_type=F32)
    ya = jnp.dot(ya_ref[0], wa_ref[...], preferred_element_type=F32)
    mix = g[:, 0:d] * yf + g[:, d:2 * d] * yh + g[:, 2 * d:3 * d] * ya
    x = x_ref[0] + gate_ref[0] * jnp.dot(mix.astype(BF16), wo_ref[...], preferred_element_type=F32)
    xo_ref[...] = x
    y = x * lax.rsqrt(jnp.mean(x * x, axis=-1, keepdims=True) + EPS) * ng_ref[...]
    ho_ref[...] = (y * (1.0 + sc_ref[0]) + sh_ref[0]).astype(ho_ref.dtype)


def merge_branches(x, yf, yh, ya, g, w_f, w_h, w_a, w_o, gate, norm_g, shift, scale, total_rows, row_offset, prior):
    b, l, d = x.shape
    tl = _tile(l, ROW_TILE)
    assert row_offset % tl == 0
    rows = lambda w: pl.BlockSpec((1, tl, w), lambda i, j: (i, j, 0))
    full = lambda a: pl.BlockSpec(a.shape, lambda i, j: (0, 0))
    per_b = pl.BlockSpec((1, 1, d), lambda i, j: (i, 0, 0))
    out_rows = pl.BlockSpec((tl, d), lambda i, j: (row_offset // tl + i * (l // tl) + j, 0))
    wf, wh, wa, wo = (w.astype(BF16) for w in (w_f, w_h, w_a, w_o))
    in_specs = [rows(d), rows(yf.shape[-1]), rows(yh.shape[-1]), rows(ya.shape[-1]), rows(3 * d),
                full(wf), full(wh), full(wa), full(wo),
                per_b, pl.BlockSpec((1, d), lambda i, j: (0, 0)), per_b, per_b]
    args = [x, yf, yh, ya, g, wf, wh, wa, wo, gate.reshape(b, 1, d), norm_g.reshape(1, d).astype(F32),
            shift.reshape(b, 1, d), scale.reshape(b, 1, d)]
    aliases = {}
    if prior is not None:
        aliases = {len(args): 0, len(args) + 1: 1}
        in_specs += [pl.BlockSpec(memory_space=pl.ANY), pl.BlockSpec(memory_space=pl.ANY)]
        args += list(prior)
    return pl.pallas_call(
        _merge_kernel,
        grid=(b, l // tl),
        in_specs=in_specs,
        out_specs=[out_rows, out_rows],
        out_shape=[jax.ShapeDtypeStruct((total_rows, d), F32), jax.ShapeDtypeStruct((total_rows, d), BF16)],
        input_output_aliases=aliases,
        compiler_params=_cparams(("parallel", "parallel")),
        name="merge",
    )(*args)


GLU_GROUP = 2 * LANES


def _glu_group_permutation():
    p = np.zeros((GLU_GROUP, GLU_GROUP), np.float32)
    j = np.arange(LANES)
    p[2 * j, j] = 1.0
    p[2 * j + 1, LANES + j] = 1.0
    return jnp.asarray(p, BF16)


def _moe_kernel(be_ref, act_ref, new_ref, rows_ref, w1_ref, b1_ref, w2_ref, b2_ref, p_ref, o_ref, w1s, w2s):
    i = pl.program_id(0)

    @pl.when(new_ref[i] > 0)
    def _():
        for q in range(w1s.shape[1] // GLU_GROUP):
            cols = slice(q * GLU_GROUP, (q + 1) * GLU_GROUP)
            w1s[:, cols] = jnp.dot(w1_ref[:, cols].astype(BF16), p_ref[...],
                                   preferred_element_type=F32).astype(BF16)
        w2s[...] = w2_ref[...].astype(BF16)

    @pl.when(act_ref[i] > 0)
    def _():
        u = jnp.dot(rows_ref[...], w1s[...], preferred_element_type=F32) + b1_ref[0]
        parts = []
        for q in range(u.shape[1] // GLU_GROUP):
            xg = jnp.minimum(u[:, q * GLU_GROUP:q * GLU_GROUP + LANES], SWIGLU_LIMIT)
            xl = jnp.clip(u[:, q * GLU_GROUP + LANES:(q + 1) * GLU_GROUP], -SWIGLU_LIMIT, SWIGLU_LIMIT)
            parts.append((xg * jax.nn.sigmoid(SWIGLU_ALPHA * xg) * (xl + 1.0)).astype(BF16))
        a = jnp.concatenate(parts, axis=1)
        y = jnp.dot(a, w2s[...], preferred_element_type=F32) + b2_ref[0]
        o_ref[...] = y.astype(o_ref.dtype)

    @pl.when(act_ref[i] == 0)
    def _():
        o_ref[...] = jnp.zeros(o_ref.shape, o_ref.dtype)


def moe_experts(rows, blk_exp, blk_act, blk_new, layer, w1_all, b1, w2_all, b2):
    r, d = rows.shape
    de = w2_all.shape[2]
    nblk = r // MOE_BLOCK
    grid_spec = pltpu.PrefetchScalarGridSpec(
        num_scalar_prefetch=3,
        grid=(nblk,),
        in_specs=[
            pl.BlockSpec((MOE_BLOCK, d), lambda i, be, act, new: (i, 0)),
            pl.BlockSpec((None, None, d, 2 * de), lambda i, be, act, new: (layer, be[i], 0, 0)),
            pl.BlockSpec((1, 1, 2 * de), lambda i, be, act, new: (be[i], 0, 0)),
            pl.BlockSpec((None, None, de, d), lambda i, be, act, new: (layer, be[i], 0, 0)),
            pl.BlockSpec((1, 1, d), lambda i, be, act, new: (be[i], 0, 0)),
            pl.BlockSpec((GLU_GROUP, GLU_GROUP), lambda i, be, act, new: (0, 0)),
        ],
        out_specs=pl.BlockSpec((MOE_BLOCK, d), lambda i, be, act, new: (i, 0)),
        scratch_shapes=[pltpu.VMEM((d, 2 * de), BF16), pltpu.VMEM((de, d), BF16)],
    )
    return pl.pallas_call(
        _moe_kernel,
        grid_spec=grid_spec,
        out_shape=jax.ShapeDtypeStruct((r, d), BF16),
        compiler_params=_cparams(("arbitrary",)),
        name="moe_experts",
    )(blk_exp, blk_act, blk_new, rows, w1_all, b1, w2_all, b2, _glu_group_permutation())


def _combine_kernel(y_ref, g_ref, x_ref, m_ref, *o_refs, split):
    g = g_ref[...]
    acc = g[:, 0:1] * y_ref[0].astype(F32)
    for j in range(1, TOP_K):
        acc = acc + g[:, j:j + 1] * y_ref[j].astype(F32)
    val = x_ref[...] + m_ref[0] * acc
    if split == 0:
        o_refs[0][...] = val
    else:
        i = pl.program_id(0)

        @pl.when(i < split)
        def _():
            o_refs[0][...] = val

        @pl.when(i >= split)
        def _():
            o_refs[1][...] = val


def moe_combine(y_sel, gate, resid, mod_blocks, tm, split):
    k, t, d = y_sel.shape
    nb = t // tm
    if split == 0:
        out_specs = [pl.BlockSpec((tm, d), lambda i: (i, 0))]
        out_shape = [jax.ShapeDtypeStruct((t, d), F32)]
    else:
        out_specs = [pl.BlockSpec((tm, d), lambda i: (jnp.minimum(i, split - 1), 0)),
                     pl.BlockSpec((tm, d), lambda i: (jnp.maximum(i - split, 0), 0))]
        out_shape = [jax.ShapeDtypeStruct((split * tm, d), F32), jax.ShapeDtypeStruct(((nb - split) * tm, d), F32)]
    return pl.pallas_call(
        functools.partial(_combine_kernel, split=split),
        grid=(nb,),
        in_specs=[pl.BlockSpec((k, tm, d), lambda i: (0, i, 0)),
                  pl.BlockSpec((tm, k), lambda i: (i, 0)),
                  pl.BlockSpec((tm, d), lambda i: (i, 0)),
                  pl.BlockSpec((1, 1, d), lambda i: (i, 0, 0))],
        out_specs=out_specs,
        out_shape=out_shape,
        compiler_params=_cparams(("arbitrary",)),
        name="moe_combine",
    )(y_sel, gate, resid, mod_blocks)


ROUTER_TM = 512


def _router_kernel(h_ref, w_ref, b_ref, tri_ref, e_ref, g_ref, r_ref, cnt_ref, carry):
    @pl.when(pl.program_id(0) == 0)
    def _():
        carry[...] = jnp.zeros(carry.shape, F32)

    logits = jnp.dot(h_ref[...], w_ref[...], preferred_element_type=F32) + b_ref[...]
    slot_lane = lax.broadcasted_iota(jnp.int32, logits.shape, 1)
    lane = slot_lane.astype(F32)
    rem = logits
    tops, idxs, hots = [], [], []
    for _ in range(TOP_K):
        m = jnp.max(rem, axis=1, keepdims=True)
        idx = jnp.min(jnp.where(rem == m, lane, float(LANES)), axis=1, keepdims=True)
        hot = lane == idx
        tops.append(m)
        idxs.append(idx)
        hots.append(hot)
        rem = jnp.where(hot, -jnp.inf, rem)
    ex = [jnp.exp(t - tops[0]) for t in tops]
    den = functools.reduce(lambda a, b: a + b, ex)
    sel = functools.reduce(lambda a, b: a + b, [hot.astype(F32) for hot in hots])
    before = jnp.dot(tri_ref[...], sel.astype(BF16), preferred_element_type=F32) + carry[...]
    e_out = jnp.zeros(logits.shape, F32)
    g_out = jnp.zeros(logits.shape, F32)
    r_out = jnp.zeros(logits.shape, F32)
    for j in range(TOP_K):
        slot = slot_lane == j
        e_out = jnp.where(slot, idxs[j], e_out)
        g_out = jnp.where(slot, ex[j] / den, g_out)
        r_out = jnp.where(slot, jnp.sum(jnp.where(hots[j], before, 0.0), axis=1, keepdims=True), r_out)
    e_ref[...] = e_out.astype(jnp.int32)
    g_ref[...] = g_out
    r_ref[...] = r_out.astype(jnp.int32)
    carry[...] = carry[...] + jnp.sum(sel, axis=0, keepdims=True)
    cnt_ref[...] = carry[...]


def route(h, w_router, b_router):
    t_tok, d = h.shape
    tm = _tile(t_tok, ROUTER_TM)
    wr = jnp.pad(w_router, ((0, 0), (0, LANES - N_EXPERTS))).astype(BF16)
    br = jnp.pad(b_router.astype(F32), (0, LANES - N_EXPERTS), constant_values=-jnp.inf).reshape(1, LANES)
    tri = jnp.asarray(np.tril(np.ones((tm, tm), np.float32), -1), BF16)
    full = lambda shape: pl.BlockSpec(shape, lambda i: (0, 0))
    rows = pl.BlockSpec((tm, LANES), lambda i: (i, 0))
    e, g, r, cnt = pl.pallas_call(
        _router_kernel,
        grid=(t_tok // tm,),
        in_specs=[pl.BlockSpec((tm, d), lambda i: (i, 0)), full((d, LANES)), full((1, LANES)), full((tm, tm))],
        out_specs=[rows, rows, rows, full((1, LANES))],
        out_shape=[jax.ShapeDtypeStruct((t_tok, LANES), jnp.int32), jax.ShapeDtypeStruct((t_tok, LANES), F32),
                   jax.ShapeDtypeStruct((t_tok, LANES), jnp.int32), jax.ShapeDtypeStruct((1, LANES), F32)],
        scratch_shapes=[pltpu.VMEM((1, LANES), F32)],
        compiler_params=_cparams(("arbitrary",)),
        name="router",
    )(h, wr, br, tri)
    return e[:, :TOP_K], g[:, :TOP_K], r[:, :TOP_K], cnt[0, :N_EXPERTS].astype(jnp.int32)


def moe_ffn(h, resid, mod_blocks, tm, split, p, between):
    t_tok, d = h.shape
    top_i, gate, rank, counts = route(h, p["w_router"], p["b_router"])
    n_assign = t_tok * TOP_K
    padded = (counts + MOE_BLOCK - 1) // MOE_BLOCK * MOE_BLOCK
    pad_end = jnp.cumsum(padded)
    pad_start = pad_end - padded
    experts = jnp.arange(N_EXPERTS, dtype=jnp.int32)
    dest = rank + jnp.sum(jnp.where(top_i[..., None] == experts, pad_start, 0), axis=-1)
    dest = dest.reshape(-1)
    n_blocks = -(-n_assign // MOE_BLOCK) + N_EXPERTS
    n_rows = n_blocks * MOE_BLOCK
    filled = jnp.zeros((n_rows,), jnp.int32).at[dest].add(jnp.arange(n_assign, dtype=jnp.int32) // TOP_K + 1)
    row_tok = jnp.where(filled > 0, filled - 1, jnp.arange(n_rows, dtype=jnp.int32) % t_tok)
    blk_start = jnp.arange(n_blocks, dtype=jnp.int32) * MOE_BLOCK
    blk_exp = jnp.minimum(jnp.sum((blk_start[:, None] >= pad_end[None, :]).astype(jnp.int32), axis=1),
                          N_EXPERTS - 1)
    blk_act = (blk_start < pad_end[-1]).astype(jnp.int32)
    blk_new = jnp.concatenate([jnp.ones((1,), jnp.int32), (blk_exp[1:] != blk_exp[:-1]).astype(jnp.int32)])
    rows = h[row_tok]
    extra = between() if between is not None else None
    y_rows = moe_experts(rows, blk_exp, blk_act, blk_new, p["layer"], p["w1_all"], p["b1"], p["w2_all"], p["b2"])
    y_sel = y_rows[dest.reshape(t_tok, TOP_K).T]
    return moe_combine(y_sel, gate, resid, mod_blocks, tm, split), extra


OFF_F = 0
OFF_HY = OFF_F + FN_WIDTH
OFF_Q = OFF_HY + (HY_ORDER + 1) * HY_WIDTH
OFF_K = OFF_Q + COL_QK
OFF_V = OFF_K + COL_QK
OFF_G = OFF_V + DA_WIDTH


def _projection_weights(w_in):
    w = w_in.astype(BF16)
    w_fv = matmul(w[:, OFF_F:OFF_HY], fourier_channel_matrix(), out_dtype=BF16, name="mm_wfold")
    w_cat = jnp.concatenate([w_fv, w[:, OFF_HY:]], axis=1)
    shift = w_fv.shape[1] - (OFF_HY - OFF_F)
    bounds = {"hy": (OFF_HY, OFF_Q), "q": (OFF_Q, OFF_K), "k": (OFF_K, OFF_V), "v": (OFF_V, OFF_G),
              "g": (OFF_G, w_in.shape[1])}
    cols = {"f": (0, w_fv.shape[1])}
    cols.update({name: (a + shift, b + shift) for name, (a, b) in bounds.items()})
    return w_cat, cols


def _project(x, mod_shift, mod_scale, p, rope):
    q_gain = p["q_norm_g"] * (DA_QK_DIM ** -0.5 * math.log2(math.e))
    return project_all(x, p["norm1_g"], mod_shift, mod_scale, p["w_cat"], p["cols"], q_gain, p["k_norm_g"],
                       rope[0], rope[1])


def _token_mixer(x, mod_shift1, mod_scale1, mod_gate, mod_shift2, mod_scale2, p, lam, lam_init, rope, kv_extra,
                 rows_out):
    b, s, d = x.shape
    v_f, z, q, k3, v3, g = _project(x, mod_shift1, mod_scale1, p, rope)

    y_f = seq_dft_real(v_f, FN_WIDTH, 1.0 / math.sqrt(s * FN_GROUP_DIM), _dft_tables_complex(s))

    z = short_conv(z, p["hy_conv_w"], p["hy_conv_b"])
    tables = _dft_tables_real(s)
    spec = p["spectra"][s]
    cb = HY_WIDTH // LANES
    y_h = long_conv_gated(z, 0, z, cb, spec[0], p["hy_bias"][0], tables)
    y_h = long_conv_gated(y_h, 0, z, 2 * cb, spec[1], p["hy_bias"][1], tables)

    if kv_extra is not None:
        k_all = jnp.concatenate([k3, kv_extra[0]], axis=1)
        v_all = jnp.concatenate([v3, kv_extra[1]], axis=1)
    else:
        k_all, v_all = k3, v3
    nk = k_all.shape[1]
    y_a = diff_attention(q.reshape(b * s, COL_QK), k_all.reshape(b * nk, COL_QK),
                         v_all.reshape(b * nk, 2 * DA_WIDTH), lam, p["subln_g"], 1.0 - lam_init, s, nk)
    y_a = y_a.reshape(b, s, DA_WIDTH)

    total_rows, row_offset, prior = rows_out
    buffers = merge_branches(x, y_f, y_h, y_a, g, p["w_f"], p["w_h"], p["w_a"], p["w_o"],
                             mod_gate, p["norm2_g"], mod_shift2, mod_scale2, total_rows, row_offset, prior)
    return buffers, (k3, v3)


def _layer_setup(p, c, c_ctx, seqs):
    b, d = c.shape
    cond = jnp.concatenate([c, c_ctx[None, :], jnp.zeros((BF16_ROWS - b - 1, d), F32)], axis=0)
    mod_all = matmul(jax.nn.silu(cond).astype(BF16), p["w_mod"].astype(BF16), epi="bias",
                     extra=(p["b_mod"].reshape(1, 6 * d).astype(F32),),
                     extra_specs=[pl.BlockSpec((1, _tile(6 * d, MM_TN)), lambda i, j: (0, j))], name="mm_mod")
    pre = {"mod_all": mod_all, "spectra": {}}
    pre["w_cat"], pre["cols"] = _projection_weights(p["w_in"])
    for s in seqs:
        filt, mag_blocks = hyena_filters(s, p["hy_w1"], p["hy_b1"], p["hy_freq1"], p["hy_w2"], p["hy_b2"],
                                         p["hy_freq2"], p["hy_w3"], p["hy_b3"])
        pre["spectra"][s] = hyena_spectra(s, filt, mag_blocks, _dft_tables_real(s))
    return pre


def _layer(l, x, xc, p, ctx_out, pre, next_setup):
    b, n_lat, d = x.shape
    n_ctx = xc.shape[1]
    lam_init = 0.8 - 0.6 * math.exp(-0.3 * l)
    lam = (jnp.exp(jnp.sum(p["lam_q"][0] * p["lam_k"][0]).astype(F32))
           - jnp.exp(jnp.sum(p["lam_q"][1] * p["lam_k"][1]).astype(F32)) + lam_init)

    mod_all = pre["mod_all"]
    mod = [mod_all[:b, i * d:(i + 1) * d] for i in range(6)]
    mod_c = [jnp.broadcast_to(mod_all[b, i * d:(i + 1) * d], (b, d)) for i in range(6)]

    pw = dict(p)
    pw.update(pre)

    no_rope = (jnp.ones((n_ctx, LANES), F32), jnp.zeros((n_ctx, LANES), F32))
    rows_c = b * n_ctx if ctx_out else 0
    total_rows = rows_c + b * n_lat
    prior = None
    if ctx_out:
        prior, kv_c = _token_mixer(xc, mod_c[0], mod_c[1], mod_c[2], mod_c[3], mod_c[4], pw, lam, lam_init,
                                   no_rope, None, (total_rows, 0, None))
    else:
        kv_c = _project(xc, mod_c[0], mod_c[1], pw, no_rope)[3:5]
    (resid, h_all), _ = _token_mixer(x, mod[0], mod[1], mod[2], mod[3], mod[4], pw, lam, lam_init,
                                     rope_tables(n_lat), kv_c, (total_rows, rows_c, prior))

    n_exp, two_f = p["b_e1"].shape
    b1 = p["b_e1"].reshape(n_exp, two_f // GLU_GROUP, LANES, 2).transpose(0, 1, 3, 2).reshape(n_exp, 1, two_f)
    pe = {
        "w_router": p["w_router"], "b_router": p["b_router"], "layer": l,
        "w1_all": p["w_e1_all"], "b1": b1, "w2_all": p["w_e2_all"], "b2": p["b_e2"][:, None, :],
    }
    tm = ROW_TILE
    assert rows_c % tm == 0 and n_lat % tm == 0
    split = rows_c // tm
    mod_blocks = jnp.concatenate([jnp.tile(mod_c[5][:1], (split, 1)), jnp.repeat(mod[5], n_lat // tm, axis=0)], axis=0)
    outs, next_pre = moe_ffn(h_all, resid, mod_blocks[:, None, :], tm, split, pe, next_setup)
    if ctx_out:
        xc = outs[0].reshape(b, n_ctx, d)
    x = outs[-1].reshape(b, n_lat, d)
    return x, xc, next_pre


_PARAM_NAMES = ("w_mod", "b_mod", "norm1_g", "norm2_g", "w_in", "hy_conv_w", "hy_conv_b", "hy_w1", "hy_b1",
                "hy_freq1", "hy_w2", "hy_b2", "hy_freq2", "hy_w3", "hy_b3", "hy_bias", "q_norm_g", "k_norm_g",
                "lam_q", "lam_k", "subln_g", "w_f", "w_h", "w_a", "w_o", "w_router", "b_router",
                "w_e1", "b_e1", "w_e2", "b_e2")


def kernel(x, c, ctx, c_ctx, w_mod, b_mod, norm1_g, norm2_g, w_in, hy_conv_w, hy_conv_b, hy_w1, hy_b1, hy_freq1,
           hy_w2, hy_b2, hy_freq2, hy_w3, hy_b3, hy_bias, q_norm_g, k_norm_g, lam_q, lam_k, subln_g, w_f, w_h,
           w_a, w_o, w_router, b_router, w_e1, b_e1, w_e2, b_e2):
    stacked = (w_mod, b_mod, norm1_g, norm2_g, w_in, hy_conv_w, hy_conv_b, hy_w1, hy_b1, hy_freq1, hy_w2, hy_b2,
               hy_freq2, hy_w3, hy_b3, hy_bias, q_norm_g, k_norm_g, lam_q, lam_k, subln_g, w_f, w_h, w_a, w_o,
               w_router, b_router, w_e1, b_e1, w_e2, b_e2)
    depth = w_mod.shape[0]
    n_lat, n_ctx = x.shape[1], ctx.shape[1]

    def layer_params(l):
        p = {name: arr[l] for name, arr in zip(_PARAM_NAMES, stacked) if name not in ("w_e1", "w_e2")}
        p["w_e1_all"], p["w_e2_all"] = w_e1, w_e2
        return p

    def setup(l):
        seqs = (n_lat, n_ctx) if l < depth - 1 else (n_lat,)
        return _layer_setup(layer_params(l), c, c_ctx, seqs)

    xc = ctx
    pre = setup(0)
    for l in range(depth):
        next_setup = functools.partial(setup, l + 1) if l + 1 < depth else None
        x, xc, pre = _layer(l, x, xc, layer_params(l), l < depth - 1, pre, next_setup)
    return x
```

```python
import functools
import math

import numpy as np
import jax
import jax.numpy as jnp
from jax import lax
from jax.experimental import pallas as pl
from jax.experimental.pallas import tpu as pltpu

F32 = jnp.float32
BF16 = jnp.bfloat16

LANES = 128
VMEM_LIMIT = 56 * 1024 * 1024

GRID_W = 64
EPS = 1e-6
SUBLN_EPS = 1e-5
FN_GROUPS = 4
FN_GROUP_DIM = 64
FN_WIDTH = FN_GROUPS * FN_GROUP_DIM
HY_WIDTH = 256
HY_ORDER = 2
HY_SHORT = 3
HY_EMB_BANDS = 16
HY_DECAY_TARGET = 1e-2
HY_FAST_DECAY = 0.3
HY_SLOW_DECAY = 1.5
DA_HEADS = 4
DA_QK_DIM = 64
DA_V_DIM = 2 * DA_QK_DIM
DA_WIDTH = DA_HEADS * DA_V_DIM
ROPE_BASE = 10000.0
N_BRANCHES = 3
COL_QK = DA_HEADS * 2 * DA_QK_DIM
N_EXPERTS = 32
TOP_K = 4
SWIGLU_ALPHA = 1.702
SWIGLU_LIMIT = 7.0
MOE_BLOCK = 512
PROJ_GATE_CHUNK = 1024
BF16_ROWS = 16
DFT_MIN_N1 = BF16_ROWS
DFT_UNROLL = 32
SPECTRUM_SLABS = 16
ROW_TILE = 512
FILTER_ROWS = 1024
MM_TM, MM_TN = 512, 1024


def _slab_loop(n, body):
    lax.fori_loop(0, n, body, 0, unroll=min(DFT_UNROLL, n))


def _dft_n2(seq):
    return min(LANES, seq // DFT_MIN_N1)


def _cparams(sem):
    return pltpu.CompilerParams(dimension_semantics=sem, vmem_limit_bytes=VMEM_LIMIT)


def _tile(n, pref):
    if n <= pref:
        return n
    for t in range(pref, 7, -1):
        if n % t == 0 and t % 8 == 0:
            return t
    return n


def _const_spec(shape):
    nd = len(shape)
    return pl.BlockSpec(shape, lambda *_: (0,) * nd, pipeline_mode=pl.Buffered(1))


def _qk_epilogue(acc, gm, gain, cos, sin):
    ms = jnp.dot((acc * acc).astype(BF16), gm, preferred_element_type=F32)
    y = acc * lax.rsqrt(ms + EPS) * gain
    n = y.shape[1]
    reps = n // LANES
    lane = lax.broadcasted_iota(jnp.int32, y.shape, 1)
    is_a = (lane % (DA_QK_DIM // 2)) < (DA_QK_DIM // 4)
    half = DA_QK_DIM // 4
    swapped = jnp.where(is_a, pltpu.roll(y, n - half, axis=1), pltpu.roll(y, half, axis=1))
    return y * jnp.tile(cos, (1, reps)) + swapped * jnp.tile(sin, (1, reps))


def _proj_kernel(x_ref, ng_ref, sh_ref, sc_ref, w_ref, gm_ref, qg_ref, kg_ref, cos_ref, sin_ref,
                 vf_ref, z_ref, q_ref, k_ref, v_ref, g_ref, *, cols):
    x = x_ref[0]
    y = x * lax.rsqrt(jnp.mean(x * x, axis=-1, keepdims=True) + EPS) * ng_ref[...]
    h = (y * (1.0 + sc_ref[0]) + sh_ref[0]).astype(BF16)

    def mm(name):
        a, b = cols[name]
        return jnp.dot(h, w_ref[:, a:b], preferred_element_type=F32)

    vf_ref[0] = mm("f")
    z_ref[0] = mm("hy")
    q_ref[0] = _qk_epilogue(mm("q"), gm_ref[...], qg_ref[...], cos_ref[...], sin_ref[...]).astype(q_ref.dtype)
    k_ref[0] = _qk_epilogue(mm("k"), gm_ref[...], kg_ref[...], cos_ref[...], sin_ref[...]).astype(k_ref.dtype)
    vv = mm("v").astype(v_ref.dtype)
    ones = jnp.ones((vv.shape[0], DA_V_DIM), v_ref.dtype)
    for hd in range(DA_HEADS):
        v_ref[0, :, 2 * hd * DA_V_DIM:(2 * hd + 1) * DA_V_DIM] = vv[:, hd * DA_V_DIM:(hd + 1) * DA_V_DIM]
        v_ref[0, :, (2 * hd + 1) * DA_V_DIM:(2 * hd + 2) * DA_V_DIM] = ones
    a, b = cols["g"]
    for c0 in range(a, b, PROJ_GATE_CHUNK):
        acc = jnp.dot(h, w_ref[:, c0:c0 + PROJ_GATE_CHUNK], preferred_element_type=F32)
        g_ref[0, :, c0 - a:c0 - a + PROJ_GATE_CHUNK] = jax.nn.sigmoid(acc).astype(g_ref.dtype)


def project_all(x, norm_g, shift, scale, w_cat, cols, q_gain, k_gain, cos, sin):
    b, s, d = x.shape
    tm = _tile(s, ROW_TILE)
    width = {name: stop - start for name, (start, stop) in cols.items()}
    width["v"] *= 2
    gm = _group_mean_matrix(COL_QK, DA_QK_DIM)
    tile_gain = lambda g: jnp.tile(g.astype(F32), COL_QK // DA_QK_DIM).reshape(1, COL_QK)
    rows = lambda w: pl.BlockSpec((1, tm, w), lambda i, j: (i, j, 0))
    per_b = pl.BlockSpec((1, 1, d), lambda i, j: (i, 0, 0))
    full = lambda shape: pl.BlockSpec(shape, lambda i, j: (0, 0))
    outs = [("f", F32), ("hy", F32), ("q", BF16), ("k", BF16), ("v", BF16), ("g", BF16)]
    return pl.pallas_call(
        functools.partial(_proj_kernel, cols=cols),
        grid=(b, s // tm),
        in_specs=[rows(d), full((1, d)), per_b, per_b, _const_spec(w_cat.shape), full(gm.shape),
                  full((1, COL_QK)), full((1, COL_QK)),
                  pl.BlockSpec((tm, LANES), lambda i, j: (j, 0)), pl.BlockSpec((tm, LANES), lambda i, j: (j, 0))],
        out_specs=[rows(width[name]) for name, _ in outs],
        out_shape=[jax.ShapeDtypeStruct((b, s, width[name]), dt) for name, dt in outs],
        compiler_params=_cparams(("parallel", "parallel")),
        name="project_all",
    )(x, norm_g.reshape(1, d).astype(F32), shift.reshape(b, 1, d), scale.reshape(b, 1, d), w_cat, gm,
      tile_gain(q_gain), tile_gain(k_gain), cos, sin)


def _mm_kernel(a_ref, w_ref, *rest, epi):
    acc = jnp.dot(a_ref[...], w_ref[...], preferred_element_type=F32)
    if epi == "plain":
        (o_ref,) = rest
    elif epi == "bias":
        b_ref, o_ref = rest
        acc = acc + b_ref[...]
    else:
        raise ValueError(epi)
    o_ref[...] = acc.astype(o_ref.dtype)


def matmul(a, w, *, out_dtype=F32, epi="plain", extra=(), extra_specs=(), tm=MM_TM, tn=MM_TN, name="mm"):
    m, k = a.shape
    k2, n = w.shape
    assert k == k2
    tm = _tile(m, tm)
    tn = _tile(n, tn)
    return pl.pallas_call(
        functools.partial(_mm_kernel, epi=epi),
        grid=(m // tm, n // tn),
        in_specs=[
            pl.BlockSpec((tm, k), lambda i, j: (i, 0)),
            pl.BlockSpec((k, tn), lambda i, j: (0, j)),
            *extra_specs,
        ],
        out_specs=pl.BlockSpec((tm, tn), lambda i, j: (i, j)),
        out_shape=jax.ShapeDtypeStruct((m, n), out_dtype),
        compiler_params=_cparams(("parallel", "parallel")),
        name=name,
    )(a, w, *extra)


def _group_mean_matrix(n, group):
    idx = np.arange(n)
    return jnp.asarray((idx[:, None] // group == idx[None, :] // group).astype(np.float32) / group, BF16)


def rope_tables(n_lat):
    rows = n_lat // GRID_W
    row = np.repeat(np.arange(rows), GRID_W).astype(np.float64)
    col = np.tile(np.arange(GRID_W), rows).astype(np.float64)
    n_freq = DA_QK_DIM // 4
    inv = ROPE_BASE ** (-np.arange(n_freq, dtype=np.float64) / n_freq)
    ang_r = row[:, None] * inv
    ang_c = col[:, None] * inv
    cos = np.concatenate([np.cos(ang_r), np.cos(ang_r), np.cos(ang_c), np.cos(ang_c)], axis=1)
    sin = np.concatenate([-np.sin(ang_r), np.sin(ang_r), -np.sin(ang_c), np.sin(ang_c)], axis=1)
    cos = np.tile(cos, (1, LANES // DA_QK_DIM))
    sin = np.tile(sin, (1, LANES // DA_QK_DIM))
    return jnp.asarray(cos, F32), jnp.asarray(sin, F32)


ATTN_TQ = 512
ATTN_TK = 768


def _attn_kernel(lam_ref, q_ref, k_ref, v_ref, g_ref, o_ref, qs_ref, s_ref, m_ref, acc_ref,
                 *, tq, tk, nkc, nqb, out_scale):
    def stack(qi, qslot):
        q = q_ref[pl.ds(pl.multiple_of(qi * tq, tq), tq), :]
        lane = lax.broadcasted_iota(jnp.int32, q.shape, 1)
        zero = jnp.zeros_like(q)
        qs_ref[qslot, 0:tq, :] = jnp.where(lane < DA_QK_DIM, q, zero)
        qs_ref[qslot, tq:2 * tq, :] = jnp.where(lane >= DA_QK_DIM, q, zero)

    def scores(qslot, j, slot):
        kj = k_ref[j * tk:(j + 1) * tk, :]
        s_ref[slot] = lax.dot_general(qs_ref[qslot], kj, (((1,), (1,)), ((), ())), preferred_element_type=F32)

    def update(j, slot):
        s = s_ref[slot]
        vj = v_ref[j * tk:(j + 1) * tk, :]
        m_prev = m_ref[...]
        m_next = jnp.maximum(m_prev, jnp.max(s, axis=1, keepdims=True))
        p = jnp.exp2(s - jnp.tile(m_next, (1, tk // LANES)))
        alpha = jnp.exp2(m_prev - m_next)
        m_ref[...] = m_next
        acc_ref[...] = acc_ref[...] * jnp.tile(alpha, (1, 2)) + jnp.dot(p.astype(BF16), vj,
                                                                          preferred_element_type=F32)

    def reset():
        m_ref[...] = jnp.full(m_ref.shape, -jnp.inf, F32)
        acc_ref[...] = jnp.zeros(acc_ref.shape, F32)

    stack(0, 0)
    scores(0, 0, 0)
    reset()

    def q_block(qi, local):
        qslot = local % 2
        t0 = local * nkc
        for j in range(nkc):
            nslot = (t0 + j + 1) % 2
            if j + 1 < nkc:
                scores(qslot, j + 1, nslot)
            else:
                nxt = jnp.minimum(qi + 1, nqb - 1)
                stack(nxt, 1 - qslot)
                scores(1 - qslot, 0, nslot)
            update(j, (t0 + j) % 2)
        o1 = acc_ref[0:tq, 0:DA_V_DIM] / acc_ref[0:tq, DA_V_DIM:]
        o2 = acc_ref[tq:2 * tq, 0:DA_V_DIM] / acc_ref[tq:2 * tq, DA_V_DIM:]
        o = o1 - lam_ref[0, 0] * o2
        o = o * lax.rsqrt(jnp.mean(o * o, axis=-1, keepdims=True) + SUBLN_EPS)
        o_ref[pl.ds(pl.multiple_of(qi * tq, tq), tq), :] = (o * g_ref[...] * out_scale).astype(o_ref.dtype)
        reset()

    group = 2
    if nqb % group == 0 and nqb > group:
        def q_group(i, c):
            for local in range(group):
                q_block(i * group + local, local)
            return c

        lax.fori_loop(0, nqb // group, q_group, 0)
    else:
        for qi in range(nqb):
            q_block(qi, qi)


def diff_attention(q, k, v_ext, lam, subln_g, out_scale, nq, nk):
    b = q.shape[0] // nq
    tq = _tile(nq, ATTN_TQ)
    tk = next(t for t in (ATTN_TK, 256, 128) if nk % t == 0)
    nqb, nkc = nq // tq, nk // tk
    kern = functools.partial(_attn_kernel, tq=tq, tk=tk, nkc=nkc, nqb=nqb, out_scale=out_scale)
    return pl.pallas_call(
        kern,
        grid=(b, DA_HEADS),
        in_specs=[
            pl.BlockSpec(memory_space=pltpu.SMEM),
            pl.BlockSpec((nq, DA_V_DIM), lambda bi, h: (bi, h)),
            pl.BlockSpec((nk, DA_V_DIM), lambda bi, h: (bi, h)),
            pl.BlockSpec((nk, 2 * DA_V_DIM), lambda bi, h: (bi, h)),
            pl.BlockSpec((1, DA_V_DIM), lambda bi, h: (0, 0)),
        ],
        out_specs=pl.BlockSpec((nq, DA_V_DIM), lambda bi, h: (bi, h)),
        out_shape=jax.ShapeDtypeStruct((b * nq, DA_WIDTH), BF16),
        scratch_shapes=[
            pltpu.VMEM((2, 2 * tq, DA_V_DIM), BF16),
            pltpu.VMEM((2, 2 * tq, tk), F32),
            pltpu.VMEM((2 * tq, LANES), F32),
            pltpu.VMEM((2 * tq, 2 * DA_V_DIM), F32),
        ],
        compiler_params=_cparams(("parallel", "parallel")),
        name="diff_attn",
    )(lam.reshape(1, 1).astype(F32), q, k, v_ext, subln_g.reshape(1, DA_V_DIM).astype(F32))


def _dft_tables_real(seq):
    n2 = _dft_n2(seq)
    n1h = seq // n2
    n1 = 2 * n1h
    n = n1 * n2
    k1 = np.arange(n1, dtype=np.float64)[None, :, None]
    nn = (n2 * np.arange(n1h, dtype=np.float64)[None, None, :] + np.arange(n2, dtype=np.float64)[:, None, None])
    ang = 2.0 * np.pi * k1 * nn / n
    e_fwd = np.concatenate([np.cos(ang), -np.sin(ang)], axis=1)
    e_inv = np.transpose(e_fwd, (0, 2, 1))
    a2 = 2.0 * np.pi * np.outer(np.arange(n2), np.arange(n2)) / n2
    c, s = np.cos(a2), np.sin(a2)
    f_fwd = np.block([[c, s], [-s, c]])
    f_inv = np.block([[c, -s], [s, c]])
    f_spec = np.block([[c, s, c, s], [-s, c, s, -c]])
    return tuple(jnp.asarray(t, BF16) for t in (e_fwd, f_fwd, f_inv, e_inv, f_spec))


def _pack_complex(re, im):
    r = lax.bitcast_convert_type(re.astype(BF16).astype(F32), jnp.uint32)
    i = lax.bitcast_convert_type(im.astype(BF16).astype(F32), jnp.uint32)
    return r | (i >> 16)


def _unpack_complex(w):
    re = lax.bitcast_convert_type(w & jnp.uint32(0xFFFF0000), F32)
    im = lax.bitcast_convert_type(w << 16, F32)
    return jnp.concatenate([re, im], axis=0).astype(BF16)


def _spectrum_kernel(hf_ref, hb_ref, sc_ref, ef_ref, fs_ref, o_ref, scr_f, scr_b, *, n1, n1h, n2, kc):
    kk = pl.program_id(2)

    @pl.when(kk == 0)
    def _():
        def stage1(j, c):
            xf = hf_ref[pl.ds(j, n1h, stride=n2), :]
            xb = hb_ref[pl.ds(j, n1h, stride=n2), :]
            row = lax.broadcasted_iota(jnp.int32, xb.shape, 0)
            xb = jnp.where((row == 0) & (j == 0), 0.0, xb)
            af = jnp.dot(ef_ref[j], xf.astype(BF16), preferred_element_type=F32)
            ab = jnp.dot(ef_ref[j], xb.astype(BF16), preferred_element_type=F32)
            r0 = pl.multiple_of(j * n1, n1)
            scr_f[pl.ds(r0, n1), :] = _pack_complex(af[:n1], af[n1:])
            scr_b[pl.ds(r0, n1), :] = _pack_complex(ab[:n1], ab[n1:])
            return c

        _slab_loop(n2, stage1)

    def stage2(t, c):
        k1 = kk * kc + t
        a = jnp.concatenate([_unpack_complex(scr_f[pl.ds(k1, n2, stride=n1), :]),
                             _unpack_complex(scr_b[pl.ds(k1, n2, stride=n1), :])], axis=0)
        o_ref[t] = (jnp.dot(fs_ref[...], a, preferred_element_type=F32) * sc_ref[...]).astype(o_ref.dtype)
        return c

    _slab_loop(kc, stage2)


def filter_spectra(filt, scale, tables):
    seq = filt.shape[0]
    n_order, _, ch = scale.shape
    n2 = _dft_n2(seq)
    n1h = seq // n2
    n1 = 2 * n1h
    e_fwd, f_spec = tables[0], tables[4]
    kc = min(n1, SPECTRUM_SLABS)
    cb = ch // LANES
    kern = functools.partial(_spectrum_kernel, n1=n1, n1h=n1h, n2=n2, kc=kc)
    return pl.pallas_call(
        kern,
        grid=(n_order, cb, n1 // kc),
        in_specs=[
            pl.BlockSpec((seq, LANES), lambda o, c, k: (0, (2 * o) * cb + c)),
            pl.BlockSpec((seq, LANES), lambda o, c, k: (0, (2 * o + 1) * cb + c)),
            pl.BlockSpec((None, 1, LANES), lambda o, c, k: (o, 0, c)),
            _const_spec(e_fwd.shape),
            _const_spec(f_spec.shape),
        ],
        out_specs=pl.BlockSpec((None, kc, 2 * n2, LANES), lambda o, c, k: (o, k, 0, c)),
        out_shape=jax.ShapeDtypeStruct((n_order, n1, 2 * n2, ch), BF16),
        scratch_shapes=[pltpu.VMEM((n1 * n2, LANES), jnp.uint32), pltpu.VMEM((n1 * n2, LANES), jnp.uint32)],
        compiler_params=_cparams(("parallel", "parallel", "arbitrary")),
        name="filter_spectra",
    )(filt, filt, scale, e_fwd, f_spec)


def _longconv_kernel(u_ref, g_ref, h_ref, bias_ref, ef_ref, ff_ref, fi_ref, ei_ref, o_ref, scr_a, scr_b,
                     *, n1, n1h, n2):
    def stage1(j, c):
        x = u_ref[pl.ds(j, n1h, stride=n2), :].astype(BF16)
        a = jnp.dot(ef_ref[j], x, preferred_element_type=F32)
        scr_a[pl.ds(pl.multiple_of(j * n1, n1), n1), :] = _pack_complex(a[:n1], a[n1:])
        return c

    _slab_loop(n2, stage1)

    def stage2(k1, c):
        a = _unpack_complex(scr_a[pl.ds(k1, n2, stride=n1), :])
        y = jnp.dot(ff_ref[...], a, preferred_element_type=F32)
        hk = h_ref[k1].astype(F32)
        yr, yi = y[:n2], y[n2:]
        hr, hi = hk[:n2], hk[n2:]
        z = jnp.concatenate([yr * hr - yi * hi, yr * hi + yi * hr], axis=0).astype(BF16)
        bk = jnp.dot(fi_ref[...], z, preferred_element_type=F32)
        scr_b[pl.ds(pl.multiple_of(k1 * n2, n2), n2), :] = _pack_complex(bk[:n2], bk[n2:])
        return c

    _slab_loop(n1, stage2)

    def stage3(j, c):
        bmat = _unpack_complex(scr_b[pl.ds(j, n1, stride=n2), :])
        y = jnp.dot(ei_ref[j], bmat, preferred_element_type=F32)
        u = u_ref[pl.ds(j, n1h, stride=n2), :]
        g = g_ref[pl.ds(j, n1h, stride=n2), :]
        o_ref[pl.ds(j, n1h, stride=n2), :] = g * (y + u * bias_ref[...])
        return c

    _slab_loop(n2, stage3)


def long_conv_gated(u, u_blk, g, g_blk, spec, bias, tables):
    b, seq, _ = u.shape
    ch = bias.shape[0]
    n2 = _dft_n2(seq)
    n1h = seq // n2
    n1 = 2 * n1h
    e_fwd, f_fwd, f_inv, e_inv = tables[:4]
    kern = functools.partial(_longconv_kernel, n1=n1, n1h=n1h, n2=n2)
    one = pl.Buffered(1)
    return pl.pallas_call(
        kern,
        grid=(ch // LANES, b),
        in_specs=[
            pl.BlockSpec((None, seq, LANES), lambda c, i: (i, 0, u_blk + c), pipeline_mode=one),
            pl.BlockSpec((None, seq, LANES), lambda c, i: (i, 0, g_blk + c), pipeline_mode=one),
            pl.BlockSpec((n1, 2 * n2, LANES), lambda c, i: (0, 0, c), pipeline_mode=one),
            pl.BlockSpec((1, LANES), lambda c, i: (0, c)),
            _const_spec(e_fwd.shape),
            _const_spec(f_fwd.shape),
            _const_spec(f_inv.shape),
            _const_spec(e_inv.shape),
        ],
        out_specs=pl.BlockSpec((None, seq, LANES), lambda c, i: (i, 0, c)),
        out_shape=jax.ShapeDtypeStruct((b, seq, ch), F32),
        scratch_shapes=[pltpu.VMEM((n1 * n2, LANES), jnp.uint32), pltpu.VMEM((n1 * n2, LANES), jnp.uint32)],
        compiler_params=_cparams(("parallel", "parallel")),
        name="long_conv",
    )(u, g, spec, bias.reshape(1, ch).astype(F32), e_fwd, f_fwd, f_inv, e_inv)


def _dft_tables_complex(seq):
    n2 = _dft_n2(seq)
    n1 = seq // n2
    k1 = np.arange(n1, dtype=np.float64)[None, :, None]
    nn = (n2 * np.arange(n1, dtype=np.float64)[None, None, :] + np.arange(n2, dtype=np.float64)[:, None, None])
    ang = 2.0 * np.pi * k1 * nn / seq
    c, s = np.cos(ang), np.sin(ang)
    e_fwd = np.concatenate([np.concatenate([c, s], axis=2), np.concatenate([-s, c], axis=2)], axis=1)
    a2 = 2.0 * np.pi * np.outer(np.arange(n2), np.arange(n2)) / n2
    f_re = np.concatenate([np.cos(a2), np.sin(a2)], axis=1)
    return jnp.asarray(e_fwd, BF16), jnp.asarray(f_re, BF16)


def _seqdft_kernel(vr_ref, vi_ref, ef_ref, fr_ref, o_ref, scr, *, n1, n2, scale):
    def stage1(j, c):
        x = jnp.concatenate([vr_ref[pl.ds(j, n1, stride=n2), :], vi_ref[pl.ds(j, n1, stride=n2), :]], axis=0)
        a = jnp.dot(ef_ref[j], x.astype(BF16), preferred_element_type=F32)
        scr[pl.ds(pl.multiple_of(j * n1, n1), n1), :] = _pack_complex(a[:n1], a[n1:])
        return c

    _slab_loop(n2, stage1)

    def stage2(k1, c):
        a = _unpack_complex(scr[pl.ds(k1, n2, stride=n1), :])
        o_ref[pl.ds(k1, n2, stride=n1), :] = jnp.dot(fr_ref[...], a, preferred_element_type=F32) * scale
        return c

    _slab_loop(n1, stage2)


def seq_dft_real(v, ch, scale, tables):
    b, seq, _ = v.shape
    n2 = _dft_n2(seq)
    n1 = seq // n2
    e_fwd, f_re = tables
    nblk = ch // LANES
    kern = functools.partial(_seqdft_kernel, n1=n1, n2=n2, scale=scale)
    return pl.pallas_call(
        kern,
        grid=(nblk, b),
        in_specs=[
            pl.BlockSpec((None, seq, LANES), lambda c, i: (i, 0, c)),
            pl.BlockSpec((None, seq, LANES), lambda c, i: (i, 0, nblk + c)),
            _const_spec(e_fwd.shape),
            _const_spec(f_re.shape),
        ],
        out_specs=pl.BlockSpec((None, seq, LANES), lambda c, i: (i, 0, c)),
        out_shape=jax.ShapeDtypeStruct((b, seq, ch), F32),
        scratch_shapes=[pltpu.VMEM((n1 * n2, LANES), jnp.uint32)],
        compiler_params=_cparams(("parallel", "parallel")),
        name="seq_dft",
    )(v, v, e_fwd, f_re)


def fourier_channel_matrix():
    a = 2.0 * np.pi * np.outer(np.arange(FN_GROUP_DIM), np.arange(FN_GROUP_DIM)) / FN_GROUP_DIM
    eye = np.eye(FN_GROUPS)
    return jnp.asarray(np.concatenate([np.kron(eye, np.cos(a)), -np.kron(eye, np.sin(a))], axis=1), BF16)


def _shortconv_kernel(u_ref, w_ref, b_ref, o_ref):
    u = u_ref[...]
    n = u.shape[0]
    row = lax.broadcasted_iota(jnp.int32, u.shape, 0)
    prev = jnp.where(row == 0, 0.0, pltpu.roll(u, 1, axis=0))
    nxt = jnp.where(row == n - 1, 0.0, pltpu.roll(u, n - 1, axis=0))
    o_ref[...] = prev * w_ref[0:1, :] + u * w_ref[1:2, :] + nxt * w_ref[2:3, :] + b_ref[...]


def short_conv(u, w, bias):
    b, seq, ch = u.shape
    return pl.pallas_call(
        _shortconv_kernel,
        grid=(b, ch // LANES),
        in_specs=[
            pl.BlockSpec((None, seq, LANES), lambda i, c: (i, 0, c)),
            pl.BlockSpec((HY_SHORT, LANES), lambda i, c: (0, c)),
            pl.BlockSpec((1, LANES), lambda i, c: (0, c)),
        ],
        out_specs=pl.BlockSpec((None, seq, LANES), lambda i, c: (i, 0, c)),
        out_shape=jax.ShapeDtypeStruct((b, seq, ch), F32),
        compiler_params=_cparams(("parallel", "parallel")),
        name="short_conv",
    )(u, w.astype(F32), bias.reshape(1, ch).astype(F32))


def _filter_kernel(emb_ref, w1_ref, b1_ref, f1_ref, w2_ref, b2_ref, f2_ref, w3_ref, b3_ref, dec_ref, o_ref, mag_ref):
    z = jnp.dot(emb_ref[...].astype(BF16), w1_ref[...], preferred_element_type=F32) + b1_ref[...]
    z = jnp.sin(f1_ref[...] * z)
    z = jnp.dot(z.astype(BF16), w2_ref[...], preferred_element_type=F32) + b2_ref[...]
    z = jnp.sin(f2_ref[...] * z)
    h = jnp.dot(z.astype(BF16), w3_ref[...], preferred_element_type=F32) + b3_ref[...]
    filt = h * dec_ref[...]
    o_ref[...] = filt
    mag_ref[0] = jnp.sum(jnp.abs(filt), axis=0, keepdims=True)


def hyena_filters(seq, hy_w1, hy_b1, hy_freq1, hy_w2, hy_b2, hy_freq2, hy_w3, hy_b3):
    t = jnp.linspace(0.0, 1.0, seq, dtype=F32)[:, None]
    ang = (2.0 * math.pi / seq) * jnp.arange(seq, dtype=F32)[:, None]
    bands = jnp.linspace(1e-4, HY_EMB_BANDS - 1, HY_EMB_BANDS, dtype=F32)[None, :]
    emb = jnp.concatenate([t, jnp.cos(bands * ang), -jnp.sin(bands * ang)], axis=-1)
    kdim = emb.shape[1]
    kpad = LANES - kdim
    emb = jnp.pad(emb, ((0, 0), (0, kpad)))
    w1 = jnp.pad(hy_w1, ((0, kpad), (0, 0))).astype(BF16)
    deltas = jnp.abs(jnp.linspace(math.log(HY_DECAY_TARGET) / HY_SLOW_DECAY,
                                  math.log(HY_DECAY_TARGET) / HY_FAST_DECAY, HY_WIDTH, dtype=F32))
    decay = jnp.tile(jnp.exp(-t * deltas), (1, 2 * HY_ORDER))
    fo = hy_w1.shape[1]
    nout = hy_w3.shape[1]
    tl = _tile(seq, FILTER_ROWS)
    row = lambda a: a.reshape(1, -1).astype(F32)
    full = lambda shape: pl.BlockSpec(shape, lambda i: (0, 0))
    return pl.pallas_call(
        _filter_kernel,
        grid=(seq // tl,),
        in_specs=[
            pl.BlockSpec((tl, LANES), lambda i: (i, 0)),
            full((LANES, fo)), full((1, fo)), full((1, fo)),
            full((fo, fo)), full((1, fo)), full((1, fo)),
            full((fo, nout)), full((1, nout)),
            pl.BlockSpec((tl, nout), lambda i: (i, 0)),
        ],
        out_specs=[pl.BlockSpec((tl, nout), lambda i: (i, 0)), pl.BlockSpec((1, 1, nout), lambda i: (i, 0, 0))],
        out_shape=[jax.ShapeDtypeStruct((seq, nout), F32), jax.ShapeDtypeStruct((seq // tl, 1, nout), F32)],
        compiler_params=_cparams(("parallel",)),
        name="hyena_filter",
    )(emb, w1, row(hy_b1), row(hy_freq1), hy_w2.astype(BF16), row(hy_b2), row(hy_freq2),
      hy_w3.astype(BF16), row(hy_b3), decay)


def hyena_spectra(seq, filt, mag_blocks, tables):
    mag = jnp.sum(mag_blocks, axis=(0, 1)).reshape(HY_ORDER, 2, HY_WIDTH)
    lag0_bwd = jnp.abs(filt[0]).reshape(HY_ORDER, 2, HY_WIDTH)[:, 1]
    norm = mag[:, 0] + mag[:, 1] - lag0_bwd
    scale = (1.0 / (2 * seq)) / norm
    return filter_spectra(filt, scale[:, None, :], tables)


def _merge_kernel(x_ref, yf_ref, yh_ref, ya_ref, g_ref, wf_ref, wh_ref, wa_ref, wo_ref,
                  gate_ref, ng_ref, sh_ref, sc_ref, *rest):
    xo_ref, ho_ref = rest[-2:]
    d = x_ref.shape[-1]
    g = g_ref[0].astype(F32)
    yf = jnp.dot(yf_ref[0].astype(BF16), wf_ref[...], preferred_element_type=F32)
    yh = jnp.dot(yh_ref[0].astype(BF16), wh_ref[...], preferred_element_type=F32)
    ya = jnp.dot(ya_ref[0], wa_ref[...], preferred_element_type=F32)
    mix = g[:, 0:d] * yf + g[:, d:2 * d] * yh + g[:, 2 * d:3 * d] * ya
    x = x_ref[0] + gate_ref[0] * jnp.dot(mix.astype(BF16), wo_ref[...], preferred_element_type=F32)
    xo_ref[...] = x
    y = x * lax.rsqrt(jnp.mean(x * x, axis=-1, keepdims=True) + EPS) * ng_ref[...]
    ho_ref[...] = (y * (1.0 + sc_ref[0]) + sh_ref[0]).astype(ho_ref.dtype)


def merge_branches(x, yf, yh, ya, g, w_f, w_h, w_a, w_o, gate, norm_g, shift, scale, total_rows, row_offset, prior):
    b, l, d = x.shape
    tl = _tile(l, ROW_TILE)
    assert row_offset % tl == 0
    rows = lambda w: pl.BlockSpec((1, tl, w), lambda i, j: (i, j, 0))
    full = lambda a: pl.BlockSpec(a.shape, lambda i, j: (0, 0))
    per_b = pl.BlockSpec((1, 1, d), lambda i, j: (i, 0, 0))
    out_rows = pl.BlockSpec((tl, d), lambda i, j: (row_offset // tl + i * (l // tl) + j, 0))
    wf, wh, wa, wo = (w.astype(BF16) for w in (w_f, w_h, w_a, w_o))
    in_specs = [rows(d), rows(yf.shape[-1]), rows(yh.shape[-1]), rows(ya.shape[-1]), rows(3 * d),
                full(wf), full(wh), full(wa), full(wo),
                per_b, pl.BlockSpec((1, d), lambda i, j: (0, 0)), per_b, per_b]
    args = [x, yf, yh, ya, g, wf, wh, wa, wo, gate.reshape(b, 1, d), norm_g.reshape(1, d).astype(F32),
            shift.reshape(b, 1, d), scale.reshape(b, 1, d)]
    aliases = {}
    if prior is not None:
        aliases = {len(args): 0, len(args) + 1: 1}
        in_specs += [pl.BlockSpec(memory_space=pl.ANY), pl.BlockSpec(memory_space=pl.ANY)]
        args += list(prior)
    return pl.pallas_call(
        _merge_kernel,
        grid=(b, l // tl),
        in_specs=in_specs,
        out_specs=[out_rows, out_rows],
        out_shape=[jax.ShapeDtypeStruct((total_rows, d), F32), jax.ShapeDtypeStruct((total_rows, d), BF16)],
        input_output_aliases=aliases,
        compiler_params=_cparams(("parallel", "parallel")),
        name="merge",
    )(*args)


GLU_GROUP = 2 * LANES


def _glu_group_permutation():
    p = np.zeros((GLU_GROUP, GLU_GROUP), np.float32)
    j = np.arange(LANES)
    p[2 * j, j] = 1.0
    p[2 * j + 1, LANES + j] = 1.0
    return jnp.asarray(p, BF16)


def _moe_kernel(be_ref, act_ref, new_ref, rows_ref, w1_ref, b1_ref, w2_ref, b2_ref, p_ref, o_ref, w1s, w2s):
    i = pl.program_id(0)

    @pl.when(new_ref[i] > 0)
    def _():
        for q in range(w1s.shape[1] // GLU_GROUP):
            cols = slice(q * GLU_GROUP, (q + 1) * GLU_GROUP)
            w1s[:, cols] = jnp.dot(w1_ref[:, cols].astype(BF16), p_ref[...],
                                   preferred_element_type=F32).astype(BF16)
        w2s[...] = w2_ref[...].astype(BF16)

    @pl.when(act_ref[i] > 0)
    def _():
        u = jnp.dot(rows_ref[...], w1s[...], preferred_element_type=F32) + b1_ref[0]
        parts = []
        for q in range(u.shape[1] // GLU_GROUP):
            xg = jnp.minimum(u[:, q * GLU_GROUP:q * GLU_GROUP + LANES], SWIGLU_LIMIT)
            xl = jnp.clip(u[:, q * GLU_GROUP + LANES:(q + 1) * GLU_GROUP], -SWIGLU_LIMIT, SWIGLU_LIMIT)
            parts.append((xg * jax.nn.sigmoid(SWIGLU_ALPHA * xg) * (xl + 1.0)).astype(BF16))
        a = jnp.concatenate(parts, axis=1)
        y = jnp.dot(a, w2s[...], preferred_element_type=F32) + b2_ref[0]
        o_ref[...] = y.astype(o_ref.dtype)

    @pl.when(act_ref[i] == 0)
    def _():
        o_ref[...] = jnp.zeros(o_ref.shape, o_ref.dtype)


def moe_experts(rows, blk_exp, blk_act, blk_new, layer, w1_all, b1, w2_all, b2):
    r, d = rows.shape
    de = w2_all.shape[2]
    nblk = r // MOE_BLOCK
    grid_spec = pltpu.PrefetchScalarGridSpec(
        num_scalar_prefetch=3,
        grid=(nblk,),
        in_specs=[
            pl.BlockSpec((MOE_BLOCK, d), lambda i, be, act, new: (i, 0)),
            pl.BlockSpec((None, None, d, 2 * de), lambda i, be, act, new: (layer, be[i], 0, 0)),
            pl.BlockSpec((1, 1, 2 * de), lambda i, be, act, new: (be[i], 0, 0)),
            pl.BlockSpec((None, None, de, d), lambda i, be, act, new: (layer, be[i], 0, 0)),
            pl.BlockSpec((1, 1, d), lambda i, be, act, new: (be[i], 0, 0)),
            pl.BlockSpec((GLU_GROUP, GLU_GROUP), lambda i, be, act, new: (0, 0)),
        ],
        out_specs=pl.BlockSpec((MOE_BLOCK, d), lambda i, be, act, new: (i, 0)),
        scratch_shapes=[pltpu.VMEM((d, 2 * de), BF16), pltpu.VMEM((de, d), BF16)],
    )
    return pl.pallas_call(
        _moe_kernel,
        grid_spec=grid_spec,
        out_shape=jax.ShapeDtypeStruct((r, d), BF16),
        compiler_params=_cparams(("arbitrary",)),
        name="moe_experts",
    )(blk_exp, blk_act, blk_new, rows, w1_all, b1, w2_all, b2, _glu_group_permutation())


def _combine_kernel(y_ref, g_ref, x_ref, m_ref, *o_refs, split):
    g = g_ref[...]
    acc = g[:, 0:1] * y_ref[0].astype(F32)
    for j in range(1, TOP_K):
        acc = acc + g[:, j:j + 1] * y_ref[j].astype(F32)
    val = x_ref[...] + m_ref[0] * acc
    if split == 0:
        o_refs[0][...] = val
    else:
        i = pl.program_id(0)

        @pl.when(i < split)
        def _():
            o_refs[0][...] = val

        @pl.when(i >= split)
        def _():
            o_refs[1][...] = val


def moe_combine(y_sel, gate, resid, mod_blocks, tm, split):
    k, t, d = y_sel.shape
    nb = t // tm
    if split == 0:
        out_specs = [pl.BlockSpec((tm, d), lambda i: (i, 0))]
        out_shape = [jax.ShapeDtypeStruct((t, d), F32)]
    else:
        out_specs = [pl.BlockSpec((tm, d), lambda i: (jnp.minimum(i, split - 1), 0)),
                     pl.BlockSpec((tm, d), lambda i: (jnp.maximum(i - split, 0), 0))]
        out_shape = [jax.ShapeDtypeStruct((split * tm, d), F32), jax.ShapeDtypeStruct(((nb - split) * tm, d), F32)]
    return pl.pallas_call(
        functools.partial(_combine_kernel, split=split),
        grid=(nb,),
        in_specs=[pl.BlockSpec((k, tm, d), lambda i: (0, i, 0)),
                  pl.BlockSpec((tm, k), lambda i: (i, 0)),
                  pl.BlockSpec((tm, d), lambda i: (i, 0)),
                  pl.BlockSpec((1, 1, d), lambda i: (i, 0, 0))],
        out_specs=out_specs,
        out_shape=out_shape,
        compiler_params=_cparams(("arbitrary",)),
        name="moe_combine",
    )(y_sel, gate, resid, mod_blocks)


ROUTER_TM = 512


def _router_kernel(h_ref, w_ref, b_ref, tri_ref, e_ref, g_ref, r_ref, cnt_ref, carry):
    @pl.when(pl.program_id(0) == 0)
    def _():
        carry[...] = jnp.zeros(carry.shape, F32)

    logits = jnp.dot(h_ref[...], w_ref[...], preferred_element_type=F32) + b_ref[...]
    slot_lane = lax.broadcasted_iota(jnp.int32, logits.shape, 1)
    lane = slot_lane.astype(F32)
    rem = logits
    tops, idxs, hots = [], [], []
    for _ in range(TOP_K):
        m = jnp.max(rem, axis=1, keepdims=True)
        idx = jnp.min(jnp.where(rem == m, lane, float(LANES)), axis=1, keepdims=True)
        hot = lane == idx
        tops.append(m)
        idxs.append(idx)
        hots.append(hot)
        rem = jnp.where(hot, -jnp.inf, rem)
    ex = [jnp.exp(t - tops[0]) for t in tops]
    den = functools.reduce(lambda a, b: a + b, ex)
    sel = functools.reduce(lambda a, b: a + b, [hot.astype(F32) for hot in hots])
    before = jnp.dot(tri_ref[...], sel.astype(BF16), preferred_element_type=F32) + carry[...]
    e_out = jnp.zeros(logits.shape, F32)
    g_out = jnp.zeros(logits.shape, F32)
    r_out = jnp.zeros(logits.shape, F32)
    for j in range(TOP_K):
        slot = slot_lane == j
        e_out = jnp.where(slot, idxs[j], e_out)
        g_out = jnp.where(slot, ex[j] / den, g_out)
        r_out = jnp.where(slot, jnp.sum(jnp.where(hots[j], before, 0.0), axis=1, keepdims=True), r_out)
    e_ref[...] = e_out.astype(jnp.int32)
    g_ref[...] = g_out
    r_ref[...] = r_out.astype(jnp.int32)
    carry[...] = carry[...] + jnp.sum(sel, axis=0, keepdims=True)
    cnt_ref[...] = carry[...]


def route(h, w_router, b_router):
    t_tok, d = h.shape
    tm = _tile(t_tok, ROUTER_TM)
    wr = jnp.pad(w_router, ((0, 0), (0, LANES - N_EXPERTS))).astype(BF16)
    br = jnp.pad(b_router.astype(F32), (0, LANES - N_EXPERTS), constant_values=-jnp.inf).reshape(1, LANES)
    tri = jnp.asarray(np.tril(np.ones((tm, tm), np.float32), -1), BF16)
    full = lambda shape: pl.BlockSpec(shape, lambda i: (0, 0))
    rows = pl.BlockSpec((tm, LANES), lambda i: (i, 0))
    e, g, r, cnt = pl.pallas_call(
        _router_kernel,
        grid=(t_tok // tm,),
        in_specs=[pl.BlockSpec((tm, d), lambda i: (i, 0)), full((d, LANES)), full((1, LANES)), full((tm, tm))],
        out_specs=[rows, rows, rows, full((1, LANES))],
        out_shape=[jax.ShapeDtypeStruct((t_tok, LANES), jnp.int32), jax.ShapeDtypeStruct((t_tok, LANES), F32),
                   jax.ShapeDtypeStruct((t_tok, LANES), jnp.int32), jax.ShapeDtypeStruct((1, LANES), F32)],
        scratch_shapes=[pltpu.VMEM((1, LANES), F32)],
        compiler_params=_cparams(("arbitrary",)),
        name="router",
    )(h, wr, br, tri)
    return e[:, :TOP_K], g[:, :TOP_K], r[:, :TOP_K], cnt[0, :N_EXPERTS].astype(jnp.int32)


def moe_ffn(h, resid, mod_blocks, tm, split, p, between):
    t_tok, d = h.shape
    top_i, gate, rank, counts = route(h, p["w_router"], p["b_router"])
    n_assign = t_tok * TOP_K
    padded = (counts + MOE_BLOCK - 1) // MOE_BLOCK * MOE_BLOCK
    pad_end = jnp.cumsum(padded)
    pad_start = pad_end - padded
    experts = jnp.arange(N_EXPERTS, dtype=jnp.int32)
    dest = rank + jnp.sum(jnp.where(top_i[..., None] == experts, pad_start, 0), axis=-1)
    dest = dest.reshape(-1)
    n_blocks = -(-n_assign // MOE_BLOCK) + N_EXPERTS
    n_rows = n_blocks * MOE_BLOCK
    filled = jnp.zeros((n_rows,), jnp.int32).at[dest].add(jnp.arange(n_assign, dtype=jnp.int32) // TOP_K + 1)
    row_tok = jnp.where(filled > 0, filled - 1, jnp.arange(n_rows, dtype=jnp.int32) % t_tok)
    blk_start = jnp.arange(n_blocks, dtype=jnp.int32) * MOE_BLOCK
    blk_exp = jnp.minimum(jnp.sum((blk_start[:, None] >= pad_end[None, :]).astype(jnp.int32), axis=1),
                          N_EXPERTS - 1)
    blk_act = (blk_start < pad_end[-1]).astype(jnp.int32)
    blk_new = jnp.concatenate([jnp.ones((1,), jnp.int32), (blk_exp[1:] != blk_exp[:-1]).astype(jnp.int32)])
    rows = h[row_tok]
    extra = between() if between is not None else None
    y_rows = moe_experts(rows, blk_exp, blk_act, blk_new, p["layer"], p["w1_all"], p["b1"], p["w2_all"], p["b2"])
    y_sel = y_rows[dest.reshape(t_tok, TOP_K).T]
    return moe_combine(y_sel, gate, resid, mod_blocks, tm, split), extra


OFF_F = 0
OFF_HY = OFF_F + FN_WIDTH
OFF_Q = OFF_HY + (HY_ORDER + 1) * HY_WIDTH
OFF_K = OFF_Q + COL_QK
OFF_V = OFF_K + COL_QK
OFF_G = OFF_V + DA_WIDTH


def _projection_weights(w_in):
    w = w_in.astype(BF16)
    w_fv = matmul(w[:, OFF_F:OFF_HY], fourier_channel_matrix(), out_dtype=BF16, name="mm_wfold")
    w_cat = jnp.concatenate([w_fv, w[:, OFF_HY:]], axis=1)
    shift = w_fv.shape[1] - (OFF_HY - OFF_F)
    bounds = {"hy": (OFF_HY, OFF_Q), "q": (OFF_Q, OFF_K), "k": (OFF_K, OFF_V), "v": (OFF_V, OFF_G),
              "g": (OFF_G, w_in.shape[1])}
    cols = {"f": (0, w_fv.shape[1])}
    cols.update({name: (a + shift, b + shift) for name, (a, b) in bounds.items()})
    return w_cat, cols


def _project(x, mod_shift, mod_scale, p, rope):
    q_gain = p["q_norm_g"] * (DA_QK_DIM ** -0.5 * math.log2(math.e))
    return project_all(x, p["norm1_g"], mod_shift, mod_scale, p["w_cat"], p["cols"], q_gain, p["k_norm_g"],
                       rope[0], rope[1])


def _token_mixer(x, mod_shift1, mod_scale1, mod_gate, mod_shift2, mod_scale2, p, lam, lam_init, rope, kv_extra,
                 rows_out):
    b, s, d = x.shape
    v_f, z, q, k3, v3, g = _project(x, mod_shift1, mod_scale1, p, rope)

    y_f = seq_dft_real(v_f, FN_WIDTH, 1.0 / math.sqrt(s * FN_GROUP_DIM), _dft_tables_complex(s))

    z = short_conv(z, p["hy_conv_w"], p["hy_conv_b"])
    tables = _dft_tables_real(s)
    spec = p["spectra"][s]
    cb = HY_WIDTH // LANES
    y_h = long_conv_gated(z, 0, z, cb, spec[0], p["hy_bias"][0], tables)
    y_h = long_conv_gated(y_h, 0, z, 2 * cb, spec[1], p["hy_bias"][1], tables)

    if kv_extra is not None:
        k_all = jnp.concatenate([k3, kv_extra[0]], axis=1)
        v_all = jnp.concatenate([v3, kv_extra[1]], axis=1)
    else:
        k_all, v_all = k3, v3
    nk = k_all.shape[1]
    y_a = diff_attention(q.reshape(b * s, COL_QK), k_all.reshape(b * nk, COL_QK),
                         v_all.reshape(b * nk, 2 * DA_WIDTH), lam, p["subln_g"], 1.0 - lam_init, s, nk)
    y_a = y_a.reshape(b, s, DA_WIDTH)

    total_rows, row_offset, prior = rows_out
    buffers = merge_branches(x, y_f, y_h, y_a, g, p["w_f"], p["w_h"], p["w_a"], p["w_o"],
                             mod_gate, p["norm2_g"], mod_shift2, mod_scale2, total_rows, row_offset, prior)
    return buffers, (k3, v3)


def _layer_setup(p, c, c_ctx, seqs):
    b, d = c.shape
    cond = jnp.concatenate([c, c_ctx[None, :], jnp.zeros((BF16_ROWS - b - 1, d), F32)], axis=0)
    mod_all = matmul(jax.nn.silu(cond).astype(BF16), p["w_mod"].astype(BF16), epi="bias",
                     extra=(p["b_mod"].reshape(1, 6 * d).astype(F32),),
                     extra_specs=[pl.BlockSpec((1, _tile(6 * d, MM_TN)), lambda i, j: (0, j))], name="mm_mod")
    pre = {"mod_all": mod_all, "spectra": {}}
    pre["w_cat"], pre["cols"] = _projection_weights(p["w_in"])
    for s in seqs:
        filt, mag_blocks = hyena_filters(s, p["hy_w1"], p["hy_b1"], p["hy_freq1"], p["hy_w2"], p["hy_b2"],
                                         p["hy_freq2"], p["hy_w3"], p["hy_b3"])
        pre["spectra"][s] = hyena_spectra(s, filt, mag_blocks, _dft_tables_real(s))
    return pre


def _layer(l, x, xc, p, ctx_out, pre, next_setup):
    b, n_lat, d = x.shape
    n_ctx = xc.shape[1]
    lam_init = 0.8 - 0.6 * math.exp(-0.3 * l)
    lam = (jnp.exp(jnp.sum(p["lam_q"][0] * p["lam_k"][0]).astype(F32))
           - jnp.exp(jnp.sum(p["lam_q"][1] * p["lam_k"][1]).astype(F32)) + lam_init)

    mod_all = pre["mod_all"]
    mod = [mod_all[:b, i * d:(i + 1) * d] for i in range(6)]
    mod_c = [jnp.broadcast_to(mod_all[b, i * d:(i + 1) * d], (b, d)) for i in range(6)]

    pw = dict(p)
    pw.update(pre)

    no_rope = (jnp.ones((n_ctx, LANES), F32), jnp.zeros((n_ctx, LANES), F32))
    rows_c = b * n_ctx if ctx_out else 0
    total_rows = rows_c + b * n_lat
    prior = None
    if ctx_out:
        prior, kv_c = _token_mixer(xc, mod_c[0], mod_c[1], mod_c[2], mod_c[3], mod_c[4], pw, lam, lam_init,
                                   no_rope, None, (total_rows, 0, None))
    else:
        kv_c = _project(xc, mod_c[0], mod_c[1], pw, no_rope)[3:5]
    (resid, h_all), _ = _token_mixer(x, mod[0], mod[1], mod[2], mod[3], mod[4], pw, lam, lam_init,
                                     rope_tables(n_lat), kv_c, (total_rows, rows_c, prior))

    n_exp, two_f = p["b_e1"].shape
    b1 = p["b_e1"].reshape(n_exp, two_f // GLU_GROUP, LANES, 2).transpose(0, 1, 3, 2).reshape(n_exp, 1, two_f)
    pe = {
        "w_router": p["w_router"], "b_router": p["b_router"], "layer": l,
        "w1_all": p["w_e1_all"], "b1": b1, "w2_all": p["w_e2_all"], "b2": p["b_e2"][:, None, :],
    }
    tm = ROW_TILE
    assert rows_c % tm == 0 and n_lat % tm == 0
    split = rows_c // tm
    mod_blocks = jnp.concatenate([jnp.tile(mod_c[5][:1], (split, 1)), jnp.repeat(mod[5], n_lat // tm, axis=0)], axis=0)
    outs, next_pre = moe_ffn(h_all, resid, mod_blocks[:, None, :], tm, split, pe, next_setup)
    if ctx_out:
        xc = outs[0].reshape(b, n_ctx, d)
    x = outs[-1].reshape(b, n_lat, d)
    return x, xc, next_pre


_PARAM_NAMES = ("w_mod", "b_mod", "norm1_g", "norm2_g", "w_in", "hy_conv_w", "hy_conv_b", "hy_w1", "hy_b1",
                "hy_freq1", "hy_w2", "hy_b2", "hy_freq2", "hy_w3", "hy_b3", "hy_bias", "q_norm_g", "k_norm_g",
                "lam_q", "lam_k", "subln_g", "w_f", "w_h", "w_a", "w_o", "w_router", "b_router",
                "w_e1", "b_e1", "w_e2", "b_e2")


def kernel(x, c, ctx, c_ctx, w_mod, b_mod, norm1_g, norm2_g, w_in, hy_conv_w, hy_conv_b, hy_w1, hy_b1, hy_freq1,
           hy_w2, hy_b2, hy_freq2, hy_w3, hy_b3, hy_bias, q_norm_g, k_norm_g, lam_q, lam_k, subln_g, w_f, w_h,
           w_a, w_o, w_router, b_router, w_e1, b_e1, w_e2, b_e2):
    stacked = (w_mod, b_mod, norm1_g, norm2_g, w_in, hy_conv_w, hy_conv_b, hy_w1, hy_b1, hy_freq1, hy_w2, hy_b2,
               hy_freq2, hy_w3, hy_b3, hy_bias, q_norm_g, k_norm_g, lam_q, lam_k, subln_g, w_f, w_h, w_a, w_o,
               w_router, b_router, w_e1, b_e1, w_e2, b_e2)
    depth = w_mod.shape[0]
    n_lat, n_ctx = x.shape[1], ctx.shape[1]

    def layer_params(l):
        p = {name: arr[l] for name, arr in zip(_PARAM_NAMES, stacked) if name not in ("w_e1", "w_e2")}
        p["w_e1_all"], p["w_e2_all"] = w_e1, w_e2
        return p

    def setup(l):
        seqs = (n_lat, n_ctx) if l < depth - 1 else (n_lat,)
        return _layer_setup(layer_params(l), c, c_ctx, seqs)

    xc = ctx
    pre = setup(0)
    for l in range(depth):
        next_setup = functools.partial(setup, l + 1) if l + 1 < depth else None
        x, xc, pre = _layer(l, x, xc, layer_params(l), l < depth - 1, pre, next_setup)
    return x
```
